```python
import math
import jax
import jax.numpy as jnp
from jax import lax
import numpy as np

D_MODEL = 1024
BATCH = 1
SEQ = 16384
DEPTH = 1
DEC_BATCH = 128
DEC_SEQ = 8
PAST_LEN = 8192
PAGE_SIZE = 128

SSM_EXPAND = 2
D_INNER = SSM_EXPAND * D_MODEL
SSM_HEAD_DIM = 64
SSM_HEADS = D_INNER // SSM_HEAD_DIM
SSM_GROUPS = 8
SSM_STATE = 128
CONV_WIDTH = 4
CONV_DIM = D_INNER + 2 * SSM_GROUPS * SSM_STATE
SSD_CHUNK = 128
ATTN_HEAD_DIM = 64
HEADS_PER_GROUP = 8
ATTN_PATTERNS = ((128, 1), (512, 4), (2048, 16))
N_ATTN_GROUPS = 3
ATTN_HEADS = N_ATTN_GROUPS * HEADS_PER_GROUP
ATTN_WIDTH = ATTN_HEADS * ATTN_HEAD_DIM
ATTN_OUT_WIDTH = HEADS_PER_GROUP * ATTN_HEAD_DIM
ATTN_BLOCK = 128
N_BRANCHES = 2
IN_PROJ_DIM = D_INNER + CONV_DIM + SSM_HEADS + 3 * ATTN_WIDTH + N_BRANCHES * D_MODEL
N_EXPERT_GROUPS = 4
EXPERTS_PER_GROUP = 8
N_EXPERTS = N_EXPERT_GROUPS * EXPERTS_PER_GROUP
TOP_K_IN_GROUP = 2
D_EXPERT = 256
EPS = 1e-6

kernel_name = 'hybrid_ssd_dilated_attn_hmoe_step'


def rms_norm(x, w):
    xf = x.astype(jnp.float32)
    y = xf * lax.rsqrt(jnp.mean(xf * xf, axis=-1, keepdims=True) + EPS)
    return (y * w.astype(jnp.float32)).astype(x.dtype)


def gated_group_rms_norm(y, z, w):
    u = y.astype(jnp.float32) * jax.nn.silu(z.astype(jnp.float32))
    shp = u.shape
    ug = u.reshape(shp[:-1] + (SSM_GROUPS, shp[-1] // SSM_GROUPS))
    ug = ug * lax.rsqrt(jnp.mean(ug * ug, axis=-1, keepdims=True) + EPS)
    return (ug.reshape(shp) * w.astype(jnp.float32)).astype(z.dtype)


def split_last(t, sizes):
    offs, acc = [], 0
    for s in sizes[:-1]:
        acc += s
        offs.append(acc)
    return jnp.split(t, offs, axis=-1)


def causal_conv(u, buf, w, bias):
    l = u.shape[1]
    up = jnp.concatenate([buf.astype(u.dtype), u], axis=1)
    out = bias
    for tap in range(CONV_WIDTH):
        out = out + w[tap] * up[:, tap:tap + l]
    return out, up[:, l:]


def ssd_scan(x, dt, a, bm, cm, h0):
    b, l, hh, pd = x.shape
    g, n = bm.shape[2], bm.shape[3]
    r = hh // g
    q = math.gcd(l, SSD_CHUNK)
    nc = l // q
    xdt = (x.astype(jnp.float32) * dt[..., None]).reshape(b, nc, q, g, r, pd)
    a_cs = jnp.cumsum((dt * a).reshape(b, nc, q, g, r).transpose(0, 1, 3, 4, 2), axis=-1)
    bf = bm.astype(jnp.float32).reshape(b, nc, q, g, n)
    cf = cm.astype(jnp.float32).reshape(b, nc, q, g, n)
    causal = jnp.tril(jnp.ones((q, q), dtype=bool))
    diff = a_cs[..., :, None] - a_cs[..., None, :]
    lmat = jnp.where(causal, jnp.exp(jnp.where(causal, diff, 0.0)), 0.0)
    cb = jnp.einsum('bclgn,bcsgn->bcgls', cf, bf)
    y_diag = jnp.einsum('bcgls,bcgrls,bcsgrp->bclgrp', cb, lmat, xdt)
    decay_to_end = jnp.exp(a_cs[..., -1:] - a_cs)
    states = jnp.einsum('bcsgn,bcgrs,bcsgrp->bcgrpn', bf, decay_to_end, xdt)
    chunk_decay = jnp.exp(a_cs[..., -1])

    def step(h, inp):
        st, dec = inp
        return dec[..., None, None] * h + st, h

    h_last, h_prev = lax.scan(step, h0.reshape(b, g, r, pd, n),
                              (states.transpose(1, 0, 2, 3, 4, 5), chunk_decay.transpose(1, 0, 2, 3)))
    h_prev = h_prev.transpose(1, 0, 2, 3, 4, 5)
    y_off = jnp.einsum('bclgn,bcgrpn,bcgrl->bclgrp', cf, h_prev, jnp.exp(a_cs))
    y = (y_diag + y_off).reshape(b, l, hh, pd)
    return y, h_last.reshape(b, hh, pd, n)


def ssm_branch(z, xbc_raw, dt_raw, conv_buf, ssm_state, w_conv, b_conv, dt_bias, a_log, d_skip, w_ssm_norm):
    xbc, new_buf = causal_conv(xbc_raw, conv_buf, w_conv, b_conv)
    xbc = jax.nn.silu(xbc)
    xs, bm, cm = split_last(xbc, [D_INNER, SSM_GROUPS * SSM_STATE, SSM_GROUPS * SSM_STATE])
    b, l, _ = xs.shape
    xs = xs.reshape(b, l, SSM_HEADS, SSM_HEAD_DIM)
    bm = bm.reshape(b, l, SSM_GROUPS, SSM_STATE)
    cm = cm.reshape(b, l, SSM_GROUPS, SSM_STATE)
    dt = jax.nn.softplus(dt_raw.astype(jnp.float32) + dt_bias.astype(jnp.float32))
    a = -jnp.exp(a_log.astype(jnp.float32))
    y, new_state = ssd_scan(xs, dt, a, bm, cm, ssm_state.astype(jnp.float32))
    y = y + d_skip.astype(jnp.float32)[:, None] * xs.astype(jnp.float32)
    y = gated_group_rms_norm(y.reshape(b, l, D_INNER), z, w_ssm_norm)
    return y, new_buf, new_state


def dilated_attn_prompt(q, k, v, window, dilation):
    b, s, h, dh = q.shape
    span = window // dilation
    n = s // dilation
    nb = -(-n // ATTN_BLOCK)
    n_pad = nb * ATTN_BLOCK

    def to_sub(t):
        t = t.reshape(b, n, dilation, h, dh).transpose(0, 2, 1, 3, 4)
        return jnp.pad(t, ((0, 0), (0, 0), (0, n_pad - n), (0, 0), (0, 0)))

    def key_blocks(t):
        t = jnp.pad(t, ((0, 0), (0, 0), (ATTN_BLOCK, 0), (0, 0), (0, 0)))
        t = t.reshape(b, dilation, nb + 1, ATTN_BLOCK, h, dh)
        return jnp.concatenate([t[:, :, :-1], t[:, :, 1:]], axis=3)

    qb = to_sub(q).reshape(b, dilation, nb, ATTN_BLOCK, h, dh)
    kb = key_blocks(to_sub(k))
    vb = key_blocks(to_sub(v))
    scores = jnp.einsum('brnqhd,brnkhd->brnhqk', qb, kb,
                        preferred_element_type=jnp.float32) * (ATTN_HEAD_DIM ** -0.5)
    qi = jnp.arange(ATTN_BLOCK)[:, None]
    kc = jnp.arange(2 * ATTN_BLOCK)[None, :]
    dist = ATTN_BLOCK + qi - kc
    key_idx = jnp.arange(nb)[:, None, None] * ATTN_BLOCK - ATTN_BLOCK + kc[None]
    mask = (dist >= 0)[None] & (dist <= span)[None] & (key_idx >= 0)
    scores = jnp.where(mask[None, None, :, None], scores, -jnp.inf)
    m = jnp.max(scores, axis=-1, keepdims=True)
    p = jnp.exp(scores - m)
    denom = jnp.sum(p, axis=-1, keepdims=True)
    lse = (m + jnp.log(denom))[..., 0]
    o = jnp.einsum('brnhqk,brnkhd->brnqhd', (p / denom).astype(v.dtype), vb)
    o = o.reshape(b, dilation, n_pad, h, dh)[:, :, :n].transpose(0, 2, 1, 3, 4).reshape(b, s, h, dh)
    lse = lse.transpose(0, 1, 2, 4, 3).reshape(b, dilation, n_pad, h)[:, :, :n]
    lse = lse.transpose(0, 2, 1, 3).reshape(b, s, h)
    return o, lse


def dilated_attn_sample(q, k_all, v_all, window, dilation, pos0):
    b, l, h, dh = q.shape
    rows = k_all.shape[1]
    span = window // dilation
    t = pos0 + jnp.arange(l)
    pos = t[:, None] - dilation * jnp.arange(span + 1)[None, :]
    row = pos - (pos0 + l - rows)
    valid = (pos >= 0) & (row >= 0)
    row = jnp.clip(row, 0, rows - 1)
    kg = k_all[:, row]
    vg = v_all[:, row]
    scores = jnp.einsum('blhd,bljhd->blhj', q, kg,
                        preferred_element_type=jnp.float32) * (ATTN_HEAD_DIM ** -0.5)
    scores = jnp.where(valid[None, :, None, :], scores, -jnp.inf)
    m = jnp.max(scores, axis=-1, keepdims=True)
    p = jnp.exp(scores - m)
    denom = jnp.sum(p, axis=-1, keepdims=True)
    lse = (m + jnp.log(denom))[..., 0]
    o = jnp.einsum('blhj,bljhd->blhd', (p / denom).astype(v_all.dtype), vg)
    return o, lse


def attn_branch(q, k, v, kv_bufs, pos0, w_q_norm, w_k_norm):
    b, l, _ = q.shape
    shp = (b, l, N_ATTN_GROUPS, HEADS_PER_GROUP, ATTN_HEAD_DIM)
    q = rms_norm(q.reshape(shp), w_q_norm)
    k = rms_norm(k.reshape(shp), w_k_norm)
    v = v.reshape(shp)
    outs, lses, new_bufs = [], [], []
    for gi, (window, dilation) in enumerate(ATTN_PATTERNS):
        qg, kg, vg = q[:, :, gi], k[:, :, gi], v[:, :, gi]
        if kv_bufs is None:
            o, lse = dilated_attn_prompt(qg, kg, vg, window, dilation)
            keep = min(window, l)
            new_bufs += [kg[:, l - keep:], vg[:, l - keep:]]
        else:
            k_buf, v_buf = kv_bufs[2 * gi], kv_bufs[2 * gi + 1]
            keep = k_buf.shape[1]
            k_all = jnp.concatenate([k_buf.astype(kg.dtype), kg], axis=1)
            v_all = jnp.concatenate([v_buf.astype(vg.dtype), vg], axis=1)
            o, lse = dilated_attn_sample(qg, k_all, v_all, window, dilation, pos0)
            new_bufs += [k_all[:, -keep:], v_all[:, -keep:]]
        outs.append(o)
        lses.append(lse)
    w = jax.nn.softmax(jnp.stack(lses, axis=0), axis=0).astype(q.dtype)
    o = jnp.einsum('gblh,gblhd->blhd', w, jnp.stack(outs, axis=0))
    return o.reshape(b, l, ATTN_OUT_WIDTH), tuple(new_bufs)


def hier_moe(x, w_router_group, b_router_group, w_router_expert, b_router_expert,
             w_expert_gate, w_expert_up, w_expert_down):
    b, l, d = x.shape
    xt = x.reshape(b * l, d)
    g_logits = jnp.einsum('td,dg->tg', xt, w_router_group).astype(jnp.float32) + b_router_group.astype(jnp.float32)
    g_prob = jax.nn.softmax(g_logits, axis=-1)
    g_idx = jnp.argmax(g_logits, axis=-1)
    g_w = jnp.take_along_axis(g_prob, g_idx[:, None], axis=1)
    e_logits = jnp.einsum('td,dge->tge', xt, w_router_expert).astype(jnp.float32) + b_router_expert.astype(jnp.float32)
    e_sel = jnp.take_along_axis(e_logits, g_idx[:, None, None], axis=1)[:, 0]
    e_val, e_idx = lax.top_k(e_sel, TOP_K_IN_GROUP)
    e_w = jax.nn.softmax(e_val, axis=-1) * g_w
    expert_id = g_idx[:, None] * EXPERTS_PER_GROUP + e_idx
    comb = jnp.einsum('tk,tke->te', e_w, jax.nn.one_hot(expert_id, N_EXPERTS, dtype=jnp.float32)).astype(x.dtype)
    hg = jnp.einsum('td,edf->tef', xt, w_expert_gate)
    hu = jnp.einsum('td,edf->tef', xt, w_expert_up)
    hidden = jax.nn.silu(hg) * hu * comb[..., None]
    y = jnp.einsum('tef,efd->td', hidden, w_expert_down)
    return y.reshape(b, l, d)


def decoder_layer(x, conv_buf, ssm_state, kv_bufs, pos0, p):
    h = rms_norm(x, p['norm_mix'])
    proj = jnp.einsum('bld,de->ble', h, p['w_in'])
    z, xbc, dt_raw, q, k, v, gates = split_last(
        proj, [D_INNER, CONV_DIM, SSM_HEADS, ATTN_WIDTH, ATTN_WIDTH, ATTN_WIDTH, N_BRANCHES * D_MODEL])
    y_ssm, new_conv, new_ssm = ssm_branch(z, xbc, dt_raw, conv_buf, ssm_state, p['w_conv'], p['b_conv'],
                                          p['dt_bias'], p['a_log'], p['d_skip'], p['w_ssm_norm'])
    y_attn, new_kv = attn_branch(q, k, v, kv_bufs, pos0, p['w_q_norm'], p['w_k_norm'])
    b, l, _ = x.shape
    g = jax.nn.sigmoid(gates.astype(jnp.float32)).astype(x.dtype).reshape(b, l, N_BRANCHES, D_MODEL)
    merged = (g[:, :, 0] * jnp.einsum('ble,ed->bld', y_ssm, p['w_br_ssm'])
              + g[:, :, 1] * jnp.einsum('ble,ed->bld', y_attn, p['w_br_attn']))
    x = x + jnp.einsum('bld,de->ble', merged, p['w_out'])
    x = x + hier_moe(rms_norm(x, p['norm_ffn']), p['w_router_group'], p['b_router_group'],
                     p['w_router_expert'], p['b_router_expert'],
                     p['w_expert_gate'], p['w_expert_up'], p['w_expert_down'])
    return x, (new_conv, new_ssm) + new_kv


def setup_inputs(seed: int = 0) -> dict:
    key = jax.random.key(seed)
    ks = jax.random.split(key, 40)
    f32 = jnp.float32

    def nrm(k, shape, scale):
        return jax.random.normal(k, shape, f32) * scale

    w_eff = [min(w, PAST_LEN) for (w, _) in ATTN_PATTERNS]
    kv_shape = lambda rows: (DEPTH, DEC_BATCH, rows, HEADS_PER_GROUP, ATTN_HEAD_DIM)
    dt0 = jnp.exp(jax.random.uniform(ks[12], (DEPTH, SSM_HEADS), f32, math.log(1e-3), math.log(1e-1)))
    return {
        'x_prompt': nrm(ks[0], (BATCH, SEQ, D_MODEL), 1.0),
        'x_sample': nrm(ks[1], (DEC_BATCH, DEC_SEQ, D_MODEL), 1.0),
        'cache_conv': nrm(ks[2], (DEPTH, DEC_BATCH, CONV_WIDTH - 1, CONV_DIM), 1.0),
        'state_ssm': nrm(ks[3], (DEPTH, DEC_BATCH, SSM_HEADS, SSM_HEAD_DIM, SSM_STATE), 0.1),
        'cache_k_w128': nrm(ks[4], kv_shape(w_eff[0]), 1.0),
        'cache_v_w128': nrm(ks[5], kv_shape(w_eff[0]), 1.0),
        'cache_k_w512': nrm(ks[6], kv_shape(w_eff[1]), 1.0),
        'cache_v_w512': nrm(ks[7], kv_shape(w_eff[1]), 1.0),
        'cache_k_w2048': nrm(ks[8], kv_shape(w_eff[2]), 1.0),
        'cache_v_w2048': nrm(ks[9], kv_shape(w_eff[2]), 1.0),
        'w_in': nrm(ks[10], (DEPTH, D_MODEL, IN_PROJ_DIM), D_MODEL ** -0.5),
        'norm_mix': 1.0 + nrm(ks[11], (DEPTH, D_MODEL), 0.02),
        'w_conv': nrm(ks[13], (DEPTH, CONV_WIDTH, CONV_DIM), CONV_WIDTH ** -0.5),
        'b_conv': nrm(ks[14], (DEPTH, CONV_DIM), 0.02),
        'dt_bias': dt0 + jnp.log(-jnp.expm1(-dt0)),
        'a_log': jnp.log(jax.random.uniform(ks[15], (DEPTH, SSM_HEADS), f32, 1.0, 16.0)),
        'd_skip': 1.0 + nrm(ks[16], (DEPTH, SSM_HEADS), 0.02),
        'w_ssm_norm': 1.0 + nrm(ks[17], (DEPTH, D_INNER), 0.02),
        'w_q_norm': 1.0 + nrm(ks[18], (DEPTH, ATTN_HEAD_DIM), 0.02),
        'w_k_norm': 1.0 + nrm(ks[19], (DEPTH, ATTN_HEAD_DIM), 0.02),
        'w_br_ssm': nrm(ks[20], (DEPTH, D_INNER, D_MODEL), D_INNER ** -0.5),
        'w_br_attn': nrm(ks[21], (DEPTH, ATTN_OUT_WIDTH, D_MODEL), ATTN_OUT_WIDTH ** -0.5),
        'w_out': nrm(ks[22], (DEPTH, D_MODEL, D_MODEL), D_MODEL ** -0.5),
        'norm_ffn': 1.0 + nrm(ks[23], (DEPTH, D_MODEL), 0.02),
        'w_router_group': nrm(ks[24], (DEPTH, D_MODEL, N_EXPERT_GROUPS), D_MODEL ** -0.5),
        'b_router_group': nrm(ks[25], (DEPTH, N_EXPERT_GROUPS), 0.01),
        'w_router_expert': nrm(ks[26], (DEPTH, D_MODEL, N_EXPERT_GROUPS, EXPERTS_PER_GROUP), D_MODEL ** -0.5),
        'b_router_expert': nrm(ks[27], (DEPTH, N_EXPERT_GROUPS, EXPERTS_PER_GROUP), 0.01),
        'w_expert_gate': nrm(ks[28], (DEPTH, N_EXPERTS, D_MODEL, D_EXPERT), D_MODEL ** -0.5),
        'w_expert_up': nrm(ks[29], (DEPTH, N_EXPERTS, D_MODEL, D_EXPERT), D_MODEL ** -0.5),
        'w_expert_down': nrm(ks[30], (DEPTH, N_EXPERTS, D_EXPERT, D_MODEL), D_EXPERT ** -0.5),
    }


def reference(x_prompt, x_sample, cache_conv, state_ssm, cache_k_w128, cache_v_w128, cache_k_w512,
              cache_v_w512, cache_k_w2048, cache_v_w2048, w_in, norm_mix, w_conv, b_conv, dt_bias, a_log,
              d_skip, w_ssm_norm, w_q_norm, w_k_norm, w_br_ssm, w_br_attn, w_out, norm_ffn, w_router_group,
              b_router_group, w_router_expert, b_router_expert, w_expert_gate, w_expert_up, w_expert_down):
    yp, ys = x_prompt, x_sample
    bp = x_prompt.shape[0]
    prompt_states, sample_states = [], []
    for i in range(DEPTH):
        p = {'w_in': w_in[i], 'norm_mix': norm_mix[i], 'w_conv': w_conv[i], 'b_conv': b_conv[i],
             'dt_bias': dt_bias[i], 'a_log': a_log[i], 'd_skip': d_skip[i], 'w_ssm_norm': w_ssm_norm[i],
             'w_q_norm': w_q_norm[i], 'w_k_norm': w_k_norm[i], 'w_br_ssm': w_br_ssm[i],
             'w_br_attn': w_br_attn[i], 'w_out': w_out[i], 'norm_ffn': norm_ffn[i],
             'w_router_group': w_router_group[i], 'b_router_group': b_router_group[i],
             'w_router_expert': w_router_expert[i], 'b_router_expert': b_router_expert[i],
             'w_expert_gate': w_expert_gate[i], 'w_expert_up': w_expert_up[i],
             'w_expert_down': w_expert_down[i]}
        zero_conv = jnp.zeros((bp, CONV_WIDTH - 1, CONV_DIM), x_prompt.dtype)
        zero_ssm = jnp.zeros((bp, SSM_HEADS, SSM_HEAD_DIM, SSM_STATE), jnp.float32)
        yp, st_p = decoder_layer(yp, zero_conv, zero_ssm, None, 0, p)
        kv_bufs = (cache_k_w128[i], cache_v_w128[i], cache_k_w512[i], cache_v_w512[i],
                   cache_k_w2048[i], cache_v_w2048[i])
        ys, st_s = decoder_layer(ys, cache_conv[i], state_ssm[i], kv_bufs, PAST_LEN, p)
        prompt_states.append(st_p)
        sample_states.append(st_s)
    (prompt_conv, prompt_ssm, prompt_k_w128, prompt_v_w128, prompt_k_w512, prompt_v_w512,
     prompt_k_w2048, prompt_v_w2048) = [jnp.stack([s[j] for s in prompt_states]) for j in range(8)]
    (sample_conv, sample_ssm, sample_k_w128, sample_v_w128, sample_k_w512, sample_v_w512,
     sample_k_w2048, sample_v_w2048) = [jnp.stack([s[j] for s in sample_states]) for j in range(8)]
    return (yp, ys, prompt_conv, prompt_ssm, prompt_k_w128, prompt_v_w128, prompt_k_w512, prompt_v_w512,
            prompt_k_w2048, prompt_v_w2048, sample_conv, sample_ssm, sample_k_w128, sample_v_w128,
            sample_k_w512, sample_v_w512, sample_k_w2048, sample_v_w2048)
```

```python
import functools
import math

import jax
import jax.numpy as jnp
from jax import lax
from jax.experimental import pallas as pl
from jax.experimental.pallas import tpu as pltpu

F32 = jnp.float32
BF16 = jnp.bfloat16

PAST_LEN = 8192
D_MODEL = 1024
D_INNER = 2048
SSM_HEAD_DIM = 64
SSM_HEADS = 32
SSM_GROUPS = 8
SSM_STATE = 128
CONV_WIDTH = 4
CONV_DIM = D_INNER + 2 * SSM_GROUPS * SSM_STATE
SSD_CHUNK = 128
ATTN_HEAD_DIM = 64
HEADS_PER_GROUP = 8
ATTN_PATTERNS = ((128, 1), (512, 4), (2048, 16))
N_ATTN_GROUPS = 3
ATTN_GROUP_WIDTH = HEADS_PER_GROUP * ATTN_HEAD_DIM
ATTN_WIDTH = N_ATTN_GROUPS * ATTN_GROUP_WIDTH
ATTN_BLOCK = 128
N_EXPERT_GROUPS = 4
EXPERTS_PER_GROUP = 8
N_EXPERTS = 32
D_EXPERT = 256
EPS = 1e-6

LANES = 128
SUBLANES = 8
VMEM_LIMIT = 56 * 1024 * 1024

PROJ_TILE = 512
OFF_Z = 0
OFF_X = 2048
OFF_B = 4096
OFF_C = 5120
OFF_G = 6144
OFF_Q = 8192
OFF_K = OFF_Q + ATTN_WIDTH
OFF_V = OFF_K + ATTN_WIDTH
PROJ_WIDTH = OFF_V + ATTN_WIDTH
N_PROJ_TILES = PROJ_WIDTH // PROJ_TILE
Q_TILE0 = OFF_Q // PROJ_TILE
K_TILE0 = OFF_K // PROJ_TILE
V_TILE0 = OFF_V // PROJ_TILE
ROUTER_WIDTH = LANES


def _split3(x):
    x1 = x.astype(BF16)
    r1 = x - x1.astype(F32)
    x2 = r1.astype(BF16)
    r2 = r1 - x2.astype(F32)
    return x1, x2, r2.astype(BF16)


def _dot(a, b):
    return jnp.dot(a, b, preferred_element_type=F32)


def _dot3(x, m):
    x1, x2, x3 = _split3(x)
    return _dot(x1, m) + _dot(x2, m) + _dot(x3, m)


def _dot3f(x, m):
    x1, x2, x3 = _split3(x)
    return _dot(x1.astype(F32), m) + _dot(x2.astype(F32), m) + _dot(x3.astype(F32), m)


def _dot_nt(a, b):
    return lax.dot_general(a, b, (((1,), (1,)), ((), ())), preferred_element_type=F32)


def _dot_tn(a, b):
    return lax.dot_general(a, b, (((0,), (0,)), ((), ())), preferred_element_type=F32)


def _iota(shape, dim):
    return lax.broadcasted_iota(jnp.int32, shape, dim)


def _shr(x, pow2):
    assert pow2 & (pow2 - 1) == 0
    return lax.shift_right_logical(x, jnp.int32(int(math.log2(pow2))))


def _in_proj_kernel(x_ref, nw_ref, w_ref, wdt_ref, qkw_ref, seg_ref, out_ref, dt_ref, h_sc):
    j = pl.program_id(1)

    @pl.when(j == 0)
    def _():
        xf = x_ref[...]
        y = xf * lax.rsqrt(jnp.mean(xf * xf, axis=-1, keepdims=True) + EPS)
        h = (y * nw_ref[...]).astype(BF16)
        h_sc[...] = h
        dt_ref[...] = _dot(h, wdt_ref[...])

    acc = _dot(h_sc[...], w_ref[...])
    is_qk = jnp.logical_and(j >= Q_TILE0, j < V_TILE0)

    @pl.when(jnp.logical_not(is_qk))
    def _():
        out_ref[...] = acc

    @pl.when(is_qk)
    def _():
        sq = acc * acc
        s1 = sq.astype(BF16)
        s2 = (sq - s1.astype(F32)).astype(BF16)
        ms = (_dot(s1, seg_ref[...]) + _dot(s2, seg_ref[...])) * (1.0 / ATTN_HEAD_DIM)
        scale = jnp.where(j < K_TILE0, ATTN_HEAD_DIM ** -0.5, 1.0).astype(F32)
        w = qkw_ref[0] * scale
        out_ref[...] = acc * lax.rsqrt(ms + EPS) * w


def _in_proj(x, norm_w, w_bf, wdt_bf, qk_w, tm):
    t = x.shape[0]
    seg = (_iota((PROJ_TILE, PROJ_TILE), 0) // ATTN_HEAD_DIM
           == _iota((PROJ_TILE, PROJ_TILE), 1) // ATTN_HEAD_DIM).astype(BF16)
    return pl.pallas_call(
        _in_proj_kernel,
        grid=(t // tm, N_PROJ_TILES),
        in_specs=[
            pl.BlockSpec((tm, D_MODEL), lambda i, j: (i, 0)),
            pl.BlockSpec((1, D_MODEL), lambda i, j: (0, 0)),
            pl.BlockSpec((D_MODEL, PROJ_TILE), lambda i, j: (0, j)),
            pl.BlockSpec((D_MODEL, LANES), lambda i, j: (0, 0)),
            pl.BlockSpec((1, 1, PROJ_TILE), lambda i, j: (j, 0, 0)),
            pl.BlockSpec((PROJ_TILE, PROJ_TILE), lambda i, j: (0, 0)),
        ],
        out_specs=[
            pl.BlockSpec((tm, PROJ_TILE), lambda i, j: (i, j)),
            pl.BlockSpec((tm, LANES), lambda i, j: (i, 0)),
        ],
        out_shape=[
            jax.ShapeDtypeStruct((t, PROJ_WIDTH), F32),
            jax.ShapeDtypeStruct((t, LANES), F32),
        ],
        scratch_shapes=[pltpu.VMEM((tm, D_MODEL), BF16)],
        compiler_params=pltpu.CompilerParams(
            dimension_semantics=("parallel", "arbitrary"), vmem_limit_bytes=VMEM_LIMIT),
        name="in_proj",
    )(x, norm_w, w_bf, wdt_bf, qk_w, seg)


def _softplus(x):
    return jnp.maximum(x, 0.0) + jnp.log1p(jnp.exp(-jnp.abs(x)))


def _conv_silu(raw, tail, w, b):
    q = raw.shape[0]
    up = jnp.concatenate([tail, raw], axis=0)
    acc = b + w[CONV_WIDTH - 1:CONV_WIDTH] * raw
    for tap in range(CONV_WIDTH - 1):
        shifted = pltpu.roll(up, CONV_WIDTH - 1 - tap, axis=0)[SUBLANES:SUBLANES + q]
        acc = acc + w[tap:tap + 1] * shifted
    return jax.nn.silu(acc)


def _ssm_chunk_kernel(z_ref, xs_ref, bm_ref, cm_ref, dt_ref, cbuf_ref, h0_ref, wconv_ref, bconv_ref,
                      dtb_ref, alog_ref, dskip_ref, wnorm_ref, expand_ref,
                      y_ref, hout_ref, tail_sc, xbc_sc, ht_sc):
    c = pl.program_id(1)
    nc = pl.num_programs(1)
    q = xs_ref.shape[0]
    gw = D_INNER // SSM_GROUPS
    n = SSM_STATE

    @pl.when(c == 0)
    def _():
        tail_sc[...] = jnp.zeros_like(tail_sc)
        tail_sc[SUBLANES - (CONV_WIDTH - 1):SUBLANES, :] = cbuf_ref[0]
        for g in range(SSM_GROUPS):
            ht_sc[g] = h0_ref[0, g * gw:(g + 1) * gw, :].T

    cw = PROJ_TILE
    for k in range(CONV_DIM // cw):
        if k < D_INNER // cw:
            raw = xs_ref[:, k * cw:(k + 1) * cw]
        elif k < (D_INNER + SSM_GROUPS * n) // cw:
            kk = k - D_INNER // cw
            raw = bm_ref[:, kk * cw:(kk + 1) * cw]
        else:
            kk = k - (D_INNER + SSM_GROUPS * n) // cw
            raw = cm_ref[:, kk * cw:(kk + 1) * cw]
        sl = slice(k * cw, (k + 1) * cw)
        xbc_sc[:, sl] = _conv_silu(raw, tail_sc[:, sl], wconv_ref[:, sl], bconv_ref[:, sl])
        tail_sc[:, sl] = raw[q - SUBLANES:q]

    dt = _softplus(dt_ref[...] + dtb_ref[...])
    a = -jnp.exp(alog_ref[...])
    dta = dt * a
    tri = (_iota((q, q), 0) >= _iota((q, q), 1)).astype(BF16)
    d1, d2, d3 = _split3(dta)
    acs = _dot(tri, d1) + _dot(tri, d2) + _dot(tri, d3)
    eye = (_iota((LANES, LANES), 0) == _iota((LANES, LANES), 1)).astype(BF16)
    c1, c2, c3 = _split3(acs)
    acs_t = _dot_nt(eye, c1) + _dot_nt(eye, c2) + _dot_nt(eye, c3)
    a_end = acs[q - 1:q, :]
    e_in = jnp.exp(acs)
    e_out = jnp.exp(a_end - acs)
    e_all = jnp.exp(a_end)

    ex = expand_ref[...]
    dt_x = _dot3(dt, ex)
    e_in_x = _dot3(e_in, ex)
    e_out_x = _dot3(e_out, ex)
    e_all_x = _dot3(jnp.broadcast_to(e_all, (2 * SUBLANES, LANES)), ex)[0:1]

    causal = _iota((q, q), 0) >= _iota((q, q), 1)
    lane_lo = _iota((q, LANES), 1) < SSM_HEAD_DIM

    for g in range(SSM_GROUPS):
        gs = slice(g * gw, (g + 1) * gw)
        xg = xbc_sc[:, gs]
        bg = xbc_sc[:, D_INNER + g * n:D_INNER + (g + 1) * n].astype(BF16)
        cg = xbc_sc[:, D_INNER + SSM_GROUPS * n + g * n:D_INNER + SSM_GROUPS * n + (g + 1) * n].astype(BF16)
        xdt = xg * dt_x[:, gs]
        cb = _dot_nt(cg, bg)
        ht = ht_sc[g]
        y_off = _dot(cg, ht.astype(BF16)) * e_in_x[:, gs]
        pieces = []
        for pair in range(2):
            ms = []
            for hh in range(2):
                h = g * 4 + pair * 2 + hh
                diff = acs[:, h:h + 1] - acs_t[h:h + 1, :]
                lmat = jnp.where(causal, jnp.exp(jnp.where(causal, diff, 0.0)), 0.0)
                ms.append((cb * lmat).astype(BF16))
            xp = xdt[:, pair * LANES:(pair + 1) * LANES]
            rhs = jnp.concatenate([jnp.where(lane_lo, xp, 0.0), jnp.where(lane_lo, 0.0, xp)],
                                  axis=0).astype(BF16)
            pieces.append(_dot(jnp.concatenate(ms, axis=1), rhs))
        y_diag = jnp.concatenate(pieces, axis=1)
        yg = y_diag + y_off + dskip_ref[:, gs] * xg
        zg = z_ref[:, gs]
        u = yg * jax.nn.silu(zg)
        u = u * lax.rsqrt(jnp.mean(u * u, axis=-1, keepdims=True) + EPS)
        y_ref[:, gs] = (u * wnorm_ref[:, gs]).astype(y_ref.dtype)
        st = _dot_tn(bg, (xdt * e_out_x[:, gs]).astype(BF16))
        ht_sc[g] = e_all_x[:, gs] * ht + st

    @pl.when(c == nc - 1)
    def _():
        for g in range(SSM_GROUPS):
            hout_ref[0, g * gw:(g + 1) * gw, :] = ht_sc[g].T


def _ssm_chunked(proj, dt_raw, conv_buf, h0, sp, batch, q):
    t = proj.shape[0]
    nc = t // batch // q
    row = lambda b, c: b * nc + c
    col = lambda off, w: off // w
    full = lambda shape: pl.BlockSpec(shape, lambda b, c: (0,) * len(shape))
    return pl.pallas_call(
        _ssm_chunk_kernel,
        grid=(batch, nc),
        in_specs=[
            pl.BlockSpec((q, D_INNER), lambda b, c: (row(b, c), col(OFF_Z, D_INNER))),
            pl.BlockSpec((q, D_INNER), lambda b, c: (row(b, c), col(OFF_X, D_INNER))),
            pl.BlockSpec((q, 1024), lambda b, c: (row(b, c), col(OFF_B, 1024))),
            pl.BlockSpec((q, 1024), lambda b, c: (row(b, c), col(OFF_C, 1024))),
            pl.BlockSpec((q, LANES), lambda b, c: (row(b, c), 0)),
            pl.BlockSpec((1, CONV_WIDTH - 1, CONV_DIM), lambda b, c: (b, 0, 0)),
            pl.BlockSpec((1, D_INNER, SSM_STATE), lambda b, c: (b, 0, 0)),
            full((CONV_WIDTH, CONV_DIM)),
            full((1, CONV_DIM)),
            full((1, LANES)),
            full((1, LANES)),
            full((1, D_INNER)),
            full((1, D_INNER)),
            full((LANES, D_INNER)),
        ],
        out_specs=[
            pl.BlockSpec((q, D_INNER), lambda b, c: (row(b, c), 0)),
            pl.BlockSpec((1, D_INNER, SSM_STATE), lambda b, c: (b, 0, 0)),
        ],
        out_shape=[
            jax.ShapeDtypeStruct((t, D_INNER), BF16),
            jax.ShapeDtypeStruct((batch, D_INNER, SSM_STATE), F32),
        ],
        scratch_shapes=[
            pltpu.VMEM((SUBLANES, CONV_DIM), F32),
            pltpu.VMEM((q, CONV_DIM), F32),
            pltpu.VMEM((SSM_GROUPS, SSM_STATE, D_INNER // SSM_GROUPS), F32),
        ],
        compiler_params=pltpu.CompilerParams(
            dimension_semantics=("parallel", "arbitrary"), vmem_limit_bytes=VMEM_LIMIT),
        name="ssm_chunked",
    )(proj, proj, proj, proj, dt_raw, conv_buf, h0, sp["w_conv"], sp["b_conv"], sp["dt_bias"],
      sp["a_log"], sp["d_skip"], sp["w_norm"], sp["expand"])


def _ssm_step_kernel(z_ref, xs_ref, bm_ref, cm_ref, dt_ref, cbuf_ref, h0_ref, wconv_ref, bconv_ref,
                     dtb_ref, alog_ref, dskip_ref, wnorm_ref, expand_ref, gsum_ref,
                     y_ref, hout_ref, tail_sc):
    q = xs_ref.shape[0]
    gw = D_INNER // SSM_GROUPS
    n = SSM_STATE
    tail_sc[...] = jnp.zeros_like(tail_sc)
    tail_sc[SUBLANES - (CONV_WIDTH - 1):SUBLANES, :] = cbuf_ref[0]
    xs = _conv_silu(xs_ref[...], tail_sc[:, 0:D_INNER], wconv_ref[:, 0:D_INNER], bconv_ref[:, 0:D_INNER])
    o_b, o_c = D_INNER, D_INNER + SSM_GROUPS * n
    bm = _conv_silu(bm_ref[...], tail_sc[:, o_b:o_c], wconv_ref[:, o_b:o_c], bconv_ref[:, o_b:o_c])
    cm = _conv_silu(cm_ref[...], tail_sc[:, o_c:], wconv_ref[:, o_c:], bconv_ref[:, o_c:])
    bm = bm.astype(BF16).astype(F32)
    cm = cm.astype(BF16).astype(F32)

    dt = _softplus(dt_ref[...] + dtb_ref[...])
    a = -jnp.exp(alog_ref[...])
    tri = (_iota((q, q), 0) >= _iota((q, q), 1)).astype(F32)
    d1, d2, d3 = _split3(dt * a)
    acs = _dot(tri, d1.astype(F32)) + _dot(tri, d2.astype(F32)) + _dot(tri, d3.astype(F32))
    a_end = acs[q - 1:q, :]
    ex = expand_ref[...]
    ex_f = ex.astype(F32)
    dt_x = _dot3f(dt, ex_f)
    e_in_x = _dot3f(jnp.exp(acs), ex_f)
    e_out_x = _dot3f(jnp.exp(a_end - acs), ex_f)
    e_all_x = _dot3f(jnp.broadcast_to(jnp.exp(a_end), (q, LANES)), ex_f)

    row = _iota((q, LANES), 0)
    prods, lms = [], []
    for s in range(q):
        prods.append(cm * bm[s:s + 1, :])
        keep = row >= s
        lms.append(jnp.where(keep, jnp.exp(jnp.where(keep, acs - acs[s:s + 1, :], 0.0)), 0.0))
    cbs = _dot3(jnp.concatenate(prods, axis=0), gsum_ref[...])
    g_x = _dot((cbs * jnp.concatenate(lms, axis=0)).astype(BF16), ex)
    xdt = xs * dt_x
    xdt_r = xdt.astype(BF16).astype(F32)
    y = dskip_ref[...] * xs
    for s in range(q):
        y = y + g_x[s * q:(s + 1) * q, :] * xdt_r[s:s + 1, :]
    xdte = xdt * e_out_x

    eye = (_iota((gw, gw), 0) == _iota((gw, gw), 1)).astype(F32)
    for g in range(SSM_GROUPS):
        gs = slice(g * gw, (g + 1) * gw)
        h0 = h0_ref[0, gs, :]
        bg = bm[:, g * n:(g + 1) * n]
        cg = cm[:, g * n:(g + 1) * n]
        yg = y[:, gs] + _dot_nt(cg, h0) * e_in_x[:, gs]
        u = yg * jax.nn.silu(z_ref[:, gs])
        u = u * lax.rsqrt(jnp.mean(u * u, axis=-1, keepdims=True) + EPS)
        y_ref[:, gs] = u * wnorm_ref[:, gs]
        e1, e2, e3 = _split3(e_all_x[:, gs])
        dec = (_dot_nt(eye, e1.astype(F32)) + _dot_nt(eye, e2.astype(F32))
               + _dot_nt(eye, e3.astype(F32)))[:, 0:1]
        xdte_t = _dot_nt(eye, xdte[:, gs].astype(BF16).astype(F32))
        hout_ref[0, gs, :] = dec * h0 + _dot(xdte_t, bg)


def _ssm_step(proj, dt_raw, conv_buf, h0, sp, batch, q):
    t = proj.shape[0]
    assert t == batch * q and q == SUBLANES
    col = lambda off, w: off // w
    full = lambda shape: pl.BlockSpec(shape, lambda b: (0,) * len(shape))
    gsum = ((_iota((SSM_GROUPS * SSM_STATE, LANES), 0) // SSM_STATE
             == _iota((SSM_GROUPS * SSM_STATE, LANES), 1) // (SSM_HEADS // SSM_GROUPS))
            & (_iota((SSM_GROUPS * SSM_STATE, LANES), 1) < SSM_HEADS)).astype(BF16)
    return pl.pallas_call(
        _ssm_step_kernel,
        grid=(batch,),
        in_specs=[
            pl.BlockSpec((q, D_INNER), lambda b: (b, col(OFF_Z, D_INNER))),
            pl.BlockSpec((q, D_INNER), lambda b: (b, col(OFF_X, D_INNER))),
            pl.BlockSpec((q, 1024), lambda b: (b, col(OFF_B, 1024))),
            pl.BlockSpec((q, 1024), lambda b: (b, col(OFF_C, 1024))),
            pl.BlockSpec((q, LANES), lambda b: (b, 0)),
            pl.BlockSpec((1, CONV_WIDTH - 1, CONV_DIM), lambda b: (b, 0, 0)),
            pl.BlockSpec((1, D_INNER, SSM_STATE), lambda b: (b, 0, 0)),
            full((CONV_WIDTH, CONV_DIM)),
            full((1, CONV_DIM)),
            full((1, LANES)),
            full((1, LANES)),
            full((1, D_INNER)),
            full((1, D_INNER)),
            full((LANES, D_INNER)),
            full((SSM_GROUPS * SSM_STATE, LANES)),
        ],
        out_specs=[
            pl.BlockSpec((q, D_INNER), lambda b: (b, 0)),
            pl.BlockSpec((1, D_INNER, SSM_STATE), lambda b: (b, 0, 0)),
        ],
        out_shape=[
            jax.ShapeDtypeStruct((t, D_INNER), F32),
            jax.ShapeDtypeStruct((batch, D_INNER, SSM_STATE), F32),
        ],
        scratch_shapes=[pltpu.VMEM((SUBLANES, CONV_DIM), F32)],
        compiler_params=pltpu.CompilerParams(
            dimension_semantics=("parallel",), vmem_limit_bytes=VMEM_LIMIT),
        name="ssm_step",
    )(proj, proj, proj, proj, dt_raw, conv_buf, h0, sp["w_conv"], sp["b_conv"], sp["dt_bias"],
      sp["a_log"], sp["d_skip"], sp["w_norm"], sp["expand"], gsum)


def _prep_params(w_in, norm_mix, w_conv, b_conv, dt_bias, a_log, d_skip, w_ssm_norm, w_q_norm, w_k_norm):
    o_z, o_xbc = 0, D_INNER
    o_dt = o_xbc + CONV_DIM
    o_q = o_dt + SSM_HEADS
    o_k = o_q + ATTN_WIDTH
    o_v = o_k + ATTN_WIDTH
    o_g = o_v + ATTN_WIDTH
    w_main = jnp.concatenate(
        [w_in[:, o_z:o_dt], w_in[:, o_g:], w_in[:, o_q:o_g]], axis=1).astype(BF16)
    w_dt = jnp.pad(w_in[:, o_dt:o_q], ((0, 0), (0, LANES - SSM_HEADS))).astype(BF16)
    qk_w = jnp.ones((N_PROJ_TILES, 1, PROJ_TILE), F32)
    qk_w = qk_w.at[Q_TILE0:K_TILE0, 0].set(jnp.tile(w_q_norm, HEADS_PER_GROUP)[None])
    qk_w = qk_w.at[K_TILE0:V_TILE0, 0].set(jnp.tile(w_k_norm, HEADS_PER_GROUP)[None])
    pad_h = lambda v: jnp.pad(v, (0, LANES - SSM_HEADS))[None]
    expand = (_iota((LANES, D_INNER), 0) == _iota((LANES, D_INNER), 1) // SSM_HEAD_DIM).astype(BF16)
    ssm = dict(w_conv=w_conv, b_conv=b_conv[None], dt_bias=pad_h(dt_bias), a_log=pad_h(a_log),
               d_skip=jnp.repeat(d_skip, SSM_HEAD_DIM)[None], w_norm=w_ssm_norm[None], expand=expand)
    return dict(w_main=w_main, w_dt=w_dt, qk_w=qk_w, norm_mix=norm_mix[None], ssm=ssm)


def _attn_prompt_kernel(q_ref, kp_ref, kc_ref, vp_ref, vc_ref, o_ref, lse_ref, *, span):
    nblk = pl.program_id(1)
    blk = q_ref.shape[0]
    qi = _iota((blk, 2 * blk), 0)
    kc = _iota((blk, 2 * blk), 1)
    dist = blk + qi - kc
    mask = (dist >= 0) & (dist <= span) & ((kc >= blk) | (nblk > 0))
    for h in range(HEADS_PER_GROUP):
        hs = slice(h * ATTN_HEAD_DIM, (h + 1) * ATTN_HEAD_DIM)
        qh = q_ref[:, hs].astype(BF16)
        kh = jnp.concatenate([kp_ref[:, hs], kc_ref[:, hs]], axis=0).astype(BF16)
        vh = jnp.concatenate([vp_ref[:, hs], vc_ref[:, hs]], axis=0).astype(BF16)
        s = jnp.where(mask, _dot_nt(qh, kh), -jnp.inf)
        m = jnp.max(s, axis=-1, keepdims=True)
        p = jnp.exp(s - m)
        den = jnp.sum(p, axis=-1, keepdims=True)
        o_ref[:, hs] = _dot((p / den).astype(BF16), vh)
        lse_ref[:, hs] = jnp.broadcast_to(m + jnp.log(den), (blk, ATTN_HEAD_DIM))


def _attn_prompt(proj, gi):
    window, dil = ATTN_PATTERNS[gi]
    s = proj.shape[0]
    n = s // dil
    assert s % dil == 0 and n % ATTN_BLOCK == 0 and window // dil <= ATTN_BLOCK
    nb = n // ATTN_BLOCK
    view = proj.reshape(n, dil * PROJ_WIDTH)
    col = lambda tile0: (lambda r, i: (i, r * N_PROJ_TILES + tile0 + gi))
    colp = lambda tile0: (lambda r, i: (jnp.maximum(i - 1, 0), r * N_PROJ_TILES + tile0 + gi))
    blk = (ATTN_BLOCK, ATTN_GROUP_WIDTH)
    o, lse = pl.pallas_call(
        functools.partial(_attn_prompt_kernel, span=window // dil),
        grid=(dil, nb),
        in_specs=[
            pl.BlockSpec(blk, col(Q_TILE0)),
            pl.BlockSpec(blk, colp(K_TILE0)),
            pl.BlockSpec(blk, col(K_TILE0)),
            pl.BlockSpec(blk, colp(V_TILE0)),
            pl.BlockSpec(blk, col(V_TILE0)),
        ],
        out_specs=[pl.BlockSpec(blk, lambda r, i: (i, r))] * 2,
        out_shape=[jax.ShapeDtypeStruct((n, dil * ATTN_GROUP_WIDTH), F32)] * 2,
        compiler_params=pltpu.CompilerParams(
            dimension_semantics=("parallel", "arbitrary"), vmem_limit_bytes=VMEM_LIMIT),
        name=f"attn_prompt_g{gi}",
    )(view, view, view, view, view)
    return o.reshape(s, ATTN_GROUP_WIDTH), lse.reshape(s, ATTN_GROUP_WIDTH)


def _attn_sample_kernel(q_ref, k_ref, v_ref, kc_ref, vc_ref, o_ref, lse_ref, ko_ref, vo_ref,
                        *, span, dil, pos0):
    l = q_ref.shape[0]
    rows = kc_ref.shape[1]
    w = ATTN_GROUP_WIDTH
    hl = HEADS_PER_GROUP * l
    pad = jnp.zeros((LANES - l, w), F32)
    k_all = jnp.concatenate([kc_ref[0], k_ref[...], pad], axis=0).astype(BF16)
    v_all = jnp.concatenate([vc_ref[0], v_ref[...], pad], axis=0).astype(BF16)
    diag = _shr(_iota((hl, w), 0), l) == _shr(_iota((hl, w), 1), ATTN_HEAD_DIM)
    q_bd = jnp.where(diag, jnp.concatenate([q_ref[...]] * HEADS_PER_GROUP, axis=0), 0.0).astype(BF16)
    s = _dot_nt(k_all, q_bd)
    i = _iota(s.shape, 0)
    t = _iota(s.shape, 1) & (l - 1)
    delta = rows + t - i
    valid = ((delta >= 0) & (delta <= span * dil) & ((delta & (dil - 1)) == 0) & (pos0 + t - delta >= 0))
    s = jnp.where(valid, s, -jnp.inf)
    m = jnp.max(s, axis=0, keepdims=True)
    p = jnp.exp(s - m)
    den = jnp.sum(p, axis=0, keepdims=True)
    res = _dot_tn((p / den).astype(BF16), v_all)
    res = jnp.where(diag, res, 0.0)
    o = res[0:l]
    for h in range(1, HEADS_PER_GROUP):
        o = o + res[h * l:(h + 1) * l]
    o_ref[...] = o
    lse = m + jnp.log(den)
    sel = jnp.where((_iota((l, hl), 1) & (l - 1)) == _iota((l, hl), 0), jnp.broadcast_to(lse, (l, hl)), 0.0)
    lse_ref[...] = _dot3f(sel, diag.astype(F32))
    ko_ref[0, 0:rows - l, :] = kc_ref[0, l:rows, :]
    ko_ref[0, rows - l:rows, :] = k_ref[...]
    vo_ref[0, 0:rows - l, :] = vc_ref[0, l:rows, :]
    vo_ref[0, rows - l:rows, :] = v_ref[...]


def _attn_sample(proj, k_cache, v_cache, gi, batch, pos0):
    window, dil = ATTN_PATTERNS[gi]
    rows = k_cache.shape[1]
    l = proj.shape[0] // batch
    assert l == SUBLANES and rows % LANES == 0 and rows >= l
    tok = lambda tile0: pl.BlockSpec((l, ATTN_GROUP_WIDTH), lambda b: (b, tile0 + gi))
    cache = pl.BlockSpec((1, rows, ATTN_GROUP_WIDTH), lambda b: (b, 0, 0))
    out_tok = pl.BlockSpec((l, ATTN_GROUP_WIDTH), lambda b: (b, 0))
    return pl.pallas_call(
        functools.partial(_attn_sample_kernel, span=window // dil, dil=dil, pos0=pos0),
        grid=(batch,),
        in_specs=[tok(Q_TILE0), tok(K_TILE0), tok(V_TILE0), cache, cache],
        out_specs=[out_tok, out_tok, cache, cache],
        out_shape=[jax.ShapeDtypeStruct((batch * l, ATTN_GROUP_WIDTH), F32)] * 2
        + [jax.ShapeDtypeStruct(k_cache.shape, F32)] * 2,
        compiler_params=pltpu.CompilerParams(
            dimension_semantics=("parallel",), vmem_limit_bytes=VMEM_LIMIT),
        name=f"attn_sample_g{gi}",
    )(proj, proj, proj, k_cache, v_cache)


def _merge_kernel(x_ref, ys_ref, o0_ref, o1_ref, o2_ref, l0_ref, l1_ref, l2_ref, g0_ref, g1_ref,
                  wbs_ref, wba_ref, wo_ref, nf_ref, wr1_ref, wr2_ref, br_ref,
                  x1_ref, xn_ref, lg_ref):
    l0, l1, l2 = l0_ref[...], l1_ref[...], l2_ref[...]
    m = jnp.maximum(jnp.maximum(l0, l1), l2)
    e0, e1, e2 = jnp.exp(l0 - m), jnp.exp(l1 - m), jnp.exp(l2 - m)
    den = e0 + e1 + e2
    y_attn = (e0 / den) * o0_ref[...] + (e1 / den) * o1_ref[...] + (e2 / den) * o2_ref[...]
    br_s = _dot(ys_ref[...].astype(BF16), wbs_ref[...])
    br_a = _dot(y_attn.astype(BF16), wba_ref[...])
    merged = jax.nn.sigmoid(g0_ref[...]) * br_s + jax.nn.sigmoid(g1_ref[...]) * br_a
    x1 = x_ref[...] + _dot(merged.astype(BF16), wo_ref[...])
    x1_ref[...] = x1
    xn = x1 * lax.rsqrt(jnp.mean(x1 * x1, axis=-1, keepdims=True) + EPS) * nf_ref[...]
    xn1 = xn.astype(BF16)
    xn_ref[...] = xn1
    xn2 = (xn - xn1.astype(F32)).astype(BF16)
    lg_ref[...] = (_dot(xn1, wr1_ref[...]) + _dot(xn2, wr1_ref[...]) + _dot(xn1, wr2_ref[...])
                   + br_ref[...])


def _merge(x, proj, y_ssm, attn, mp, tm):
    t = x.shape[0]
    row = lambda w: pl.BlockSpec((tm, w), lambda i: (i, 0))
    full = lambda a: pl.BlockSpec(a.shape, lambda i: (0,) * a.ndim)
    (o0, l0), (o1, l1), (o2, l2) = attn
    weights = [mp["w_br_ssm"], mp["w_br_attn"], mp["w_out"], mp["norm_ffn"], mp["w_r1"], mp["w_r2"], mp["b_r"]]
    return pl.pallas_call(
        _merge_kernel,
        grid=(t // tm,),
        in_specs=[row(D_MODEL), row(D_INNER)] + [row(ATTN_GROUP_WIDTH)] * 6 + [
            pl.BlockSpec((tm, D_MODEL), lambda i: (i, OFF_G // D_MODEL)),
            pl.BlockSpec((tm, D_MODEL), lambda i: (i, OFF_G // D_MODEL + 1)),
        ] + [full(w) for w in weights],
        out_specs=[row(D_MODEL), row(D_MODEL), row(ROUTER_WIDTH)],
        out_shape=[
            jax.ShapeDtypeStruct((t, D_MODEL), F32),
            jax.ShapeDtypeStruct((t, D_MODEL), BF16),
            jax.ShapeDtypeStruct((t, ROUTER_WIDTH), F32),
        ],
        compiler_params=pltpu.CompilerParams(
            dimension_semantics=("parallel",), vmem_limit_bytes=VMEM_LIMIT),
        name="merge",
    )(x, y_ssm, o0, o1, o2, l0, l1, l2, proj, proj, *weights)


def _prep_merge_params(w_br_ssm, w_br_attn, w_out, norm_ffn, w_router_group, b_router_group,
                       w_router_expert, b_router_expert):
    w_r = jnp.concatenate([w_router_group, w_router_expert.reshape(D_MODEL, N_EXPERTS)], axis=1)
    w_r = jnp.pad(w_r, ((0, 0), (0, ROUTER_WIDTH - w_r.shape[1])))
    w_r1 = w_r.astype(BF16)
    w_r2 = (w_r - w_r1.astype(F32)).astype(BF16)
    b_r = jnp.concatenate([b_router_group, b_router_expert.reshape(N_EXPERTS)])
    b_r = jnp.pad(b_r, (0, ROUTER_WIDTH - b_r.shape[0]))[None]
    return dict(w_br_ssm=w_br_ssm.astype(BF16), w_br_attn=w_br_attn.astype(BF16), w_out=w_out.astype(BF16),
                norm_ffn=norm_ffn[None], w_r1=w_r1, w_r2=w_r2, b_r=b_r)


def _route(logits):
    lane = _iota(logits.shape, 1)
    big = jnp.int32(2 ** 30)
    neg = -jnp.inf
    gl = jnp.where(lane < N_EXPERT_GROUPS, logits, neg)
    gmax = jnp.max(gl, axis=-1, keepdims=True)
    gidx = jnp.min(jnp.where(gl == gmax, lane, big), axis=-1, keepdims=True)
    g_w = 1.0 / jnp.sum(jnp.exp(gl - gmax), axis=-1, keepdims=True)
    lo = N_EXPERT_GROUPS + gidx * EXPERTS_PER_GROUP
    el = jnp.where((lane >= lo) & (lane < lo + EXPERTS_PER_GROUP), logits, neg)
    v1 = jnp.max(el, axis=-1, keepdims=True)
    i1 = jnp.min(jnp.where(el == v1, lane, big), axis=-1, keepdims=True)
    el2 = jnp.where(lane == i1, neg, el)
    v2 = jnp.max(el2, axis=-1, keepdims=True)
    i2 = jnp.min(jnp.where(el2 == v2, lane, big), axis=-1, keepdims=True)
    e2 = jnp.exp(v2 - v1)
    w1 = (1.0 / (1.0 + e2)) * g_w
    w2 = (e2 / (1.0 + e2)) * g_w
    return jnp.where(lane == i1, w1, jnp.where(lane == i2, w2, 0.0))


def _moe_dense_kernel(x1_ref, xn_ref, lg_ref, wg_ref, wu_ref, wd_ref, out_ref, comb_sc, acc_sc):
    e = pl.program_id(1)

    @pl.when(e == 0)
    def _():
        comb_sc[...] = _route(lg_ref[...])
        acc_sc[...] = jnp.zeros_like(acc_sc)

    lane = _iota(comb_sc.shape, 1)
    cw = jnp.sum(jnp.where(lane == e + N_EXPERT_GROUPS, comb_sc[...], 0.0), axis=-1, keepdims=True)
    xn = xn_ref[...]
    hidden = jax.nn.silu(_dot(xn, wg_ref[0])) * _dot(xn, wu_ref[0]) * cw
    acc_sc[...] += _dot(hidden.astype(BF16), wd_ref[0])

    @pl.when(e == pl.num_programs(1) - 1)
    def _():
        out_ref[...] = x1_ref[...] + acc_sc[...]


def _moe_dense(x1, xn, logits, wg, wu, wd, tm):
    t = x1.shape[0]
    return pl.pallas_call(
        _moe_dense_kernel,
        grid=(t // tm, N_EXPERTS),
        in_specs=[
            pl.BlockSpec((tm, D_MODEL), lambda i, e: (i, 0)),
            pl.BlockSpec((tm, D_MODEL), lambda i, e: (i, 0)),
            pl.BlockSpec((tm, ROUTER_WIDTH), lambda i, e: (i, 0)),
            pl.BlockSpec((1, D_MODEL, D_EXPERT), lambda i, e: (e, 0, 0)),
            pl.BlockSpec((1, D_MODEL, D_EXPERT), lambda i, e: (e, 0, 0)),
            pl.BlockSpec((1, D_EXPERT, D_MODEL), lambda i, e: (e, 0, 0)),
        ],
        out_specs=pl.BlockSpec((tm, D_MODEL), lambda i, e: (i, 0)),
        out_shape=jax.ShapeDtypeStruct((t, D_MODEL), F32),
        scratch_shapes=[pltpu.VMEM((tm, ROUTER_WIDTH), F32), pltpu.VMEM((tm, D_MODEL), F32)],
        compiler_params=pltpu.CompilerParams(
            dimension_semantics=("parallel", "arbitrary"), vmem_limit_bytes=VMEM_LIMIT),
        name="moe_dense",
    )(x1, xn, logits, wg, wu, wd)


def _tile_rows(t, cap):
    tm = min(t, cap)
    assert t % tm == 0
    return tm


def _decoder_layer(x, conv_buf, ssm_state, kv_bufs, pos0, pp, mp, ep):
    b, l, d = x.shape
    t = b * l
    x2 = x.reshape(t, d)
    proj, dt_raw = _in_proj(x2, pp["norm_mix"], pp["w_main"], pp["w_dt"], pp["qk_w"], _tile_rows(t, 1024))
    proj3 = proj.reshape(b, l, PROJ_WIDTH)
    h0 = ssm_state.reshape(b, D_INNER, SSM_STATE)
    if kv_bufs is None:
        assert b == 1 and l % SSD_CHUNK == 0
        y_ssm, h_new = _ssm_chunked(proj, dt_raw, conv_buf, h0, pp["ssm"], b, SSD_CHUNK)
        attn = [_attn_prompt(proj, gi) for gi in range(N_ATTN_GROUPS)]
        new_kv = []
        for gi, (window, _) in enumerate(ATTN_PATTERNS):
            keep = min(window, l)
            for off in (OFF_K, OFF_V):
                c0 = off + gi * ATTN_GROUP_WIDTH
                new_kv.append(proj3[:, l - keep:, c0:c0 + ATTN_GROUP_WIDTH]
                              .reshape(b, keep, HEADS_PER_GROUP, ATTN_HEAD_DIM))
    else:
        y_ssm, h_new = _ssm_step(proj, dt_raw, conv_buf, h0, pp["ssm"], b, l)
        attn, new_kv = [], []
        for gi in range(N_ATTN_GROUPS):
            kc, vc = kv_bufs[2 * gi], kv_bufs[2 * gi + 1]
            rows = kc.shape[1]
            o, lse, kn, vn = _attn_sample(proj, kc.reshape(b, rows, ATTN_GROUP_WIDTH),
                                          vc.reshape(b, rows, ATTN_GROUP_WIDTH), gi, b, pos0)
            attn.append((o, lse))
            new_kv += [kn.reshape(kc.shape), vn.reshape(vc.shape)]
    new_conv = proj3[:, l - (CONV_WIDTH - 1):, OFF_X:OFF_X + CONV_DIM]
    tm = _tile_rows(t, 512)
    x1, xn, logits = _merge(x2, proj, y_ssm, attn, mp, tm)
    y = _moe_dense(x1, xn, logits, ep["wg"], ep["wu"], ep["wd"], tm)
    new_ssm = h_new.reshape(b, SSM_HEADS, SSM_HEAD_DIM, SSM_STATE)
    return y.reshape(b, l, d), (new_conv, new_ssm) + tuple(new_kv)


def kernel(x_prompt, x_sample, cache_conv, state_ssm, cache_k_w128, cache_v_w128, cache_k_w512, cache_v_w512,
           cache_k_w2048, cache_v_w2048, w_in, norm_mix, w_conv, b_conv, dt_bias, a_log, d_skip, w_ssm_norm,
           w_q_norm, w_k_norm, w_br_ssm, w_br_attn, w_out, norm_ffn, w_router_group, b_router_group,
           w_router_expert, b_router_expert, w_expert_gate, w_expert_up, w_expert_down):
    depth = w_in.shape[0]
    yp, ys = x_prompt, x_sample
    bp = x_prompt.shape[0]
    prompt_states, sample_states = [], []
    for i in range(depth):
        pp = _prep_params(w_in[i], norm_mix[i], w_conv[i], b_conv[i], dt_bias[i], a_log[i], d_skip[i],
                          w_ssm_norm[i], w_q_norm[i], w_k_norm[i])
        mp = _prep_merge_params(w_br_ssm[i], w_br_attn[i], w_out[i], norm_ffn[i], w_router_group[i],
                                b_router_group[i], w_router_expert[i], b_router_expert[i])
        ep = dict(wg=w_expert_gate[i].astype(BF16), wu=w_expert_up[i].astype(BF16),
                  wd=w_expert_down[i].astype(BF16))
        zero_conv = jnp.zeros((bp, CONV_WIDTH - 1, CONV_DIM), x_prompt.dtype)
        zero_ssm = jnp.zeros((bp, SSM_HEADS, SSM_HEAD_DIM, SSM_STATE), F32)
        yp, st_p = _decoder_layer(yp, zero_conv, zero_ssm, None, 0, pp, mp, ep)
        kv_bufs = (cache_k_w128[i], cache_v_w128[i], cache_k_w512[i], cache_v_w512[i],
                   cache_k_w2048[i], cache_v_w2048[i])
        ys, st_s = _decoder_layer(ys, cache_conv[i], state_ssm[i], kv_bufs, PAST_LEN, pp, mp, ep)
        prompt_states.append(st_p)
        sample_states.append(st_s)
    stack = lambda states: [jnp.stack([s[j] for s in states]) for j in range(8)]
    return (yp, ys) + tuple(stack(prompt_states)) + tuple(stack(sample_states))
```

```python
import functools
import math

import jax
import jax.numpy as jnp
from jax import lax
from jax.experimental import pallas as pl
from jax.experimental.pallas import tpu as pltpu

F32 = jnp.float32
BF16 = jnp.bfloat16

PAST_LEN = 8192
D_MODEL = 1024
D_INNER = 2048
SSM_HEAD_DIM = 64
SSM_HEADS = 32
SSM_GROUPS = 8
SSM_STATE = 128
CONV_WIDTH = 4
CONV_DIM = D_INNER + 2 * SSM_GROUPS * SSM_STATE
SSD_CHUNK = 128
ATTN_HEAD_DIM = 64
HEADS_PER_GROUP = 8
ATTN_PATTERNS = ((128, 1), (512, 4), (2048, 16))
N_ATTN_GROUPS = 3
ATTN_GROUP_WIDTH = HEADS_PER_GROUP * ATTN_HEAD_DIM
ATTN_WIDTH = N_ATTN_GROUPS * ATTN_GROUP_WIDTH
ATTN_BLOCK = 128
N_EXPERT_GROUPS = 4
EXPERTS_PER_GROUP = 8
N_EXPERTS = 32
D_EXPERT = 256
EPS = 1e-6

LANES = 128
SUBLANES = 8
VMEM_LIMIT = 56 * 1024 * 1024

PROJ_TILE = 512
OFF_Z = 0
OFF_X = 2048
OFF_B = 4096
OFF_C = 5120
OFF_G = 6144
OFF_Q = 8192
OFF_K = OFF_Q + ATTN_WIDTH
OFF_V = OFF_K + ATTN_WIDTH
PROJ_WIDTH = OFF_V + ATTN_WIDTH
N_PROJ_TILES = PROJ_WIDTH // PROJ_TILE
Q_TILE0 = OFF_Q // PROJ_TILE
K_TILE0 = OFF_K // PROJ_TILE
V_TILE0 = OFF_V // PROJ_TILE
ROUTER_WIDTH = LANES


def _split3(x):
    x1 = x.astype(BF16)
    r1 = x - x1.astype(F32)
    x2 = r1.astype(BF16)
    r2 = r1 - x2.astype(F32)
    return x1, x2, r2.astype(BF16)


def _dot(a, b):
    return jnp.dot(a, b, preferred_element_type=F32)


def _dot3(x, m):
    x1, x2, x3 = _split3(x)
    return _dot(x1, m) + _dot(x2, m) + _dot(x3, m)


def _dot3f(x, m):
    x1, x2, x3 = _split3(x)
    return _dot(x1.astype(F32), m) + _dot(x2.astype(F32), m) + _dot(x3.astype(F32), m)


def _dot_nt(a, b):
    return lax.dot_general(a, b, (((1,), (1,)), ((), ())), preferred_element_type=F32)


def _dot_tn(a, b):
    return lax.dot_general(a, b, (((0,), (0,)), ((), ())), preferred_element_type=F32)


def _iota(shape, dim):
    return lax.broadcasted_iota(jnp.int32, shape, dim)


def _shr(x, pow2):
    assert pow2 & (pow2 - 1) == 0
    return lax.shift_right_logical(x, jnp.int32(int(math.log2(pow2))))


def _in_proj_kernel(x_ref, nw_ref, w_ref, wdt_ref, qkw_ref, seg_ref, out_ref, dt_ref, h_sc):
    j = pl.program_id(1)

    @pl.when(j == 0)
    def _():
        xf = x_ref[...]
        y = xf * lax.rsqrt(jnp.mean(xf * xf, axis=-1, keepdims=True) + EPS)
        h = (y * nw_ref[...]).astype(BF16)
        h_sc[...] = h
        dt_ref[...] = _dot(h, wdt_ref[...])

    acc = _dot(h_sc[...], w_ref[...])
    is_qk = jnp.logical_and(j >= Q_TILE0, j < V_TILE0)

    @pl.when(jnp.logical_not(is_qk))
    def _():
        out_ref[...] = acc

    @pl.when(is_qk)
    def _():
        sq = acc * acc
        s1 = sq.astype(BF16)
        s2 = (sq - s1.astype(F32)).astype(BF16)
        ms = (_dot(s1, seg_ref[...]) + _dot(s2, seg_ref[...])) * (1.0 / ATTN_HEAD_DIM)
        scale = jnp.where(j < K_TILE0, ATTN_HEAD_DIM ** -0.5, 1.0).astype(F32)
        w = qkw_ref[0] * scale
        out_ref[...] = acc * lax.rsqrt(ms + EPS) * w


def _in_proj(x, norm_w, w_bf, wdt_bf, qk_w, tm):
    t = x.shape[0]
    seg = (_iota((PROJ_TILE, PROJ_TILE), 0) // ATTN_HEAD_DIM
           == _iota((PROJ_TILE, PROJ_TILE), 1) // ATTN_HEAD_DIM).astype(BF16)
    return pl.pallas_call(
        _in_proj_kernel,
        grid=(t // tm, N_PROJ_TILES),
        in_specs=[
            pl.BlockSpec((tm, D_MODEL), lambda i, j: (i, 0)),
            pl.BlockSpec((1, D_MODEL), lambda i, j: (0, 0)),
            pl.BlockSpec((D_MODEL, PROJ_TILE), lambda i, j: (0, j)),
            pl.BlockSpec((D_MODEL, LANES), lambda i, j: (0, 0)),
            pl.BlockSpec((1, 1, PROJ_TILE), lambda i, j: (j, 0, 0)),
            pl.BlockSpec((PROJ_TILE, PROJ_TILE), lambda i, j: (0, 0)),
        ],
        out_specs=[
            pl.BlockSpec((tm, PROJ_TILE), lambda i, j: (i, j)),
            pl.BlockSpec((tm, LANES), lambda i, j: (i, 0)),
        ],
        out_shape=[
            jax.ShapeDtypeStruct((t, PROJ_WIDTH), F32),
            jax.ShapeDtypeStruct((t, LANES), F32),
        ],
        scratch_shapes=[pltpu.VMEM((tm, D_MODEL), BF16)],
        compiler_params=pltpu.CompilerParams(
            dimension_semantics=("parallel", "arbitrary"), vmem_limit_bytes=VMEM_LIMIT),
        name="in_proj",
    )(x, norm_w, w_bf, wdt_bf, qk_w, seg)


def _softplus(x):
    return jnp.maximum(x, 0.0) + jnp.log1p(jnp.exp(-jnp.abs(x)))


def _conv_silu(raw, tail, w, b):
    q = raw.shape[0]
    up = jnp.concatenate([tail, raw], axis=0)
    acc = b + w[CONV_WIDTH - 1:CONV_WIDTH] * raw
    for tap in range(CONV_WIDTH - 1):
        shifted = pltpu.roll(up, CONV_WIDTH - 1 - tap, axis=0)[SUBLANES:SUBLANES + q]
        acc = acc + w[tap:tap + 1] * shifted
    return jax.nn.silu(acc)


def _ssm_chunk_kernel(z_ref, xs_ref, bm_ref, cm_ref, dt_ref, cbuf_ref, h0_ref, wconv_ref, bconv_ref,
                      dtb_ref, alog_ref, dskip_ref, wnorm_ref, expand_ref,
                      y_ref, hout_ref, tail_sc, xbc_sc, ht_sc):
    c = pl.program_id(1)
    nc = pl.num_programs(1)
    q = xs_ref.shape[0]
    gw = D_INNER // SSM_GROUPS
    n = SSM_STATE

    @pl.when(c == 0)
    def _():
        tail_sc[...] = jnp.zeros_like(tail_sc)
        tail_sc[SUBLANES - (CONV_WIDTH - 1):SUBLANES, :] = cbuf_ref[0]
        for g in range(SSM_GROUPS):
            ht_sc[g] = h0_ref[0, g * gw:(g + 1) * gw, :].T

    cw = PROJ_TILE
    for k in range(CONV_DIM // cw):
        if k < D_INNER // cw:
            raw = xs_ref[:, k * cw:(k + 1) * cw]
        elif k < (D_INNER + SSM_GROUPS * n) // cw:
            kk = k - D_INNER // cw
            raw = bm_ref[:, kk * cw:(kk + 1) * cw]
        else:
            kk = k - (D_INNER + SSM_GROUPS * n) // cw
            raw = cm_ref[:, kk * cw:(kk + 1) * cw]
        sl = slice(k * cw, (k + 1) * cw)
        xbc_sc[:, sl] = _conv_silu(raw, tail_sc[:, sl], wconv_ref[:, sl], bconv_ref[:, sl])
        tail_sc[:, sl] = raw[q - SUBLANES:q]

    dt = _softplus(dt_ref[...] + dtb_ref[...])
    a = -jnp.exp(alog_ref[...])
    dta = dt * a
    tri = (_iota((q, q), 0) >= _iota((q, q), 1)).astype(BF16)
    d1, d2, d3 = _split3(dta)
    acs = _dot(tri, d1) + _dot(tri, d2) + _dot(tri, d3)
    eye = (_iota((LANES, LANES), 0) == _iota((LANES, LANES), 1)).astype(BF16)
    c1, c2, c3 = _split3(acs)
    acs_t = _dot_nt(eye, c1) + _dot_nt(eye, c2) + _dot_nt(eye, c3)
    a_end = acs[q - 1:q, :]
    e_in = jnp.exp(acs)
    e_out = jnp.exp(a_end - acs)
    e_all = jnp.exp(a_end)

    ex = expand_ref[...]
    dt_x = _dot3(dt, ex)
    e_in_x = _dot3(e_in, ex)
    e_out_x = _dot3(e_out, ex)
    e_all_x = _dot3(jnp.broadcast_to(e_all, (2 * SUBLANES, LANES)), ex)[0:1]

    causal = _iota((q, q), 0) >= _iota((q, q), 1)
    lane_lo = _iota((q, LANES), 1) < SSM_HEAD_DIM

    for g in range(SSM_GROUPS):
        gs = slice(g * gw, (g + 1) * gw)
        xg = xbc_sc[:, gs]
        bg = xbc_sc[:, D_INNER + g * n:D_INNER + (g + 1) * n].astype(BF16)
        cg = xbc_sc[:, D_INNER + SSM_GROUPS * n + g * n:D_INNER + SSM_GROUPS * n + (g + 1) * n].astype(BF16)
        xdt = xg * dt_x[:, gs]
        cb = _dot_nt(cg, bg)
        ht = ht_sc[g]
        y_off = _dot(cg, ht.astype(BF16)) * e_in_x[:, gs]
        pieces = []
        for pair in range(2):
            ms = []
            for hh in range(2):
                h = g * 4 + pair * 2 + hh
                diff = acs[:, h:h + 1] - acs_t[h:h + 1, :]
                lmat = jnp.where(causal, jnp.exp(jnp.where(causal, diff, 0.0)), 0.0)
                ms.append((cb * lmat).astype(BF16))
            xp = xdt[:, pair * LANES:(pair + 1) * LANES]
            rhs = jnp.concatenate([jnp.where(lane_lo, xp, 0.0), jnp.where(lane_lo, 0.0, xp)],
                                  axis=0).astype(BF16)
            pieces.append(_dot(jnp.concatenate(ms, axis=1), rhs))
        y_diag = jnp.concatenate(pieces, axis=1)
        yg = y_diag + y_off + dskip_ref[:, gs] * xg
        zg = z_ref[:, gs]
        u = yg * jax.nn.silu(zg)
        u = u * lax.rsqrt(jnp.mean(u * u, axis=-1, keepdims=True) + EPS)
        y_ref[:, gs] = (u * wnorm_ref[:, gs]).astype(y_ref.dtype)
        st = _dot_tn(bg, (xdt * e_out_x[:, gs]).astype(BF16))
        ht_sc[g] = e_all_x[:, gs] * ht + st

    @pl.when(c == nc - 1)
    def _():
        for g in range(SSM_GROUPS):
            hout_ref[0, g * gw:(g + 1) * gw, :] = ht_sc[g].T


def _ssm_chunked(proj, dt_raw, conv_buf, h0, sp, batch, q):
    t = proj.shape[0]
    nc = t // batch // q
    row = lambda b, c: b * nc + c
    col = lambda off, w: off // w
    full = lambda shape: pl.BlockSpec(shape, lambda b, c: (0,) * len(shape))
    return pl.pallas_call(
        _ssm_chunk_kernel,
        grid=(batch, nc),
        in_specs=[
            pl.BlockSpec((q, D_INNER), lambda b, c: (row(b, c), col(OFF_Z, D_INNER))),
            pl.BlockSpec((q, D_INNER), lambda b, c: (row(b, c), col(OFF_X, D_INNER))),
            pl.BlockSpec((q, 1024), lambda b, c: (row(b, c), col(OFF_B, 1024))),
            pl.BlockSpec((q, 1024), lambda b, c: (row(b, c), col(OFF_C, 1024))),
            pl.BlockSpec((q, LANES), lambda b, c: (row(b, c), 0)),
            pl.BlockSpec((1, CONV_WIDTH - 1, CONV_DIM), lambda b, c: (b, 0, 0)),
            pl.BlockSpec((1, D_INNER, SSM_STATE), lambda b, c: (b, 0, 0)),
            full((CONV_WIDTH, CONV_DIM)),
            full((1, CONV_DIM)),
            full((1, LANES)),
            full((1, LANES)),
            full((1, D_INNER)),
            full((1, D_INNER)),
            full((LANES, D_INNER)),
        ],
        out_specs=[
            pl.BlockSpec((q, D_INNER), lambda b, c: (row(b, c), 0)),
            pl.BlockSpec((1, D_INNER, SSM_STATE), lambda b, c: (b, 0, 0)),
        ],
        out_shape=[
            jax.ShapeDtypeStruct((t, D_INNER), BF16),
            jax.ShapeDtypeStruct((batch, D_INNER, SSM_STATE), F32),
        ],
        scratch_shapes=[
            pltpu.VMEM((SUBLANES, CONV_DIM), F32),
            pltpu.VMEM((q, CONV_DIM), F32),
            pltpu.VMEM((SSM_GROUPS, SSM_STATE, D_INNER // SSM_GROUPS), F32),
        ],
        compiler_params=pltpu.CompilerParams(
            dimension_semantics=("parallel", "arbitrary"), vmem_limit_bytes=VMEM_LIMIT),
        name="ssm_chunked",
    )(proj, proj, proj, proj, dt_raw, conv_buf, h0, sp["w_conv"], sp["b_conv"], sp["dt_bias"],
      sp["a_log"], sp["d_skip"], sp["w_norm"], sp["expand"])


def _ssm_step_kernel(z_ref, xs_ref, bm_ref, cm_ref, dt_ref, cbuf_ref, h0_ref, wconv_ref, bconv_ref,
                     dtb_ref, alog_ref, dskip_ref, wnorm_ref, expand_ref, gsum_ref,
                     y_ref, hout_ref, tail_sc):
    q = xs_ref.shape[0]
    gw = D_INNER // SSM_GROUPS
    n = SSM_STATE
    tail_sc[...] = jnp.zeros_like(tail_sc)
    tail_sc[SUBLANES - (CONV_WIDTH - 1):SUBLANES, :] = cbuf_ref[0]
    xs = _conv_silu(xs_ref[...], tail_sc[:, 0:D_INNER], wconv_ref[:, 0:D_INNER], bconv_ref[:, 0:D_INNER])
    o_b, o_c = D_INNER, D_INNER + SSM_GROUPS * n
    bm = _conv_silu(bm_ref[...], tail_sc[:, o_b:o_c], wconv_ref[:, o_b:o_c], bconv_ref[:, o_b:o_c])
    cm = _conv_silu(cm_ref[...], tail_sc[:, o_c:], wconv_ref[:, o_c:], bconv_ref[:, o_c:])
    bm = bm.astype(BF16).astype(F32)
    cm = cm.astype(BF16).astype(F32)

    dt = _softplus(dt_ref[...] + dtb_ref[...])
    a = -jnp.exp(alog_ref[...])
    tri = (_iota((q, q), 0) >= _iota((q, q), 1)).astype(F32)
    d1, d2, d3 = _split3(dt * a)
    acs = _dot(tri, d1.astype(F32)) + _dot(tri, d2.astype(F32)) + _dot(tri, d3.astype(F32))
    a_end = acs[q - 1:q, :]

    row = _iota((q, LANES), 0)
    prods, lms = [], []
    for s in range(q):
        prods.append(cm * bm[s:s + 1, :])
        keep = row >= s
        lms.append(jnp.where(keep, jnp.exp(jnp.where(keep, acs - acs[s:s + 1, :], 0.0)), 0.0))
    qq = q * q
    cb3 = _dot(jnp.concatenate(_split3(jnp.concatenate(prods, axis=0)), axis=0), gsum_ref[...])
    cbs = cb3[0:qq] + cb3[qq:2 * qq] + cb3[2 * qq:3 * qq]
    gmat = (cbs * jnp.concatenate(lms, axis=0)).astype(BF16)
    quant = jnp.concatenate([dt, jnp.exp(acs), jnp.exp(a_end - acs),
                             jnp.broadcast_to(jnp.exp(a_end), (q, LANES))], axis=0)
    nq = 4 * q
    big = _dot(jnp.concatenate(_split3(quant) + (gmat,), axis=0), expand_ref[...])
    qx = big[0:nq] + big[nq:2 * nq] + big[2 * nq:3 * nq]
    dt_x, e_in_x, e_out_x, e_all_x = (qx[i * q:(i + 1) * q] for i in range(4))
    g_x = big[3 * nq:3 * nq + qq]
    xdt = xs * dt_x
    xdt_r = xdt.astype(BF16).astype(F32)
    y = dskip_ref[...] * xs
    for s in range(q):
        y = y + g_x[s * q:(s + 1) * q, :] * xdt_r[s:s + 1, :]
    xdte = xdt * e_out_x

    eye = (_iota((gw, gw), 0) == _iota((gw, gw), 1)).astype(F32)
    for g in range(SSM_GROUPS):
        gs = slice(g * gw, (g + 1) * gw)
        h0 = h0_ref[0, gs, :]
        bg = bm[:, g * n:(g + 1) * n]
        cg = cm[:, g * n:(g + 1) * n]
        yg = y[:, gs] + _dot_nt(cg, h0) * e_in_x[:, gs]
        u = yg * jax.nn.silu(z_ref[:, gs])
        u = u * lax.rsqrt(jnp.mean(u * u, axis=-1, keepdims=True) + EPS)
        y_ref[:, gs] = u * wnorm_ref[:, gs]
        e1, e2, e3 = _split3(e_all_x[:, gs])
        srow = _iota((q, gw), 0)
        e_rows = jnp.where(srow == 0, e1.astype(F32),
                           jnp.where(srow == 1, e2.astype(F32), jnp.where(srow == 2, e3.astype(F32), 0.0)))
        cols = _dot_nt(eye, jnp.concatenate([xdte[:, gs].astype(BF16).astype(F32), e_rows], axis=0))
        dec = jnp.sum(cols[:, q:2 * q], axis=-1, keepdims=True)
        hout_ref[0, gs, :] = dec * h0 + _dot(cols[:, 0:q], bg)


def _ssm_step(proj, dt_raw, conv_buf, h0, sp, batch, q):
    t = proj.shape[0]
    assert t == batch * q and q == SUBLANES
    col = lambda off, w: off // w
    full = lambda shape: pl.BlockSpec(shape, lambda b: (0,) * len(shape))
    gsum = ((_iota((SSM_GROUPS * SSM_STATE, LANES), 0) // SSM_STATE
             == _iota((SSM_GROUPS * SSM_STATE, LANES), 1) // (SSM_HEADS // SSM_GROUPS))
            & (_iota((SSM_GROUPS * SSM_STATE, LANES), 1) < SSM_HEADS)).astype(BF16)
    return pl.pallas_call(
        _ssm_step_kernel,
        grid=(batch,),
        in_specs=[
            pl.BlockSpec((q, D_INNER), lambda b: (b, col(OFF_Z, D_INNER))),
            pl.BlockSpec((q, D_INNER), lambda b: (b, col(OFF_X, D_INNER))),
            pl.BlockSpec((q, 1024), lambda b: (b, col(OFF_B, 1024))),
            pl.BlockSpec((q, 1024), lambda b: (b, col(OFF_C, 1024))),
            pl.BlockSpec((q, LANES), lambda b: (b, 0)),
            pl.BlockSpec((1, CONV_WIDTH - 1, CONV_DIM), lambda b: (b, 0, 0)),
            pl.BlockSpec((1, D_INNER, SSM_STATE), lambda b: (b, 0, 0)),
            full((CONV_WIDTH, CONV_DIM)),
            full((1, CONV_DIM)),
            full((1, LANES)),
            full((1, LANES)),
            full((1, D_INNER)),
            full((1, D_INNER)),
            full((LANES, D_INNER)),
            full((SSM_GROUPS * SSM_STATE, LANES)),
        ],
        out_specs=[
            pl.BlockSpec((q, D_INNER), lambda b: (b, 0)),
            pl.BlockSpec((1, D_INNER, SSM_STATE), lambda b: (b, 0, 0)),
        ],
        out_shape=[
            jax.ShapeDtypeStruct((t, D_INNER), F32),
            jax.ShapeDtypeStruct((batch, D_INNER, SSM_STATE), F32),
        ],
        scratch_shapes=[pltpu.VMEM((SUBLANES, CONV_DIM), F32)],
        compiler_params=pltpu.CompilerParams(
            dimension_semantics=("parallel",), vmem_limit_bytes=VMEM_LIMIT),
        name="ssm_step",
    )(proj, proj, proj, proj, dt_raw, conv_buf, h0, sp["w_conv"], sp["b_conv"], sp["dt_bias"],
      sp["a_log"], sp["d_skip"], sp["w_norm"], sp["expand"], gsum)


def _prep_params(w_in, norm_mix, w_conv, b_conv, dt_bias, a_log, d_skip, w_ssm_norm, w_q_norm, w_k_norm):
    o_z, o_xbc = 0, D_INNER
    o_dt = o_xbc + CONV_DIM
    o_q = o_dt + SSM_HEADS
    o_k = o_q + ATTN_WIDTH
    o_v = o_k + ATTN_WIDTH
    o_g = o_v + ATTN_WIDTH
    w_main = jnp.concatenate(
        [w_in[:, o_z:o_dt], w_in[:, o_g:], w_in[:, o_q:o_g]], axis=1).astype(BF16)
    w_dt = jnp.pad(w_in[:, o_dt:o_q], ((0, 0), (0, LANES - SSM_HEADS))).astype(BF16)
    qk_w = jnp.ones((N_PROJ_TILES, 1, PROJ_TILE), F32)
    qk_w = qk_w.at[Q_TILE0:K_TILE0, 0].set(jnp.tile(w_q_norm, HEADS_PER_GROUP)[None])
    qk_w = qk_w.at[K_TILE0:V_TILE0, 0].set(jnp.tile(w_k_norm, HEADS_PER_GROUP)[None])
    pad_h = lambda v: jnp.pad(v, (0, LANES - SSM_HEADS))[None]
    expand = (_iota((LANES, D_INNER), 0) == _iota((LANES, D_INNER), 1) // SSM_HEAD_DIM).astype(BF16)
    ssm = dict(w_conv=w_conv, b_conv=b_conv[None], dt_bias=pad_h(dt_bias), a_log=pad_h(a_log),
               d_skip=jnp.repeat(d_skip, SSM_HEAD_DIM)[None], w_norm=w_ssm_norm[None], expand=expand)
    return dict(w_main=w_main, w_dt=w_dt, qk_w=qk_w, norm_mix=norm_mix[None], ssm=ssm)


def _attn_prompt_kernel(q_ref, kp_ref, kc_ref, vp_ref, vc_ref, o_ref, lse_ref, *, span):
    nblk = pl.program_id(1)
    blk = q_ref.shape[0]
    qi = _iota((blk, 2 * blk), 0)
    kc = _iota((blk, 2 * blk), 1)
    dist = blk + qi - kc
    mask = (dist >= 0) & (dist <= span) & ((kc >= blk) | (nblk > 0))
    for h in range(HEADS_PER_GROUP):
        hs = slice(h * ATTN_HEAD_DIM, (h + 1) * ATTN_HEAD_DIM)
        qh = q_ref[:, hs].astype(BF16)
        kh = jnp.concatenate([kp_ref[:, hs], kc_ref[:, hs]], axis=0).astype(BF16)
        vh = jnp.concatenate([vp_ref[:, hs], vc_ref[:, hs]], axis=0).astype(BF16)
        s = jnp.where(mask, _dot_nt(qh, kh), -jnp.inf)
        m = jnp.max(s, axis=-1, keepdims=True)
        p = jnp.exp(s - m)
        den = jnp.sum(p, axis=-1, keepdims=True)
        o_ref[:, hs] = _dot((p / den).astype(BF16), vh)
        lse_ref[:, hs] = jnp.broadcast_to(m + jnp.log(den), (blk, ATTN_HEAD_DIM))


def _attn_prompt(proj, gi):
    window, dil = ATTN_PATTERNS[gi]
    s = proj.shape[0]
    n = s // dil
    assert s % dil == 0 and n % ATTN_BLOCK == 0 and window // dil <= ATTN_BLOCK
    nb = n // ATTN_BLOCK
    view = proj.reshape(n, dil * PROJ_WIDTH)
    col = lambda tile0: (lambda r, i: (i, r * N_PROJ_TILES + tile0 + gi))
    colp = lambda tile0: (lambda r, i: (jnp.maximum(i - 1, 0), r * N_PROJ_TILES + tile0 + gi))
    blk = (ATTN_BLOCK, ATTN_GROUP_WIDTH)
    o, lse = pl.pallas_call(
        functools.partial(_attn_prompt_kernel, span=window // dil),
        grid=(dil, nb),
        in_specs=[
            pl.BlockSpec(blk, col(Q_TILE0)),
            pl.BlockSpec(blk, colp(K_TILE0)),
            pl.BlockSpec(blk, col(K_TILE0)),
            pl.BlockSpec(blk, colp(V_TILE0)),
            pl.BlockSpec(blk, col(V_TILE0)),
        ],
        out_specs=[pl.BlockSpec(blk, lambda r, i: (i, r))] * 2,
        out_shape=[jax.ShapeDtypeStruct((n, dil * ATTN_GROUP_WIDTH), F32)] * 2,
        compiler_params=pltpu.CompilerParams(
            dimension_semantics=("parallel", "arbitrary"), vmem_limit_bytes=VMEM_LIMIT),
        name=f"attn_prompt_g{gi}",
    )(view, view, view, view, view)
    return o.reshape(s, ATTN_GROUP_WIDTH), lse.reshape(s, ATTN_GROUP_WIDTH)


def _attn_sample_kernel(q_ref, k_ref, v_ref, kc_ref, vc_ref, o_ref, lse_ref, ko_ref, vo_ref,
                        q_sc, kn_sc, vn_sc, sems, *, span, dil, pos0):
    b = pl.program_id(0)
    l = q_ref.shape[0]
    rows = kc_ref.shape[1]
    nh, hd = HEADS_PER_GROUP, ATTN_HEAD_DIM
    per = rows // dil
    for h in range(nh):
        hs = slice(h * hd, (h + 1) * hd)
        q_sc[:, h, :] = q_ref[:, hs]
        kn_sc[:, h, :] = k_ref[:, hs]
        vn_sc[:, h, :] = v_ref[:, hs]

    copies = [
        pltpu.make_async_copy(kc_ref.at[0, pl.ds(l, rows - l)], ko_ref.at[b, pl.ds(0, rows - l)], sems.at[0]),
        pltpu.make_async_copy(vc_ref.at[0, pl.ds(l, rows - l)], vo_ref.at[b, pl.ds(0, rows - l)], sems.at[1]),
        pltpu.make_async_copy(kn_sc, ko_ref.at[b, pl.ds(rows - l, l)], sems.at[2]),
        pltpu.make_async_copy(vn_sc, vo_ref.at[b, pl.ds(rows - l, l)], sems.at[3]),
    ]
    for cp in copies:
        cp.start()

    kn2 = kn_sc[...].reshape(l * nh, hd)
    vn2 = vn_sc[...].reshape(l * nh, hd)
    lane_c = _iota((nh, per * nh), 1)
    head_c = lane_c & (nh - 1)
    j_c = _shr(lane_c, nh)
    sub_c = _iota((nh, per * nh), 0)
    lane_n = _iota((nh, l * nh), 1)
    head_n = lane_n & (nh - 1)
    tok_n = _shr(lane_n, nh)
    sub_n = _iota((nh, l * nh), 0)
    wide = (nh, nh * hd)
    own = _iota(wide, 0) == _shr(_iota(wide, 1), hd)
    spread = (_iota((hd, nh * hd), 0) == (_iota((hd, nh * hd), 1) & (hd - 1))).astype(F32)
    for t in range(l):
        r = (rows + t) % dil
        kt = kc_ref[0, pl.ds(r, per, stride=dil)].reshape(per * nh, hd)
        vt = vc_ref[0, pl.ds(r, per, stride=dil)].reshape(per * nh, hd)
        qt = q_sc[t]
        delta_c = rows + t - (j_c * dil + r)
        ok_c = (head_c == sub_c) & (delta_c <= span * dil) & (pos0 + t - delta_c >= 0)
        delta_n = t - tok_n
        ok_n = ((head_n == sub_n) & (delta_n >= 0) & (delta_n <= span * dil)
                & ((delta_n & (dil - 1)) == 0))
        s_c = jnp.where(ok_c, _dot_nt(qt, kt), -jnp.inf)
        s_n = jnp.where(ok_n, _dot_nt(qt, kn2), -jnp.inf)
        m = jnp.maximum(jnp.max(s_c, axis=-1, keepdims=True), jnp.max(s_n, axis=-1, keepdims=True))
        p_c = jnp.exp(s_c - m)
        p_n = jnp.exp(s_n - m)
        den = jnp.sum(p_c, axis=-1, keepdims=True) + jnp.sum(p_n, axis=-1, keepdims=True)
        o_t = _dot(p_c / den, vt) + _dot(p_n / den, vn2)
        o_row = jnp.sum(jnp.where(own, _dot3f(o_t, spread), 0.0), axis=0, keepdims=True)
        lse_row = jnp.sum(jnp.where(own, jnp.broadcast_to(m + jnp.log(den), wide), 0.0), axis=0, keepdims=True)
        o_ref[t:t + 1, :] = o_row
        lse_ref[t:t + 1, :] = lse_row

    for cp in copies:
        cp.wait()


def _attn_sample(proj, k_cache, v_cache, gi, batch, pos0):
    window, dil = ATTN_PATTERNS[gi]
    rows = k_cache.shape[1]
    l = proj.shape[0] // batch
    assert l == SUBLANES and rows % dil == 0 and rows > l
    tok = lambda tile0: pl.BlockSpec((l, ATTN_GROUP_WIDTH), lambda b: (b, tile0 + gi))
    cache = pl.BlockSpec((1, rows, HEADS_PER_GROUP, ATTN_HEAD_DIM), lambda b: (b, 0, 0, 0))
    out_tok = pl.BlockSpec((l, ATTN_GROUP_WIDTH), lambda b: (b, 0))
    hbm = pl.BlockSpec(memory_space=pl.ANY)
    tile3 = (l, HEADS_PER_GROUP, ATTN_HEAD_DIM)
    return pl.pallas_call(
        functools.partial(_attn_sample_kernel, span=window // dil, dil=dil, pos0=pos0),
        grid=(batch,),
        in_specs=[tok(Q_TILE0), tok(K_TILE0), tok(V_TILE0), cache, cache],
        out_specs=[out_tok, out_tok, hbm, hbm],
        out_shape=[jax.ShapeDtypeStruct((batch * l, ATTN_GROUP_WIDTH), F32)] * 2
        + [jax.ShapeDtypeStruct(k_cache.shape, F32)] * 2,
        scratch_shapes=[pltpu.VMEM(tile3, F32), pltpu.VMEM(tile3, F32), pltpu.VMEM(tile3, F32),
                        pltpu.SemaphoreType.DMA((4,))],
        compiler_params=pltpu.CompilerParams(
            dimension_semantics=("arbitrary",), vmem_limit_bytes=VMEM_LIMIT),
        name=f"attn_sample_g{gi}",
    )(proj, proj, proj, k_cache, v_cache)


def _merge_kernel(x_ref, ys_ref, o0_ref, o1_ref, o2_ref, l0_ref, l1_ref, l2_ref, g0_ref, g1_ref,
                  wbs_ref, wba_ref, wo_ref, nf_ref, wr1_ref, wr2_ref, br_ref,
                  x1_ref, xn_ref, lg_ref):
    l0, l1, l2 = l0_ref[...], l1_ref[...], l2_ref[...]
    m = jnp.maximum(jnp.maximum(l0, l1), l2)
    e0, e1, e2 = jnp.exp(l0 - m), jnp.exp(l1 - m), jnp.exp(l2 - m)
    den = e0 + e1 + e2
    y_attn = (e0 / den) * o0_ref[...] + (e1 / den) * o1_ref[...] + (e2 / den) * o2_ref[...]
    br_s = _dot(ys_ref[...].astype(BF16), wbs_ref[...])
    br_a = _dot(y_attn.astype(BF16), wba_ref[...])
    merged = jax.nn.sigmoid(g0_ref[...]) * br_s + jax.nn.sigmoid(g1_ref[...]) * br_a
    x1 = x_ref[...] + _dot(merged.astype(BF16), wo_ref[...])
    x1_ref[...] = x1
    xn = x1 * lax.rsqrt(jnp.mean(x1 * x1, axis=-1, keepdims=True) + EPS) * nf_ref[...]
    xn1 = xn.astype(BF16)
    xn_ref[...] = xn1
    xn2 = (xn - xn1.astype(F32)).astype(BF16)
    lg_ref[...] = (_dot(xn1, wr1_ref[...]) + _dot(xn2, wr1_ref[...]) + _dot(xn1, wr2_ref[...])
                   + br_ref[...])


def _merge(x, proj, y_ssm, attn, mp, tm):
    t = x.shape[0]
    row = lambda w: pl.BlockSpec((tm, w), lambda i: (i, 0))
    full = lambda a: pl.BlockSpec(a.shape, lambda i: (0,) * a.ndim)
    (o0, l0), (o1, l1), (o2, l2) = attn
    weights = [mp["w_br_ssm"], mp["w_br_attn"], mp["w_out"], mp["norm_ffn"], mp["w_r1"], mp["w_r2"], mp["b_r"]]
    return pl.pallas_call(
        _merge_kernel,
        grid=(t // tm,),
        in_specs=[row(D_MODEL), row(D_INNER)] + [row(ATTN_GROUP_WIDTH)] * 6 + [
            pl.BlockSpec((tm, D_MODEL), lambda i: (i, OFF_G // D_MODEL)),
            pl.BlockSpec((tm, D_MODEL), lambda i: (i, OFF_G // D_MODEL + 1)),
        ] + [full(w) for w in weights],
        out_specs=[row(D_MODEL), row(D_MODEL), row(ROUTER_WIDTH)],
        out_shape=[
            jax.ShapeDtypeStruct((t, D_MODEL), F32),
            jax.ShapeDtypeStruct((t, D_MODEL), BF16),
            jax.ShapeDtypeStruct((t, ROUTER_WIDTH), F32),
        ],
        compiler_params=pltpu.CompilerParams(
            dimension_semantics=("parallel",), vmem_limit_bytes=VMEM_LIMIT),
        name="merge",
    )(x, y_ssm, o0, o1, o2, l0, l1, l2, proj, proj, *weights)


def _prep_merge_params(w_br_ssm, w_br_attn, w_out, norm_ffn, w_router_group, b_router_group,
                       w_router_expert, b_router_expert):
    w_r = jnp.concatenate([w_router_group, w_router_expert.reshape(D_MODEL, N_EXPERTS)], axis=1)
    w_r = jnp.pad(w_r, ((0, 0), (0, ROUTER_WIDTH - w_r.shape[1])))
    w_r1 = w_r.astype(BF16)
    w_r2 = (w_r - w_r1.astype(F32)).astype(BF16)
    b_r = jnp.concatenate([b_router_group, b_router_expert.reshape(N_EXPERTS)])
    b_r = jnp.pad(b_r, (0, ROUTER_WIDTH - b_r.shape[0]))[None]
    return dict(w_br_ssm=w_br_ssm.astype(BF16), w_br_attn=w_br_attn.astype(BF16), w_out=w_out.astype(BF16),
                norm_ffn=norm_ffn[None], w_r1=w_r1, w_r2=w_r2, b_r=b_r)


def _route(logits):
    lane = _iota(logits.shape, 1)
    big = jnp.int32(2 ** 30)
    neg = -jnp.inf
    gl = jnp.where(lane < N_EXPERT_GROUPS, logits, neg)
    gmax = jnp.max(gl, axis=-1, keepdims=True)
    gidx = jnp.min(jnp.where(gl == gmax, lane, big), axis=-1, keepdims=True)
    g_w = 1.0 / jnp.sum(jnp.exp(gl - gmax), axis=-1, keepdims=True)
    lo = N_EXPERT_GROUPS + gidx * EXPERTS_PER_GROUP
    el = jnp.where((lane >= lo) & (lane < lo + EXPERTS_PER_GROUP), logits, neg)
    v1 = jnp.max(el, axis=-1, keepdims=True)
    i1 = jnp.min(jnp.where(el == v1, lane, big), axis=-1, keepdims=True)
    el2 = jnp.where(lane == i1, neg, el)
    v2 = jnp.max(el2, axis=-1, keepdims=True)
    i2 = jnp.min(jnp.where(el2 == v2, lane, big), axis=-1, keepdims=True)
    e2 = jnp.exp(v2 - v1)
    w1 = (1.0 / (1.0 + e2)) * g_w
    w2 = (e2 / (1.0 + e2)) * g_w
    return jnp.where(lane == i1, w1, jnp.where(lane == i2, w2, 0.0)), gidx


MOE_BLOCK = 256


def _moe_kernel(x1_ref, xn_ref, lg_ref, wg_ref, wu_ref, wd_ref, out_ref,
                comb_sc, oh_sc, cum_sc, oht_sc, cumt_sc):
    g = pl.program_id(1)
    tm = x1_ref.shape[0]
    blk = MOE_BLOCK

    @pl.when(g == 0)
    def _():
        comb, gidx = _route(lg_ref[...])
        c1, c2, c3 = _split3(comb)
        comb_sc[0] = c1
        comb_sc[1] = c2
        comb_sc[2] = c3
        onehot = (_iota((tm, LANES), 1) == gidx).astype(F32)
        oh_bf = onehot.astype(BF16)
        lower = (_iota((tm, tm), 0) > _iota((tm, tm), 1)).astype(BF16)
        upper = (_iota((tm, tm), 0) < _iota((tm, tm), 1)).astype(BF16)
        oh_sc[...] = onehot
        cum_sc[...] = _dot(lower, oh_bf)
        oht_sc[...] = onehot.T[0:SUBLANES]
        cumt_sc[...] = _dot_tn(oh_bf, upper)[0:SUBLANES]
        out_ref[...] = x1_ref[...]

    lane = _iota((tm, LANES), 1)
    pick = lane == g
    member_c = jnp.sum(jnp.where(pick, oh_sc[...], 0.0), axis=-1, keepdims=True)
    rank_c = jnp.sum(jnp.where(pick, cum_sc[...], 0.0), axis=-1, keepdims=True)
    member_r = oht_sc[pl.ds(g, 1), :]
    rank_r = cumt_sc[pl.ds(g, 1), :]
    count = jnp.sum(member_r).astype(jnp.int32)
    nblk = (count + (blk - 1)) // blk
    xn = xn_ref[...]
    wd = wd_ref[...].reshape(EXPERTS_PER_GROUP * D_EXPERT, D_MODEL)
    lane_b = _iota((blk, LANES), 1)

    def body(k, carry):
        base = (k * blk).astype(F32)
        want_r = base + _iota((blk, tm), 0).astype(F32)
        sel = jnp.where((rank_r == want_r) & (member_r > 0.0), 1.0, 0.0).astype(BF16)
        want_c = base + _iota((tm, blk), 1).astype(F32)
        sel_t = jnp.where((rank_c == want_c) & (member_c > 0.0), 1.0, 0.0).astype(BF16)
        xg = _dot(sel, xn).astype(BF16)
        cg = _dot(sel, comb_sc[0]) + _dot(sel, comb_sc[1]) + _dot(sel, comb_sc[2])
        hidden = []
        for e in range(EXPERTS_PER_GROUP):
            col = N_EXPERT_GROUPS + g * EXPERTS_PER_GROUP + e
            cw = jnp.sum(jnp.where(lane_b == col, cg, 0.0), axis=-1, keepdims=True)
            h = jax.nn.silu(_dot(xg, wg_ref[e])) * _dot(xg, wu_ref[e]) * cw
            hidden.append(h.astype(BF16))
        out = _dot(jnp.concatenate(hidden, axis=1), wd)
        o1 = out.astype(BF16)
        o2 = (out - o1.astype(F32)).astype(BF16)
        out_ref[...] += _dot(sel_t, o1) + _dot(sel_t, o2)
        return carry

    lax.fori_loop(0, nblk, body, 0)


def _moe(x1, xn, logits, wg, wu, wd, tm):
    t = x1.shape[0]
    ge = EXPERTS_PER_GROUP
    return pl.pallas_call(
        _moe_kernel,
        grid=(t // tm, N_EXPERT_GROUPS),
        in_specs=[
            pl.BlockSpec((tm, D_MODEL), lambda i, g: (i, 0), pipeline_mode=pl.Buffered(1)),
            pl.BlockSpec((tm, D_MODEL), lambda i, g: (i, 0)),
            pl.BlockSpec((tm, ROUTER_WIDTH), lambda i, g: (i, 0), pipeline_mode=pl.Buffered(1)),
            pl.BlockSpec((ge, D_MODEL, D_EXPERT), lambda i, g: (g, 0, 0)),
            pl.BlockSpec((ge, D_MODEL, D_EXPERT), lambda i, g: (g, 0, 0)),
            pl.BlockSpec((ge, D_EXPERT, D_MODEL), lambda i, g: (g, 0, 0)),
        ],
        out_specs=pl.BlockSpec((tm, D_MODEL), lambda i, g: (i, 0)),
        out_shape=jax.ShapeDtypeStruct((t, D_MODEL), F32),
        scratch_shapes=[
            pltpu.VMEM((3, tm, ROUTER_WIDTH), BF16),
            pltpu.VMEM((tm, LANES), F32),
            pltpu.VMEM((tm, LANES), F32),
            pltpu.VMEM((SUBLANES, tm), F32),
            pltpu.VMEM((SUBLANES, tm), F32),
        ],
        compiler_params=pltpu.CompilerParams(
            dimension_semantics=("parallel", "arbitrary"), vmem_limit_bytes=VMEM_LIMIT),
        name="moe",
    )(x1, xn, logits, wg, wu, wd)


def _moe_dense_kernel(x1_ref, xn_ref, lg_ref, wg_ref, wu_ref, wd_ref, out_ref, comb_sc, acc_sc):
    e = pl.program_id(1)

    @pl.when(e == 0)
    def _():
        comb_sc[...] = _route(lg_ref[...])[0]
        acc_sc[...] = jnp.zeros_like(acc_sc)

    lane = _iota(comb_sc.shape, 1)
    cw = jnp.sum(jnp.where(lane == e + N_EXPERT_GROUPS, comb_sc[...], 0.0), axis=-1, keepdims=True)
    xn = xn_ref[...]
    hidden = jax.nn.silu(_dot(xn, wg_ref[0])) * _dot(xn, wu_ref[0]) * cw
    acc_sc[...] += _dot(hidden.astype(BF16), wd_ref[0])

    @pl.when(e == pl.num_programs(1) - 1)
    def _():
        out_ref[...] = x1_ref[...] + acc_sc[...]


def _moe_dense(x1, xn, logits, wg, wu, wd, tm):
    t = x1.shape[0]
    return pl.pallas_call(
        _moe_dense_kernel,
        grid=(t // tm, N_EXPERTS),
        in_specs=[
            pl.BlockSpec((tm, D_MODEL), lambda i, e: (i, 0)),
            pl.BlockSpec((tm, D_MODEL), lambda i, e: (i, 0)),
            pl.BlockSpec((tm, ROUTER_WIDTH), lambda i, e: (i, 0)),
            pl.BlockSpec((1, D_MODEL, D_EXPERT), lambda i, e: (e, 0, 0)),
            pl.BlockSpec((1, D_MODEL, D_EXPERT), lambda i, e: (e, 0, 0)),
            pl.BlockSpec((1, D_EXPERT, D_MODEL), lambda i, e: (e, 0, 0)),
        ],
        out_specs=pl.BlockSpec((tm, D_MODEL), lambda i, e: (i, 0)),
        out_shape=jax.ShapeDtypeStruct((t, D_MODEL), F32),
        scratch_shapes=[pltpu.VMEM((tm, ROUTER_WIDTH), F32), pltpu.VMEM((tm, D_MODEL), F32)],
        compiler_params=pltpu.CompilerParams(
            dimension_semantics=("parallel", "arbitrary"), vmem_limit_bytes=VMEM_LIMIT),
        name="moe_dense",
    )(x1, xn, logits, wg, wu, wd)


def _tile_rows(t, cap):
    tm = min(t, cap)
    assert t % tm == 0
    return tm


def _decoder_layer(x, conv_buf, ssm_state, kv_bufs, pos0, pp, mp, ep):
    b, l, d = x.shape
    t = b * l
    x2 = x.reshape(t, d)
    proj, dt_raw = _in_proj(x2, pp["norm_mix"], pp["w_main"], pp["w_dt"], pp["qk_w"], _tile_rows(t, 1024))
    proj3 = proj.reshape(b, l, PROJ_WIDTH)
    h0 = ssm_state.reshape(b, D_INNER, SSM_STATE)
    if kv_bufs is None:
        assert b == 1 and l % SSD_CHUNK == 0
        y_ssm, h_new = _ssm_chunked(proj, dt_raw, conv_buf, h0, pp["ssm"], b, SSD_CHUNK)
        attn = [_attn_prompt(proj, gi) for gi in range(N_ATTN_GROUPS)]
        new_kv = []
        for gi, (window, _) in enumerate(ATTN_PATTERNS):
            keep = min(window, l)
            for off in (OFF_K, OFF_V):
                c0 = off + gi * ATTN_GROUP_WIDTH
                new_kv.append(proj3[:, l - keep:, c0:c0 + ATTN_GROUP_WIDTH]
                              .reshape(b, keep, HEADS_PER_GROUP, ATTN_HEAD_DIM))
    else:
        y_ssm, h_new = _ssm_step(proj, dt_raw, conv_buf, h0, pp["ssm"], b, l)
        attn, new_kv = [], []
        for gi in range(N_ATTN_GROUPS):
            kc, vc = kv_bufs[2 * gi], kv_bufs[2 * gi + 1]
            o, lse, kn, vn = _attn_sample(proj, kc, vc, gi, b, pos0)
            attn.append((o, lse))
            new_kv += [kn, vn]
    new_conv = proj3[:, l - (CONV_WIDTH - 1):, OFF_X:OFF_X + CONV_DIM]
    tm = _tile_rows(t, 512)
    x1, xn, logits = _merge(x2, proj, y_ssm, attn, mp, tm)
    y = _moe(x1, xn, logits, ep["wg"], ep["wu"], ep["wd"], _tile_rows(t, 1024))
    new_ssm = h_new.reshape(b, SSM_HEADS, SSM_HEAD_DIM, SSM_STATE)
    return y.reshape(b, l, d), (new_conv, new_ssm) + tuple(new_kv)


def kernel(x_prompt, x_sample, cache_conv, state_ssm, cache_k_w128, cache_v_w128, cache_k_w512, cache_v_w512,
           cache_k_w2048, cache_v_w2048, w_in, norm_mix, w_conv, b_conv, dt_bias, a_log, d_skip, w_ssm_norm,
           w_q_norm, w_k_norm, w_br_ssm, w_br_attn, w_out, norm_ffn, w_router_group, b_router_group,
           w_router_expert, b_router_expert, w_expert_gate, w_expert_up, w_expert_down):
    depth = w_in.shape[0]
    yp, ys = x_prompt, x_sample
    bp = x_prompt.shape[0]
    prompt_states, sample_states = [], []
    for i in range(depth):
        pp = _prep_params(w_in[i], norm_mix[i], w_conv[i], b_conv[i], dt_bias[i], a_log[i], d_skip[i],
                          w_ssm_norm[i], w_q_norm[i], w_k_norm[i])
        mp = _prep_merge_params(w_br_ssm[i], w_br_attn[i], w_out[i], norm_ffn[i], w_router_group[i],
                                b_router_group[i], w_router_expert[i], b_router_expert[i])
        ep = dict(wg=w_expert_gate[i].astype(BF16), wu=w_expert_up[i].astype(BF16),
                  wd=w_expert_down[i].astype(BF16))
        zero_conv = jnp.zeros((bp, CONV_WIDTH - 1, CONV_DIM), x_prompt.dtype)
        zero_ssm = jnp.zeros((bp, SSM_HEADS, SSM_HEAD_DIM, SSM_STATE), F32)
        yp, st_p = _decoder_layer(yp, zero_conv, zero_ssm, None, 0, pp, mp, ep)
        kv_bufs = (cache_k_w128[i], cache_v_w128[i], cache_k_w512[i], cache_v_w512[i],
                   cache_k_w2048[i], cache_v_w2048[i])
        ys, st_s = _decoder_layer(ys, cache_conv[i], state_ssm[i], kv_bufs, PAST_LEN, pp, mp, ep)
        prompt_states.append(st_p)
        sample_states.append(st_s)
    stack = lambda states: [jnp.stack([s[j] for s in states]) for j in range(8)]
    return (yp, ys) + tuple(stack(prompt_states)) + tuple(stack(sample_states))
```

```python
import functools
import math

import jax
import jax.numpy as jnp
from jax import lax
from jax.experimental import pallas as pl
from jax.experimental.pallas import tpu as pltpu

F32 = jnp.float32
BF16 = jnp.bfloat16

PAST_LEN = 8192
D_MODEL = 1024
D_INNER = 2048
SSM_HEAD_DIM = 64
SSM_HEADS = 32
SSM_GROUPS = 8
SSM_STATE = 128
CONV_WIDTH = 4
CONV_DIM = D_INNER + 2 * SSM_GROUPS * SSM_STATE
SSD_CHUNK = 128
ATTN_HEAD_DIM = 64
HEADS_PER_GROUP = 8
ATTN_PATTERNS = ((128, 1), (512, 4), (2048, 16))
N_ATTN_GROUPS = 3
ATTN_GROUP_WIDTH = HEADS_PER_GROUP * ATTN_HEAD_DIM
ATTN_WIDTH = N_ATTN_GROUPS * ATTN_GROUP_WIDTH
ATTN_BLOCK = 128
N_EXPERT_GROUPS = 4
EXPERTS_PER_GROUP = 8
N_EXPERTS = 32
D_EXPERT = 256
EPS = 1e-6

LANES = 128
SUBLANES = 8
VMEM_LIMIT = 56 * 1024 * 1024

PROJ_TILE = 512
OFF_Z = 0
OFF_X = 2048
OFF_B = 4096
OFF_C = 5120
OFF_G = 6144
OFF_Q = 8192
OFF_K = OFF_Q + ATTN_WIDTH
OFF_V = OFF_K + ATTN_WIDTH
PROJ_WIDTH = OFF_V + ATTN_WIDTH
N_PROJ_TILES = PROJ_WIDTH // PROJ_TILE
Q_TILE0 = OFF_Q // PROJ_TILE
K_TILE0 = OFF_K // PROJ_TILE
V_TILE0 = OFF_V // PROJ_TILE
ROUTER_WIDTH = LANES


def _split3(x):
    x1 = x.astype(BF16)
    r1 = x - x1.astype(F32)
    x2 = r1.astype(BF16)
    r2 = r1 - x2.astype(F32)
    return x1, x2, r2.astype(BF16)


def _dot(a, b):
    return jnp.dot(a, b, preferred_element_type=F32)


def _dot3(x, m):
    x1, x2, x3 = _split3(x)
    return _dot(x1, m) + _dot(x2, m) + _dot(x3, m)


def _dot3f(x, m):
    x1, x2, x3 = _split3(x)
    return _dot(x1.astype(F32), m) + _dot(x2.astype(F32), m) + _dot(x3.astype(F32), m)


def _dot_nt(a, b):
    return lax.dot_general(a, b, (((1,), (1,)), ((), ())), preferred_element_type=F32)


def _dot_tn(a, b):
    return lax.dot_general(a, b, (((0,), (0,)), ((), ())), preferred_element_type=F32)


def _iota(shape, dim):
    return lax.broadcasted_iota(jnp.int32, shape, dim)


def _shr(x, pow2):
    assert pow2 & (pow2 - 1) == 0
    return lax.shift_right_logical(x, jnp.int32(int(math.log2(pow2))))


def _in_proj_kernel(x_ref, nw_ref, w_ref, wdt_ref, qkw_ref, seg_ref, out_ref, dt_ref, h_sc):
    j = pl.program_id(1)

    @pl.when(j == 0)
    def _():
        xf = x_ref[...]
        y = xf * lax.rsqrt(jnp.mean(xf * xf, axis=-1, keepdims=True) + EPS)
        h = (y * nw_ref[...]).astype(BF16)
        h_sc[...] = h
        dt_ref[...] = _dot(h, wdt_ref[...])

    acc = _dot(h_sc[...], w_ref[...])
    is_qk = jnp.logical_and(j >= Q_TILE0, j < V_TILE0)

    @pl.when(jnp.logical_not(is_qk))
    def _():
        out_ref[...] = acc

    @pl.when(is_qk)
    def _():
        sq = acc * acc
        s1 = sq.astype(BF16)
        s2 = (sq - s1.astype(F32)).astype(BF16)
        ms = (_dot(s1, seg_ref[...]) + _dot(s2, seg_ref[...])) * (1.0 / ATTN_HEAD_DIM)
        scale = jnp.where(j < K_TILE0, ATTN_HEAD_DIM ** -0.5, 1.0).astype(F32)
        w = qkw_ref[0] * scale
        out_ref[...] = acc * lax.rsqrt(ms + EPS) * w


def _in_proj(x, norm_w, w_bf, wdt_bf, qk_w, tm):
    t = x.shape[0]
    seg = (_iota((PROJ_TILE, PROJ_TILE), 0) // ATTN_HEAD_DIM
           == _iota((PROJ_TILE, PROJ_TILE), 1) // ATTN_HEAD_DIM).astype(BF16)
    return pl.pallas_call(
        _in_proj_kernel,
        grid=(t // tm, N_PROJ_TILES),
        in_specs=[
            pl.BlockSpec((tm, D_MODEL), lambda i, j: (i, 0)),
            pl.BlockSpec((1, D_MODEL), lambda i, j: (0, 0)),
            pl.BlockSpec((D_MODEL, PROJ_TILE), lambda i, j: (0, j)),
            pl.BlockSpec((D_MODEL, LANES), lambda i, j: (0, 0)),
            pl.BlockSpec((1, 1, PROJ_TILE), lambda i, j: (j, 0, 0)),
            pl.BlockSpec((PROJ_TILE, PROJ_TILE), lambda i, j: (0, 0)),
        ],
        out_specs=[
            pl.BlockSpec((tm, PROJ_TILE), lambda i, j: (i, j)),
            pl.BlockSpec((tm, LANES), lambda i, j: (i, 0)),
        ],
        out_shape=[
            jax.ShapeDtypeStruct((t, PROJ_WIDTH), F32),
            jax.ShapeDtypeStruct((t, LANES), F32),
        ],
        scratch_shapes=[pltpu.VMEM((tm, D_MODEL), BF16)],
        compiler_params=pltpu.CompilerParams(
            dimension_semantics=("parallel", "arbitrary"), vmem_limit_bytes=VMEM_LIMIT),
        name="in_proj",
    )(x, norm_w, w_bf, wdt_bf, qk_w, seg)


def _softplus(x):
    return jnp.maximum(x, 0.0) + jnp.log1p(jnp.exp(-jnp.abs(x)))


def _conv_silu(raw, tail, w, b):
    q = raw.shape[0]
    up = jnp.concatenate([tail, raw], axis=0)
    acc = b + w[CONV_WIDTH - 1:CONV_WIDTH] * raw
    for tap in range(CONV_WIDTH - 1):
        shifted = pltpu.roll(up, CONV_WIDTH - 1 - tap, axis=0)[SUBLANES:SUBLANES + q]
        acc = acc + w[tap:tap + 1] * shifted
    return jax.nn.silu(acc)


def _ssm_chunk_kernel(z_ref, xs_ref, bm_ref, cm_ref, dt_ref, cbuf_ref, h0_ref, wconv_ref, bconv_ref,
                      dtb_ref, alog_ref, dskip_ref, wnorm_ref, expand_ref,
                      y_ref, hout_ref, tail_sc, xbc_sc, ht_sc):
    c = pl.program_id(1)
    nc = pl.num_programs(1)
    q = xs_ref.shape[0]
    gw = D_INNER // SSM_GROUPS
    n = SSM_STATE

    @pl.when(c == 0)
    def _():
        tail_sc[...] = jnp.zeros_like(tail_sc)
        tail_sc[SUBLANES - (CONV_WIDTH - 1):SUBLANES, :] = cbuf_ref[0]
        for g in range(SSM_GROUPS):
            ht_sc[g] = h0_ref[0, g * gw:(g + 1) * gw, :].T

    cw = PROJ_TILE
    for k in range(CONV_DIM // cw):
        if k < D_INNER // cw:
            raw = xs_ref[:, k * cw:(k + 1) * cw]
        elif k < (D_INNER + SSM_GROUPS * n) // cw:
            kk = k - D_INNER // cw
            raw = bm_ref[:, kk * cw:(kk + 1) * cw]
        else:
            kk = k - (D_INNER + SSM_GROUPS * n) // cw
            raw = cm_ref[:, kk * cw:(kk + 1) * cw]
        sl = slice(k * cw, (k + 1) * cw)
        xbc_sc[:, sl] = _conv_silu(raw, tail_sc[:, sl], wconv_ref[:, sl], bconv_ref[:, sl])
        tail_sc[:, sl] = raw[q - SUBLANES:q]

    dt = _softplus(dt_ref[...] + dtb_ref[...])
    a = -jnp.exp(alog_ref[...])
    dta = dt * a
    tri = (_iota((q, q), 0) >= _iota((q, q), 1)).astype(BF16)
    d1, d2, d3 = _split3(dta)
    acs = _dot(tri, d1) + _dot(tri, d2) + _dot(tri, d3)
    eye = (_iota((LANES, LANES), 0) == _iota((LANES, LANES), 1)).astype(BF16)
    c1, c2, c3 = _split3(acs)
    acs_t = _dot_nt(eye, c1) + _dot_nt(eye, c2) + _dot_nt(eye, c3)
    a_end = acs[q - 1:q, :]
    e_in = jnp.exp(acs)
    e_out = jnp.exp(a_end - acs)
    e_all = jnp.exp(a_end)

    ex = expand_ref[...]
    dt_x = _dot3(dt, ex)
    e_in_x = _dot3(e_in, ex)
    e_out_x = _dot3(e_out, ex)
    e_all_x = _dot3(jnp.broadcast_to(e_all, (2 * SUBLANES, LANES)), ex)[0:1]

    causal = _iota((q, q), 0) >= _iota((q, q), 1)
    lane_lo = _iota((q, LANES), 1) < SSM_HEAD_DIM

    for g in range(SSM_GROUPS):
        gs = slice(g * gw, (g + 1) * gw)
        xg = xbc_sc[:, gs]
        bg = xbc_sc[:, D_INNER + g * n:D_INNER + (g + 1) * n].astype(BF16)
        cg = xbc_sc[:, D_INNER + SSM_GROUPS * n + g * n:D_INNER + SSM_GROUPS * n + (g + 1) * n].astype(BF16)
        xdt = xg * dt_x[:, gs]
        cb = _dot_nt(cg, bg)
        ht = ht_sc[g]
        y_off = _dot(cg, ht.astype(BF16)) * e_in_x[:, gs]
        pieces = []
        for pair in range(2):
            ms = []
            for hh in range(2):
                h = g * 4 + pair * 2 + hh
                diff = acs[:, h:h + 1] - acs_t[h:h + 1, :]
                lmat = jnp.where(causal, jnp.exp(jnp.where(causal, diff, 0.0)), 0.0)
                ms.append((cb * lmat).astype(BF16))
            xp = xdt[:, pair * LANES:(pair + 1) * LANES]
            rhs = jnp.concatenate([jnp.where(lane_lo, xp, 0.0), jnp.where(lane_lo, 0.0, xp)],
                                  axis=0).astype(BF16)
            pieces.append(_dot(jnp.concatenate(ms, axis=1), rhs))
        y_diag = jnp.concatenate(pieces, axis=1)
        yg = y_diag + y_off + dskip_ref[:, gs] * xg
        zg = z_ref[:, gs]
        u = yg * jax.nn.silu(zg)
        u = u * lax.rsqrt(jnp.mean(u * u, axis=-1, keepdims=True) + EPS)
        y_ref[:, gs] = (u * wnorm_ref[:, gs]).astype(y_ref.dtype)
        st = _dot_tn(bg, (xdt * e_out_x[:, gs]).astype(BF16))
        ht_sc[g] = e_all_x[:, gs] * ht + st

    @pl.when(c == nc - 1)
    def _():
        for g in range(SSM_GROUPS):
            hout_ref[0, g * gw:(g + 1) * gw, :] = ht_sc[g].T


def _ssm_chunked(proj, dt_raw, conv_buf, h0, sp, batch, q):
    t = proj.shape[0]
    nc = t // batch // q
    row = lambda b, c: b * nc + c
    col = lambda off, w: off // w
    full = lambda shape: pl.BlockSpec(shape, lambda b, c: (0,) * len(shape))
    return pl.pallas_call(
        _ssm_chunk_kernel,
        grid=(batch, nc),
        in_specs=[
            pl.BlockSpec((q, D_INNER), lambda b, c: (row(b, c), col(OFF_Z, D_INNER))),
            pl.BlockSpec((q, D_INNER), lambda b, c: (row(b, c), col(OFF_X, D_INNER))),
            pl.BlockSpec((q, 1024), lambda b, c: (row(b, c), col(OFF_B, 1024))),
            pl.BlockSpec((q, 1024), lambda b, c: (row(b, c), col(OFF_C, 1024))),
            pl.BlockSpec((q, LANES), lambda b, c: (row(b, c), 0)),
            pl.BlockSpec((1, CONV_WIDTH - 1, CONV_DIM), lambda b, c: (b, 0, 0)),
            pl.BlockSpec((1, D_INNER, SSM_STATE), lambda b, c: (b, 0, 0)),
            full((CONV_WIDTH, CONV_DIM)),
            full((1, CONV_DIM)),
            full((1, LANES)),
            full((1, LANES)),
            full((1, D_INNER)),
            full((1, D_INNER)),
            full((LANES, D_INNER)),
        ],
        out_specs=[
            pl.BlockSpec((q, D_INNER), lambda b, c: (row(b, c), 0)),
            pl.BlockSpec((1, D_INNER, SSM_STATE), lambda b, c: (b, 0, 0)),
        ],
        out_shape=[
            jax.ShapeDtypeStruct((t, D_INNER), BF16),
            jax.ShapeDtypeStruct((batch, D_INNER, SSM_STATE), F32),
        ],
        scratch_shapes=[
            pltpu.VMEM((SUBLANES, CONV_DIM), F32),
            pltpu.VMEM((q, CONV_DIM), F32),
            pltpu.VMEM((SSM_GROUPS, SSM_STATE, D_INNER // SSM_GROUPS), F32),
        ],
        compiler_params=pltpu.CompilerParams(
            dimension_semantics=("parallel", "arbitrary"), vmem_limit_bytes=VMEM_LIMIT),
        name="ssm_chunked",
    )(proj, proj, proj, proj, dt_raw, conv_buf, h0, sp["w_conv"], sp["b_conv"], sp["dt_bias"],
      sp["a_log"], sp["d_skip"], sp["w_norm"], sp["expand"])


def _ssm_step_kernel(z_ref, xs_ref, bm_ref, cm_ref, dt_ref, cbuf_ref, h0_ref, wconv_ref, bconv_ref,
                     dtb_ref, alog_ref, dskip_ref, wnorm_ref, expand_ref, gsum_ref,
                     y_ref, hout_ref, tail_sc):
    q = xs_ref.shape[0]
    gw = D_INNER // SSM_GROUPS
    n = SSM_STATE
    tail_sc[...] = jnp.zeros_like(tail_sc)
    tail_sc[SUBLANES - (CONV_WIDTH - 1):SUBLANES, :] = cbuf_ref[0]
    xs = _conv_silu(xs_ref[...], tail_sc[:, 0:D_INNER], wconv_ref[:, 0:D_INNER], bconv_ref[:, 0:D_INNER])
    o_b, o_c = D_INNER, D_INNER + SSM_GROUPS * n
    bm = _conv_silu(bm_ref[...], tail_sc[:, o_b:o_c], wconv_ref[:, o_b:o_c], bconv_ref[:, o_b:o_c])
    cm = _conv_silu(cm_ref[...], tail_sc[:, o_c:], wconv_ref[:, o_c:], bconv_ref[:, o_c:])
    bm = bm.astype(BF16).astype(F32)
    cm = cm.astype(BF16).astype(F32)

    dt = _softplus(dt_ref[...] + dtb_ref[...])
    a = -jnp.exp(alog_ref[...])
    tri = (_iota((q, q), 0) >= _iota((q, q), 1)).astype(F32)
    d1, d2, d3 = _split3(dt * a)
    acs = _dot(tri, d1.astype(F32)) + _dot(tri, d2.astype(F32)) + _dot(tri, d3.astype(F32))
    a_end = acs[q - 1:q, :]

    row = _iota((q, LANES), 0)
    prods, lms = [], []
    for s in range(q):
        prods.append(cm * bm[s:s + 1, :])
        keep = row >= s
        lms.append(jnp.where(keep, jnp.exp(jnp.where(keep, acs - acs[s:s + 1, :], 0.0)), 0.0))
    qq = q * q
    cb3 = _dot(jnp.concatenate(_split3(jnp.concatenate(prods, axis=0)), axis=0), gsum_ref[...])
    cbs = cb3[0:qq] + cb3[qq:2 * qq] + cb3[2 * qq:3 * qq]
    gmat = (cbs * jnp.concatenate(lms, axis=0)).astype(BF16)
    quant = jnp.concatenate([dt, jnp.exp(acs), jnp.exp(a_end - acs),
                             jnp.broadcast_to(jnp.exp(a_end), (q, LANES))], axis=0)
    nq = 4 * q
    big = _dot(jnp.concatenate(_split3(quant) + (gmat,), axis=0), expand_ref[...])
    qx = big[0:nq] + big[nq:2 * nq] + big[2 * nq:3 * nq]
    dt_x, e_in_x, e_out_x, e_all_x = (qx[i * q:(i + 1) * q] for i in range(4))
    g_x = big[3 * nq:3 * nq + qq]
    xdt = xs * dt_x
    xdt_r = xdt.astype(BF16).astype(F32)
    y = dskip_ref[...] * xs
    for s in range(q):
        y = y + g_x[s * q:(s + 1) * q, :] * xdt_r[s:s + 1, :]
    xdte = xdt * e_out_x

    eye = (_iota((gw, gw), 0) == _iota((gw, gw), 1)).astype(F32)
    for g in range(SSM_GROUPS):
        gs = slice(g * gw, (g + 1) * gw)
        h0 = h0_ref[0, gs, :]
        bg = bm[:, g * n:(g + 1) * n]
        cg = cm[:, g * n:(g + 1) * n]
        yg = y[:, gs] + _dot_nt(cg, h0) * e_in_x[:, gs]
        u = yg * jax.nn.silu(z_ref[:, gs])
        u = u * lax.rsqrt(jnp.mean(u * u, axis=-1, keepdims=True) + EPS)
        y_ref[:, gs] = u * wnorm_ref[:, gs]
        e1, e2, e3 = _split3(e_all_x[:, gs])
        srow = _iota((q, gw), 0)
        e_rows = jnp.where(srow == 0, e1.astype(F32),
                           jnp.where(srow == 1, e2.astype(F32), jnp.where(srow == 2, e3.astype(F32), 0.0)))
        cols = _dot_nt(eye, jnp.concatenate([xdte[:, gs].astype(BF16).astype(F32), e_rows], axis=0))
        dec = jnp.sum(cols[:, q:2 * q], axis=-1, keepdims=True)
        hout_ref[0, gs, :] = dec * h0 + _dot(cols[:, 0:q], bg)


def _ssm_step(proj, dt_raw, conv_buf, h0, sp, batch, q):
    t = proj.shape[0]
    assert t == batch * q and q == SUBLANES
    col = lambda off, w: off // w
    full = lambda shape: pl.BlockSpec(shape, lambda b: (0,) * len(shape))
    gsum = ((_iota((SSM_GROUPS * SSM_STATE, LANES), 0) // SSM_STATE
             == _iota((SSM_GROUPS * SSM_STATE, LANES), 1) // (SSM_HEADS // SSM_GROUPS))
            & (_iota((SSM_GROUPS * SSM_STATE, LANES), 1) < SSM_HEADS)).astype(BF16)
    return pl.pallas_call(
        _ssm_step_kernel,
        grid=(batch,),
        in_specs=[
            pl.BlockSpec((q, D_INNER), lambda b: (b, col(OFF_Z, D_INNER))),
            pl.BlockSpec((q, D_INNER), lambda b: (b, col(OFF_X, D_INNER))),
            pl.BlockSpec((q, 1024), lambda b: (b, col(OFF_B, 1024))),
            pl.BlockSpec((q, 1024), lambda b: (b, col(OFF_C, 1024))),
            pl.BlockSpec((q, LANES), lambda b: (b, 0)),
            pl.BlockSpec((1, CONV_WIDTH - 1, CONV_DIM), lambda b: (b, 0, 0)),
            pl.BlockSpec((1, D_INNER, SSM_STATE), lambda b: (b, 0, 0)),
            full((CONV_WIDTH, CONV_DIM)),
            full((1, CONV_DIM)),
            full((1, LANES)),
            full((1, LANES)),
            full((1, D_INNER)),
            full((1, D_INNER)),
            full((LANES, D_INNER)),
            full((SSM_GROUPS * SSM_STATE, LANES)),
        ],
        out_specs=[
            pl.BlockSpec((q, D_INNER), lambda b: (b, 0)),
            pl.BlockSpec((1, D_INNER, SSM_STATE), lambda b: (b, 0, 0)),
        ],
        out_shape=[
            jax.ShapeDtypeStruct((t, D_INNER), F32),
            jax.ShapeDtypeStruct((batch, D_INNER, SSM_STATE), F32),
        ],
        scratch_shapes=[pltpu.VMEM((SUBLANES, CONV_DIM), F32)],
        compiler_params=pltpu.CompilerParams(
            dimension_semantics=("parallel",), vmem_limit_bytes=VMEM_LIMIT),
        name="ssm_step",
    )(proj, proj, proj, proj, dt_raw, conv_buf, h0, sp["w_conv"], sp["b_conv"], sp["dt_bias"],
      sp["a_log"], sp["d_skip"], sp["w_norm"], sp["expand"], gsum)


def _prep_params(w_in, norm_mix, w_conv, b_conv, dt_bias, a_log, d_skip, w_ssm_norm, w_q_norm, w_k_norm):
    o_z, o_xbc = 0, D_INNER
    o_dt = o_xbc + CONV_DIM
    o_q = o_dt + SSM_HEADS
    o_k = o_q + ATTN_WIDTH
    o_v = o_k + ATTN_WIDTH
    o_g = o_v + ATTN_WIDTH
    w_main = jnp.concatenate(
        [w_in[:, o_z:o_dt], w_in[:, o_g:], w_in[:, o_q:o_g]], axis=1).astype(BF16)
    w_dt = jnp.pad(w_in[:, o_dt:o_q], ((0, 0), (0, LANES - SSM_HEADS))).astype(BF16)
    qk_w = jnp.ones((N_PROJ_TILES, 1, PROJ_TILE), F32)
    qk_w = qk_w.at[Q_TILE0:K_TILE0, 0].set(jnp.tile(w_q_norm, HEADS_PER_GROUP)[None])
    qk_w = qk_w.at[K_TILE0:V_TILE0, 0].set(jnp.tile(w_k_norm, HEADS_PER_GROUP)[None])
    pad_h = lambda v: jnp.pad(v, (0, LANES - SSM_HEADS))[None]
    expand = (_iota((LANES, D_INNER), 0) == _iota((LANES, D_INNER), 1) // SSM_HEAD_DIM).astype(BF16)
    ssm = dict(w_conv=w_conv, b_conv=b_conv[None], dt_bias=pad_h(dt_bias), a_log=pad_h(a_log),
               d_skip=jnp.repeat(d_skip, SSM_HEAD_DIM)[None], w_norm=w_ssm_norm[None], expand=expand)
    return dict(w_main=w_main, w_dt=w_dt, qk_w=qk_w, norm_mix=norm_mix[None], ssm=ssm)


def _attn_prompt_kernel(q_ref, kp_ref, kc_ref, vp_ref, vc_ref, o_ref, lse_ref, *, span):
    nblk = pl.program_id(1)
    blk = q_ref.shape[0]
    qi = _iota((blk, 2 * blk), 0)
    kc = _iota((blk, 2 * blk), 1)
    dist = blk + qi - kc
    mask = (dist >= 0) & (dist <= span) & ((kc >= blk) | (nblk > 0))
    for h in range(HEADS_PER_GROUP):
        hs = slice(h * ATTN_HEAD_DIM, (h + 1) * ATTN_HEAD_DIM)
        qh = q_ref[:, hs].astype(BF16)
        kh = jnp.concatenate([kp_ref[:, hs], kc_ref[:, hs]], axis=0).astype(BF16)
        vh = jnp.concatenate([vp_ref[:, hs], vc_ref[:, hs]], axis=0).astype(BF16)
        s = jnp.where(mask, _dot_nt(qh, kh), -jnp.inf)
        m = jnp.max(s, axis=-1, keepdims=True)
        p = jnp.exp(s - m)
        den = jnp.sum(p, axis=-1, keepdims=True)
        o_ref[:, hs] = _dot((p / den).astype(BF16), vh)
        lse_ref[:, hs] = jnp.broadcast_to(m + jnp.log(den), (blk, ATTN_HEAD_DIM))


def _attn_prompt(proj, gi):
    window, dil = ATTN_PATTERNS[gi]
    s = proj.shape[0]
    n = s // dil
    assert s % dil == 0 and n % ATTN_BLOCK == 0 and window // dil <= ATTN_BLOCK
    nb = n // ATTN_BLOCK
    view = proj.reshape(n, dil * PROJ_WIDTH)
    col = lambda tile0: (lambda r, i: (i, r * N_PROJ_TILES + tile0 + gi))
    colp = lambda tile0: (lambda r, i: (jnp.maximum(i - 1, 0), r * N_PROJ_TILES + tile0 + gi))
    blk = (ATTN_BLOCK, ATTN_GROUP_WIDTH)
    o, lse = pl.pallas_call(
        functools.partial(_attn_prompt_kernel, span=window // dil),
        grid=(dil, nb),
        in_specs=[
            pl.BlockSpec(blk, col(Q_TILE0)),
            pl.BlockSpec(blk, colp(K_TILE0)),
            pl.BlockSpec(blk, col(K_TILE0)),
            pl.BlockSpec(blk, colp(V_TILE0)),
            pl.BlockSpec(blk, col(V_TILE0)),
        ],
        out_specs=[pl.BlockSpec(blk, lambda r, i: (i, r))] * 2,
        out_shape=[jax.ShapeDtypeStruct((n, dil * ATTN_GROUP_WIDTH), F32)] * 2,
        compiler_params=pltpu.CompilerParams(
            dimension_semantics=("parallel", "arbitrary"), vmem_limit_bytes=VMEM_LIMIT),
        name=f"attn_prompt_g{gi}",
    )(view, view, view, view, view)
    return o.reshape(s, ATTN_GROUP_WIDTH), lse.reshape(s, ATTN_GROUP_WIDTH)


def _attn_sample_kernel(q_ref, k_ref, v_ref, kc_ref, vc_ref, o_ref, lse_ref, ko_ref, vo_ref,
                        *, span, dil, pos0):
    l = q_ref.shape[0]
    nh, hd = HEADS_PER_GROUP, ATTN_HEAD_DIM
    rows = kc_ref.shape[3]
    q, kn, vn = q_ref[...], k_ref[...], v_ref[...]
    t_c = _iota((l, rows), 0)
    delta_c = rows + t_c - _iota((l, rows), 1)
    ok_c = ((delta_c <= span * dil) & ((delta_c & (dil - 1)) == 0) & (pos0 + t_c - delta_c >= 0))
    delta_n = _iota((l, l), 0) - _iota((l, l), 1)
    ok_n = (delta_n >= 0) & (delta_n <= span * dil) & ((delta_n & (dil - 1)) == 0)
    for h in range(nh):
        hs = slice(h * hd, (h + 1) * hd)
        qh = q[:, hs]
        s_c = jnp.where(ok_c, _dot(qh, kc_ref[0, h]), -jnp.inf)
        s_n = jnp.where(ok_n, _dot_nt(qh, kn[:, hs]), -jnp.inf)
        m = jnp.maximum(jnp.max(s_c, axis=-1, keepdims=True), jnp.max(s_n, axis=-1, keepdims=True))
        p_c = jnp.exp(s_c - m)
        p_n = jnp.exp(s_n - m)
        den = jnp.sum(p_c, axis=-1, keepdims=True) + jnp.sum(p_n, axis=-1, keepdims=True)
        o_ref[:, hs] = _dot_nt(p_c / den, vc_ref[0, h]) + _dot(p_n / den, vn[:, hs])
        lse_ref[:, hs] = jnp.broadcast_to(m + jnp.log(den), (l, hd))

    w = nh * hd
    eye = (_iota((w, w), 0) == _iota((w, w), 1)).astype(F32)
    lane = _iota((hd, LANES), 1)
    for new, c_ref, out_ref in ((kn, kc_ref, ko_ref), (vn, vc_ref, vo_ref)):
        padded = jnp.concatenate([jnp.zeros((LANES - l, w), F32), new], axis=0)
        p1, p2, p3 = _split3(padded)
        tail = (_dot_nt(eye, p1.astype(F32)) + _dot_nt(eye, p2.astype(F32))
                + _dot_nt(eye, p3.astype(F32)))
        for h in range(nh):
            shifted = pltpu.roll(c_ref[0, h], rows - l, axis=1)
            if rows > LANES:
                out_ref[0, h, :, 0:rows - LANES] = shifted[:, 0:rows - LANES]
            out_ref[0, h, :, rows - LANES:rows] = jnp.where(
                lane < LANES - l, shifted[:, rows - LANES:rows], tail[h * hd:(h + 1) * hd])


def _attn_sample(proj, k_cache, v_cache, gi, batch, pos0):
    window, dil = ATTN_PATTERNS[gi]
    rows = k_cache.shape[1]
    l = proj.shape[0] // batch
    assert l == SUBLANES and rows % LANES == 0
    tok = lambda tile0: pl.BlockSpec((l, ATTN_GROUP_WIDTH), lambda b: (b, tile0 + gi))
    cache = pl.BlockSpec((1, HEADS_PER_GROUP, ATTN_HEAD_DIM, rows), lambda b: (b, 0, 0, 0))
    out_tok = pl.BlockSpec((l, ATTN_GROUP_WIDTH), lambda b: (b, 0))
    to_t = lambda c: jnp.transpose(c, (0, 2, 3, 1))
    t_shape = jax.ShapeDtypeStruct((batch, HEADS_PER_GROUP, ATTN_HEAD_DIM, rows), F32)
    o, lse, k_t, v_t = pl.pallas_call(
        functools.partial(_attn_sample_kernel, span=window // dil, dil=dil, pos0=pos0),
        grid=(batch,),
        in_specs=[tok(Q_TILE0), tok(K_TILE0), tok(V_TILE0), cache, cache],
        out_specs=[out_tok, out_tok, cache, cache],
        out_shape=[jax.ShapeDtypeStruct((batch * l, ATTN_GROUP_WIDTH), F32)] * 2 + [t_shape] * 2,
        compiler_params=pltpu.CompilerParams(
            dimension_semantics=("parallel",), vmem_limit_bytes=VMEM_LIMIT),
        name=f"attn_sample_g{gi}",
    )(proj, proj, proj, to_t(k_cache), to_t(v_cache))
    from_t = lambda c: jnp.transpose(c, (0, 3, 1, 2))
    return o, lse, from_t(k_t), from_t(v_t)


def _merge_kernel(x_ref, ys_ref, o0_ref, o1_ref, o2_ref, l0_ref, l1_ref, l2_ref, g0_ref, g1_ref,
                  wbs_ref, wba_ref, wo_ref, nf_ref, wr1_ref, wr2_ref, br_ref,
                  x1_ref, xn_ref, lg_ref):
    l0, l1, l2 = l0_ref[...], l1_ref[...], l2_ref[...]
    m = jnp.maximum(jnp.maximum(l0, l1), l2)
    e0, e1, e2 = jnp.exp(l0 - m), jnp.exp(l1 - m), jnp.exp(l2 - m)
    den = e0 + e1 + e2
    y_attn = (e0 / den) * o0_ref[...] + (e1 / den) * o1_ref[...] + (e2 / den) * o2_ref[...]
    br_s = _dot(ys_ref[...].astype(BF16), wbs_ref[...])
    br_a = _dot(y_attn.astype(BF16), wba_ref[...])
    merged = jax.nn.sigmoid(g0_ref[...]) * br_s + jax.nn.sigmoid(g1_ref[...]) * br_a
    x1 = x_ref[...] + _dot(merged.astype(BF16), wo_ref[...])
    x1_ref[...] = x1
    xn = x1 * lax.rsqrt(jnp.mean(x1 * x1, axis=-1, keepdims=True) + EPS) * nf_ref[...]
    xn1 = xn.astype(BF16)
    xn_ref[...] = xn1
    xn2 = (xn - xn1.astype(F32)).astype(BF16)
    lg_ref[...] = (_dot(xn1, wr1_ref[...]) + _dot(xn2, wr1_ref[...]) + _dot(xn1, wr2_ref[...])
                   + br_ref[...])


def _merge(x, proj, y_ssm, attn, mp, tm):
    t = x.shape[0]
    row = lambda w: pl.BlockSpec((tm, w), lambda i: (i, 0))
    full = lambda a: pl.BlockSpec(a.shape, lambda i: (0,) * a.ndim)
    (o0, l0), (o1, l1), (o2, l2) = attn
    weights = [mp["w_br_ssm"], mp["w_br_attn"], mp["w_out"], mp["norm_ffn"], mp["w_r1"], mp["w_r2"], mp["b_r"]]
    return pl.pallas_call(
        _merge_kernel,
        grid=(t // tm,),
        in_specs=[row(D_MODEL), row(D_INNER)] + [row(ATTN_GROUP_WIDTH)] * 6 + [
            pl.BlockSpec((tm, D_MODEL), lambda i: (i, OFF_G // D_MODEL)),
            pl.BlockSpec((tm, D_MODEL), lambda i: (i, OFF_G // D_MODEL + 1)),
        ] + [full(w) for w in weights],
        out_specs=[row(D_MODEL), row(D_MODEL), row(ROUTER_WIDTH)],
        out_shape=[
            jax.ShapeDtypeStruct((t, D_MODEL), F32),
            jax.ShapeDtypeStruct((t, D_MODEL), BF16),
            jax.ShapeDtypeStruct((t, ROUTER_WIDTH), F32),
        ],
        compiler_params=pltpu.CompilerParams(
            dimension_semantics=("parallel",), vmem_limit_bytes=VMEM_LIMIT),
        name="merge",
    )(x, y_ssm, o0, o1, o2, l0, l1, l2, proj, proj, *weights)


def _prep_merge_params(w_br_ssm, w_br_attn, w_out, norm_ffn, w_router_group, b_router_group,
                       w_router_expert, b_router_expert):
    w_r = jnp.concatenate([w_router_group, w_router_expert.reshape(D_MODEL, N_EXPERTS)], axis=1)
    w_r = jnp.pad(w_r, ((0, 0), (0, ROUTER_WIDTH - w_r.shape[1])))
    w_r1 = w_r.astype(BF16)
    w_r2 = (w_r - w_r1.astype(F32)).astype(BF16)
    b_r = jnp.concatenate([b_router_group, b_router_expert.reshape(N_EXPERTS)])
    b_r = jnp.pad(b_r, (0, ROUTER_WIDTH - b_r.shape[0]))[None]
    return dict(w_br_ssm=w_br_ssm.astype(BF16), w_br_attn=w_br_attn.astype(BF16), w_out=w_out.astype(BF16),
                norm_ffn=norm_ffn[None], w_r1=w_r1, w_r2=w_r2, b_r=b_r)


def _route(logits):
    lane = _iota(logits.shape, 1)
    big = jnp.int32(2 ** 30)
    neg = -jnp.inf
    gl = jnp.where(lane < N_EXPERT_GROUPS, logits, neg)
    gmax = jnp.max(gl, axis=-1, keepdims=True)
    gidx = jnp.min(jnp.where(gl == gmax, lane, big), axis=-1, keepdims=True)
    g_w = 1.0 / jnp.sum(jnp.exp(gl - gmax), axis=-1, keepdims=True)
    lo = N_EXPERT_GROUPS + gidx * EXPERTS_PER_GROUP
    el = jnp.where((lane >= lo) & (lane < lo + EXPERTS_PER_GROUP), logits, neg)
    v1 = jnp.max(el, axis=-1, keepdims=True)
    i1 = jnp.min(jnp.where(el == v1, lane, big), axis=-1, keepdims=True)
    el2 = jnp.where(lane == i1, neg, el)
    v2 = jnp.max(el2, axis=-1, keepdims=True)
    i2 = jnp.min(jnp.where(el2 == v2, lane, big), axis=-1, keepdims=True)
    e2 = jnp.exp(v2 - v1)
    w1 = (1.0 / (1.0 + e2)) * g_w
    w2 = (e2 / (1.0 + e2)) * g_w
    return jnp.where(lane == i1, w1, jnp.where(lane == i2, w2, 0.0)), gidx


MOE_BLOCK = 256


def _moe_kernel(x1_ref, xn_ref, lg_ref, wg_ref, wu_ref, wd_ref, out_ref,
                comb_sc, oh_sc, cum_sc, oht_sc, cumt_sc):
    g = pl.program_id(1)
    tm = x1_ref.shape[0]
    blk = MOE_BLOCK

    @pl.when(g == 0)
    def _():
        comb, gidx = _route(lg_ref[...])
        c1, c2, c3 = _split3(comb)
        comb_sc[0] = c1
        comb_sc[1] = c2
        comb_sc[2] = c3
        onehot = (_iota((tm, LANES), 1) == gidx).astype(F32)
        oh_bf = onehot.astype(BF16)
        lower = (_iota((tm, tm), 0) > _iota((tm, tm), 1)).astype(BF16)
        upper = (_iota((tm, tm), 0) < _iota((tm, tm), 1)).astype(BF16)
        oh_sc[...] = onehot
        cum_sc[...] = _dot(lower, oh_bf)
        oht_sc[...] = onehot.T[0:SUBLANES]
        cumt_sc[...] = _dot_tn(oh_bf, upper)[0:SUBLANES]
        out_ref[...] = x1_ref[...]

    lane = _iota((tm, LANES), 1)
    pick = lane == g
    member_c = jnp.sum(jnp.where(pick, oh_sc[...], 0.0), axis=-1, keepdims=True)
    rank_c = jnp.sum(jnp.where(pick, cum_sc[...], 0.0), axis=-1, keepdims=True)
    member_r = oht_sc[pl.ds(g, 1), :]
    rank_r = cumt_sc[pl.ds(g, 1), :]
    count = jnp.sum(member_r).astype(jnp.int32)
    nblk = (count + (blk - 1)) // blk
    xn = xn_ref[...]
    wd = wd_ref[...].reshape(EXPERTS_PER_GROUP * D_EXPERT, D_MODEL)
    lane_b = _iota((blk, LANES), 1)

    def body(k, carry):
        base = (k * blk).astype(F32)
        want_r = base + _iota((blk, tm), 0).astype(F32)
        sel = jnp.where((rank_r == want_r) & (member_r > 0.0), 1.0, 0.0).astype(BF16)
        want_c = base + _iota((tm, blk), 1).astype(F32)
        sel_t = jnp.where((rank_c == want_c) & (member_c > 0.0), 1.0, 0.0).astype(BF16)
        xg = _dot(sel, xn).astype(BF16)
        cg = _dot(sel, comb_sc[0]) + _dot(sel, comb_sc[1]) + _dot(sel, comb_sc[2])
        hidden = []
        for e in range(EXPERTS_PER_GROUP):
            col = N_EXPERT_GROUPS + g * EXPERTS_PER_GROUP + e
            cw = jnp.sum(jnp.where(lane_b == col, cg, 0.0), axis=-1, keepdims=True)
            h = jax.nn.silu(_dot(xg, wg_ref[e])) * _dot(xg, wu_ref[e]) * cw
            hidden.append(h.astype(BF16))
        out = _dot(jnp.concatenate(hidden, axis=1), wd)
        o1 = out.astype(BF16)
        o2 = (out - o1.astype(F32)).astype(BF16)
        out_ref[...] += _dot(sel_t, o1) + _dot(sel_t, o2)
        return carry

    lax.fori_loop(0, nblk, body, 0)


def _moe(x1, xn, logits, wg, wu, wd, tm):
    t = x1.shape[0]
    ge = EXPERTS_PER_GROUP
    return pl.pallas_call(
        _moe_kernel,
        grid=(t // tm, N_EXPERT_GROUPS),
        in_specs=[
            pl.BlockSpec((tm, D_MODEL), lambda i, g: (i, 0), pipeline_mode=pl.Buffered(1)),
            pl.BlockSpec((tm, D_MODEL), lambda i, g: (i, 0)),
            pl.BlockSpec((tm, ROUTER_WIDTH), lambda i, g: (i, 0), pipeline_mode=pl.Buffered(1)),
            pl.BlockSpec((ge, D_MODEL, D_EXPERT), lambda i, g: (g, 0, 0)),
            pl.BlockSpec((ge, D_MODEL, D_EXPERT), lambda i, g: (g, 0, 0)),
            pl.BlockSpec((ge, D_EXPERT, D_MODEL), lambda i, g: (g, 0, 0)),
        ],
        out_specs=pl.BlockSpec((tm, D_MODEL), lambda i, g: (i, 0)),
        out_shape=jax.ShapeDtypeStruct((t, D_MODEL), F32),
        scratch_shapes=[
            pltpu.VMEM((3, tm, ROUTER_WIDTH), BF16),
            pltpu.VMEM((tm, LANES), F32),
            pltpu.VMEM((tm, LANES), F32),
            pltpu.VMEM((SUBLANES, tm), F32),
            pltpu.VMEM((SUBLANES, tm), F32),
        ],
        compiler_params=pltpu.CompilerParams(
            dimension_semantics=("parallel", "arbitrary"), vmem_limit_bytes=VMEM_LIMIT),
        name="moe",
    )(x1, xn, logits, wg, wu, wd)


def _moe_dense_kernel(x1_ref, xn_ref, lg_ref, wg_ref, wu_ref, wd_ref, out_ref, comb_sc, acc_sc):
    e = pl.program_id(1)

    @pl.when(e == 0)
    def _():
        comb_sc[...] = _route(lg_ref[...])[0]
        acc_sc[...] = jnp.zeros_like(acc_sc)

    lane = _iota(comb_sc.shape, 1)
    cw = jnp.sum(jnp.where(lane == e + N_EXPERT_GROUPS, comb_sc[...], 0.0), axis=-1, keepdims=True)
    xn = xn_ref[...]
    hidden = jax.nn.silu(_dot(xn, wg_ref[0])) * _dot(xn, wu_ref[0]) * cw
    acc_sc[...] += _dot(hidden.astype(BF16), wd_ref[0])

    @pl.when(e == pl.num_programs(1) - 1)
    def _():
        out_ref[...] = x1_ref[...] + acc_sc[...]


def _moe_dense(x1, xn, logits, wg, wu, wd, tm):
    t = x1.shape[0]
    return pl.pallas_call(
        _moe_dense_kernel,
        grid=(t // tm, N_EXPERTS),
        in_specs=[
            pl.BlockSpec((tm, D_MODEL), lambda i, e: (i, 0)),
            pl.BlockSpec((tm, D_MODEL), lambda i, e: (i, 0)),
            pl.BlockSpec((tm, ROUTER_WIDTH), lambda i, e: (i, 0)),
            pl.BlockSpec((1, D_MODEL, D_EXPERT), lambda i, e: (e, 0, 0)),
            pl.BlockSpec((1, D_MODEL, D_EXPERT), lambda i, e: (e, 0, 0)),
            pl.BlockSpec((1, D_EXPERT, D_MODEL), lambda i, e: (e, 0, 0)),
        ],
        out_specs=pl.BlockSpec((tm, D_MODEL), lambda i, e: (i, 0)),
        out_shape=jax.ShapeDtypeStruct((t, D_MODEL), F32),
        scratch_shapes=[pltpu.VMEM((tm, ROUTER_WIDTH), F32), pltpu.VMEM((tm, D_MODEL), F32)],
        compiler_params=pltpu.CompilerParams(
            dimension_semantics=("parallel", "arbitrary"), vmem_limit_bytes=VMEM_LIMIT),
        name="moe_dense",
    )(x1, xn, logits, wg, wu, wd)


def _tile_rows(t, cap):
    tm = min(t, cap)
    assert t % tm == 0
    return tm


def _decoder_layer(x, conv_buf, ssm_state, kv_bufs, pos0, pp, mp, ep):
    b, l, d = x.shape
    t = b * l
    x2 = x.reshape(t, d)
    proj, dt_raw = _in_proj(x2, pp["norm_mix"], pp["w_main"], pp["w_dt"], pp["qk_w"], _tile_rows(t, 1024))
    proj3 = proj.reshape(b, l, PROJ_WIDTH)
    h0 = ssm_state.reshape(b, D_INNER, SSM_STATE)
    if kv_bufs is None:
        assert b == 1 and l % SSD_CHUNK == 0
        y_ssm, h_new = _ssm_chunked(proj, dt_raw, conv_buf, h0, pp["ssm"], b, SSD_CHUNK)
        attn = [_attn_prompt(proj, gi) for gi in range(N_ATTN_GROUPS)]
        new_kv = []
        for gi, (window, _) in enumerate(ATTN_PATTERNS):
            keep = min(window, l)
            for off in (OFF_K, OFF_V):
                c0 = off + gi * ATTN_GROUP_WIDTH
                new_kv.append(proj3[:, l - keep:, c0:c0 + ATTN_GROUP_WIDTH]
                              .reshape(b, keep, HEADS_PER_GROUP, ATTN_HEAD_DIM))
    else:
        y_ssm, h_new = _ssm_step(proj, dt_raw, conv_buf, h0, pp["ssm"], b, l)
        attn, new_kv = [], []
        for gi in range(N_ATTN_GROUPS):
            kc, vc = kv_bufs[2 * gi], kv_bufs[2 * gi + 1]
            o, lse, kn, vn = _attn_sample(proj, kc, vc, gi, b, pos0)
            attn.append((o, lse))
            new_kv += [kn, vn]
    new_conv = proj3[:, l - (CONV_WIDTH - 1):, OFF_X:OFF_X + CONV_DIM]
    tm = _tile_rows(t, 512)
    x1, xn, logits = _merge(x2, proj, y_ssm, attn, mp, tm)
    y = _moe(x1, xn, logits, ep["wg"], ep["wu"], ep["wd"], _tile_rows(t, 1024))
    new_ssm = h_new.reshape(b, SSM_HEADS, SSM_HEAD_DIM, SSM_STATE)
    return y.reshape(b, l, d), (new_conv, new_ssm) + tuple(new_kv)


def kernel(x_prompt, x_sample, cache_conv, state_ssm, cache_k_w128, cache_v_w128, cache_k_w512, cache_v_w512,
           cache_k_w2048, cache_v_w2048, w_in, norm_mix, w_conv, b_conv, dt_bias, a_log, d_skip, w_ssm_norm,
           w_q_norm, w_k_norm, w_br_ssm, w_br_attn, w_out, norm_ffn, w_router_group, b_router_group,
           w_router_expert, b_router_expert, w_expert_gate, w_expert_up, w_expert_down):
    depth = w_in.shape[0]
    yp, ys = x_prompt, x_sample
    bp = x_prompt.shape[0]
    prompt_states, sample_states = [], []
    for i in range(depth):
        pp = _prep_params(w_in[i], norm_mix[i], w_conv[i], b_conv[i], dt_bias[i], a_log[i], d_skip[i],
                          w_ssm_norm[i], w_q_norm[i], w_k_norm[i])
        mp = _prep_merge_params(w_br_ssm[i], w_br_attn[i], w_out[i], norm_ffn[i], w_router_group[i],
                                b_router_group[i], w_router_expert[i], b_router_expert[i])
        ep = dict(wg=w_expert_gate[i].astype(BF16), wu=w_expert_up[i].astype(BF16),
                  wd=w_expert_down[i].astype(BF16))
        zero_conv = jnp.zeros((bp, CONV_WIDTH - 1, CONV_DIM), x_prompt.dtype)
        zero_ssm = jnp.zeros((bp, SSM_HEADS, SSM_HEAD_DIM, SSM_STATE), F32)
        yp, st_p = _decoder_layer(yp, zero_conv, zero_ssm, None, 0, pp, mp, ep)
        kv_bufs = (cache_k_w128[i], cache_v_w128[i], cache_k_w512[i], cache_v_w512[i],
                   cache_k_w2048[i], cache_v_w2048[i])
        ys, st_s = _decoder_layer(ys, cache_conv[i], state_ssm[i], kv_bufs, PAST_LEN, pp, mp, ep)
        prompt_states.append(st_p)
        sample_states.append(st_s)
    stack = lambda states: [jnp.stack([s[j] for s in states]) for j in range(8)]
    return (yp, ys) + tuple(stack(prompt_states)) + tuple(stack(sample_states))
```

```python
import functools
import math

import jax
import jax.numpy as jnp
from jax import lax
from jax.experimental import pallas as pl
from jax.experimental.pallas import tpu as pltpu

F32 = jnp.float32
BF16 = jnp.bfloat16

PAST_LEN = 8192
D_MODEL = 1024
D_INNER = 2048
SSM_HEAD_DIM = 64
SSM_HEADS = 32
SSM_GROUPS = 8
SSM_STATE = 128
CONV_WIDTH = 4
CONV_DIM = D_INNER + 2 * SSM_GROUPS * SSM_STATE
SSD_CHUNK = 128
ATTN_HEAD_DIM = 64
HEADS_PER_GROUP = 8
ATTN_PATTERNS = ((128, 1), (512, 4), (2048, 16))
N_ATTN_GROUPS = 3
ATTN_GROUP_WIDTH = HEADS_PER_GROUP * ATTN_HEAD_DIM
ATTN_WIDTH = N_ATTN_GROUPS * ATTN_GROUP_WIDTH
ATTN_BLOCK = 128
N_EXPERT_GROUPS = 4
EXPERTS_PER_GROUP = 8
N_EXPERTS = 32
D_EXPERT = 256
EPS = 1e-6

LANES = 128
SUBLANES = 8
VMEM_LIMIT = 56 * 1024 * 1024

PROJ_TILE = 512
OFF_Z = 0
OFF_X = 2048
OFF_B = 4096
OFF_C = 5120
OFF_G = 6144
OFF_Q = 8192
OFF_K = OFF_Q + ATTN_WIDTH
OFF_V = OFF_K + ATTN_WIDTH
PROJ_WIDTH = OFF_V + ATTN_WIDTH
N_PROJ_TILES = PROJ_WIDTH // PROJ_TILE
Q_TILE0 = OFF_Q // PROJ_TILE
K_TILE0 = OFF_K // PROJ_TILE
V_TILE0 = OFF_V // PROJ_TILE
ROUTER_WIDTH = LANES


def _split3(x):
    x1 = x.astype(BF16)
    r1 = x - x1.astype(F32)
    x2 = r1.astype(BF16)
    r2 = r1 - x2.astype(F32)
    return x1, x2, r2.astype(BF16)


def _dot(a, b):
    return jnp.dot(a, b, preferred_element_type=F32)


def _dot3(x, m):
    x1, x2, x3 = _split3(x)
    return _dot(x1, m) + _dot(x2, m) + _dot(x3, m)


def _dot3f(x, m):
    x1, x2, x3 = _split3(x)
    return _dot(x1.astype(F32), m) + _dot(x2.astype(F32), m) + _dot(x3.astype(F32), m)


def _dot_nt(a, b):
    return lax.dot_general(a, b, (((1,), (1,)), ((), ())), preferred_element_type=F32)


def _dot_tn(a, b):
    return lax.dot_general(a, b, (((0,), (0,)), ((), ())), preferred_element_type=F32)


def _iota(shape, dim):
    return lax.broadcasted_iota(jnp.int32, shape, dim)


def _shr(x, pow2):
    assert pow2 & (pow2 - 1) == 0
    return lax.shift_right_logical(x, jnp.int32(int(math.log2(pow2))))


def _in_proj_kernel(x_ref, nw_ref, w_ref, wdt_ref, qkw_ref, seg_ref, out_ref, dt_ref, h_sc):
    j = pl.program_id(1)

    @pl.when(j == 0)
    def _():
        xf = x_ref[...]
        y = xf * lax.rsqrt(jnp.mean(xf * xf, axis=-1, keepdims=True) + EPS)
        h = (y * nw_ref[...]).astype(BF16)
        h_sc[...] = h
        dt_ref[...] = _dot(h, wdt_ref[...])

    acc = _dot(h_sc[...], w_ref[...])
    is_qk = jnp.logical_and(j >= Q_TILE0, j < V_TILE0)

    @pl.when(jnp.logical_not(is_qk))
    def _():
        out_ref[...] = acc

    @pl.when(is_qk)
    def _():
        sq = acc * acc
        s1 = sq.astype(BF16)
        s2 = (sq - s1.astype(F32)).astype(BF16)
        ms = (_dot(s1, seg_ref[...]) + _dot(s2, seg_ref[...])) * (1.0 / ATTN_HEAD_DIM)
        scale = jnp.where(j < K_TILE0, ATTN_HEAD_DIM ** -0.5, 1.0).astype(F32)
        w = qkw_ref[0] * scale
        out_ref[...] = acc * lax.rsqrt(ms + EPS) * w


def _in_proj(x, norm_w, w_bf, wdt_bf, qk_w, tm):
    t = x.shape[0]
    seg = (_iota((PROJ_TILE, PROJ_TILE), 0) // ATTN_HEAD_DIM
           == _iota((PROJ_TILE, PROJ_TILE), 1) // ATTN_HEAD_DIM).astype(BF16)
    return pl.pallas_call(
        _in_proj_kernel,
        grid=(t // tm, N_PROJ_TILES),
        in_specs=[
            pl.BlockSpec((tm, D_MODEL), lambda i, j: (i, 0)),
            pl.BlockSpec((1, D_MODEL), lambda i, j: (0, 0)),
            pl.BlockSpec((D_MODEL, PROJ_TILE), lambda i, j: (0, j)),
            pl.BlockSpec((D_MODEL, LANES), lambda i, j: (0, 0)),
            pl.BlockSpec((1, 1, PROJ_TILE), lambda i, j: (j, 0, 0)),
            pl.BlockSpec((PROJ_TILE, PROJ_TILE), lambda i, j: (0, 0)),
        ],
        out_specs=[
            pl.BlockSpec((tm, PROJ_TILE), lambda i, j: (i, j)),
            pl.BlockSpec((tm, LANES), lambda i, j: (i, 0)),
        ],
        out_shape=[
            jax.ShapeDtypeStruct((t, PROJ_WIDTH), F32),
            jax.ShapeDtypeStruct((t, LANES), F32),
        ],
        scratch_shapes=[pltpu.VMEM((tm, D_MODEL), BF16)],
        compiler_params=pltpu.CompilerParams(
            dimension_semantics=("parallel", "arbitrary"), vmem_limit_bytes=VMEM_LIMIT),
        name="in_proj",
    )(x, norm_w, w_bf, wdt_bf, qk_w, seg)


def _softplus(x):
    return jnp.maximum(x, 0.0) + jnp.log1p(jnp.exp(-jnp.abs(x)))


def _conv_silu(raw, tail, w, b):
    q = raw.shape[0]
    up = jnp.concatenate([tail, raw], axis=0)
    acc = b + w[CONV_WIDTH - 1:CONV_WIDTH] * raw
    for tap in range(CONV_WIDTH - 1):
        shifted = pltpu.roll(up, CONV_WIDTH - 1 - tap, axis=0)[SUBLANES:SUBLANES + q]
        acc = acc + w[tap:tap + 1] * shifted
    return jax.nn.silu(acc)


def _ssm_chunk_kernel(z_ref, xs_ref, bm_ref, cm_ref, dt_ref, cbuf_ref, h0_ref, wconv_ref, bconv_ref,
                      dtb_ref, alog_ref, dskip_ref, wnorm_ref, expand_ref,
                      y_ref, hout_ref, tail_sc, xbc_sc, ht_sc):
    c = pl.program_id(1)
    nc = pl.num_programs(1)
    q = xs_ref.shape[0]
    gw = D_INNER // SSM_GROUPS
    n = SSM_STATE

    @pl.when(c == 0)
    def _():
        tail_sc[...] = jnp.zeros_like(tail_sc)
        tail_sc[SUBLANES - (CONV_WIDTH - 1):SUBLANES, :] = cbuf_ref[0]
        for g in range(SSM_GROUPS):
            ht_sc[g] = h0_ref[0, g * gw:(g + 1) * gw, :].T

    cw = PROJ_TILE
    for k in range(CONV_DIM // cw):
        if k < D_INNER // cw:
            raw = xs_ref[:, k * cw:(k + 1) * cw]
        elif k < (D_INNER + SSM_GROUPS * n) // cw:
            kk = k - D_INNER // cw
            raw = bm_ref[:, kk * cw:(kk + 1) * cw]
        else:
            kk = k - (D_INNER + SSM_GROUPS * n) // cw
            raw = cm_ref[:, kk * cw:(kk + 1) * cw]
        sl = slice(k * cw, (k + 1) * cw)
        xbc_sc[:, sl] = _conv_silu(raw, tail_sc[:, sl], wconv_ref[:, sl], bconv_ref[:, sl])
        tail_sc[:, sl] = raw[q - SUBLANES:q]

    dt = _softplus(dt_ref[...] + dtb_ref[...])
    a = -jnp.exp(alog_ref[...])
    dta = dt * a
    tri = (_iota((q, q), 0) >= _iota((q, q), 1)).astype(BF16)
    d1, d2, d3 = _split3(dta)
    acs = _dot(tri, d1) + _dot(tri, d2) + _dot(tri, d3)
    eye = (_iota((LANES, LANES), 0) == _iota((LANES, LANES), 1)).astype(BF16)
    c1, c2, c3 = _split3(acs)
    acs_t = _dot_nt(eye, c1) + _dot_nt(eye, c2) + _dot_nt(eye, c3)
    a_end = acs[q - 1:q, :]
    e_in = jnp.exp(acs)
    e_out = jnp.exp(a_end - acs)
    e_all = jnp.exp(a_end)

    ex = expand_ref[...]
    dt_x = _dot3(dt, ex)
    e_in_x = _dot3(e_in, ex)
    e_out_x = _dot3(e_out, ex)
    e_all_x = _dot3(jnp.broadcast_to(e_all, (2 * SUBLANES, LANES)), ex)[0:1]

    causal = _iota((q, q), 0) >= _iota((q, q), 1)
    lane_lo = _iota((q, LANES), 1) < SSM_HEAD_DIM

    for g in range(SSM_GROUPS):
        gs = slice(g * gw, (g + 1) * gw)
        xg = xbc_sc[:, gs]
        bg = xbc_sc[:, D_INNER + g * n:D_INNER + (g + 1) * n].astype(BF16)
        cg = xbc_sc[:, D_INNER + SSM_GROUPS * n + g * n:D_INNER + SSM_GROUPS * n + (g + 1) * n].astype(BF16)
        xdt = xg * dt_x[:, gs]
        cb = _dot_nt(cg, bg)
        ht = ht_sc[g]
        y_off = _dot(cg, ht.astype(BF16)) * e_in_x[:, gs]
        pieces = []
        for pair in range(2):
            ms = []
            for hh in range(2):
                h = g * 4 + pair * 2 + hh
                diff = acs[:, h:h + 1] - acs_t[h:h + 1, :]
                lmat = jnp.where(causal, jnp.exp(jnp.where(causal, diff, 0.0)), 0.0)
                ms.append((cb * lmat).astype(BF16))
            xp = xdt[:, pair * LANES:(pair + 1) * LANES]
            rhs = jnp.concatenate([jnp.where(lane_lo, xp, 0.0), jnp.where(lane_lo, 0.0, xp)],
                                  axis=0).astype(BF16)
            pieces.append(_dot(jnp.concatenate(ms, axis=1), rhs))
        y_diag = jnp.concatenate(pieces, axis=1)
        yg = y_diag + y_off + dskip_ref[:, gs] * xg
        zg = z_ref[:, gs]
        u = yg * jax.nn.silu(zg)
        u = u * lax.rsqrt(jnp.mean(u * u, axis=-1, keepdims=True) + EPS)
        y_ref[:, gs] = (u * wnorm_ref[:, gs]).astype(y_ref.dtype)
        st = _dot_tn(bg, (xdt * e_out_x[:, gs]).astype(BF16))
        ht_sc[g] = e_all_x[:, gs] * ht + st

    @pl.when(c == nc - 1)
    def _():
        for g in range(SSM_GROUPS):
            hout_ref[0, g * gw:(g + 1) * gw, :] = ht_sc[g].T


def _ssm_chunked(proj, dt_raw, conv_buf, h0, sp, batch, q):
    t = proj.shape[0]
    nc = t // batch // q
    row = lambda b, c: b * nc + c
    col = lambda off, w: off // w
    full = lambda shape: pl.BlockSpec(shape, lambda b, c: (0,) * len(shape))
    return pl.pallas_call(
        _ssm_chunk_kernel,
        grid=(batch, nc),
        in_specs=[
            pl.BlockSpec((q, D_INNER), lambda b, c: (row(b, c), col(OFF_Z, D_INNER))),
            pl.BlockSpec((q, D_INNER), lambda b, c: (row(b, c), col(OFF_X, D_INNER))),
            pl.BlockSpec((q, 1024), lambda b, c: (row(b, c), col(OFF_B, 1024))),
            pl.BlockSpec((q, 1024), lambda b, c: (row(b, c), col(OFF_C, 1024))),
            pl.BlockSpec((q, LANES), lambda b, c: (row(b, c), 0)),
            pl.BlockSpec((1, CONV_WIDTH - 1, CONV_DIM), lambda b, c: (b, 0, 0)),
            pl.BlockSpec((1, D_INNER, SSM_STATE), lambda b, c: (b, 0, 0)),
            full((CONV_WIDTH, CONV_DIM)),
            full((1, CONV_DIM)),
            full((1, LANES)),
            full((1, LANES)),
            full((1, D_INNER)),
            full((1, D_INNER)),
            full((LANES, D_INNER)),
        ],
        out_specs=[
            pl.BlockSpec((q, D_INNER), lambda b, c: (row(b, c), 0)),
            pl.BlockSpec((1, D_INNER, SSM_STATE), lambda b, c: (b, 0, 0)),
        ],
        out_shape=[
            jax.ShapeDtypeStruct((t, D_INNER), BF16),
            jax.ShapeDtypeStruct((batch, D_INNER, SSM_STATE), F32),
        ],
        scratch_shapes=[
            pltpu.VMEM((SUBLANES, CONV_DIM), F32),
            pltpu.VMEM((q, CONV_DIM), F32),
            pltpu.VMEM((SSM_GROUPS, SSM_STATE, D_INNER // SSM_GROUPS), F32),
        ],
        compiler_params=pltpu.CompilerParams(
            dimension_semantics=("parallel", "arbitrary"), vmem_limit_bytes=VMEM_LIMIT),
        name="ssm_chunked",
    )(proj, proj, proj, proj, dt_raw, conv_buf, h0, sp["w_conv"], sp["b_conv"], sp["dt_bias"],
      sp["a_log"], sp["d_skip"], sp["w_norm"], sp["expand"])


def _ssm_step_kernel(z_ref, xs_ref, bm_ref, cm_ref, dt_ref, cbuf_ref, h0_ref, wconv_ref, bconv_ref,
                     dtb_ref, alog_ref, dskip_ref, wnorm_ref, expand_ref, gsum_ref,
                     y_ref, hout_ref, tail_sc):
    q = xs_ref.shape[0]
    gw = D_INNER // SSM_GROUPS
    n = SSM_STATE
    tail_sc[...] = jnp.zeros_like(tail_sc)
    tail_sc[SUBLANES - (CONV_WIDTH - 1):SUBLANES, :] = cbuf_ref[0]
    xs = _conv_silu(xs_ref[...], tail_sc[:, 0:D_INNER], wconv_ref[:, 0:D_INNER], bconv_ref[:, 0:D_INNER])
    o_b, o_c = D_INNER, D_INNER + SSM_GROUPS * n
    bm = _conv_silu(bm_ref[...], tail_sc[:, o_b:o_c], wconv_ref[:, o_b:o_c], bconv_ref[:, o_b:o_c])
    cm = _conv_silu(cm_ref[...], tail_sc[:, o_c:], wconv_ref[:, o_c:], bconv_ref[:, o_c:])
    bm = bm.astype(BF16).astype(F32)
    cm = cm.astype(BF16).astype(F32)

    dt = _softplus(dt_ref[...] + dtb_ref[...])
    a = -jnp.exp(alog_ref[...])
    tri = (_iota((q, q), 0) >= _iota((q, q), 1)).astype(F32)
    d1, d2, d3 = _split3(dt * a)
    acs = _dot(tri, d1.astype(F32)) + _dot(tri, d2.astype(F32)) + _dot(tri, d3.astype(F32))
    a_end = acs[q - 1:q, :]

    row = _iota((q, LANES), 0)
    prods, lms = [], []
    for s in range(q):
        prods.append(cm * bm[s:s + 1, :])
        keep = row >= s
        lms.append(jnp.where(keep, jnp.exp(jnp.where(keep, acs - acs[s:s + 1, :], 0.0)), 0.0))
    qq = q * q
    cb3 = _dot(jnp.concatenate(_split3(jnp.concatenate(prods, axis=0)), axis=0), gsum_ref[...])
    cbs = cb3[0:qq] + cb3[qq:2 * qq] + cb3[2 * qq:3 * qq]
    gmat = (cbs * jnp.concatenate(lms, axis=0)).astype(BF16)
    quant = jnp.concatenate([dt, jnp.exp(acs), jnp.exp(a_end - acs),
                             jnp.broadcast_to(jnp.exp(a_end), (q, LANES))], axis=0)
    nq = 4 * q
    big = _dot(jnp.concatenate(_split3(quant) + (gmat,), axis=0), expand_ref[...])
    qx = big[0:nq] + big[nq:2 * nq] + big[2 * nq:3 * nq]
    dt_x, e_in_x, e_out_x, e_all_x = (qx[i * q:(i + 1) * q] for i in range(4))
    g_x = big[3 * nq:3 * nq + qq]
    xdt = xs * dt_x
    xdt_r = xdt.astype(BF16).astype(F32)
    y = dskip_ref[...] * xs
    for s in range(q):
        y = y + g_x[s * q:(s + 1) * q, :] * xdt_r[s:s + 1, :]
    xdte = xdt * e_out_x

    eye = (_iota((gw, gw), 0) == _iota((gw, gw), 1)).astype(F32)
    for g in range(SSM_GROUPS):
        gs = slice(g * gw, (g + 1) * gw)
        h0 = h0_ref[0, gs, :]
        bg = bm[:, g * n:(g + 1) * n]
        cg = cm[:, g * n:(g + 1) * n]
        yg = y[:, gs] + _dot_nt(cg, h0) * e_in_x[:, gs]
        u = yg * jax.nn.silu(z_ref[:, gs])
        u = u * lax.rsqrt(jnp.mean(u * u, axis=-1, keepdims=True) + EPS)
        y_ref[:, gs] = u * wnorm_ref[:, gs]
        e1, e2, e3 = _split3(e_all_x[:, gs])
        srow = _iota((q, gw), 0)
        e_rows = jnp.where(srow == 0, e1.astype(F32),
                           jnp.where(srow == 1, e2.astype(F32), jnp.where(srow == 2, e3.astype(F32), 0.0)))
        cols = _dot_nt(eye, jnp.concatenate([xdte[:, gs].astype(BF16).astype(F32), e_rows], axis=0))
        dec = jnp.sum(cols[:, q:2 * q], axis=-1, keepdims=True)
        hout_ref[0, gs, :] = dec * h0 + _dot(cols[:, 0:q], bg)


def _ssm_step(proj, dt_raw, conv_buf, h0, sp, batch, q):
    t = proj.shape[0]
    assert t == batch * q and q == SUBLANES
    col = lambda off, w: off // w
    full = lambda shape: pl.BlockSpec(shape, lambda b: (0,) * len(shape))
    gsum = ((_iota((SSM_GROUPS * SSM_STATE, LANES), 0) // SSM_STATE
             == _iota((SSM_GROUPS * SSM_STATE, LANES), 1) // (SSM_HEADS // SSM_GROUPS))
            & (_iota((SSM_GROUPS * SSM_STATE, LANES), 1) < SSM_HEADS)).astype(BF16)
    return pl.pallas_call(
        _ssm_step_kernel,
        grid=(batch,),
        in_specs=[
            pl.BlockSpec((q, D_INNER), lambda b: (b, col(OFF_Z, D_INNER))),
            pl.BlockSpec((q, D_INNER), lambda b: (b, col(OFF_X, D_INNER))),
            pl.BlockSpec((q, 1024), lambda b: (b, col(OFF_B, 1024))),
            pl.BlockSpec((q, 1024), lambda b: (b, col(OFF_C, 1024))),
            pl.BlockSpec((q, LANES), lambda b: (b, 0)),
            pl.BlockSpec((1, CONV_WIDTH - 1, CONV_DIM), lambda b: (b, 0, 0)),
            pl.BlockSpec((1, D_INNER, SSM_STATE), lambda b: (b, 0, 0)),
            full((CONV_WIDTH, CONV_DIM)),
            full((1, CONV_DIM)),
            full((1, LANES)),
            full((1, LANES)),
            full((1, D_INNER)),
            full((1, D_INNER)),
            full((LANES, D_INNER)),
            full((SSM_GROUPS * SSM_STATE, LANES)),
        ],
        out_specs=[
            pl.BlockSpec((q, D_INNER), lambda b: (b, 0)),
            pl.BlockSpec((1, D_INNER, SSM_STATE), lambda b: (b, 0, 0)),
        ],
        out_shape=[
            jax.ShapeDtypeStruct((t, D_INNER), F32),
            jax.ShapeDtypeStruct((batch, D_INNER, SSM_STATE), F32),
        ],
        scratch_shapes=[pltpu.VMEM((SUBLANES, CONV_DIM), F32)],
        compiler_params=pltpu.CompilerParams(
            dimension_semantics=("parallel",), vmem_limit_bytes=VMEM_LIMIT),
        name="ssm_step",
    )(proj, proj, proj, proj, dt_raw, conv_buf, h0, sp["w_conv"], sp["b_conv"], sp["dt_bias"],
      sp["a_log"], sp["d_skip"], sp["w_norm"], sp["expand"], gsum)


def _prep_params(w_in, norm_mix, w_conv, b_conv, dt_bias, a_log, d_skip, w_ssm_norm, w_q_norm, w_k_norm):
    o_z, o_xbc = 0, D_INNER
    o_dt = o_xbc + CONV_DIM
    o_q = o_dt + SSM_HEADS
    o_k = o_q + ATTN_WIDTH
    o_v = o_k + ATTN_WIDTH
    o_g = o_v + ATTN_WIDTH
    w_main = jnp.concatenate(
        [w_in[:, o_z:o_dt], w_in[:, o_g:], w_in[:, o_q:o_g]], axis=1).astype(BF16)
    w_dt = jnp.pad(w_in[:, o_dt:o_q], ((0, 0), (0, LANES - SSM_HEADS))).astype(BF16)
    qk_w = jnp.ones((N_PROJ_TILES, 1, PROJ_TILE), F32)
    qk_w = qk_w.at[Q_TILE0:K_TILE0, 0].set(jnp.tile(w_q_norm, HEADS_PER_GROUP)[None])
    qk_w = qk_w.at[K_TILE0:V_TILE0, 0].set(jnp.tile(w_k_norm, HEADS_PER_GROUP)[None])
    pad_h = lambda v: jnp.pad(v, (0, LANES - SSM_HEADS))[None]
    expand = (_iota((LANES, D_INNER), 0) == _iota((LANES, D_INNER), 1) // SSM_HEAD_DIM).astype(BF16)
    ssm = dict(w_conv=w_conv, b_conv=b_conv[None], dt_bias=pad_h(dt_bias), a_log=pad_h(a_log),
               d_skip=jnp.repeat(d_skip, SSM_HEAD_DIM)[None], w_norm=w_ssm_norm[None], expand=expand)
    return dict(w_main=w_main, w_dt=w_dt, qk_w=qk_w, norm_mix=norm_mix[None], ssm=ssm)


def _attn_prompt_kernel(q_ref, k_ref, v_ref, o_ref, lse_ref, kp_sc, vp_sc, ot_sc, lt_sc, *, span, dil):
    i = pl.program_id(1)
    blk = ATTN_BLOCK
    heads = q_ref.shape[1] // ATTN_HEAD_DIM

    @pl.when(i == 0)
    def _():
        kp_sc[...] = jnp.zeros_like(kp_sc)
        vp_sc[...] = jnp.zeros_like(vp_sc)

    qi = _iota((blk, 2 * blk), 0)
    kc = _iota((blk, 2 * blk), 1)
    dist = blk + qi - kc
    mask = (dist >= 0) & (dist <= span) & ((kc >= blk) | (i > 0))

    def residue(r, carry):
        rows = pl.ds(r, blk, stride=dil) if dil > 1 else slice(None)
        q = q_ref[rows, :].astype(BF16)
        k2 = jnp.concatenate([kp_sc[rows, :], k_ref[rows, :]], axis=0).astype(BF16)
        v2 = jnp.concatenate([vp_sc[rows, :], v_ref[rows, :]], axis=0).astype(BF16)
        for h in range(heads):
            hs = slice(h * ATTN_HEAD_DIM, (h + 1) * ATTN_HEAD_DIM)
            s = jnp.where(mask, _dot_nt(q[:, hs], k2[:, hs]), -jnp.inf)
            m = jnp.max(s, axis=-1, keepdims=True)
            p = jnp.exp(s - m)
            den = jnp.sum(p, axis=-1, keepdims=True)
            ot_sc[:, hs] = _dot((p / den).astype(BF16), v2[:, hs])
            lt_sc[:, hs] = jnp.broadcast_to(m + jnp.log(den), (blk, ATTN_HEAD_DIM))
        o_ref[rows, :] = ot_sc[...]
        lse_ref[rows, :] = lt_sc[...]
        return carry

    if dil > 1:
        lax.fori_loop(0, dil, residue, 0)
    else:
        residue(0, 0)
    kp_sc[...] = k_ref[...]
    vp_sc[...] = v_ref[...]


def _attn_prompt(proj, gi):
    window, dil = ATTN_PATTERNS[gi]
    s = proj.shape[0]
    chunk = ATTN_BLOCK * dil
    assert s % chunk == 0 and window // dil <= ATTN_BLOCK
    w = LANES if dil > 1 else ATTN_GROUP_WIDTH
    split = ATTN_GROUP_WIDTH // w
    col = lambda tile0: (lambda j, i: (i, (tile0 + gi) * split + j))
    blk = (chunk, w)
    return pl.pallas_call(
        functools.partial(_attn_prompt_kernel, span=window // dil, dil=dil),
        grid=(split, s // chunk),
        in_specs=[pl.BlockSpec(blk, col(Q_TILE0)), pl.BlockSpec(blk, col(K_TILE0)),
                  pl.BlockSpec(blk, col(V_TILE0))],
        out_specs=[pl.BlockSpec(blk, lambda j, i: (i, j))] * 2,
        out_shape=[jax.ShapeDtypeStruct((s, ATTN_GROUP_WIDTH), F32)] * 2,
        scratch_shapes=[pltpu.VMEM(blk, F32), pltpu.VMEM(blk, F32),
                        pltpu.VMEM((ATTN_BLOCK, w), F32), pltpu.VMEM((ATTN_BLOCK, w), F32)],
        compiler_params=pltpu.CompilerParams(
            dimension_semantics=("parallel", "arbitrary"), vmem_limit_bytes=VMEM_LIMIT),
        name=f"attn_prompt_g{gi}",
    )(proj, proj, proj)


def _attn_sample_kernel(q_ref, k_ref, v_ref, kc_ref, vc_ref, o_ref, lse_ref, ko_ref, vo_ref,
                        *, span, dil, pos0):
    l = q_ref.shape[0]
    nh, hd = HEADS_PER_GROUP, ATTN_HEAD_DIM
    rows = kc_ref.shape[3]
    q, kn, vn = q_ref[...], k_ref[...], v_ref[...]
    t_c = _iota((l, rows), 0)
    delta_c = rows + t_c - _iota((l, rows), 1)
    ok_c = ((delta_c <= span * dil) & ((delta_c & (dil - 1)) == 0) & (pos0 + t_c - delta_c >= 0))
    delta_n = _iota((l, l), 0) - _iota((l, l), 1)
    ok_n = (delta_n >= 0) & (delta_n <= span * dil) & ((delta_n & (dil - 1)) == 0)
    for h in range(nh):
        hs = slice(h * hd, (h + 1) * hd)
        qh = q[:, hs]
        s_c = jnp.where(ok_c, _dot(qh, kc_ref[0, h]), -jnp.inf)
        s_n = jnp.where(ok_n, _dot_nt(qh, kn[:, hs]), -jnp.inf)
        m = jnp.maximum(jnp.max(s_c, axis=-1, keepdims=True), jnp.max(s_n, axis=-1, keepdims=True))
        p_c = jnp.exp(s_c - m)
        p_n = jnp.exp(s_n - m)
        den = jnp.sum(p_c, axis=-1, keepdims=True) + jnp.sum(p_n, axis=-1, keepdims=True)
        o_ref[:, hs] = _dot_nt(p_c / den, vc_ref[0, h]) + _dot(p_n / den, vn[:, hs])
        lse_ref[:, hs] = jnp.broadcast_to(m + jnp.log(den), (l, hd))

    w = nh * hd
    lane = _iota((hd, LANES), 1)
    for new, c_ref, out_ref in ((kn, kc_ref, ko_ref), (vn, vc_ref, vo_ref)):
        padded = jnp.concatenate([jnp.zeros((LANES - l, w), F32), new], axis=0)
        tail = padded.T
        for h in range(nh):
            shifted = pltpu.roll(c_ref[0, h], rows - l, axis=1)
            if rows > LANES:
                out_ref[0, h, :, 0:rows - LANES] = shifted[:, 0:rows - LANES]
            out_ref[0, h, :, rows - LANES:rows] = jnp.where(
                lane < LANES - l, shifted[:, rows - LANES:rows], tail[h * hd:(h + 1) * hd])


def _attn_sample(proj, k_cache, v_cache, gi, batch, pos0):
    window, dil = ATTN_PATTERNS[gi]
    rows = k_cache.shape[1]
    l = proj.shape[0] // batch
    assert l == SUBLANES and rows % LANES == 0
    tok = lambda tile0: pl.BlockSpec((l, ATTN_GROUP_WIDTH), lambda b: (b, tile0 + gi))
    cache = pl.BlockSpec((1, HEADS_PER_GROUP, ATTN_HEAD_DIM, rows), lambda b: (b, 0, 0, 0))
    out_tok = pl.BlockSpec((l, ATTN_GROUP_WIDTH), lambda b: (b, 0))
    to_t = lambda c: jnp.transpose(c, (0, 2, 3, 1))
    t_shape = jax.ShapeDtypeStruct((batch, HEADS_PER_GROUP, ATTN_HEAD_DIM, rows), F32)
    o, lse, k_t, v_t = pl.pallas_call(
        functools.partial(_attn_sample_kernel, span=window // dil, dil=dil, pos0=pos0),
        grid=(batch,),
        in_specs=[tok(Q_TILE0), tok(K_TILE0), tok(V_TILE0), cache, cache],
        out_specs=[out_tok, out_tok, cache, cache],
        out_shape=[jax.ShapeDtypeStruct((batch * l, ATTN_GROUP_WIDTH), F32)] * 2 + [t_shape] * 2,
        compiler_params=pltpu.CompilerParams(
            dimension_semantics=("parallel",), vmem_limit_bytes=VMEM_LIMIT),
        name=f"attn_sample_g{gi}",
    )(proj, proj, proj, to_t(k_cache), to_t(v_cache))
    from_t = lambda c: jnp.transpose(c, (0, 3, 1, 2))
    return o, lse, from_t(k_t), from_t(v_t)


def _merge_kernel(x_ref, ys_ref, o0_ref, o1_ref, o2_ref, l0_ref, l1_ref, l2_ref, g0_ref, g1_ref,
                  wbs_ref, wba_ref, wo_ref, nf_ref, wr1_ref, wr2_ref, br_ref,
                  x1_ref, xn_ref, lg_ref):
    l0, l1, l2 = l0_ref[...], l1_ref[...], l2_ref[...]
    m = jnp.maximum(jnp.maximum(l0, l1), l2)
    e0, e1, e2 = jnp.exp(l0 - m), jnp.exp(l1 - m), jnp.exp(l2 - m)
    den = e0 + e1 + e2
    y_attn = (e0 / den) * o0_ref[...] + (e1 / den) * o1_ref[...] + (e2 / den) * o2_ref[...]
    br_s = _dot(ys_ref[...].astype(BF16), wbs_ref[...])
    br_a = _dot(y_attn.astype(BF16), wba_ref[...])
    merged = jax.nn.sigmoid(g0_ref[...]) * br_s + jax.nn.sigmoid(g1_ref[...]) * br_a
    x1 = x_ref[...] + _dot(merged.astype(BF16), wo_ref[...])
    x1_ref[...] = x1
    xn = x1 * lax.rsqrt(jnp.mean(x1 * x1, axis=-1, keepdims=True) + EPS) * nf_ref[...]
    xn1 = xn.astype(BF16)
    xn_ref[...] = xn1
    xn2 = (xn - xn1.astype(F32)).astype(BF16)
    lg_ref[...] = (_dot(xn1, wr1_ref[...]) + _dot(xn2, wr1_ref[...]) + _dot(xn1, wr2_ref[...])
                   + br_ref[...])


def _merge(x, proj, y_ssm, attn, mp, tm):
    t = x.shape[0]
    row = lambda w: pl.BlockSpec((tm, w), lambda i: (i, 0))
    full = lambda a: pl.BlockSpec(a.shape, lambda i: (0,) * a.ndim)
    (o0, l0), (o1, l1), (o2, l2) = attn
    weights = [mp["w_br_ssm"], mp["w_br_attn"], mp["w_out"], mp["norm_ffn"], mp["w_r1"], mp["w_r2"], mp["b_r"]]
    return pl.pallas_call(
        _merge_kernel,
        grid=(t // tm,),
        in_specs=[row(D_MODEL), row(D_INNER)] + [row(ATTN_GROUP_WIDTH)] * 6 + [
            pl.BlockSpec((tm, D_MODEL), lambda i: (i, OFF_G // D_MODEL)),
            pl.BlockSpec((tm, D_MODEL), lambda i: (i, OFF_G // D_MODEL + 1)),
        ] + [full(w) for w in weights],
        out_specs=[row(D_MODEL), row(D_MODEL), row(ROUTER_WIDTH)],
        out_shape=[
            jax.ShapeDtypeStruct((t, D_MODEL), F32),
            jax.ShapeDtypeStruct((t, D_MODEL), BF16),
            jax.ShapeDtypeStruct((t, ROUTER_WIDTH), F32),
        ],
        compiler_params=pltpu.CompilerParams(
            dimension_semantics=("parallel",), vmem_limit_bytes=VMEM_LIMIT),
        name="merge",
    )(x, y_ssm, o0, o1, o2, l0, l1, l2, proj, proj, *weights)


def _prep_merge_params(w_br_ssm, w_br_attn, w_out, norm_ffn, w_router_group, b_router_group,
                       w_router_expert, b_router_expert):
    w_r = jnp.concatenate([w_router_group, w_router_expert.reshape(D_MODEL, N_EXPERTS)], axis=1)
    w_r = jnp.pad(w_r, ((0, 0), (0, ROUTER_WIDTH - w_r.shape[1])))
    w_r1 = w_r.astype(BF16)
    w_r2 = (w_r - w_r1.astype(F32)).astype(BF16)
    b_r = jnp.concatenate([b_router_group, b_router_expert.reshape(N_EXPERTS)])
    b_r = jnp.pad(b_r, (0, ROUTER_WIDTH - b_r.shape[0]))[None]
    return dict(w_br_ssm=w_br_ssm.astype(BF16), w_br_attn=w_br_attn.astype(BF16), w_out=w_out.astype(BF16),
                norm_ffn=norm_ffn[None], w_r1=w_r1, w_r2=w_r2, b_r=b_r)


def _route(logits):
    lane = _iota(logits.shape, 1)
    big = jnp.int32(2 ** 30)
    neg = -jnp.inf
    gl = jnp.where(lane < N_EXPERT_GROUPS, logits, neg)
    gmax = jnp.max(gl, axis=-1, keepdims=True)
    gidx = jnp.min(jnp.where(gl == gmax, lane, big), axis=-1, keepdims=True)
    g_w = 1.0 / jnp.sum(jnp.exp(gl - gmax), axis=-1, keepdims=True)
    lo = N_EXPERT_GROUPS + gidx * EXPERTS_PER_GROUP
    el = jnp.where((lane >= lo) & (lane < lo + EXPERTS_PER_GROUP), logits, neg)
    v1 = jnp.max(el, axis=-1, keepdims=True)
    i1 = jnp.min(jnp.where(el == v1, lane, big), axis=-1, keepdims=True)
    el2 = jnp.where(lane == i1, neg, el)
    v2 = jnp.max(el2, axis=-1, keepdims=True)
    i2 = jnp.min(jnp.where(el2 == v2, lane, big), axis=-1, keepdims=True)
    e2 = jnp.exp(v2 - v1)
    w1 = (1.0 / (1.0 + e2)) * g_w
    w2 = (e2 / (1.0 + e2)) * g_w
    return jnp.where(lane == i1, w1, jnp.where(lane == i2, w2, 0.0)), gidx


MOE_BLOCK = 256


def _moe_kernel(x1_ref, xn_ref, lg_ref, wg_ref, wu_ref, wd_ref, out_ref,
                comb_sc, oh_sc, cum_sc, oht_sc, cumt_sc):
    g = pl.program_id(1)
    tm = x1_ref.shape[0]
    blk = MOE_BLOCK

    @pl.when(g == 0)
    def _():
        comb, gidx = _route(lg_ref[...])
        c1, c2, c3 = _split3(comb)
        comb_sc[0] = c1
        comb_sc[1] = c2
        comb_sc[2] = c3
        onehot = (_iota((tm, LANES), 1) == gidx).astype(F32)
        oh_bf = onehot.astype(BF16)
        lower = (_iota((tm, tm), 0) > _iota((tm, tm), 1)).astype(BF16)
        upper = (_iota((tm, tm), 0) < _iota((tm, tm), 1)).astype(BF16)
        oh_sc[...] = onehot
        cum_sc[...] = _dot(lower, oh_bf)
        oht_sc[...] = onehot.T[0:SUBLANES]
        cumt_sc[...] = _dot_tn(oh_bf, upper)[0:SUBLANES]
        out_ref[...] = x1_ref[...]

    lane = _iota((tm, LANES), 1)
    pick = lane == g
    member_c = jnp.sum(jnp.where(pick, oh_sc[...], 0.0), axis=-1, keepdims=True)
    rank_c = jnp.sum(jnp.where(pick, cum_sc[...], 0.0), axis=-1, keepdims=True)
    member_r = oht_sc[pl.ds(g, 1), :]
    rank_r = cumt_sc[pl.ds(g, 1), :]
    count = jnp.sum(member_r).astype(jnp.int32)
    nblk = (count + (blk - 1)) // blk
    xn = xn_ref[...]
    wd = wd_ref[...].reshape(EXPERTS_PER_GROUP * D_EXPERT, D_MODEL)
    lane_b = _iota((blk, LANES), 1)

    def body(k, carry):
        base = (k * blk).astype(F32)
        want_r = base + _iota((blk, tm), 0).astype(F32)
        sel = jnp.where((rank_r == want_r) & (member_r > 0.0), 1.0, 0.0).astype(BF16)
        want_c = base + _iota((tm, blk), 1).astype(F32)
        sel_t = jnp.where((rank_c == want_c) & (member_c > 0.0), 1.0, 0.0).astype(BF16)
        xg = _dot(sel, xn).astype(BF16)
        cg = _dot(sel, comb_sc[0]) + _dot(sel, comb_sc[1]) + _dot(sel, comb_sc[2])
        hidden = []
        for e in range(EXPERTS_PER_GROUP):
            col = N_EXPERT_GROUPS + g * EXPERTS_PER_GROUP + e
            cw = jnp.sum(jnp.where(lane_b == col, cg, 0.0), axis=-1, keepdims=True)
            h = jax.nn.silu(_dot(xg, wg_ref[e])) * _dot(xg, wu_ref[e]) * cw
            hidden.append(h.astype(BF16))
        out = _dot(jnp.concatenate(hidden, axis=1), wd)
        o1 = out.astype(BF16)
        o2 = (out - o1.astype(F32)).astype(BF16)
        out_ref[...] += _dot(sel_t, o1) + _dot(sel_t, o2)
        return carry

    lax.fori_loop(0, nblk, body, 0)


def _moe(x1, xn, logits, wg, wu, wd, tm):
    t = x1.shape[0]
    ge = EXPERTS_PER_GROUP
    return pl.pallas_call(
        _moe_kernel,
        grid=(t // tm, N_EXPERT_GROUPS),
        in_specs=[
            pl.BlockSpec((tm, D_MODEL), lambda i, g: (i, 0), pipeline_mode=pl.Buffered(1)),
            pl.BlockSpec((tm, D_MODEL), lambda i, g: (i, 0)),
            pl.BlockSpec((tm, ROUTER_WIDTH), lambda i, g: (i, 0), pipeline_mode=pl.Buffered(1)),
            pl.BlockSpec((ge, D_MODEL, D_EXPERT), lambda i, g: (g, 0, 0)),
            pl.BlockSpec((ge, D_MODEL, D_EXPERT), lambda i, g: (g, 0, 0)),
            pl.BlockSpec((ge, D_EXPERT, D_MODEL), lambda i, g: (g, 0, 0)),
        ],
        out_specs=pl.BlockSpec((tm, D_MODEL), lambda i, g: (i, 0)),
        out_shape=jax.ShapeDtypeStruct((t, D_MODEL), F32),
        scratch_shapes=[
            pltpu.VMEM((3, tm, ROUTER_WIDTH), BF16),
            pltpu.VMEM((tm, LANES), F32),
            pltpu.VMEM((tm, LANES), F32),
            pltpu.VMEM((SUBLANES, tm), F32),
            pltpu.VMEM((SUBLANES, tm), F32),
        ],
        compiler_params=pltpu.CompilerParams(
            dimension_semantics=("parallel", "arbitrary"), vmem_limit_bytes=VMEM_LIMIT),
        name="moe",
    )(x1, xn, logits, wg, wu, wd)


def _moe_dense_kernel(x1_ref, xn_ref, lg_ref, wg_ref, wu_ref, wd_ref, out_ref, comb_sc, acc_sc):
    e = pl.program_id(1)

    @pl.when(e == 0)
    def _():
        comb_sc[...] = _route(lg_ref[...])[0]
        acc_sc[...] = jnp.zeros_like(acc_sc)

    lane = _iota(comb_sc.shape, 1)
    cw = jnp.sum(jnp.where(lane == e + N_EXPERT_GROUPS, comb_sc[...], 0.0), axis=-1, keepdims=True)
    xn = xn_ref[...]
    hidden = jax.nn.silu(_dot(xn, wg_ref[0])) * _dot(xn, wu_ref[0]) * cw
    acc_sc[...] += _dot(hidden.astype(BF16), wd_ref[0])

    @pl.when(e == pl.num_programs(1) - 1)
    def _():
        out_ref[...] = x1_ref[...] + acc_sc[...]


def _moe_dense(x1, xn, logits, wg, wu, wd, tm):
    t = x1.shape[0]
    return pl.pallas_call(
        _moe_dense_kernel,
        grid=(t // tm, N_EXPERTS),
        in_specs=[
            pl.BlockSpec((tm, D_MODEL), lambda i, e: (i, 0)),
            pl.BlockSpec((tm, D_MODEL), lambda i, e: (i, 0)),
            pl.BlockSpec((tm, ROUTER_WIDTH), lambda i, e: (i, 0)),
            pl.BlockSpec((1, D_MODEL, D_EXPERT), lambda i, e: (e, 0, 0)),
            pl.BlockSpec((1, D_MODEL, D_EXPERT), lambda i, e: (e, 0, 0)),
            pl.BlockSpec((1, D_EXPERT, D_MODEL), lambda i, e: (e, 0, 0)),
        ],
        out_specs=pl.BlockSpec((tm, D_MODEL), lambda i, e: (i, 0)),
        out_shape=jax.ShapeDtypeStruct((t, D_MODEL), F32),
        scratch_shapes=[pltpu.VMEM((tm, ROUTER_WIDTH), F32), pltpu.VMEM((tm, D_MODEL), F32)],
        compiler_params=pltpu.CompilerParams(
            dimension_semantics=("parallel", "arbitrary"), vmem_limit_bytes=VMEM_LIMIT),
        name="moe_dense",
    )(x1, xn, logits, wg, wu, wd)


def _tile_rows(t, cap):
    tm = min(t, cap)
    assert t % tm == 0
    return tm


def _decoder_layer(x, conv_buf, ssm_state, kv_bufs, pos0, pp, mp, ep):
    b, l, d = x.shape
    t = b * l
    x2 = x.reshape(t, d)
    proj, dt_raw = _in_proj(x2, pp["norm_mix"], pp["w_main"], pp["w_dt"], pp["qk_w"], _tile_rows(t, 2048))
    proj3 = proj.reshape(b, l, PROJ_WIDTH)
    h0 = ssm_state.reshape(b, D_INNER, SSM_STATE)
    if kv_bufs is None:
        assert b == 1 and l % SSD_CHUNK == 0
        y_ssm, h_new = _ssm_chunked(proj, dt_raw, conv_buf, h0, pp["ssm"], b, SSD_CHUNK)
        attn = [_attn_prompt(proj, gi) for gi in range(N_ATTN_GROUPS)]
        new_kv = []
        for gi, (window, _) in enumerate(ATTN_PATTERNS):
            keep = min(window, l)
            for off in (OFF_K, OFF_V):
                c0 = off + gi * ATTN_GROUP_WIDTH
                new_kv.append(proj3[:, l - keep:, c0:c0 + ATTN_GROUP_WIDTH]
                              .reshape(b, keep, HEADS_PER_GROUP, ATTN_HEAD_DIM))
    else:
        y_ssm, h_new = _ssm_step(proj, dt_raw, conv_buf, h0, pp["ssm"], b, l)
        attn, new_kv = [], []
        for gi in range(N_ATTN_GROUPS):
            kc, vc = kv_bufs[2 * gi], kv_bufs[2 * gi + 1]
            o, lse, kn, vn = _attn_sample(proj, kc, vc, gi, b, pos0)
            attn.append((o, lse))
            new_kv += [kn, vn]
    new_conv = proj3[:, l - (CONV_WIDTH - 1):, OFF_X:OFF_X + CONV_DIM]
    tm = _tile_rows(t, 512)
    x1, xn, logits = _merge(x2, proj, y_ssm, attn, mp, tm)
    y = _moe(x1, xn, logits, ep["wg"], ep["wu"], ep["wd"], _tile_rows(t, 1024))
    new_ssm = h_new.reshape(b, SSM_HEADS, SSM_HEAD_DIM, SSM_STATE)
    return y.reshape(b, l, d), (new_conv, new_ssm) + tuple(new_kv)


def kernel(x_prompt, x_sample, cache_conv, state_ssm, cache_k_w128, cache_v_w128, cache_k_w512, cache_v_w512,
           cache_k_w2048, cache_v_w2048, w_in, norm_mix, w_conv, b_conv, dt_bias, a_log, d_skip, w_ssm_norm,
           w_q_norm, w_k_norm, w_br_ssm, w_br_attn, w_out, norm_ffn, w_router_group, b_router_group,
           w_router_expert, b_router_expert, w_expert_gate, w_expert_up, w_expert_down):
    depth = w_in.shape[0]
    yp, ys = x_prompt, x_sample
    bp = x_prompt.shape[0]
    prompt_states, sample_states = [], []
    for i in range(depth):
        pp = _prep_params(w_in[i], norm_mix[i], w_conv[i], b_conv[i], dt_bias[i], a_log[i], d_skip[i],
                          w_ssm_norm[i], w_q_norm[i], w_k_norm[i])
        mp = _prep_merge_params(w_br_ssm[i], w_br_attn[i], w_out[i], norm_ffn[i], w_router_group[i],
                                b_router_group[i], w_router_expert[i], b_router_expert[i])
        ep = dict(wg=w_expert_gate[i].astype(BF16), wu=w_expert_up[i].astype(BF16),
                  wd=w_expert_down[i].astype(BF16))
        zero_conv = jnp.zeros((bp, CONV_WIDTH - 1, CONV_DIM), x_prompt.dtype)
        zero_ssm = jnp.zeros((bp, SSM_HEADS, SSM_HEAD_DIM, SSM_STATE), F32)
        yp, st_p = _decoder_layer(yp, zero_conv, zero_ssm, None, 0, pp, mp, ep)
        kv_bufs = (cache_k_w128[i], cache_v_w128[i], cache_k_w512[i], cache_v_w512[i],
                   cache_k_w2048[i], cache_v_w2048[i])
        ys, st_s = _decoder_layer(ys, cache_conv[i], state_ssm[i], kv_bufs, PAST_LEN, pp, mp, ep)
        prompt_states.append(st_p)
        sample_states.append(st_s)
    stack = lambda states: [jnp.stack([s[j] for s in states]) for j in range(8)]
    return (yp, ys) + tuple(stack(prompt_states)) + tuple(stack(sample_states))
```

```python
import functools
import math

import jax
import jax.numpy as jnp
from jax import lax
from jax.experimental import pallas as pl
from jax.experimental.pallas import tpu as pltpu

F32 = jnp.float32
BF16 = jnp.bfloat16

PAST_LEN = 8192
D_MODEL = 1024
D_INNER = 2048
SSM_HEAD_DIM = 64
SSM_HEADS = 32
SSM_GROUPS = 8
SSM_STATE = 128
CONV_WIDTH = 4
CONV_DIM = D_INNER + 2 * SSM_GROUPS * SSM_STATE
SSD_CHUNK = 128
ATTN_HEAD_DIM = 64
HEADS_PER_GROUP = 8
ATTN_PATTERNS = ((128, 1), (512, 4), (2048, 16))
N_ATTN_GROUPS = 3
ATTN_GROUP_WIDTH = HEADS_PER_GROUP * ATTN_HEAD_DIM
ATTN_WIDTH = N_ATTN_GROUPS * ATTN_GROUP_WIDTH
ATTN_BLOCK = 128
N_EXPERT_GROUPS = 4
EXPERTS_PER_GROUP = 8
N_EXPERTS = 32
D_EXPERT = 256
EPS = 1e-6

LANES = 128
SUBLANES = 8
VMEM_LIMIT = 56 * 1024 * 1024

PROJ_TILE = 512
OFF_Z = 0
OFF_X = 2048
OFF_B = 4096
OFF_C = 5120
OFF_G = 6144
OFF_Q = 8192
OFF_K = OFF_Q + ATTN_WIDTH
OFF_V = OFF_K + ATTN_WIDTH
PROJ_WIDTH = OFF_V + ATTN_WIDTH
N_PROJ_TILES = PROJ_WIDTH // PROJ_TILE
Q_TILE0 = OFF_Q // PROJ_TILE
K_TILE0 = OFF_K // PROJ_TILE
V_TILE0 = OFF_V // PROJ_TILE
ROUTER_WIDTH = LANES


def _split3(x):
    x1 = x.astype(BF16)
    r1 = x - x1.astype(F32)
    x2 = r1.astype(BF16)
    r2 = r1 - x2.astype(F32)
    return x1, x2, r2.astype(BF16)


def _dot(a, b):
    return jnp.dot(a, b, preferred_element_type=F32)


def _dot3(x, m):
    x1, x2, x3 = _split3(x)
    return _dot(x1, m) + _dot(x2, m) + _dot(x3, m)


def _dot3f(x, m):
    x1, x2, x3 = _split3(x)
    return _dot(x1.astype(F32), m) + _dot(x2.astype(F32), m) + _dot(x3.astype(F32), m)


def _dot_nt(a, b):
    return lax.dot_general(a, b, (((1,), (1,)), ((), ())), preferred_element_type=F32)


def _dot_tn(a, b):
    return lax.dot_general(a, b, (((0,), (0,)), ((), ())), preferred_element_type=F32)


def _iota(shape, dim):
    return lax.broadcasted_iota(jnp.int32, shape, dim)


def _shr(x, pow2):
    assert pow2 & (pow2 - 1) == 0
    return lax.shift_right_logical(x, jnp.int32(int(math.log2(pow2))))


def _in_proj_kernel(x_ref, nw_ref, w_ref, wdt_ref, qkw_ref, seg_ref, out_ref, dt_ref, h_sc):
    j = pl.program_id(1)

    @pl.when(j == 0)
    def _():
        xf = x_ref[...]
        y = xf * lax.rsqrt(jnp.mean(xf * xf, axis=-1, keepdims=True) + EPS)
        h = (y * nw_ref[...]).astype(BF16)
        h_sc[...] = h
        dt_ref[...] = _dot(h, wdt_ref[...])

    acc = _dot(h_sc[...], w_ref[...])
    is_qk = jnp.logical_and(j >= Q_TILE0, j < V_TILE0)

    @pl.when(jnp.logical_not(is_qk))
    def _():
        out_ref[...] = acc

    @pl.when(is_qk)
    def _():
        sq = acc * acc
        s1 = sq.astype(BF16)
        s2 = (sq - s1.astype(F32)).astype(BF16)
        ms = (_dot(s1, seg_ref[...]) + _dot(s2, seg_ref[...])) * (1.0 / ATTN_HEAD_DIM)
        scale = jnp.where(j < K_TILE0, ATTN_HEAD_DIM ** -0.5, 1.0).astype(F32)
        w = qkw_ref[0] * scale
        out_ref[...] = acc * lax.rsqrt(ms + EPS) * w


def _in_proj(x, norm_w, w_bf, wdt_bf, qk_w, tm):
    t = x.shape[0]
    seg = (_iota((PROJ_TILE, PROJ_TILE), 0) // ATTN_HEAD_DIM
           == _iota((PROJ_TILE, PROJ_TILE), 1) // ATTN_HEAD_DIM).astype(BF16)
    return pl.pallas_call(
        _in_proj_kernel,
        grid=(t // tm, N_PROJ_TILES),
        in_specs=[
            pl.BlockSpec((tm, D_MODEL), lambda i, j: (i, 0)),
            pl.BlockSpec((1, D_MODEL), lambda i, j: (0, 0)),
            pl.BlockSpec((D_MODEL, PROJ_TILE), lambda i, j: (0, j)),
            pl.BlockSpec((D_MODEL, LANES), lambda i, j: (0, 0)),
            pl.BlockSpec((1, 1, PROJ_TILE), lambda i, j: (j, 0, 0)),
            pl.BlockSpec((PROJ_TILE, PROJ_TILE), lambda i, j: (0, 0)),
        ],
        out_specs=[
            pl.BlockSpec((tm, PROJ_TILE), lambda i, j: (i, j)),
            pl.BlockSpec((tm, LANES), lambda i, j: (i, 0)),
        ],
        out_shape=[
            jax.ShapeDtypeStruct((t, PROJ_WIDTH), F32),
            jax.ShapeDtypeStruct((t, LANES), F32),
        ],
        scratch_shapes=[pltpu.VMEM((tm, D_MODEL), BF16)],
        compiler_params=pltpu.CompilerParams(
            dimension_semantics=("parallel", "arbitrary"), vmem_limit_bytes=VMEM_LIMIT),
        name="in_proj",
    )(x, norm_w, w_bf, wdt_bf, qk_w, seg)


def _softplus(x):
    return jnp.maximum(x, 0.0) + jnp.log1p(jnp.exp(-jnp.abs(x)))


def _conv_silu(raw, tail, w, b):
    q = raw.shape[0]
    up = jnp.concatenate([tail, raw], axis=0)
    acc = b + w[CONV_WIDTH - 1:CONV_WIDTH] * raw
    for tap in range(CONV_WIDTH - 1):
        shifted = pltpu.roll(up, CONV_WIDTH - 1 - tap, axis=0)[SUBLANES:SUBLANES + q]
        acc = acc + w[tap:tap + 1] * shifted
    return jax.nn.silu(acc)


def _ssm_chunk_kernel(z_ref, xs_ref, bm_ref, cm_ref, dt_ref, cbuf_ref, h0_ref, wconv_ref, bconv_ref,
                      dtb_ref, alog_ref, dskip_ref, wnorm_ref, expand_ref,
                      y_ref, hout_ref, tail_sc, xbc_sc, ht_sc):
    c = pl.program_id(1)
    nc = pl.num_programs(1)
    q = xs_ref.shape[0]
    gw = D_INNER // SSM_GROUPS
    n = SSM_STATE

    @pl.when(c == 0)
    def _():
        tail_sc[...] = jnp.zeros_like(tail_sc)
        tail_sc[SUBLANES - (CONV_WIDTH - 1):SUBLANES, :] = cbuf_ref[0]
        for g in range(SSM_GROUPS):
            ht_sc[g] = h0_ref[0, g * gw:(g + 1) * gw, :].T

    cw = PROJ_TILE
    for k in range(CONV_DIM // cw):
        if k < D_INNER // cw:
            raw = xs_ref[:, k * cw:(k + 1) * cw]
        elif k < (D_INNER + SSM_GROUPS * n) // cw:
            kk = k - D_INNER // cw
            raw = bm_ref[:, kk * cw:(kk + 1) * cw]
        else:
            kk = k - (D_INNER + SSM_GROUPS * n) // cw
            raw = cm_ref[:, kk * cw:(kk + 1) * cw]
        sl = slice(k * cw, (k + 1) * cw)
        xbc_sc[:, sl] = _conv_silu(raw, tail_sc[:, sl], wconv_ref[:, sl], bconv_ref[:, sl])
        tail_sc[:, sl] = raw[q - SUBLANES:q]

    dt = _softplus(dt_ref[...] + dtb_ref[...])
    a = -jnp.exp(alog_ref[...])
    dta = dt * a
    tri = (_iota((q, q), 0) >= _iota((q, q), 1)).astype(BF16)
    d1, d2, d3 = _split3(dta)
    acs = _dot(tri, d1) + _dot(tri, d2) + _dot(tri, d3)
    eye = (_iota((LANES, LANES), 0) == _iota((LANES, LANES), 1)).astype(BF16)
    c1, c2, c3 = _split3(acs)
    acs_t = _dot_nt(eye, c1) + _dot_nt(eye, c2) + _dot_nt(eye, c3)
    a_end = acs[q - 1:q, :]
    e_in = jnp.exp(acs)
    e_out = jnp.exp(a_end - acs)
    e_all = jnp.exp(a_end)

    ex = expand_ref[...]
    dt_x = _dot3(dt, ex)
    e_in_x = _dot3(e_in, ex)
    e_out_x = _dot3(e_out, ex)
    e_all_x = _dot3(jnp.broadcast_to(e_all, (2 * SUBLANES, LANES)), ex)[0:1]

    causal = _iota((q, q), 0) >= _iota((q, q), 1)
    lane_lo = _iota((q, LANES), 1) < SSM_HEAD_DIM

    for g in range(SSM_GROUPS):
        gs = slice(g * gw, (g + 1) * gw)
        xg = xbc_sc[:, gs]
        bg = xbc_sc[:, D_INNER + g * n:D_INNER + (g + 1) * n].astype(BF16)
        cg = xbc_sc[:, D_INNER + SSM_GROUPS * n + g * n:D_INNER + SSM_GROUPS * n + (g + 1) * n].astype(BF16)
        xdt = xg * dt_x[:, gs]
        cb = _dot_nt(cg, bg)
        ht = ht_sc[g]
        y_off = _dot(cg, ht.astype(BF16)) * e_in_x[:, gs]
        pieces = []
        for pair in range(2):
            ms = []
            for hh in range(2):
                h = g * 4 + pair * 2 + hh
                diff = acs[:, h:h + 1] - acs_t[h:h + 1, :]
                lmat = jnp.where(causal, jnp.exp(jnp.where(causal, diff, 0.0)), 0.0)
                ms.append((cb * lmat).astype(BF16))
            xp = xdt[:, pair * LANES:(pair + 1) * LANES]
            rhs = jnp.concatenate([jnp.where(lane_lo, xp, 0.0), jnp.where(lane_lo, 0.0, xp)],
                                  axis=0).astype(BF16)
            pieces.append(_dot(jnp.concatenate(ms, axis=1), rhs))
        y_diag = jnp.concatenate(pieces, axis=1)
        yg = y_diag + y_off + dskip_ref[:, gs] * xg
        zg = z_ref[:, gs]
        u = yg * jax.nn.silu(zg)
        u = u * lax.rsqrt(jnp.mean(u * u, axis=-1, keepdims=True) + EPS)
        y_ref[:, gs] = (u * wnorm_ref[:, gs]).astype(y_ref.dtype)
        st = _dot_tn(bg, (xdt * e_out_x[:, gs]).astype(BF16))
        ht_sc[g] = e_all_x[:, gs] * ht + st

    @pl.when(c == nc - 1)
    def _():
        for g in range(SSM_GROUPS):
            hout_ref[0, g * gw:(g + 1) * gw, :] = ht_sc[g].T


def _ssm_chunked(proj, dt_raw, conv_buf, h0, sp, batch, q):
    t = proj.shape[0]
    nc = t // batch // q
    row = lambda b, c: b * nc + c
    col = lambda off, w: off // w
    full = lambda shape: pl.BlockSpec(shape, lambda b, c: (0,) * len(shape))
    return pl.pallas_call(
        _ssm_chunk_kernel,
        grid=(batch, nc),
        in_specs=[
            pl.BlockSpec((q, D_INNER), lambda b, c: (row(b, c), col(OFF_Z, D_INNER))),
            pl.BlockSpec((q, D_INNER), lambda b, c: (row(b, c), col(OFF_X, D_INNER))),
            pl.BlockSpec((q, 1024), lambda b, c: (row(b, c), col(OFF_B, 1024))),
            pl.BlockSpec((q, 1024), lambda b, c: (row(b, c), col(OFF_C, 1024))),
            pl.BlockSpec((q, LANES), lambda b, c: (row(b, c), 0)),
            pl.BlockSpec((1, CONV_WIDTH - 1, CONV_DIM), lambda b, c: (b, 0, 0)),
            pl.BlockSpec((1, D_INNER, SSM_STATE), lambda b, c: (b, 0, 0)),
            full((CONV_WIDTH, CONV_DIM)),
            full((1, CONV_DIM)),
            full((1, LANES)),
            full((1, LANES)),
            full((1, D_INNER)),
            full((1, D_INNER)),
            full((LANES, D_INNER)),
        ],
        out_specs=[
            pl.BlockSpec((q, D_INNER), lambda b, c: (row(b, c), 0)),
            pl.BlockSpec((1, D_INNER, SSM_STATE), lambda b, c: (b, 0, 0)),
        ],
        out_shape=[
            jax.ShapeDtypeStruct((t, D_INNER), BF16),
            jax.ShapeDtypeStruct((batch, D_INNER, SSM_STATE), F32),
        ],
        scratch_shapes=[
            pltpu.VMEM((SUBLANES, CONV_DIM), F32),
            pltpu.VMEM((q, CONV_DIM), F32),
            pltpu.VMEM((SSM_GROUPS, SSM_STATE, D_INNER // SSM_GROUPS), F32),
        ],
        compiler_params=pltpu.CompilerParams(
            dimension_semantics=("parallel", "arbitrary"), vmem_limit_bytes=VMEM_LIMIT),
        name="ssm_chunked",
    )(proj, proj, proj, proj, dt_raw, conv_buf, h0, sp["w_conv"], sp["b_conv"], sp["dt_bias"],
      sp["a_log"], sp["d_skip"], sp["w_norm"], sp["expand"])


def _ssm_step_kernel(z_ref, xs_ref, bm_ref, cm_ref, dt_ref, cbuf_ref, h0_ref, wconv_ref, bconv_ref,
                     dtb_ref, alog_ref, dskip_ref, wnorm_ref, expand_ref, gsum_ref,
                     y_ref, hout_ref, tail_sc):
    q = xs_ref.shape[0]
    gw = D_INNER // SSM_GROUPS
    n = SSM_STATE
    tail_sc[...] = jnp.zeros_like(tail_sc)
    tail_sc[SUBLANES - (CONV_WIDTH - 1):SUBLANES, :] = cbuf_ref[0]
    xs = _conv_silu(xs_ref[...], tail_sc[:, 0:D_INNER], wconv_ref[:, 0:D_INNER], bconv_ref[:, 0:D_INNER])
    o_b, o_c = D_INNER, D_INNER + SSM_GROUPS * n
    bm = _conv_silu(bm_ref[...], tail_sc[:, o_b:o_c], wconv_ref[:, o_b:o_c], bconv_ref[:, o_b:o_c])
    cm = _conv_silu(cm_ref[...], tail_sc[:, o_c:], wconv_ref[:, o_c:], bconv_ref[:, o_c:])
    bm = bm.astype(BF16).astype(F32)
    cm = cm.astype(BF16).astype(F32)

    dt = _softplus(dt_ref[...] + dtb_ref[...])
    a = -jnp.exp(alog_ref[...])
    tri = (_iota((q, q), 0) >= _iota((q, q), 1)).astype(F32)
    d1, d2, d3 = _split3(dt * a)
    acs = _dot(tri, d1.astype(F32)) + _dot(tri, d2.astype(F32)) + _dot(tri, d3.astype(F32))
    a_end = acs[q - 1:q, :]

    row = _iota((q, LANES), 0)
    prods, lms = [], []
    for s in range(q):
        prods.append(cm * bm[s:s + 1, :])
        keep = row >= s
        lms.append(jnp.where(keep, jnp.exp(jnp.where(keep, acs - acs[s:s + 1, :], 0.0)), 0.0))
    qq = q * q
    cb3 = _dot(jnp.concatenate(_split3(jnp.concatenate(prods, axis=0)), axis=0), gsum_ref[...])
    cbs = cb3[0:qq] + cb3[qq:2 * qq] + cb3[2 * qq:3 * qq]
    gmat = (cbs * jnp.concatenate(lms, axis=0)).astype(BF16)
    quant = jnp.concatenate([dt, jnp.exp(acs), jnp.exp(a_end - acs),
                             jnp.broadcast_to(jnp.exp(a_end), (q, LANES))], axis=0)
    nq = 4 * q
    big = _dot(jnp.concatenate(_split3(quant) + (gmat,), axis=0), expand_ref[...])
    qx = big[0:nq] + big[nq:2 * nq] + big[2 * nq:3 * nq]
    dt_x, e_in_x, e_out_x, e_all_x = (qx[i * q:(i + 1) * q] for i in range(4))
    g_x = big[3 * nq:3 * nq + qq]
    xdt = xs * dt_x
    xdt_r = xdt.astype(BF16).astype(F32)
    y = dskip_ref[...] * xs
    for s in range(q):
        y = y + g_x[s * q:(s + 1) * q, :] * xdt_r[s:s + 1, :]
    xdte = xdt * e_out_x

    eye = (_iota((gw, gw), 0) == _iota((gw, gw), 1)).astype(F32)
    for g in range(SSM_GROUPS):
        gs = slice(g * gw, (g + 1) * gw)
        h0 = h0_ref[0, gs, :]
        bg = bm[:, g * n:(g + 1) * n]
        cg = cm[:, g * n:(g + 1) * n]
        yg = y[:, gs] + _dot_nt(cg, h0) * e_in_x[:, gs]
        u = yg * jax.nn.silu(z_ref[:, gs])
        u = u * lax.rsqrt(jnp.mean(u * u, axis=-1, keepdims=True) + EPS)
        y_ref[:, gs] = u * wnorm_ref[:, gs]
        e1, e2, e3 = _split3(e_all_x[:, gs])
        srow = _iota((q, gw), 0)
        e_rows = jnp.where(srow == 0, e1.astype(F32),
                           jnp.where(srow == 1, e2.astype(F32), jnp.where(srow == 2, e3.astype(F32), 0.0)))
        cols = _dot_nt(eye, jnp.concatenate([xdte[:, gs].astype(BF16).astype(F32), e_rows], axis=0))
        dec = jnp.sum(cols[:, q:2 * q], axis=-1, keepdims=True)
        hout_ref[0, gs, :] = dec * h0 + _dot(cols[:, 0:q], bg)


def _ssm_step(proj, dt_raw, conv_buf, h0, sp, batch, q):
    t = proj.shape[0]
    assert t == batch * q and q == SUBLANES
    col = lambda off, w: off // w
    full = lambda shape: pl.BlockSpec(shape, lambda b: (0,) * len(shape))
    gsum = ((_iota((SSM_GROUPS * SSM_STATE, LANES), 0) // SSM_STATE
             == _iota((SSM_GROUPS * SSM_STATE, LANES), 1) // (SSM_HEADS // SSM_GROUPS))
            & (_iota((SSM_GROUPS * SSM_STATE, LANES), 1) < SSM_HEADS)).astype(BF16)
    return pl.pallas_call(
        _ssm_step_kernel,
        grid=(batch,),
        in_specs=[
            pl.BlockSpec((q, D_INNER), lambda b: (b, col(OFF_Z, D_INNER))),
            pl.BlockSpec((q, D_INNER), lambda b: (b, col(OFF_X, D_INNER))),
            pl.BlockSpec((q, 1024), lambda b: (b, col(OFF_B, 1024))),
            pl.BlockSpec((q, 1024), lambda b: (b, col(OFF_C, 1024))),
            pl.BlockSpec((q, LANES), lambda b: (b, 0)),
            pl.BlockSpec((1, CONV_WIDTH - 1, CONV_DIM), lambda b: (b, 0, 0)),
            pl.BlockSpec((1, D_INNER, SSM_STATE), lambda b: (b, 0, 0)),
            full((CONV_WIDTH, CONV_DIM)),
            full((1, CONV_DIM)),
            full((1, LANES)),
            full((1, LANES)),
            full((1, D_INNER)),
            full((1, D_INNER)),
            full((LANES, D_INNER)),
            full((SSM_GROUPS * SSM_STATE, LANES)),
        ],
        out_specs=[
            pl.BlockSpec((q, D_INNER), lambda b: (b, 0)),
            pl.BlockSpec((1, D_INNER, SSM_STATE), lambda b: (b, 0, 0)),
        ],
        out_shape=[
            jax.ShapeDtypeStruct((t, D_INNER), F32),
            jax.ShapeDtypeStruct((batch, D_INNER, SSM_STATE), F32),
        ],
        scratch_shapes=[pltpu.VMEM((SUBLANES, CONV_DIM), F32)],
        compiler_params=pltpu.CompilerParams(
            dimension_semantics=("parallel",), vmem_limit_bytes=VMEM_LIMIT),
        name="ssm_step",
    )(proj, proj, proj, proj, dt_raw, conv_buf, h0, sp["w_conv"], sp["b_conv"], sp["dt_bias"],
      sp["a_log"], sp["d_skip"], sp["w_norm"], sp["expand"], gsum)


def _prep_params(w_in, norm_mix, w_conv, b_conv, dt_bias, a_log, d_skip, w_ssm_norm, w_q_norm, w_k_norm):
    o_z, o_xbc = 0, D_INNER
    o_dt = o_xbc + CONV_DIM
    o_q = o_dt + SSM_HEADS
    o_k = o_q + ATTN_WIDTH
    o_v = o_k + ATTN_WIDTH
    o_g = o_v + ATTN_WIDTH
    w_main = jnp.concatenate(
        [w_in[:, o_z:o_dt], w_in[:, o_g:], w_in[:, o_q:o_g]], axis=1).astype(BF16)
    w_dt = jnp.pad(w_in[:, o_dt:o_q], ((0, 0), (0, LANES - SSM_HEADS))).astype(BF16)
    qk_w = jnp.ones((N_PROJ_TILES, 1, PROJ_TILE), F32)
    qk_w = qk_w.at[Q_TILE0:K_TILE0, 0].set(jnp.tile(w_q_norm, HEADS_PER_GROUP)[None])
    qk_w = qk_w.at[K_TILE0:V_TILE0, 0].set(jnp.tile(w_k_norm, HEADS_PER_GROUP)[None])
    pad_h = lambda v: jnp.pad(v, (0, LANES - SSM_HEADS))[None]
    expand = (_iota((LANES, D_INNER), 0) == _iota((LANES, D_INNER), 1) // SSM_HEAD_DIM).astype(BF16)
    ssm = dict(w_conv=w_conv, b_conv=b_conv[None], dt_bias=pad_h(dt_bias), a_log=pad_h(a_log),
               d_skip=jnp.repeat(d_skip, SSM_HEAD_DIM)[None], w_norm=w_ssm_norm[None], expand=expand)
    return dict(w_main=w_main, w_dt=w_dt, qk_w=qk_w, norm_mix=norm_mix[None], ssm=ssm)


def _attn_prompt_kernel(q_ref, k_ref, v_ref, o_ref, lse_ref, kp_sc, vp_sc, *, span, dil):
    i = pl.program_id(1)
    blk = ATTN_BLOCK
    pairs = q_ref.shape[1] // LANES

    @pl.when(i == 0)
    def _():
        kp_sc[...] = jnp.zeros_like(kp_sc)
        vp_sc[...] = jnp.zeros_like(vp_sc)

    qi = _iota((blk, 2 * blk), 0)
    kc = _iota((blk, 2 * blk), 1)
    dist = blk + qi - kc
    mask = (dist >= 0) & (dist <= span) & ((kc >= blk) | (i > 0))
    lo_q = _iota((blk, LANES), 1) < ATTN_HEAD_DIM
    lo_kv = _iota((2 * blk, LANES), 1) < ATTN_HEAD_DIM

    def residue(r, carry):
        rows = pl.ds(r, blk, stride=dil) if dil > 1 else slice(None)
        for pr in range(pairs):
            ls = slice(pr * LANES, (pr + 1) * LANES)
            q = q_ref[rows, ls].astype(BF16)
            k2 = jnp.concatenate([kp_sc[rows, ls], k_ref[rows, ls]], axis=0).astype(BF16)
            v2 = jnp.concatenate([vp_sc[rows, ls], v_ref[rows, ls]], axis=0).astype(BF16)
            ps, lses = [], []
            for keep in (lo_q, jnp.logical_not(lo_q)):
                s = jnp.where(mask, _dot_nt(jnp.where(keep, q, 0.0), k2), -jnp.inf)
                m = jnp.max(s, axis=-1, keepdims=True)
                p = jnp.exp(s - m)
                den = jnp.sum(p, axis=-1, keepdims=True)
                ps.append((p / den).astype(BF16))
                lses.append(m + jnp.log(den))
            v_bd = jnp.concatenate([jnp.where(lo_kv, v2, 0.0), jnp.where(lo_kv, 0.0, v2)], axis=0)
            o_ref[rows, ls] = _dot(jnp.concatenate(ps, axis=1), v_bd)
            lse_ref[rows, ls] = jnp.where(lo_q, lses[0], lses[1])
        return carry

    if dil > 1:
        lax.fori_loop(0, dil, residue, 0, unroll=2)
    else:
        residue(0, 0)
    kp_sc[...] = k_ref[...]
    vp_sc[...] = v_ref[...]


def _attn_prompt(proj, gi):
    window, dil = ATTN_PATTERNS[gi]
    s = proj.shape[0]
    chunk = ATTN_BLOCK * dil
    assert s % chunk == 0 and window // dil <= ATTN_BLOCK
    w = LANES if dil > 1 else ATTN_GROUP_WIDTH
    split = ATTN_GROUP_WIDTH // w
    col = lambda tile0: (lambda j, i: (i, (tile0 + gi) * split + j))
    blk = (chunk, w)
    return pl.pallas_call(
        functools.partial(_attn_prompt_kernel, span=window // dil, dil=dil),
        grid=(split, s // chunk),
        in_specs=[pl.BlockSpec(blk, col(Q_TILE0)), pl.BlockSpec(blk, col(K_TILE0)),
                  pl.BlockSpec(blk, col(V_TILE0))],
        out_specs=[pl.BlockSpec(blk, lambda j, i: (i, j))] * 2,
        out_shape=[jax.ShapeDtypeStruct((s, ATTN_GROUP_WIDTH), F32)] * 2,
        scratch_shapes=[pltpu.VMEM(blk, F32), pltpu.VMEM(blk, F32)],
        compiler_params=pltpu.CompilerParams(
            dimension_semantics=("parallel", "arbitrary"), vmem_limit_bytes=VMEM_LIMIT),
        name=f"attn_prompt_g{gi}",
    )(proj, proj, proj)


def _attn_sample_kernel(q_ref, k_ref, v_ref, kc_ref, vc_ref, o_ref, lse_ref, ko_ref, vo_ref,
                        *, span, dil, pos0):
    l = SUBLANES
    nb = kc_ref.shape[0]
    nh, hd = HEADS_PER_GROUP, ATTN_HEAD_DIM
    rows = kc_ref.shape[3]
    w = nh * hd
    t_c = _iota((l, rows), 0)
    delta_c = rows + t_c - _iota((l, rows), 1)
    ok_c = ((delta_c <= span * dil) & ((delta_c & (dil - 1)) == 0) & (pos0 + t_c - delta_c >= 0))
    delta_n = _iota((l, l), 0) - _iota((l, l), 1)
    ok_n = (delta_n >= 0) & (delta_n <= span * dil) & ((delta_n & (dil - 1)) == 0)
    lane = _iota((hd, LANES), 1)
    for bb in range(nb):
        rs = slice(bb * l, (bb + 1) * l)
        q, kn, vn = q_ref[rs, :], k_ref[rs, :], v_ref[rs, :]
        for h in range(nh):
            hs = slice(h * hd, (h + 1) * hd)
            qh = q[:, hs]
            s_c = jnp.where(ok_c, _dot(qh, kc_ref[bb, h]), -jnp.inf)
            s_n = jnp.where(ok_n, _dot_nt(qh, kn[:, hs]), -jnp.inf)
            m = jnp.maximum(jnp.max(s_c, axis=-1, keepdims=True), jnp.max(s_n, axis=-1, keepdims=True))
            p_c = jnp.exp(s_c - m)
            p_n = jnp.exp(s_n - m)
            den = jnp.sum(p_c, axis=-1, keepdims=True) + jnp.sum(p_n, axis=-1, keepdims=True)
            o_ref[rs, hs] = _dot_nt(p_c / den, vc_ref[bb, h]) + _dot(p_n / den, vn[:, hs])
            lse_ref[rs, hs] = jnp.broadcast_to(m + jnp.log(den), (l, hd))

        for new, c_ref, out_ref in ((kn, kc_ref, ko_ref), (vn, vc_ref, vo_ref)):
            padded = jnp.concatenate([jnp.zeros((LANES - l, w), F32), new], axis=0)
            tail = padded.T
            for h in range(nh):
                shifted = pltpu.roll(c_ref[bb, h], rows - l, axis=1)
                if rows > LANES:
                    out_ref[bb, h, :, 0:rows - LANES] = shifted[:, 0:rows - LANES]
                out_ref[bb, h, :, rows - LANES:rows] = jnp.where(
                    lane < LANES - l, shifted[:, rows - LANES:rows], tail[h * hd:(h + 1) * hd])


SAMPLE_CACHE_ROWS_PER_STEP = 2048
SAMPLE_MAX_SEQS_PER_STEP = 8


def _attn_sample(proj, k_cache, v_cache, gi, batch, pos0):
    window, dil = ATTN_PATTERNS[gi]
    rows = k_cache.shape[1]
    l = proj.shape[0] // batch
    assert l == SUBLANES and rows % LANES == 0
    nb = max(1, min(batch, SAMPLE_CACHE_ROWS_PER_STEP // rows, SAMPLE_MAX_SEQS_PER_STEP))
    assert batch % nb == 0
    tok = lambda tile0: pl.BlockSpec((nb * l, ATTN_GROUP_WIDTH), lambda b: (b, tile0 + gi))
    cache = pl.BlockSpec((nb, HEADS_PER_GROUP, ATTN_HEAD_DIM, rows), lambda b: (b, 0, 0, 0))
    out_tok = pl.BlockSpec((nb * l, ATTN_GROUP_WIDTH), lambda b: (b, 0))
    to_t = lambda c: jnp.transpose(c, (0, 2, 3, 1))
    t_shape = jax.ShapeDtypeStruct((batch, HEADS_PER_GROUP, ATTN_HEAD_DIM, rows), F32)
    o, lse, k_t, v_t = pl.pallas_call(
        functools.partial(_attn_sample_kernel, span=window // dil, dil=dil, pos0=pos0),
        grid=(batch // nb,),
        in_specs=[tok(Q_TILE0), tok(K_TILE0), tok(V_TILE0), cache, cache],
        out_specs=[out_tok, out_tok, cache, cache],
        out_shape=[jax.ShapeDtypeStruct((batch * l, ATTN_GROUP_WIDTH), F32)] * 2 + [t_shape] * 2,
        compiler_params=pltpu.CompilerParams(
            dimension_semantics=("parallel",), vmem_limit_bytes=VMEM_LIMIT),
        name=f"attn_sample_g{gi}",
    )(proj, proj, proj, to_t(k_cache), to_t(v_cache))
    from_t = lambda c: jnp.transpose(c, (0, 3, 1, 2))
    return o, lse, from_t(k_t), from_t(v_t)


def _merge_kernel(x_ref, ys_ref, o0_ref, o1_ref, o2_ref, l0_ref, l1_ref, l2_ref, g0_ref, g1_ref,
                  wbs_ref, wba_ref, wo_ref, nf_ref, wr1_ref, wr2_ref, br_ref,
                  x1_ref, xn_ref, lg_ref):
    l0, l1, l2 = l0_ref[...], l1_ref[...], l2_ref[...]
    m = jnp.maximum(jnp.maximum(l0, l1), l2)
    e0, e1, e2 = jnp.exp(l0 - m), jnp.exp(l1 - m), jnp.exp(l2 - m)
    den = e0 + e1 + e2
    y_attn = (e0 / den) * o0_ref[...] + (e1 / den) * o1_ref[...] + (e2 / den) * o2_ref[...]
    br_s = _dot(ys_ref[...].astype(BF16), wbs_ref[...])
    br_a = _dot(y_attn.astype(BF16), wba_ref[...])
    merged = jax.nn.sigmoid(g0_ref[...]) * br_s + jax.nn.sigmoid(g1_ref[...]) * br_a
    x1 = x_ref[...] + _dot(merged.astype(BF16), wo_ref[...])
    x1_ref[...] = x1
    xn = x1 * lax.rsqrt(jnp.mean(x1 * x1, axis=-1, keepdims=True) + EPS) * nf_ref[...]
    xn1 = xn.astype(BF16)
    xn_ref[...] = xn1
    xn2 = (xn - xn1.astype(F32)).astype(BF16)
    lg_ref[...] = (_dot(xn1, wr1_ref[...]) + _dot(xn2, wr1_ref[...]) + _dot(xn1, wr2_ref[...])
                   + br_ref[...])


def _merge(x, proj, y_ssm, attn, mp, tm):
    t = x.shape[0]
    row = lambda w: pl.BlockSpec((tm, w), lambda i: (i, 0))
    full = lambda a: pl.BlockSpec(a.shape, lambda i: (0,) * a.ndim)
    (o0, l0), (o1, l1), (o2, l2) = attn
    weights = [mp["w_br_ssm"], mp["w_br_attn"], mp["w_out"], mp["norm_ffn"], mp["w_r1"], mp["w_r2"], mp["b_r"]]
    return pl.pallas_call(
        _merge_kernel,
        grid=(t // tm,),
        in_specs=[row(D_MODEL), row(D_INNER)] + [row(ATTN_GROUP_WIDTH)] * 6 + [
            pl.BlockSpec((tm, D_MODEL), lambda i: (i, OFF_G // D_MODEL)),
            pl.BlockSpec((tm, D_MODEL), lambda i: (i, OFF_G // D_MODEL + 1)),
        ] + [full(w) for w in weights],
        out_specs=[row(D_MODEL), row(D_MODEL), row(ROUTER_WIDTH)],
        out_shape=[
            jax.ShapeDtypeStruct((t, D_MODEL), F32),
            jax.ShapeDtypeStruct((t, D_MODEL), BF16),
            jax.ShapeDtypeStruct((t, ROUTER_WIDTH), F32),
        ],
        compiler_params=pltpu.CompilerParams(
            dimension_semantics=("parallel",), vmem_limit_bytes=VMEM_LIMIT),
        name="merge",
    )(x, y_ssm, o0, o1, o2, l0, l1, l2, proj, proj, *weights)


def _prep_merge_params(w_br_ssm, w_br_attn, w_out, norm_ffn, w_router_group, b_router_group,
                       w_router_expert, b_router_expert):
    w_r = jnp.concatenate([w_router_group, w_router_expert.reshape(D_MODEL, N_EXPERTS)], axis=1)
    w_r = jnp.pad(w_r, ((0, 0), (0, ROUTER_WIDTH - w_r.shape[1])))
    w_r1 = w_r.astype(BF16)
    w_r2 = (w_r - w_r1.astype(F32)).astype(BF16)
    b_r = jnp.concatenate([b_router_group, b_router_expert.reshape(N_EXPERTS)])
    b_r = jnp.pad(b_r, (0, ROUTER_WIDTH - b_r.shape[0]))[None]
    return dict(w_br_ssm=w_br_ssm.astype(BF16), w_br_attn=w_br_attn.astype(BF16), w_out=w_out.astype(BF16),
                norm_ffn=norm_ffn[None], w_r1=w_r1, w_r2=w_r2, b_r=b_r)


def _route(logits):
    lane = _iota(logits.shape, 1)
    big = jnp.int32(2 ** 30)
    neg = -jnp.inf
    gl = jnp.where(lane < N_EXPERT_GROUPS, logits, neg)
    gmax = jnp.max(gl, axis=-1, keepdims=True)
    gidx = jnp.min(jnp.where(gl == gmax, lane, big), axis=-1, keepdims=True)
    g_w = 1.0 / jnp.sum(jnp.exp(gl - gmax), axis=-1, keepdims=True)
    lo = N_EXPERT_GROUPS + gidx * EXPERTS_PER_GROUP
    el = jnp.where((lane >= lo) & (lane < lo + EXPERTS_PER_GROUP), logits, neg)
    v1 = jnp.max(el, axis=-1, keepdims=True)
    i1 = jnp.min(jnp.where(el == v1, lane, big), axis=-1, keepdims=True)
    el2 = jnp.where(lane == i1, neg, el)
    v2 = jnp.max(el2, axis=-1, keepdims=True)
    i2 = jnp.min(jnp.where(el2 == v2, lane, big), axis=-1, keepdims=True)
    e2 = jnp.exp(v2 - v1)
    w1 = (1.0 / (1.0 + e2)) * g_w
    w2 = (e2 / (1.0 + e2)) * g_w
    return jnp.where(lane == i1, w1, jnp.where(lane == i2, w2, 0.0)), gidx


MOE_BLOCK = 256


def _moe_kernel(x1_ref, xn_ref, lg_ref, wg_ref, wu_ref, wd_ref, out_ref,
                comb_sc, oh_sc, cum_sc, oht_sc, cumt_sc):
    g = pl.program_id(1)
    tm = x1_ref.shape[0]
    blk = MOE_BLOCK

    @pl.when(g == 0)
    def _():
        comb, gidx = _route(lg_ref[...])
        c1, c2, c3 = _split3(comb)
        comb_sc[0] = c1
        comb_sc[1] = c2
        comb_sc[2] = c3
        onehot = (_iota((tm, LANES), 1) == gidx).astype(F32)
        oh_bf = onehot.astype(BF16)
        lower = (_iota((tm, tm), 0) > _iota((tm, tm), 1)).astype(BF16)
        upper = (_iota((tm, tm), 0) < _iota((tm, tm), 1)).astype(BF16)
        oh_sc[...] = onehot
        cum_sc[...] = _dot(lower, oh_bf)
        oht_sc[...] = onehot.T[0:SUBLANES]
        cumt_sc[...] = _dot_tn(oh_bf, upper)[0:SUBLANES]
        out_ref[...] = x1_ref[...]

    lane = _iota((tm, LANES), 1)
    pick = lane == g
    member_c = jnp.sum(jnp.where(pick, oh_sc[...], 0.0), axis=-1, keepdims=True)
    rank_c = jnp.sum(jnp.where(pick, cum_sc[...], 0.0), axis=-1, keepdims=True)
    member_r = oht_sc[pl.ds(g, 1), :]
    rank_r = cumt_sc[pl.ds(g, 1), :]
    count = jnp.sum(member_r).astype(jnp.int32)
    nblk = (count + (blk - 1)) // blk
    xn = xn_ref[...]
    wd = wd_ref[...].reshape(EXPERTS_PER_GROUP * D_EXPERT, D_MODEL)
    lane_b = _iota((blk, LANES), 1)

    def body(k, carry):
        base = (k * blk).astype(F32)
        want_r = base + _iota((blk, tm), 0).astype(F32)
        sel = jnp.where((rank_r == want_r) & (member_r > 0.0), 1.0, 0.0).astype(BF16)
        want_c = base + _iota((tm, blk), 1).astype(F32)
        sel_t = jnp.where((rank_c == want_c) & (member_c > 0.0), 1.0, 0.0).astype(BF16)
        xg = _dot(sel, xn).astype(BF16)
        cg = _dot(sel, comb_sc[0]) + _dot(sel, comb_sc[1]) + _dot(sel, comb_sc[2])
        hidden = []
        for e in range(EXPERTS_PER_GROUP):
            col = N_EXPERT_GROUPS + g * EXPERTS_PER_GROUP + e
            cw = jnp.sum(jnp.where(lane_b == col, cg, 0.0), axis=-1, keepdims=True)
            h = jax.nn.silu(_dot(xg, wg_ref[e])) * _dot(xg, wu_ref[e]) * cw
            hidden.append(h.astype(BF16))
        out = _dot(jnp.concatenate(hidden, axis=1), wd)
        o1 = out.astype(BF16)
        o2 = (out - o1.astype(F32)).astype(BF16)
        out_ref[...] += _dot(sel_t, o1) + _dot(sel_t, o2)
        return carry

    lax.fori_loop(0, nblk, body, 0)


def _moe(x1, xn, logits, wg, wu, wd, tm):
    t = x1.shape[0]
    ge = EXPERTS_PER_GROUP
    return pl.pallas_call(
        _moe_kernel,
        grid=(t // tm, N_EXPERT_GROUPS),
        in_specs=[
            pl.BlockSpec((tm, D_MODEL), lambda i, g: (i, 0), pipeline_mode=pl.Buffered(1)),
            pl.BlockSpec((tm, D_MODEL), lambda i, g: (i, 0)),
            pl.BlockSpec((tm, ROUTER_WIDTH), lambda i, g: (i, 0), pipeline_mode=pl.Buffered(1)),
            pl.BlockSpec((ge, D_MODEL, D_EXPERT), lambda i, g: (g, 0, 0)),
            pl.BlockSpec((ge, D_MODEL, D_EXPERT), lambda i, g: (g, 0, 0)),
            pl.BlockSpec((ge, D_EXPERT, D_MODEL), lambda i, g: (g, 0, 0)),
        ],
        out_specs=pl.BlockSpec((tm, D_MODEL), lambda i, g: (i, 0)),
        out_shape=jax.ShapeDtypeStruct((t, D_MODEL), F32),
        scratch_shapes=[
            pltpu.VMEM((3, tm, ROUTER_WIDTH), BF16),
            pltpu.VMEM((tm, LANES), F32),
            pltpu.VMEM((tm, LANES), F32),
            pltpu.VMEM((SUBLANES, tm), F32),
            pltpu.VMEM((SUBLANES, tm), F32),
        ],
        compiler_params=pltpu.CompilerParams(
            dimension_semantics=("parallel", "arbitrary"), vmem_limit_bytes=VMEM_LIMIT),
        name="moe",
    )(x1, xn, logits, wg, wu, wd)


def _moe_dense_kernel(x1_ref, xn_ref, lg_ref, wg_ref, wu_ref, wd_ref, out_ref, comb_sc, acc_sc):
    e = pl.program_id(1)

    @pl.when(e == 0)
    def _():
        comb_sc[...] = _route(lg_ref[...])[0]
        acc_sc[...] = jnp.zeros_like(acc_sc)

    lane = _iota(comb_sc.shape, 1)
    cw = jnp.sum(jnp.where(lane == e + N_EXPERT_GROUPS, comb_sc[...], 0.0), axis=-1, keepdims=True)
    xn = xn_ref[...]
    hidden = jax.nn.silu(_dot(xn, wg_ref[0])) * _dot(xn, wu_ref[0]) * cw
    acc_sc[...] += _dot(hidden.astype(BF16), wd_ref[0])

    @pl.when(e == pl.num_programs(1) - 1)
    def _():
        out_ref[...] = x1_ref[...] + acc_sc[...]


def _moe_dense(x1, xn, logits, wg, wu, wd, tm):
    t = x1.shape[0]
    return pl.pallas_call(
        _moe_dense_kernel,
        grid=(t // tm, N_EXPERTS),
        in_specs=[
            pl.BlockSpec((tm, D_MODEL), lambda i, e: (i, 0)),
            pl.BlockSpec((tm, D_MODEL), lambda i, e: (i, 0)),
            pl.BlockSpec((tm, ROUTER_WIDTH), lambda i, e: (i, 0)),
            pl.BlockSpec((1, D_MODEL, D_EXPERT), lambda i, e: (e, 0, 0)),
            pl.BlockSpec((1, D_MODEL, D_EXPERT), lambda i, e: (e, 0, 0)),
            pl.BlockSpec((1, D_EXPERT, D_MODEL), lambda i, e: (e, 0, 0)),
        ],
        out_specs=pl.BlockSpec((tm, D_MODEL), lambda i, e: (i, 0)),
        out_shape=jax.ShapeDtypeStruct((t, D_MODEL), F32),
        scratch_shapes=[pltpu.VMEM((tm, ROUTER_WIDTH), F32), pltpu.VMEM((tm, D_MODEL), F32)],
        compiler_params=pltpu.CompilerParams(
            dimension_semantics=("parallel", "arbitrary"), vmem_limit_bytes=VMEM_LIMIT),
        name="moe_dense",
    )(x1, xn, logits, wg, wu, wd)


def _tile_rows(t, cap):
    tm = min(t, cap)
    assert t % tm == 0
    return tm


def _decoder_layer(x, conv_buf, ssm_state, kv_bufs, pos0, pp, mp, ep):
    b, l, d = x.shape
    t = b * l
    x2 = x.reshape(t, d)
    proj, dt_raw = _in_proj(x2, pp["norm_mix"], pp["w_main"], pp["w_dt"], pp["qk_w"], _tile_rows(t, 2048))
    proj3 = proj.reshape(b, l, PROJ_WIDTH)
    h0 = ssm_state.reshape(b, D_INNER, SSM_STATE)
    if kv_bufs is None:
        assert b == 1 and l % SSD_CHUNK == 0
        y_ssm, h_new = _ssm_chunked(proj, dt_raw, conv_buf, h0, pp["ssm"], b, SSD_CHUNK)
        attn = [_attn_prompt(proj, gi) for gi in range(N_ATTN_GROUPS)]
        new_kv = []
        for gi, (window, _) in enumerate(ATTN_PATTERNS):
            keep = min(window, l)
            for off in (OFF_K, OFF_V):
                c0 = off + gi * ATTN_GROUP_WIDTH
                new_kv.append(proj3[:, l - keep:, c0:c0 + ATTN_GROUP_WIDTH]
                              .reshape(b, keep, HEADS_PER_GROUP, ATTN_HEAD_DIM))
    else:
        y_ssm, h_new = _ssm_step(proj, dt_raw, conv_buf, h0, pp["ssm"], b, l)
        attn, new_kv = [], []
        for gi in range(N_ATTN_GROUPS):
            kc, vc = kv_bufs[2 * gi], kv_bufs[2 * gi + 1]
            o, lse, kn, vn = _attn_sample(proj, kc, vc, gi, b, pos0)
            attn.append((o, lse))
            new_kv += [kn, vn]
    new_conv = proj3[:, l - (CONV_WIDTH - 1):, OFF_X:OFF_X + CONV_DIM]
    tm = _tile_rows(t, 512)
    x1, xn, logits = _merge(x2, proj, y_ssm, attn, mp, tm)
    y = _moe(x1, xn, logits, ep["wg"], ep["wu"], ep["wd"], _tile_rows(t, 1024))
    new_ssm = h_new.reshape(b, SSM_HEADS, SSM_HEAD_DIM, SSM_STATE)
    return y.reshape(b, l, d), (new_conv, new_ssm) + tuple(new_kv)


def kernel(x_prompt, x_sample, cache_conv, state_ssm, cache_k_w128, cache_v_w128, cache_k_w512, cache_v_w512,
           cache_k_w2048, cache_v_w2048, w_in, norm_mix, w_conv, b_conv, dt_bias, a_log, d_skip, w_ssm_norm,
           w_q_norm, w_k_norm, w_br_ssm, w_br_attn, w_out, norm_ffn, w_router_group, b_router_group,
           w_router_expert, b_router_expert, w_expert_gate, w_expert_up, w_expert_down):
    depth = w_in.shape[0]
    yp, ys = x_prompt, x_sample
    bp = x_prompt.shape[0]
    prompt_states, sample_states = [], []
    for i in range(depth):
        pp = _prep_params(w_in[i], norm_mix[i], w_conv[i], b_conv[i], dt_bias[i], a_log[i], d_skip[i],
                          w_ssm_norm[i], w_q_norm[i], w_k_norm[i])
        mp = _prep_merge_params(w_br_ssm[i], w_br_attn[i], w_out[i], norm_ffn[i], w_router_group[i],
                                b_router_group[i], w_router_expert[i], b_router_expert[i])
        ep = dict(wg=w_expert_gate[i].astype(BF16), wu=w_expert_up[i].astype(BF16),
                  wd=w_expert_down[i].astype(BF16))
        zero_conv = jnp.zeros((bp, CONV_WIDTH - 1, CONV_DIM), x_prompt.dtype)
        zero_ssm = jnp.zeros((bp, SSM_HEADS, SSM_HEAD_DIM, SSM_STATE), F32)
        yp, st_p = _decoder_layer(yp, zero_conv, zero_ssm, None, 0, pp, mp, ep)
        kv_bufs = (cache_k_w128[i], cache_v_w128[i], cache_k_w512[i], cache_v_w512[i],
                   cache_k_w2048[i], cache_v_w2048[i])
        ys, st_s = _decoder_layer(ys, cache_conv[i], state_ssm[i], kv_bufs, PAST_LEN, pp, mp, ep)
        prompt_states.append(st_p)
        sample_states.append(st_s)
    stack = lambda states: [jnp.stack([s[j] for s in states]) for j in range(8)]
    return (yp, ys) + tuple(stack(prompt_states)) + tuple(stack(sample_states))
```

```python
import functools
import math

import jax
import jax.numpy as jnp
from jax import lax
from jax.experimental import pallas as pl
from jax.experimental.pallas import tpu as pltpu

F32 = jnp.float32
BF16 = jnp.bfloat16

PAST_LEN = 8192
D_MODEL = 1024
D_INNER = 2048
SSM_HEAD_DIM = 64
SSM_HEADS = 32
SSM_GROUPS = 8
SSM_STATE = 128
CONV_WIDTH = 4
CONV_DIM = D_INNER + 2 * SSM_GROUPS * SSM_STATE
SSD_CHUNK = 128
ATTN_HEAD_DIM = 64
HEADS_PER_GROUP = 8
ATTN_PATTERNS = ((128, 1), (512, 4), (2048, 16))
N_ATTN_GROUPS = 3
ATTN_GROUP_WIDTH = HEADS_PER_GROUP * ATTN_HEAD_DIM
ATTN_WIDTH = N_ATTN_GROUPS * ATTN_GROUP_WIDTH
ATTN_BLOCK = 128
N_EXPERT_GROUPS = 4
EXPERTS_PER_GROUP = 8
N_EXPERTS = 32
D_EXPERT = 256
EPS = 1e-6

LANES = 128
SUBLANES = 8
VMEM_LIMIT = 56 * 1024 * 1024

PROJ_TILE = 512
OFF_Z = 0
OFF_X = 2048
OFF_B = 4096
OFF_C = 5120
OFF_G = 6144
OFF_Q = 8192
OFF_K = OFF_Q + ATTN_WIDTH
OFF_V = OFF_K + ATTN_WIDTH
PROJ_WIDTH = OFF_V + ATTN_WIDTH
N_PROJ_TILES = PROJ_WIDTH // PROJ_TILE
Q_TILE0 = OFF_Q // PROJ_TILE
K_TILE0 = OFF_K // PROJ_TILE
V_TILE0 = OFF_V // PROJ_TILE
ROUTER_WIDTH = LANES


def _split3(x):
    x1 = x.astype(BF16)
    r1 = x - x1.astype(F32)
    x2 = r1.astype(BF16)
    r2 = r1 - x2.astype(F32)
    return x1, x2, r2.astype(BF16)


def _dot(a, b):
    return jnp.dot(a, b, preferred_element_type=F32)


def _dot3(x, m):
    x1, x2, x3 = _split3(x)
    return _dot(x1, m) + _dot(x2, m) + _dot(x3, m)


def _dot3f(x, m):
    x1, x2, x3 = _split3(x)
    return _dot(x1.astype(F32), m) + _dot(x2.astype(F32), m) + _dot(x3.astype(F32), m)


def _dot_nt(a, b):
    return lax.dot_general(a, b, (((1,), (1,)), ((), ())), preferred_element_type=F32)


def _dot_tn(a, b):
    return lax.dot_general(a, b, (((0,), (0,)), ((), ())), preferred_element_type=F32)


def _iota(shape, dim):
    return lax.broadcasted_iota(jnp.int32, shape, dim)


def _shr(x, pow2):
    assert pow2 & (pow2 - 1) == 0
    return lax.shift_right_logical(x, jnp.int32(int(math.log2(pow2))))


def _in_proj_kernel(x_ref, nw_ref, w_ref, wdt_ref, qkw_ref, seg_ref, out_ref, dt_ref, h_sc):
    j = pl.program_id(1)

    @pl.when(j == 0)
    def _():
        xf = x_ref[...]
        y = xf * lax.rsqrt(jnp.mean(xf * xf, axis=-1, keepdims=True) + EPS)
        h = (y * nw_ref[...]).astype(BF16)
        h_sc[...] = h
        dt_ref[...] = _dot(h, wdt_ref[...])

    acc = _dot(h_sc[...], w_ref[...])
    is_qk = jnp.logical_and(j >= Q_TILE0, j < V_TILE0)

    @pl.when(jnp.logical_not(is_qk))
    def _():
        out_ref[...] = acc

    @pl.when(is_qk)
    def _():
        sq = acc * acc
        s1 = sq.astype(BF16)
        s2 = (sq - s1.astype(F32)).astype(BF16)
        ms = (_dot(s1, seg_ref[...]) + _dot(s2, seg_ref[...])) * (1.0 / ATTN_HEAD_DIM)
        scale = jnp.where(j < K_TILE0, ATTN_HEAD_DIM ** -0.5, 1.0).astype(F32)
        w = qkw_ref[0] * scale
        out_ref[...] = acc * lax.rsqrt(ms + EPS) * w


def _in_proj(x, norm_w, w_bf, wdt_bf, qk_w, tm):
    t = x.shape[0]
    seg = (_iota((PROJ_TILE, PROJ_TILE), 0) // ATTN_HEAD_DIM
           == _iota((PROJ_TILE, PROJ_TILE), 1) // ATTN_HEAD_DIM).astype(BF16)
    return pl.pallas_call(
        _in_proj_kernel,
        grid=(t // tm, N_PROJ_TILES),
        in_specs=[
            pl.BlockSpec((tm, D_MODEL), lambda i, j: (i, 0)),
            pl.BlockSpec((1, D_MODEL), lambda i, j: (0, 0)),
            pl.BlockSpec((D_MODEL, PROJ_TILE), lambda i, j: (0, j)),
            pl.BlockSpec((D_MODEL, LANES), lambda i, j: (0, 0)),
            pl.BlockSpec((1, 1, PROJ_TILE), lambda i, j: (j, 0, 0)),
            pl.BlockSpec((PROJ_TILE, PROJ_TILE), lambda i, j: (0, 0)),
        ],
        out_specs=[
            pl.BlockSpec((tm, PROJ_TILE), lambda i, j: (i, j)),
            pl.BlockSpec((tm, LANES), lambda i, j: (i, 0)),
        ],
        out_shape=[
            jax.ShapeDtypeStruct((t, PROJ_WIDTH), F32),
            jax.ShapeDtypeStruct((t, LANES), F32),
        ],
        scratch_shapes=[pltpu.VMEM((tm, D_MODEL), BF16)],
        compiler_params=pltpu.CompilerParams(
            dimension_semantics=("parallel", "arbitrary"), vmem_limit_bytes=VMEM_LIMIT),
        name="in_proj",
    )(x, norm_w, w_bf, wdt_bf, qk_w, seg)


def _softplus(x):
    return jnp.maximum(x, 0.0) + jnp.log1p(jnp.exp(-jnp.abs(x)))


def _conv_silu(raw, tail, w, b):
    q = raw.shape[0]
    up = jnp.concatenate([tail, raw], axis=0)
    acc = b + w[CONV_WIDTH - 1:CONV_WIDTH] * raw
    for tap in range(CONV_WIDTH - 1):
        shifted = pltpu.roll(up, CONV_WIDTH - 1 - tap, axis=0)[SUBLANES:SUBLANES + q]
        acc = acc + w[tap:tap + 1] * shifted
    return jax.nn.silu(acc)


def _ssm_chunk_kernel(z_ref, xs_ref, bm_ref, cm_ref, dt_ref, cbuf_ref, h0_ref, wconv_ref, bconv_ref,
                      dtb_ref, alog_ref, dskip_ref, wnorm_ref, expand_ref,
                      y_ref, hout_ref, tail_sc, xbc_sc, ht_sc):
    c = pl.program_id(1)
    nc = pl.num_programs(1)
    q = xs_ref.shape[0]
    gw = D_INNER // SSM_GROUPS
    n = SSM_STATE

    @pl.when(c == 0)
    def _():
        tail_sc[...] = jnp.zeros_like(tail_sc)
        tail_sc[SUBLANES - (CONV_WIDTH - 1):SUBLANES, :] = cbuf_ref[0]
        for g in range(SSM_GROUPS):
            ht_sc[g] = h0_ref[0, g * gw:(g + 1) * gw, :].T

    cw = PROJ_TILE
    for k in range(CONV_DIM // cw):
        if k < D_INNER // cw:
            raw = xs_ref[:, k * cw:(k + 1) * cw]
        elif k < (D_INNER + SSM_GROUPS * n) // cw:
            kk = k - D_INNER // cw
            raw = bm_ref[:, kk * cw:(kk + 1) * cw]
        else:
            kk = k - (D_INNER + SSM_GROUPS * n) // cw
            raw = cm_ref[:, kk * cw:(kk + 1) * cw]
        sl = slice(k * cw, (k + 1) * cw)
        xbc_sc[:, sl] = _conv_silu(raw, tail_sc[:, sl], wconv_ref[:, sl], bconv_ref[:, sl])
        tail_sc[:, sl] = raw[q - SUBLANES:q]

    dt = _softplus(dt_ref[...] + dtb_ref[...])
    a = -jnp.exp(alog_ref[...])
    dta = dt * a
    tri = (_iota((q, q), 0) >= _iota((q, q), 1)).astype(BF16)
    d1, d2, d3 = _split3(dta)
    acs = _dot(tri, d1) + _dot(tri, d2) + _dot(tri, d3)
    eye = (_iota((LANES, LANES), 0) == _iota((LANES, LANES), 1)).astype(BF16)
    c1, c2, c3 = _split3(acs)
    acs_t = _dot_nt(eye, c1) + _dot_nt(eye, c2) + _dot_nt(eye, c3)
    a_end = acs[q - 1:q, :]
    e_in = jnp.exp(acs)
    e_out = jnp.exp(a_end - acs)
    e_all = jnp.exp(a_end)

    ex = expand_ref[...]
    dt_x = _dot3(dt, ex)
    e_in_x = _dot3(e_in, ex)
    e_out_x = _dot3(e_out, ex)
    e_all_x = _dot3(jnp.broadcast_to(e_all, (2 * SUBLANES, LANES)), ex)[0:1]

    causal = _iota((q, q), 0) >= _iota((q, q), 1)
    lane_lo = _iota((q, LANES), 1) < SSM_HEAD_DIM

    for g in range(SSM_GROUPS):
        gs = slice(g * gw, (g + 1) * gw)
        xg = xbc_sc[:, gs]
        bg = xbc_sc[:, D_INNER + g * n:D_INNER + (g + 1) * n].astype(BF16)
        cg = xbc_sc[:, D_INNER + SSM_GROUPS * n + g * n:D_INNER + SSM_GROUPS * n + (g + 1) * n].astype(BF16)
        xdt = xg * dt_x[:, gs]
        cb = _dot_nt(cg, bg)
        ht = ht_sc[g]
        y_off = _dot(cg, ht.astype(BF16)) * e_in_x[:, gs]
        pieces = []
        for pair in range(2):
            ms = []
            for hh in range(2):
                h = g * 4 + pair * 2 + hh
                diff = acs[:, h:h + 1] - acs_t[h:h + 1, :]
                lmat = jnp.where(causal, jnp.exp(jnp.where(causal, diff, 0.0)), 0.0)
                ms.append((cb * lmat).astype(BF16))
            xp = xdt[:, pair * LANES:(pair + 1) * LANES]
            rhs = jnp.concatenate([jnp.where(lane_lo, xp, 0.0), jnp.where(lane_lo, 0.0, xp)],
                                  axis=0).astype(BF16)
            pieces.append(_dot(jnp.concatenate(ms, axis=1), rhs))
        y_diag = jnp.concatenate(pieces, axis=1)
        yg = y_diag + y_off + dskip_ref[:, gs] * xg
        zg = z_ref[:, gs]
        u = yg * jax.nn.silu(zg)
        u = u * lax.rsqrt(jnp.mean(u * u, axis=-1, keepdims=True) + EPS)
        y_ref[:, gs] = (u * wnorm_ref[:, gs]).astype(y_ref.dtype)
        st = _dot_tn(bg, (xdt * e_out_x[:, gs]).astype(BF16))
        ht_sc[g] = e_all_x[:, gs] * ht + st

    @pl.when(c == nc - 1)
    def _():
        for g in range(SSM_GROUPS):
            hout_ref[0, g * gw:(g + 1) * gw, :] = ht_sc[g].T


def _ssm_chunked(proj, dt_raw, conv_buf, h0, sp, batch, q):
    t = proj.shape[0]
    nc = t // batch // q
    row = lambda b, c: b * nc + c
    col = lambda off, w: off // w
    full = lambda shape: pl.BlockSpec(shape, lambda b, c: (0,) * len(shape))
    return pl.pallas_call(
        _ssm_chunk_kernel,
        grid=(batch, nc),
        in_specs=[
            pl.BlockSpec((q, D_INNER), lambda b, c: (row(b, c), col(OFF_Z, D_INNER))),
            pl.BlockSpec((q, D_INNER), lambda b, c: (row(b, c), col(OFF_X, D_INNER))),
            pl.BlockSpec((q, 1024), lambda b, c: (row(b, c), col(OFF_B, 1024))),
            pl.BlockSpec((q, 1024), lambda b, c: (row(b, c), col(OFF_C, 1024))),
            pl.BlockSpec((q, LANES), lambda b, c: (row(b, c), 0)),
            pl.BlockSpec((1, CONV_WIDTH - 1, CONV_DIM), lambda b, c: (b, 0, 0)),
            pl.BlockSpec((1, D_INNER, SSM_STATE), lambda b, c: (b, 0, 0)),
            full((CONV_WIDTH, CONV_DIM)),
            full((1, CONV_DIM)),
            full((1, LANES)),
            full((1, LANES)),
            full((1, D_INNER)),
            full((1, D_INNER)),
            full((LANES, D_INNER)),
        ],
        out_specs=[
            pl.BlockSpec((q, D_INNER), lambda b, c: (row(b, c), 0)),
            pl.BlockSpec((1, D_INNER, SSM_STATE), lambda b, c: (b, 0, 0)),
        ],
        out_shape=[
            jax.ShapeDtypeStruct((t, D_INNER), BF16),
            jax.ShapeDtypeStruct((batch, D_INNER, SSM_STATE), F32),
        ],
        scratch_shapes=[
            pltpu.VMEM((SUBLANES, CONV_DIM), F32),
            pltpu.VMEM((q, CONV_DIM), F32),
            pltpu.VMEM((SSM_GROUPS, SSM_STATE, D_INNER // SSM_GROUPS), F32),
        ],
        compiler_params=pltpu.CompilerParams(
            dimension_semantics=("parallel", "arbitrary"), vmem_limit_bytes=VMEM_LIMIT),
        name="ssm_chunked",
    )(proj, proj, proj, proj, dt_raw, conv_buf, h0, sp["w_conv"], sp["b_conv"], sp["dt_bias"],
      sp["a_log"], sp["d_skip"], sp["w_norm"], sp["expand"])


def _ssm_step_kernel(z_ref, xs_ref, bm_ref, cm_ref, dt_ref, cbuf_ref, h0_ref, wconv_ref, bconv_ref,
                     dtb_ref, alog_ref, dskip_ref, wnorm_ref, expand_ref, gsum_ref,
                     y_ref, hout_ref, tail_sc):
    q = xs_ref.shape[0]
    gw = D_INNER // SSM_GROUPS
    n = SSM_STATE
    tail_sc[...] = jnp.zeros_like(tail_sc)
    tail_sc[SUBLANES - (CONV_WIDTH - 1):SUBLANES, :] = cbuf_ref[0]
    xs = _conv_silu(xs_ref[...], tail_sc[:, 0:D_INNER], wconv_ref[:, 0:D_INNER], bconv_ref[:, 0:D_INNER])
    o_b, o_c = D_INNER, D_INNER + SSM_GROUPS * n
    bm = _conv_silu(bm_ref[...], tail_sc[:, o_b:o_c], wconv_ref[:, o_b:o_c], bconv_ref[:, o_b:o_c])
    cm = _conv_silu(cm_ref[...], tail_sc[:, o_c:], wconv_ref[:, o_c:], bconv_ref[:, o_c:])
    bm = bm.astype(BF16).astype(F32)
    cm = cm.astype(BF16).astype(F32)

    dt = _softplus(dt_ref[...] + dtb_ref[...])
    a = -jnp.exp(alog_ref[...])
    tri = (_iota((q, q), 0) >= _iota((q, q), 1)).astype(F32)
    d1, d2, d3 = _split3(dt * a)
    acs = _dot(tri, d1.astype(F32)) + _dot(tri, d2.astype(F32)) + _dot(tri, d3.astype(F32))
    a_end = acs[q - 1:q, :]

    row = _iota((q, LANES), 0)
    prods, lms = [], []
    for s in range(q):
        prods.append(cm * bm[s:s + 1, :])
        keep = row >= s
        lms.append(jnp.where(keep, jnp.exp(jnp.where(keep, acs - acs[s:s + 1, :], 0.0)), 0.0))
    qq = q * q
    cb3 = _dot(jnp.concatenate(_split3(jnp.concatenate(prods, axis=0)), axis=0), gsum_ref[...])
    cbs = cb3[0:qq] + cb3[qq:2 * qq] + cb3[2 * qq:3 * qq]
    gmat = (cbs * jnp.concatenate(lms, axis=0)).astype(BF16)
    quant = jnp.concatenate([dt, jnp.exp(acs), jnp.exp(a_end - acs),
                             jnp.broadcast_to(jnp.exp(a_end), (q, LANES))], axis=0)
    nq = 4 * q
    big = _dot(jnp.concatenate(_split3(quant) + (gmat,), axis=0), expand_ref[...])
    qx = big[0:nq] + big[nq:2 * nq] + big[2 * nq:3 * nq]
    dt_x, e_in_x, e_out_x, e_all_x = (qx[i * q:(i + 1) * q] for i in range(4))
    g_x = big[3 * nq:3 * nq + qq]
    xdt = xs * dt_x
    xdt_r = xdt.astype(BF16).astype(F32)
    y = dskip_ref[...] * xs
    for s in range(q):
        y = y + g_x[s * q:(s + 1) * q, :] * xdt_r[s:s + 1, :]
    xdte = xdt * e_out_x

    eye = (_iota((gw, gw), 0) == _iota((gw, gw), 1)).astype(BF16)
    zpad = jnp.zeros((q, SSM_GROUPS * n), F32)
    bm16 = jnp.concatenate([bm, zpad], axis=0).astype(BF16)
    cm16 = jnp.concatenate([cm, zpad], axis=0).astype(BF16)
    for g in range(SSM_GROUPS):
        gs = slice(g * gw, (g + 1) * gw)
        h0 = h0_ref[0, gs, :]
        yg = y[:, gs] + _dot_nt(cm16[:, g * n:(g + 1) * n], h0.astype(BF16))[0:q] * e_in_x[:, gs]
        u = yg * jax.nn.silu(z_ref[:, gs])
        u = u * lax.rsqrt(jnp.mean(u * u, axis=-1, keepdims=True) + EPS)
        y_ref[:, gs] = u * wnorm_ref[:, gs]
        e1, e2, e3 = _split3(e_all_x[:, gs])
        srow = _iota((q, gw), 0)
        e_rows = jnp.where(srow == 0, e1.astype(F32),
                           jnp.where(srow == 1, e2.astype(F32), jnp.where(srow == 2, e3.astype(F32), 0.0)))
        cols = _dot_nt(eye, jnp.concatenate([xdte[:, gs], e_rows], axis=0).astype(BF16))
        dec = jnp.sum(cols[:, q:2 * q], axis=-1, keepdims=True)
        hout_ref[0, gs, :] = dec * h0 + _dot(cols.astype(BF16), bm16[:, g * n:(g + 1) * n])


def _ssm_step(proj, dt_raw, conv_buf, h0, sp, batch, q):
    t = proj.shape[0]
    assert t == batch * q and q == SUBLANES
    col = lambda off, w: off // w
    full = lambda shape: pl.BlockSpec(shape, lambda b: (0,) * len(shape))
    gsum = ((_iota((SSM_GROUPS * SSM_STATE, LANES), 0) // SSM_STATE
             == _iota((SSM_GROUPS * SSM_STATE, LANES), 1) // (SSM_HEADS // SSM_GROUPS))
            & (_iota((SSM_GROUPS * SSM_STATE, LANES), 1) < SSM_HEADS)).astype(BF16)
    return pl.pallas_call(
        _ssm_step_kernel,
        grid=(batch,),
        in_specs=[
            pl.BlockSpec((q, D_INNER), lambda b: (b, col(OFF_Z, D_INNER))),
            pl.BlockSpec((q, D_INNER), lambda b: (b, col(OFF_X, D_INNER))),
            pl.BlockSpec((q, 1024), lambda b: (b, col(OFF_B, 1024))),
            pl.BlockSpec((q, 1024), lambda b: (b, col(OFF_C, 1024))),
            pl.BlockSpec((q, LANES), lambda b: (b, 0)),
            pl.BlockSpec((1, CONV_WIDTH - 1, CONV_DIM), lambda b: (b, 0, 0)),
            pl.BlockSpec((1, D_INNER, SSM_STATE), lambda b: (b, 0, 0)),
            full((CONV_WIDTH, CONV_DIM)),
            full((1, CONV_DIM)),
            full((1, LANES)),
            full((1, LANES)),
            full((1, D_INNER)),
            full((1, D_INNER)),
            full((LANES, D_INNER)),
            full((SSM_GROUPS * SSM_STATE, LANES)),
        ],
        out_specs=[
            pl.BlockSpec((q, D_INNER), lambda b: (b, 0)),
            pl.BlockSpec((1, D_INNER, SSM_STATE), lambda b: (b, 0, 0)),
        ],
        out_shape=[
            jax.ShapeDtypeStruct((t, D_INNER), F32),
            jax.ShapeDtypeStruct((batch, D_INNER, SSM_STATE), F32),
        ],
        scratch_shapes=[pltpu.VMEM((SUBLANES, CONV_DIM), F32)],
        compiler_params=pltpu.CompilerParams(
            dimension_semantics=("parallel",), vmem_limit_bytes=VMEM_LIMIT),
        name="ssm_step",
    )(proj, proj, proj, proj, dt_raw, conv_buf, h0, sp["w_conv"], sp["b_conv"], sp["dt_bias"],
      sp["a_log"], sp["d_skip"], sp["w_norm"], sp["expand"], gsum)


def _prep_params(w_in, norm_mix, w_conv, b_conv, dt_bias, a_log, d_skip, w_ssm_norm, w_q_norm, w_k_norm):
    o_z, o_xbc = 0, D_INNER
    o_dt = o_xbc + CONV_DIM
    o_q = o_dt + SSM_HEADS
    o_k = o_q + ATTN_WIDTH
    o_v = o_k + ATTN_WIDTH
    o_g = o_v + ATTN_WIDTH
    w_main = jnp.concatenate(
        [w_in[:, o_z:o_dt], w_in[:, o_g:], w_in[:, o_q:o_g]], axis=1).astype(BF16)
    w_dt = jnp.pad(w_in[:, o_dt:o_q], ((0, 0), (0, LANES - SSM_HEADS))).astype(BF16)
    qk_w = jnp.ones((N_PROJ_TILES, 1, PROJ_TILE), F32)
    qk_w = qk_w.at[Q_TILE0:K_TILE0, 0].set(jnp.tile(w_q_norm, HEADS_PER_GROUP)[None])
    qk_w = qk_w.at[K_TILE0:V_TILE0, 0].set(jnp.tile(w_k_norm, HEADS_PER_GROUP)[None])
    pad_h = lambda v: jnp.pad(v, (0, LANES - SSM_HEADS))[None]
    expand = (_iota((LANES, D_INNER), 0) == _iota((LANES, D_INNER), 1) // SSM_HEAD_DIM).astype(BF16)
    ssm = dict(w_conv=w_conv, b_conv=b_conv[None], dt_bias=pad_h(dt_bias), a_log=pad_h(a_log),
               d_skip=jnp.repeat(d_skip, SSM_HEAD_DIM)[None], w_norm=w_ssm_norm[None], expand=expand)
    return dict(w_main=w_main, w_dt=w_dt, qk_w=qk_w, norm_mix=norm_mix[None], ssm=ssm)


def _attn_prompt_kernel(q_ref, k_ref, v_ref, o_ref, lse_ref, kp_sc, vp_sc, *, span, dil):
    i = pl.program_id(1)
    blk = ATTN_BLOCK
    pairs = q_ref.shape[1] // LANES

    @pl.when(i == 0)
    def _():
        kp_sc[...] = jnp.zeros_like(kp_sc)
        vp_sc[...] = jnp.zeros_like(vp_sc)

    qi = _iota((blk, 2 * blk), 0)
    kc = _iota((blk, 2 * blk), 1)
    dist = blk + qi - kc
    mask = (dist >= 0) & (dist <= span) & ((kc >= blk) | (i > 0))
    lo_q = _iota((blk, LANES), 1) < ATTN_HEAD_DIM
    lo_kv = _iota((2 * blk, LANES), 1) < ATTN_HEAD_DIM

    def residue(r, carry):
        rows = pl.ds(r, blk, stride=dil) if dil > 1 else slice(None)
        for pr in range(pairs):
            ls = slice(pr * LANES, (pr + 1) * LANES)
            q = q_ref[rows, ls].astype(BF16)
            k2 = jnp.concatenate([kp_sc[rows, ls], k_ref[rows, ls]], axis=0).astype(BF16)
            v2 = jnp.concatenate([vp_sc[rows, ls], v_ref[rows, ls]], axis=0).astype(BF16)
            ps, lses = [], []
            for keep in (lo_q, jnp.logical_not(lo_q)):
                s = jnp.where(mask, _dot_nt(jnp.where(keep, q, 0.0), k2), -jnp.inf)
                m = jnp.max(s, axis=-1, keepdims=True)
                p = jnp.exp(s - m)
                den = jnp.sum(p, axis=-1, keepdims=True)
                ps.append((p / den).astype(BF16))
                lses.append(m + jnp.log(den))
            v_bd = jnp.concatenate([jnp.where(lo_kv, v2, 0.0), jnp.where(lo_kv, 0.0, v2)], axis=0)
            o_ref[rows, ls] = _dot(jnp.concatenate(ps, axis=1), v_bd)
            lse_ref[rows, ls] = jnp.where(lo_q, lses[0], lses[1])
        return carry

    if dil > 1:
        lax.fori_loop(0, dil, residue, 0, unroll=2)
    else:
        residue(0, 0)
    kp_sc[...] = k_ref[...]
    vp_sc[...] = v_ref[...]


def _attn_prompt(proj, gi):
    window, dil = ATTN_PATTERNS[gi]
    s = proj.shape[0]
    chunk = ATTN_BLOCK * dil
    assert s % chunk == 0 and window // dil <= ATTN_BLOCK
    w = LANES if dil > 1 else ATTN_GROUP_WIDTH
    split = ATTN_GROUP_WIDTH // w
    col = lambda tile0: (lambda j, i: (i, (tile0 + gi) * split + j))
    blk = (chunk, w)
    return pl.pallas_call(
        functools.partial(_attn_prompt_kernel, span=window // dil, dil=dil),
        grid=(split, s // chunk),
        in_specs=[pl.BlockSpec(blk, col(Q_TILE0)), pl.BlockSpec(blk, col(K_TILE0)),
                  pl.BlockSpec(blk, col(V_TILE0))],
        out_specs=[pl.BlockSpec(blk, lambda j, i: (i, j))] * 2,
        out_shape=[jax.ShapeDtypeStruct((s, ATTN_GROUP_WIDTH), F32)] * 2,
        scratch_shapes=[pltpu.VMEM(blk, F32), pltpu.VMEM(blk, F32)],
        compiler_params=pltpu.CompilerParams(
            dimension_semantics=("parallel", "arbitrary"), vmem_limit_bytes=VMEM_LIMIT),
        name=f"attn_prompt_g{gi}",
    )(proj, proj, proj)


def _attn_sample_kernel(q_ref, k_ref, v_ref, kc_ref, vc_ref, o_ref, lse_ref, ko_ref, vo_ref,
                        *, span, dil, pos0):
    l = SUBLANES
    nb = kc_ref.shape[0]
    nh, hd = HEADS_PER_GROUP, ATTN_HEAD_DIM
    rows = kc_ref.shape[3]
    w = nh * hd
    t_c = _iota((l, rows), 0)
    delta_c = rows + t_c - _iota((l, rows), 1)
    ok_c = ((delta_c <= span * dil) & ((delta_c & (dil - 1)) == 0) & (pos0 + t_c - delta_c >= 0))
    delta_n = _iota((l, l), 0) - _iota((l, l), 1)
    ok_n = (delta_n >= 0) & (delta_n <= span * dil) & ((delta_n & (dil - 1)) == 0)
    lane = _iota((hd, LANES), 1)
    hl = nh * l
    own = _shr(_iota((hl, w), 0), l) == _shr(_iota((hl, w), 1), hd)
    ok_c = jnp.concatenate([ok_c] * nh, axis=0)
    ok_n = jnp.concatenate([ok_n] * nh, axis=0)
    for bb in range(nb):
        rs = slice(bb * l, (bb + 1) * l)
        q, kn, vn = q_ref[rs, :], k_ref[rs, :], v_ref[rs, :]
        q_bd = jnp.where(own, jnp.concatenate([q] * nh, axis=0), 0.0)
        kt = kc_ref[bb].reshape(w, rows)
        vt = vc_ref[bb].reshape(w, rows)
        s_c = jnp.where(ok_c, _dot(q_bd, kt), -jnp.inf)
        s_n = jnp.where(ok_n, _dot_nt(q_bd, kn), -jnp.inf)
        m = jnp.maximum(jnp.max(s_c, axis=-1, keepdims=True), jnp.max(s_n, axis=-1, keepdims=True))
        p_c = jnp.exp(s_c - m)
        p_n = jnp.exp(s_n - m)
        den = jnp.sum(p_c, axis=-1, keepdims=True) + jnp.sum(p_n, axis=-1, keepdims=True)
        res = jnp.where(own, _dot_nt(p_c / den, vt) + _dot(p_n / den, vn), 0.0)
        lse = jnp.where(own, m + jnp.log(den), 0.0)
        o, lse_x = res[0:l], lse[0:l]
        for h in range(1, nh):
            o = o + res[h * l:(h + 1) * l]
            lse_x = lse_x + lse[h * l:(h + 1) * l]
        o_ref[rs, :] = o
        lse_ref[rs, :] = lse_x

        for new, c_ref, out_ref in ((kn, kc_ref, ko_ref), (vn, vc_ref, vo_ref)):
            padded = jnp.concatenate([jnp.zeros((LANES - l, w), F32), new], axis=0)
            tail = padded.T
            for h in range(nh):
                shifted = pltpu.roll(c_ref[bb, h], rows - l, axis=1)
                if rows > LANES:
                    out_ref[bb, h, :, 0:rows - LANES] = shifted[:, 0:rows - LANES]
                out_ref[bb, h, :, rows - LANES:rows] = jnp.where(
                    lane < LANES - l, shifted[:, rows - LANES:rows], tail[h * hd:(h + 1) * hd])


SAMPLE_CACHE_ROWS_PER_STEP = 2048
SAMPLE_MAX_SEQS_PER_STEP = 8


def _attn_sample(proj, k_cache, v_cache, gi, batch, pos0):
    window, dil = ATTN_PATTERNS[gi]
    rows = k_cache.shape[1]
    l = proj.shape[0] // batch
    assert l == SUBLANES and rows % LANES == 0
    nb = max(1, min(batch, SAMPLE_CACHE_ROWS_PER_STEP // rows, SAMPLE_MAX_SEQS_PER_STEP))
    assert batch % nb == 0
    tok = lambda tile0: pl.BlockSpec((nb * l, ATTN_GROUP_WIDTH), lambda b: (b, tile0 + gi))
    cache = pl.BlockSpec((nb, HEADS_PER_GROUP, ATTN_HEAD_DIM, rows), lambda b: (b, 0, 0, 0))
    out_tok = pl.BlockSpec((nb * l, ATTN_GROUP_WIDTH), lambda b: (b, 0))
    to_t = lambda c: jnp.transpose(c, (0, 2, 3, 1))
    t_shape = jax.ShapeDtypeStruct((batch, HEADS_PER_GROUP, ATTN_HEAD_DIM, rows), F32)
    o, lse, k_t, v_t = pl.pallas_call(
        functools.partial(_attn_sample_kernel, span=window // dil, dil=dil, pos0=pos0),
        grid=(batch // nb,),
        in_specs=[tok(Q_TILE0), tok(K_TILE0), tok(V_TILE0), cache, cache],
        out_specs=[out_tok, out_tok, cache, cache],
        out_shape=[jax.ShapeDtypeStruct((batch * l, ATTN_GROUP_WIDTH), F32)] * 2 + [t_shape] * 2,
        compiler_params=pltpu.CompilerParams(
            dimension_semantics=("parallel",), vmem_limit_bytes=VMEM_LIMIT),
        name=f"attn_sample_g{gi}",
    )(proj, proj, proj, to_t(k_cache), to_t(v_cache))
    from_t = lambda c: jnp.transpose(c, (0, 3, 1, 2))
    return o, lse, from_t(k_t), from_t(v_t)


def _merge_kernel(x_ref, ys_ref, o0_ref, o1_ref, o2_ref, l0_ref, l1_ref, l2_ref, g0_ref, g1_ref,
                  wbs_ref, wba_ref, wo_ref, nf_ref, wr1_ref, wr2_ref, br_ref,
                  x1_ref, xn_ref, lg_ref):
    l0, l1, l2 = l0_ref[...], l1_ref[...], l2_ref[...]
    m = jnp.maximum(jnp.maximum(l0, l1), l2)
    e0, e1, e2 = jnp.exp(l0 - m), jnp.exp(l1 - m), jnp.exp(l2 - m)
    den = e0 + e1 + e2
    y_attn = (e0 / den) * o0_ref[...] + (e1 / den) * o1_ref[...] + (e2 / den) * o2_ref[...]
    br_s = _dot(ys_ref[...].astype(BF16), wbs_ref[...])
    br_a = _dot(y_attn.astype(BF16), wba_ref[...])
    merged = jax.nn.sigmoid(g0_ref[...]) * br_s + jax.nn.sigmoid(g1_ref[...]) * br_a
    x1 = x_ref[...] + _dot(merged.astype(BF16), wo_ref[...])
    x1_ref[...] = x1
    xn = x1 * lax.rsqrt(jnp.mean(x1 * x1, axis=-1, keepdims=True) + EPS) * nf_ref[...]
    xn1 = xn.astype(BF16)
    xn_ref[...] = xn1
    xn2 = (xn - xn1.astype(F32)).astype(BF16)
    lg_ref[...] = (_dot(xn1, wr1_ref[...]) + _dot(xn2, wr1_ref[...]) + _dot(xn1, wr2_ref[...])
                   + br_ref[...])


def _merge(x, proj, y_ssm, attn, mp, tm):
    t = x.shape[0]
    row = lambda w: pl.BlockSpec((tm, w), lambda i: (i, 0))
    full = lambda a: pl.BlockSpec(a.shape, lambda i: (0,) * a.ndim)
    (o0, l0), (o1, l1), (o2, l2) = attn
    weights = [mp["w_br_ssm"], mp["w_br_attn"], mp["w_out"], mp["norm_ffn"], mp["w_r1"], mp["w_r2"], mp["b_r"]]
    return pl.pallas_call(
        _merge_kernel,
        grid=(t // tm,),
        in_specs=[row(D_MODEL), row(D_INNER)] + [row(ATTN_GROUP_WIDTH)] * 6 + [
            pl.BlockSpec((tm, D_MODEL), lambda i: (i, OFF_G // D_MODEL)),
            pl.BlockSpec((tm, D_MODEL), lambda i: (i, OFF_G // D_MODEL + 1)),
        ] + [full(w) for w in weights],
        out_specs=[row(D_MODEL), row(D_MODEL), row(ROUTER_WIDTH)],
        out_shape=[
            jax.ShapeDtypeStruct((t, D_MODEL), F32),
            jax.ShapeDtypeStruct((t, D_MODEL), BF16),
            jax.ShapeDtypeStruct((t, ROUTER_WIDTH), F32),
        ],
        compiler_params=pltpu.CompilerParams(
            dimension_semantics=("parallel",), vmem_limit_bytes=VMEM_LIMIT),
        name="merge",
    )(x, y_ssm, o0, o1, o2, l0, l1, l2, proj, proj, *weights)


def _prep_merge_params(w_br_ssm, w_br_attn, w_out, norm_ffn, w_router_group, b_router_group,
                       w_router_expert, b_router_expert):
    w_r = jnp.concatenate([w_router_group, w_router_expert.reshape(D_MODEL, N_EXPERTS)], axis=1)
    w_r = jnp.pad(w_r, ((0, 0), (0, ROUTER_WIDTH - w_r.shape[1])))
    w_r1 = w_r.astype(BF16)
    w_r2 = (w_r - w_r1.astype(F32)).astype(BF16)
    b_r = jnp.concatenate([b_router_group, b_router_expert.reshape(N_EXPERTS)])
    b_r = jnp.pad(b_r, (0, ROUTER_WIDTH - b_r.shape[0]))[None]
    return dict(w_br_ssm=w_br_ssm.astype(BF16), w_br_attn=w_br_attn.astype(BF16), w_out=w_out.astype(BF16),
                norm_ffn=norm_ffn[None], w_r1=w_r1, w_r2=w_r2, b_r=b_r)


def _route(logits):
    lane = _iota(logits.shape, 1)
    big = jnp.int32(2 ** 30)
    neg = -jnp.inf
    gl = jnp.where(lane < N_EXPERT_GROUPS, logits, neg)
    gmax = jnp.max(gl, axis=-1, keepdims=True)
    gidx = jnp.min(jnp.where(gl == gmax, lane, big), axis=-1, keepdims=True)
    g_w = 1.0 / jnp.sum(jnp.exp(gl - gmax), axis=-1, keepdims=True)
    lo = N_EXPERT_GROUPS + gidx * EXPERTS_PER_GROUP
    el = jnp.where((lane >= lo) & (lane < lo + EXPERTS_PER_GROUP), logits, neg)
    v1 = jnp.max(el, axis=-1, keepdims=True)
    i1 = jnp.min(jnp.where(el == v1, lane, big), axis=-1, keepdims=True)
    el2 = jnp.where(lane == i1, neg, el)
    v2 = jnp.max(el2, axis=-1, keepdims=True)
    i2 = jnp.min(jnp.where(el2 == v2, lane, big), axis=-1, keepdims=True)
    e2 = jnp.exp(v2 - v1)
    w1 = (1.0 / (1.0 + e2)) * g_w
    w2 = (e2 / (1.0 + e2)) * g_w
    return jnp.where(lane == i1, w1, jnp.where(lane == i2, w2, 0.0)), gidx


MOE_BLOCK = 256


def _moe_kernel(x1_ref, xn_ref, lg_ref, wg_ref, wu_ref, wd_ref, out_ref,
                comb_sc, oh_sc, cum_sc, oht_sc, cumt_sc):
    g = pl.program_id(1)
    tm = x1_ref.shape[0]
    blk = MOE_BLOCK

    @pl.when(g == 0)
    def _():
        comb, gidx = _route(lg_ref[...])
        c1, c2, c3 = _split3(comb)
        comb_sc[0] = c1
        comb_sc[1] = c2
        comb_sc[2] = c3
        onehot = (_iota((tm, LANES), 1) == gidx).astype(F32)
        oh_bf = onehot.astype(BF16)
        lower = (_iota((tm, tm), 0) > _iota((tm, tm), 1)).astype(BF16)
        upper = (_iota((tm, tm), 0) < _iota((tm, tm), 1)).astype(BF16)
        oh_sc[...] = onehot
        cum_sc[...] = _dot(lower, oh_bf)
        oht_sc[...] = onehot.T[0:SUBLANES]
        cumt_sc[...] = _dot_tn(oh_bf, upper)[0:SUBLANES]
        out_ref[...] = x1_ref[...]

    lane = _iota((tm, LANES), 1)
    pick = lane == g
    member_c = jnp.sum(jnp.where(pick, oh_sc[...], 0.0), axis=-1, keepdims=True)
    rank_c = jnp.sum(jnp.where(pick, cum_sc[...], 0.0), axis=-1, keepdims=True)
    member_r = oht_sc[pl.ds(g, 1), :]
    rank_r = cumt_sc[pl.ds(g, 1), :]
    count = jnp.sum(member_r).astype(jnp.int32)
    nblk = (count + (blk - 1)) // blk
    xn = xn_ref[...]
    wd = wd_ref[...].reshape(EXPERTS_PER_GROUP * D_EXPERT, D_MODEL)
    lane_b = _iota((blk, LANES), 1)

    def body(k, carry):
        base = (k * blk).astype(F32)
        want_r = base + _iota((blk, tm), 0).astype(F32)
        sel = jnp.where((rank_r == want_r) & (member_r > 0.0), 1.0, 0.0).astype(BF16)
        want_c = base + _iota((tm, blk), 1).astype(F32)
        sel_t = jnp.where((rank_c == want_c) & (member_c > 0.0), 1.0, 0.0).astype(BF16)
        xg = _dot(sel, xn).astype(BF16)
        cg = _dot(sel, comb_sc[0]) + _dot(sel, comb_sc[1]) + _dot(sel, comb_sc[2])
        hidden = []
        for e in range(EXPERTS_PER_GROUP):
            col = N_EXPERT_GROUPS + g * EXPERTS_PER_GROUP + e
            cw = jnp.sum(jnp.where(lane_b == col, cg, 0.0), axis=-1, keepdims=True)
            h = jax.nn.silu(_dot(xg, wg_ref[e])) * _dot(xg, wu_ref[e]) * cw
            hidden.append(h.astype(BF16))
        out = _dot(jnp.concatenate(hidden, axis=1), wd)
        o1 = out.astype(BF16)
        o2 = (out - o1.astype(F32)).astype(BF16)
        out_ref[...] += _dot(sel_t, o1) + _dot(sel_t, o2)
        return carry

    lax.fori_loop(0, nblk, body, 0)


def _moe(x1, xn, logits, wg, wu, wd, tm):
    t = x1.shape[0]
    ge = EXPERTS_PER_GROUP
    return pl.pallas_call(
        _moe_kernel,
        grid=(t // tm, N_EXPERT_GROUPS),
        in_specs=[
            pl.BlockSpec((tm, D_MODEL), lambda i, g: (i, 0), pipeline_mode=pl.Buffered(1)),
            pl.BlockSpec((tm, D_MODEL), lambda i, g: (i, 0)),
            pl.BlockSpec((tm, ROUTER_WIDTH), lambda i, g: (i, 0), pipeline_mode=pl.Buffered(1)),
            pl.BlockSpec((ge, D_MODEL, D_EXPERT), lambda i, g: (g, 0, 0)),
            pl.BlockSpec((ge, D_MODEL, D_EXPERT), lambda i, g: (g, 0, 0)),
            pl.BlockSpec((ge, D_EXPERT, D_MODEL), lambda i, g: (g, 0, 0)),
        ],
        out_specs=pl.BlockSpec((tm, D_MODEL), lambda i, g: (i, 0)),
        out_shape=jax.ShapeDtypeStruct((t, D_MODEL), F32),
        scratch_shapes=[
            pltpu.VMEM((3, tm, ROUTER_WIDTH), BF16),
            pltpu.VMEM((tm, LANES), F32),
            pltpu.VMEM((tm, LANES), F32),
            pltpu.VMEM((SUBLANES, tm), F32),
            pltpu.VMEM((SUBLANES, tm), F32),
        ],
        compiler_params=pltpu.CompilerParams(
            dimension_semantics=("parallel", "arbitrary"), vmem_limit_bytes=VMEM_LIMIT),
        name="moe",
    )(x1, xn, logits, wg, wu, wd)


def _moe_dense_kernel(x1_ref, xn_ref, lg_ref, wg_ref, wu_ref, wd_ref, out_ref, comb_sc, acc_sc):
    e = pl.program_id(1)

    @pl.when(e == 0)
    def _():
        comb_sc[...] = _route(lg_ref[...])[0]
        acc_sc[...] = jnp.zeros_like(acc_sc)

    lane = _iota(comb_sc.shape, 1)
    cw = jnp.sum(jnp.where(lane == e + N_EXPERT_GROUPS, comb_sc[...], 0.0), axis=-1, keepdims=True)
    xn = xn_ref[...]
    hidden = jax.nn.silu(_dot(xn, wg_ref[0])) * _dot(xn, wu_ref[0]) * cw
    acc_sc[...] += _dot(hidden.astype(BF16), wd_ref[0])

    @pl.when(e == pl.num_programs(1) - 1)
    def _():
        out_ref[...] = x1_ref[...] + acc_sc[...]


def _moe_dense(x1, xn, logits, wg, wu, wd, tm):
    t = x1.shape[0]
    return pl.pallas_call(
        _moe_dense_kernel,
        grid=(t // tm, N_EXPERTS),
        in_specs=[
            pl.BlockSpec((tm, D_MODEL), lambda i, e: (i, 0)),
            pl.BlockSpec((tm, D_MODEL), lambda i, e: (i, 0)),
            pl.BlockSpec((tm, ROUTER_WIDTH), lambda i, e: (i, 0)),
            pl.BlockSpec((1, D_MODEL, D_EXPERT), lambda i, e: (e, 0, 0)),
            pl.BlockSpec((1, D_MODEL, D_EXPERT), lambda i, e: (e, 0, 0)),
            pl.BlockSpec((1, D_EXPERT, D_MODEL), lambda i, e: (e, 0, 0)),
        ],
        out_specs=pl.BlockSpec((tm, D_MODEL), lambda i, e: (i, 0)),
        out_shape=jax.ShapeDtypeStruct((t, D_MODEL), F32),
        scratch_shapes=[pltpu.VMEM((tm, ROUTER_WIDTH), F32), pltpu.VMEM((tm, D_MODEL), F32)],
        compiler_params=pltpu.CompilerParams(
            dimension_semantics=("parallel", "arbitrary"), vmem_limit_bytes=VMEM_LIMIT),
        name="moe_dense",
    )(x1, xn, logits, wg, wu, wd)


def _tile_rows(t, cap):
    tm = min(t, cap)
    assert t % tm == 0
    return tm


def _decoder_layer(x, conv_buf, ssm_state, kv_bufs, pos0, pp, mp, ep):
    b, l, d = x.shape
    t = b * l
    x2 = x.reshape(t, d)
    proj, dt_raw = _in_proj(x2, pp["norm_mix"], pp["w_main"], pp["w_dt"], pp["qk_w"], _tile_rows(t, 2048))
    proj3 = proj.reshape(b, l, PROJ_WIDTH)
    h0 = ssm_state.reshape(b, D_INNER, SSM_STATE)
    if kv_bufs is None:
        assert b == 1 and l % SSD_CHUNK == 0
        y_ssm, h_new = _ssm_chunked(proj, dt_raw, conv_buf, h0, pp["ssm"], b, SSD_CHUNK)
        attn = [_attn_prompt(proj, gi) for gi in range(N_ATTN_GROUPS)]
        new_kv = []
        for gi, (window, _) in enumerate(ATTN_PATTERNS):
            keep = min(window, l)
            for off in (OFF_K, OFF_V):
                c0 = off + gi * ATTN_GROUP_WIDTH
                new_kv.append(proj3[:, l - keep:, c0:c0 + ATTN_GROUP_WIDTH]
                              .reshape(b, keep, HEADS_PER_GROUP, ATTN_HEAD_DIM))
    else:
        y_ssm, h_new = _ssm_step(proj, dt_raw, conv_buf, h0, pp["ssm"], b, l)
        attn, new_kv = [], []
        for gi in range(N_ATTN_GROUPS):
            kc, vc = kv_bufs[2 * gi], kv_bufs[2 * gi + 1]
            o, lse, kn, vn = _attn_sample(proj, kc, vc, gi, b, pos0)
            attn.append((o, lse))
            new_kv += [kn, vn]
    new_conv = proj3[:, l - (CONV_WIDTH - 1):, OFF_X:OFF_X + CONV_DIM]
    tm = _tile_rows(t, 512)
    x1, xn, logits = _merge(x2, proj, y_ssm, attn, mp, tm)
    y = _moe(x1, xn, logits, ep["wg"], ep["wu"], ep["wd"], _tile_rows(t, 1024))
    new_ssm = h_new.reshape(b, SSM_HEADS, SSM_HEAD_DIM, SSM_STATE)
    return y.reshape(b, l, d), (new_conv, new_ssm) + tuple(new_kv)


def kernel(x_prompt, x_sample, cache_conv, state_ssm, cache_k_w128, cache_v_w128, cache_k_w512, cache_v_w512,
           cache_k_w2048, cache_v_w2048, w_in, norm_mix, w_conv, b_conv, dt_bias, a_log, d_skip, w_ssm_norm,
           w_q_norm, w_k_norm, w_br_ssm, w_br_attn, w_out, norm_ffn, w_router_group, b_router_group,
           w_router_expert, b_router_expert, w_expert_gate, w_expert_up, w_expert_down):
    depth = w_in.shape[0]
    yp, ys = x_prompt, x_sample
    bp = x_prompt.shape[0]
    prompt_states, sample_states = [], []
    for i in range(depth):
        pp = _prep_params(w_in[i], norm_mix[i], w_conv[i], b_conv[i], dt_bias[i], a_log[i], d_skip[i],
                          w_ssm_norm[i], w_q_norm[i], w_k_norm[i])
        mp = _prep_merge_params(w_br_ssm[i], w_br_attn[i], w_out[i], norm_ffn[i], w_router_group[i],
                                b_router_group[i], w_router_expert[i], b_router_expert[i])
        ep = dict(wg=w_expert_gate[i].astype(BF16), wu=w_expert_up[i].astype(BF16),
                  wd=w_expert_down[i].astype(BF16))
        zero_conv = jnp.zeros((bp, CONV_WIDTH - 1, CONV_DIM), x_prompt.dtype)
        zero_ssm = jnp.zeros((bp, SSM_HEADS, SSM_HEAD_DIM, SSM_STATE), F32)
        yp, st_p = _decoder_layer(yp, zero_conv, zero_ssm, None, 0, pp, mp, ep)
        kv_bufs = (cache_k_w128[i], cache_v_w128[i], cache_k_w512[i], cache_v_w512[i],
                   cache_k_w2048[i], cache_v_w2048[i])
        ys, st_s = _decoder_layer(ys, cache_conv[i], state_ssm[i], kv_bufs, PAST_LEN, pp, mp, ep)
        prompt_states.append(st_p)
        sample_states.append(st_s)
    stack = lambda states: [jnp.stack([s[j] for s in states]) for j in range(8)]
    return (yp, ys) + tuple(stack(prompt_states)) + tuple(stack(sample_states))
```

```python
import functools
import math

import jax
import jax.numpy as jnp
from jax import lax
from jax.experimental import pallas as pl
from jax.experimental.pallas import tpu as pltpu

F32 = jnp.float32
BF16 = jnp.bfloat16

PAST_LEN = 8192
D_MODEL = 1024
D_INNER = 2048
SSM_HEAD_DIM = 64
SSM_HEADS = 32
SSM_GROUPS = 8
SSM_STATE = 128
CONV_WIDTH = 4
CONV_DIM = D_INNER + 2 * SSM_GROUPS * SSM_STATE
SSD_CHUNK = 128
ATTN_HEAD_DIM = 64
HEADS_PER_GROUP = 8
ATTN_PATTERNS = ((128, 1), (512, 4), (2048, 16))
N_ATTN_GROUPS = 3
ATTN_GROUP_WIDTH = HEADS_PER_GROUP * ATTN_HEAD_DIM
ATTN_WIDTH = N_ATTN_GROUPS * ATTN_GROUP_WIDTH
ATTN_BLOCK = 128
N_EXPERT_GROUPS = 4
EXPERTS_PER_GROUP = 8
N_EXPERTS = 32
D_EXPERT = 256
EPS = 1e-6

LANES = 128
SUBLANES = 8
VMEM_LIMIT = 56 * 1024 * 1024

PROJ_TILE = 512
OFF_Z = 0
OFF_X = 2048
OFF_B = 4096
OFF_C = 5120
OFF_G = 6144
OFF_Q = 8192
OFF_K = OFF_Q + ATTN_WIDTH
OFF_V = OFF_K + ATTN_WIDTH
PROJ_WIDTH = OFF_V + ATTN_WIDTH
N_PROJ_TILES = PROJ_WIDTH // PROJ_TILE
Q_TILE0 = OFF_Q // PROJ_TILE
K_TILE0 = OFF_K // PROJ_TILE
V_TILE0 = OFF_V // PROJ_TILE
ROUTER_WIDTH = LANES


def _split3(x):
    x1 = x.astype(BF16)
    r1 = x - x1.astype(F32)
    x2 = r1.astype(BF16)
    r2 = r1 - x2.astype(F32)
    return x1, x2, r2.astype(BF16)


def _dot(a, b):
    return jnp.dot(a, b, preferred_element_type=F32)


def _dot3(x, m):
    x1, x2, x3 = _split3(x)
    return _dot(x1, m) + _dot(x2, m) + _dot(x3, m)


def _dot3f(x, m):
    x1, x2, x3 = _split3(x)
    return _dot(x1.astype(F32), m) + _dot(x2.astype(F32), m) + _dot(x3.astype(F32), m)


def _dot_nt(a, b):
    return lax.dot_general(a, b, (((1,), (1,)), ((), ())), preferred_element_type=F32)


def _dot_tn(a, b):
    return lax.dot_general(a, b, (((0,), (0,)), ((), ())), preferred_element_type=F32)


def _iota(shape, dim):
    return lax.broadcasted_iota(jnp.int32, shape, dim)


def _shr(x, pow2):
    assert pow2 & (pow2 - 1) == 0
    return lax.shift_right_logical(x, jnp.int32(int(math.log2(pow2))))


def _in_proj_kernel(x_ref, nw_ref, w_ref, wdt_ref, qkw_ref, seg_ref, out_ref, dt_ref, h_sc):
    j = pl.program_id(1)

    @pl.when(j == 0)
    def _():
        xf = x_ref[...]
        y = xf * lax.rsqrt(jnp.mean(xf * xf, axis=-1, keepdims=True) + EPS)
        h = (y * nw_ref[...]).astype(BF16)
        h_sc[...] = h
        dt_ref[...] = _dot(h, wdt_ref[...])

    acc = _dot(h_sc[...], w_ref[...])
    is_qk = jnp.logical_and(j >= Q_TILE0, j < V_TILE0)

    @pl.when(jnp.logical_not(is_qk))
    def _():
        out_ref[...] = acc

    @pl.when(is_qk)
    def _():
        sq = acc * acc
        s1 = sq.astype(BF16)
        s2 = (sq - s1.astype(F32)).astype(BF16)
        ms = (_dot(s1, seg_ref[...]) + _dot(s2, seg_ref[...])) * (1.0 / ATTN_HEAD_DIM)
        scale = jnp.where(j < K_TILE0, ATTN_HEAD_DIM ** -0.5, 1.0).astype(F32)
        w = qkw_ref[0] * scale
        out_ref[...] = acc * lax.rsqrt(ms + EPS) * w


def _in_proj(x, norm_w, w_bf, wdt_bf, qk_w, tm):
    t = x.shape[0]
    seg = (_iota((PROJ_TILE, PROJ_TILE), 0) // ATTN_HEAD_DIM
           == _iota((PROJ_TILE, PROJ_TILE), 1) // ATTN_HEAD_DIM).astype(BF16)
    return pl.pallas_call(
        _in_proj_kernel,
        grid=(t // tm, N_PROJ_TILES),
        in_specs=[
            pl.BlockSpec((tm, D_MODEL), lambda i, j: (i, 0)),
            pl.BlockSpec((1, D_MODEL), lambda i, j: (0, 0)),
            pl.BlockSpec((D_MODEL, PROJ_TILE), lambda i, j: (0, j)),
            pl.BlockSpec((D_MODEL, LANES), lambda i, j: (0, 0)),
            pl.BlockSpec((1, 1, PROJ_TILE), lambda i, j: (j, 0, 0)),
            pl.BlockSpec((PROJ_TILE, PROJ_TILE), lambda i, j: (0, 0)),
        ],
        out_specs=[
            pl.BlockSpec((tm, PROJ_TILE), lambda i, j: (i, j)),
            pl.BlockSpec((tm, LANES), lambda i, j: (i, 0)),
        ],
        out_shape=[
            jax.ShapeDtypeStruct((t, PROJ_WIDTH), F32),
            jax.ShapeDtypeStruct((t, LANES), F32),
        ],
        scratch_shapes=[pltpu.VMEM((tm, D_MODEL), BF16)],
        compiler_params=pltpu.CompilerParams(
            dimension_semantics=("parallel", "arbitrary"), vmem_limit_bytes=VMEM_LIMIT),
        name="in_proj",
    )(x, norm_w, w_bf, wdt_bf, qk_w, seg)


def _softplus(x):
    return jnp.maximum(x, 0.0) + jnp.log1p(jnp.exp(-jnp.abs(x)))


def _conv_silu(raw, tail, w, b):
    q = raw.shape[0]
    up = jnp.concatenate([tail, raw], axis=0)
    acc = b + w[CONV_WIDTH - 1:CONV_WIDTH] * raw
    for tap in range(CONV_WIDTH - 1):
        shifted = pltpu.roll(up, CONV_WIDTH - 1 - tap, axis=0)[SUBLANES:SUBLANES + q]
        acc = acc + w[tap:tap + 1] * shifted
    return jax.nn.silu(acc)


def _ssm_chunk_kernel(z_ref, xs_ref, bm_ref, cm_ref, dt_ref, cbuf_ref, h0_ref, wconv_ref, bconv_ref,
                      dtb_ref, alog_ref, dskip_ref, wnorm_ref, expand_ref,
                      y_ref, hout_ref, tail_sc, xbc_sc, ht_sc):
    c = pl.program_id(1)
    nc = pl.num_programs(1)
    q = xs_ref.shape[0]
    gw = D_INNER // SSM_GROUPS
    n = SSM_STATE

    @pl.when(c == 0)
    def _():
        tail_sc[...] = jnp.zeros_like(tail_sc)
        tail_sc[SUBLANES - (CONV_WIDTH - 1):SUBLANES, :] = cbuf_ref[0]
        for g in range(SSM_GROUPS):
            ht_sc[g] = h0_ref[0, g * gw:(g + 1) * gw, :].T

    cw = PROJ_TILE
    for k in range(CONV_DIM // cw):
        if k < D_INNER // cw:
            raw = xs_ref[:, k * cw:(k + 1) * cw]
        elif k < (D_INNER + SSM_GROUPS * n) // cw:
            kk = k - D_INNER // cw
            raw = bm_ref[:, kk * cw:(kk + 1) * cw]
        else:
            kk = k - (D_INNER + SSM_GROUPS * n) // cw
            raw = cm_ref[:, kk * cw:(kk + 1) * cw]
        sl = slice(k * cw, (k + 1) * cw)
        xbc_sc[:, sl] = _conv_silu(raw, tail_sc[:, sl], wconv_ref[:, sl], bconv_ref[:, sl])
        tail_sc[:, sl] = raw[q - SUBLANES:q]

    dt = _softplus(dt_ref[...] + dtb_ref[...])
    a = -jnp.exp(alog_ref[...])
    dta = dt * a
    tri = (_iota((q, q), 0) >= _iota((q, q), 1)).astype(BF16)
    d1, d2, d3 = _split3(dta)
    acs = _dot(tri, d1) + _dot(tri, d2) + _dot(tri, d3)
    eye = (_iota((LANES, LANES), 0) == _iota((LANES, LANES), 1)).astype(BF16)
    c1, c2, c3 = _split3(acs)
    acs_t = _dot_nt(eye, c1) + _dot_nt(eye, c2) + _dot_nt(eye, c3)
    a_end = acs[q - 1:q, :]
    e_in = jnp.exp(acs)
    e_out = jnp.exp(a_end - acs)
    e_all = jnp.exp(a_end)

    ex = expand_ref[...]
    dt_x = _dot3(dt, ex)
    e_in_x = _dot3(e_in, ex)
    e_out_x = _dot3(e_out, ex)
    e_all_x = _dot3(jnp.broadcast_to(e_all, (2 * SUBLANES, LANES)), ex)[0:1]

    causal = _iota((q, q), 0) >= _iota((q, q), 1)
    lane_lo = _iota((q, LANES), 1) < SSM_HEAD_DIM

    for g in range(SSM_GROUPS):
        gs = slice(g * gw, (g + 1) * gw)
        xg = xbc_sc[:, gs]
        bg = xbc_sc[:, D_INNER + g * n:D_INNER + (g + 1) * n].astype(BF16)
        cg = xbc_sc[:, D_INNER + SSM_GROUPS * n + g * n:D_INNER + SSM_GROUPS * n + (g + 1) * n].astype(BF16)
        xdt = xg * dt_x[:, gs]
        cb = _dot_nt(cg, bg)
        ht = ht_sc[g]
        y_off = _dot(cg, ht.astype(BF16)) * e_in_x[:, gs]
        pieces = []
        for pair in range(2):
            ms = []
            for hh in range(2):
                h = g * 4 + pair * 2 + hh
                diff = acs[:, h:h + 1] - acs_t[h:h + 1, :]
                lmat = jnp.where(causal, jnp.exp(jnp.where(causal, diff, 0.0)), 0.0)
                ms.append((cb * lmat).astype(BF16))
            xp = xdt[:, pair * LANES:(pair + 1) * LANES]
            rhs = jnp.concatenate([jnp.where(lane_lo, xp, 0.0), jnp.where(lane_lo, 0.0, xp)],
                                  axis=0).astype(BF16)
            pieces.append(_dot(jnp.concatenate(ms, axis=1), rhs))
        y_diag = jnp.concatenate(pieces, axis=1)
        yg = y_diag + y_off + dskip_ref[:, gs] * xg
        zg = z_ref[:, gs]
        u = yg * jax.nn.silu(zg)
        u = u * lax.rsqrt(jnp.mean(u * u, axis=-1, keepdims=True) + EPS)
        y_ref[:, gs] = (u * wnorm_ref[:, gs]).astype(y_ref.dtype)
        st = _dot_tn(bg, (xdt * e_out_x[:, gs]).astype(BF16))
        ht_sc[g] = e_all_x[:, gs] * ht + st

    @pl.when(c == nc - 1)
    def _():
        for g in range(SSM_GROUPS):
            hout_ref[0, g * gw:(g + 1) * gw, :] = ht_sc[g].T


def _ssm_chunked(proj, dt_raw, conv_buf, h0, sp, batch, q):
    t = proj.shape[0]
    nc = t // batch // q
    row = lambda b, c: b * nc + c
    col = lambda off, w: off // w
    full = lambda shape: pl.BlockSpec(shape, lambda b, c: (0,) * len(shape))
    return pl.pallas_call(
        _ssm_chunk_kernel,
        grid=(batch, nc),
        in_specs=[
            pl.BlockSpec((q, D_INNER), lambda b, c: (row(b, c), col(OFF_Z, D_INNER))),
            pl.BlockSpec((q, D_INNER), lambda b, c: (row(b, c), col(OFF_X, D_INNER))),
            pl.BlockSpec((q, 1024), lambda b, c: (row(b, c), col(OFF_B, 1024))),
            pl.BlockSpec((q, 1024), lambda b, c: (row(b, c), col(OFF_C, 1024))),
            pl.BlockSpec((q, LANES), lambda b, c: (row(b, c), 0)),
            pl.BlockSpec((1, CONV_WIDTH - 1, CONV_DIM), lambda b, c: (b, 0, 0)),
            pl.BlockSpec((1, D_INNER, SSM_STATE), lambda b, c: (b, 0, 0)),
            full((CONV_WIDTH, CONV_DIM)),
            full((1, CONV_DIM)),
            full((1, LANES)),
            full((1, LANES)),
            full((1, D_INNER)),
            full((1, D_INNER)),
            full((LANES, D_INNER)),
        ],
        out_specs=[
            pl.BlockSpec((q, D_INNER), lambda b, c: (row(b, c), 0)),
            pl.BlockSpec((1, D_INNER, SSM_STATE), lambda b, c: (b, 0, 0)),
        ],
        out_shape=[
            jax.ShapeDtypeStruct((t, D_INNER), BF16),
            jax.ShapeDtypeStruct((batch, D_INNER, SSM_STATE), F32),
        ],
        scratch_shapes=[
            pltpu.VMEM((SUBLANES, CONV_DIM), F32),
            pltpu.VMEM((q, CONV_DIM), F32),
            pltpu.VMEM((SSM_GROUPS, SSM_STATE, D_INNER // SSM_GROUPS), F32),
        ],
        compiler_params=pltpu.CompilerParams(
            dimension_semantics=("parallel", "arbitrary"), vmem_limit_bytes=VMEM_LIMIT),
        name="ssm_chunked",
    )(proj, proj, proj, proj, dt_raw, conv_buf, h0, sp["w_conv"], sp["b_conv"], sp["dt_bias"],
      sp["a_log"], sp["d_skip"], sp["w_norm"], sp["expand"])


def _ssm_step_kernel(z_ref, xs_ref, bm_ref, cm_ref, dt_ref, cbuf_ref, h0_ref, wconv_ref, bconv_ref,
                     dtb_ref, alog_ref, dskip_ref, wnorm_ref, expand_ref, gsum_ref,
                     y_ref, hout_ref, tail_sc):
    q = xs_ref.shape[0]
    gw = D_INNER // SSM_GROUPS
    n = SSM_STATE
    tail_sc[...] = jnp.zeros_like(tail_sc)
    tail_sc[SUBLANES - (CONV_WIDTH - 1):SUBLANES, :] = cbuf_ref[0]
    xs = _conv_silu(xs_ref[...], tail_sc[:, 0:D_INNER], wconv_ref[:, 0:D_INNER], bconv_ref[:, 0:D_INNER])
    o_b, o_c = D_INNER, D_INNER + SSM_GROUPS * n
    bm = _conv_silu(bm_ref[...], tail_sc[:, o_b:o_c], wconv_ref[:, o_b:o_c], bconv_ref[:, o_b:o_c])
    cm = _conv_silu(cm_ref[...], tail_sc[:, o_c:], wconv_ref[:, o_c:], bconv_ref[:, o_c:])
    bm = bm.astype(BF16).astype(F32)
    cm = cm.astype(BF16).astype(F32)

    dt = _softplus(dt_ref[...] + dtb_ref[...])
    a = -jnp.exp(alog_ref[...])
    tri = (_iota((q, q), 0) >= _iota((q, q), 1)).astype(F32)
    d1, d2, d3 = _split3(dt * a)
    acs = _dot(tri, d1.astype(F32)) + _dot(tri, d2.astype(F32)) + _dot(tri, d3.astype(F32))
    a_end = acs[q - 1:q, :]

    row = _iota((q, LANES), 0)
    prods, lms = [], []
    for s in range(q):
        prods.append(cm * bm[s:s + 1, :])
        keep = row >= s
        lms.append(jnp.where(keep, jnp.exp(jnp.where(keep, acs - acs[s:s + 1, :], 0.0)), 0.0))
    qq = q * q
    cb3 = _dot(jnp.concatenate(_split3(jnp.concatenate(prods, axis=0)), axis=0), gsum_ref[...])
    cbs = cb3[0:qq] + cb3[qq:2 * qq] + cb3[2 * qq:3 * qq]
    gmat = (cbs * jnp.concatenate(lms, axis=0)).astype(BF16)
    quant = jnp.concatenate([dt, jnp.exp(acs), jnp.exp(a_end - acs),
                             jnp.broadcast_to(jnp.exp(a_end), (q, LANES))], axis=0)
    nq = 4 * q
    big = _dot(jnp.concatenate(_split3(quant) + (gmat,), axis=0), expand_ref[...])
    qx = big[0:nq] + big[nq:2 * nq] + big[2 * nq:3 * nq]
    dt_x, e_in_x, e_out_x, e_all_x = (qx[i * q:(i + 1) * q] for i in range(4))
    g_x = big[3 * nq:3 * nq + qq]
    xdt = xs * dt_x
    xdt_r = xdt.astype(BF16).astype(F32)
    y = dskip_ref[...] * xs
    for s in range(q):
        y = y + g_x[s * q:(s + 1) * q, :] * xdt_r[s:s + 1, :]
    xdte = xdt * e_out_x

    zpad = jnp.zeros((q, SSM_GROUPS * n), F32)
    cm16 = jnp.concatenate([cm, zpad], axis=0).astype(BF16)
    ones_lo = jnp.where(_iota((2 * q, n), 0) >= q, 1.0, 0.0)
    for g in range(SSM_GROUPS):
        gs = slice(g * gw, (g + 1) * gw)
        h0 = h0_ref[0, gs, :]
        yg = y[:, gs] + _dot_nt(cm16[:, g * n:(g + 1) * n], h0.astype(BF16))[0:q] * e_in_x[:, gs]
        u = yg * jax.nn.silu(z_ref[:, gs])
        u = u * lax.rsqrt(jnp.mean(u * u, axis=-1, keepdims=True) + EPS)
        y_ref[:, gs] = u * wnorm_ref[:, gs]
        e1, e2, e3 = _split3(e_all_x[:, gs])
        srow = _iota((q, gw), 0)
        e_rows = jnp.where(srow == 0, e1.astype(F32),
                           jnp.where(srow == 1, e2.astype(F32), jnp.where(srow == 2, e3.astype(F32), 0.0)))
        lhs = jnp.concatenate([xdte[:, gs].astype(BF16).astype(F32), e_rows], axis=0)
        rhs = jnp.concatenate([jnp.concatenate([bm[:, g * n:(g + 1) * n], zpad[:, 0:n]], axis=0), ones_lo],
                              axis=1)
        both = _dot_tn(lhs, rhs)
        hout_ref[0, gs, :] = both[:, n:2 * n] * h0 + both[:, 0:n]


def _ssm_step(proj, dt_raw, conv_buf, h0, sp, batch, q):
    t = proj.shape[0]
    assert t == batch * q and q == SUBLANES
    col = lambda off, w: off // w
    full = lambda shape: pl.BlockSpec(shape, lambda b: (0,) * len(shape))
    gsum = ((_iota((SSM_GROUPS * SSM_STATE, LANES), 0) // SSM_STATE
             == _iota((SSM_GROUPS * SSM_STATE, LANES), 1) // (SSM_HEADS // SSM_GROUPS))
            & (_iota((SSM_GROUPS * SSM_STATE, LANES), 1) < SSM_HEADS)).astype(BF16)
    return pl.pallas_call(
        _ssm_step_kernel,
        grid=(batch,),
        in_specs=[
            pl.BlockSpec((q, D_INNER), lambda b: (b, col(OFF_Z, D_INNER))),
            pl.BlockSpec((q, D_INNER), lambda b: (b, col(OFF_X, D_INNER))),
            pl.BlockSpec((q, 1024), lambda b: (b, col(OFF_B, 1024))),
            pl.BlockSpec((q, 1024), lambda b: (b, col(OFF_C, 1024))),
            pl.BlockSpec((q, LANES), lambda b: (b, 0)),
            pl.BlockSpec((1, CONV_WIDTH - 1, CONV_DIM), lambda b: (b, 0, 0)),
            pl.BlockSpec((1, D_INNER, SSM_STATE), lambda b: (b, 0, 0)),
            full((CONV_WIDTH, CONV_DIM)),
            full((1, CONV_DIM)),
            full((1, LANES)),
            full((1, LANES)),
            full((1, D_INNER)),
            full((1, D_INNER)),
            full((LANES, D_INNER)),
            full((SSM_GROUPS * SSM_STATE, LANES)),
        ],
        out_specs=[
            pl.BlockSpec((q, D_INNER), lambda b: (b, 0)),
            pl.BlockSpec((1, D_INNER, SSM_STATE), lambda b: (b, 0, 0)),
        ],
        out_shape=[
            jax.ShapeDtypeStruct((t, D_INNER), F32),
            jax.ShapeDtypeStruct((batch, D_INNER, SSM_STATE), F32),
        ],
        scratch_shapes=[pltpu.VMEM((SUBLANES, CONV_DIM), F32)],
        compiler_params=pltpu.CompilerParams(
            dimension_semantics=("parallel",), vmem_limit_bytes=VMEM_LIMIT),
        name="ssm_step",
    )(proj, proj, proj, proj, dt_raw, conv_buf, h0, sp["w_conv"], sp["b_conv"], sp["dt_bias"],
      sp["a_log"], sp["d_skip"], sp["w_norm"], sp["expand"], gsum)


def _prep_params(w_in, norm_mix, w_conv, b_conv, dt_bias, a_log, d_skip, w_ssm_norm, w_q_norm, w_k_norm):
    o_z, o_xbc = 0, D_INNER
    o_dt = o_xbc + CONV_DIM
    o_q = o_dt + SSM_HEADS
    o_k = o_q + ATTN_WIDTH
    o_v = o_k + ATTN_WIDTH
    o_g = o_v + ATTN_WIDTH
    w_main = jnp.concatenate(
        [w_in[:, o_z:o_dt], w_in[:, o_g:], w_in[:, o_q:o_g]], axis=1).astype(BF16)
    w_dt = jnp.pad(w_in[:, o_dt:o_q], ((0, 0), (0, LANES - SSM_HEADS))).astype(BF16)
    qk_w = jnp.ones((N_PROJ_TILES, 1, PROJ_TILE), F32)
    qk_w = qk_w.at[Q_TILE0:K_TILE0, 0].set(jnp.tile(w_q_norm, HEADS_PER_GROUP)[None])
    qk_w = qk_w.at[K_TILE0:V_TILE0, 0].set(jnp.tile(w_k_norm, HEADS_PER_GROUP)[None])
    pad_h = lambda v: jnp.pad(v, (0, LANES - SSM_HEADS))[None]
    expand = (_iota((LANES, D_INNER), 0) == _iota((LANES, D_INNER), 1) // SSM_HEAD_DIM).astype(BF16)
    ssm = dict(w_conv=w_conv, b_conv=b_conv[None], dt_bias=pad_h(dt_bias), a_log=pad_h(a_log),
               d_skip=jnp.repeat(d_skip, SSM_HEAD_DIM)[None], w_norm=w_ssm_norm[None], expand=expand)
    return dict(w_main=w_main, w_dt=w_dt, qk_w=qk_w, norm_mix=norm_mix[None], ssm=ssm)


def _attn_prompt_kernel(q_ref, k_ref, v_ref, o_ref, lse_ref, kp_sc, vp_sc, *, span, dil):
    i = pl.program_id(1)
    blk = ATTN_BLOCK
    pairs = q_ref.shape[1] // LANES

    @pl.when(i == 0)
    def _():
        kp_sc[...] = jnp.zeros_like(kp_sc)
        vp_sc[...] = jnp.zeros_like(vp_sc)

    qi = _iota((blk, 2 * blk), 0)
    kc = _iota((blk, 2 * blk), 1)
    dist = blk + qi - kc
    mask = (dist >= 0) & (dist <= span) & ((kc >= blk) | (i > 0))
    lo_q = _iota((blk, LANES), 1) < ATTN_HEAD_DIM
    lo_kv = _iota((2 * blk, LANES), 1) < ATTN_HEAD_DIM

    def residue(r, carry):
        rows = pl.ds(r, blk, stride=dil) if dil > 1 else slice(None)
        for pr in range(pairs):
            ls = slice(pr * LANES, (pr + 1) * LANES)
            q = q_ref[rows, ls].astype(BF16)
            k2 = jnp.concatenate([kp_sc[rows, ls], k_ref[rows, ls]], axis=0).astype(BF16)
            v2 = jnp.concatenate([vp_sc[rows, ls], v_ref[rows, ls]], axis=0).astype(BF16)
            ps, lses = [], []
            for keep in (lo_q, jnp.logical_not(lo_q)):
                s = jnp.where(mask, _dot_nt(jnp.where(keep, q, 0.0), k2), -jnp.inf)
                m = jnp.max(s, axis=-1, keepdims=True)
                p = jnp.exp(s - m)
                den = jnp.sum(p, axis=-1, keepdims=True)
                ps.append((p / den).astype(BF16))
                lses.append(m + jnp.log(den))
            v_bd = jnp.concatenate([jnp.where(lo_kv, v2, 0.0), jnp.where(lo_kv, 0.0, v2)], axis=0)
            o_ref[rows, ls] = _dot(jnp.concatenate(ps, axis=1), v_bd)
            lse_ref[rows, ls] = jnp.where(lo_q, lses[0], lses[1])
        return carry

    if dil > 1:
        lax.fori_loop(0, dil, residue, 0, unroll=4)
    else:
        residue(0, 0)
    kp_sc[...] = k_ref[...]
    vp_sc[...] = v_ref[...]


def _attn_prompt(proj, gi):
    window, dil = ATTN_PATTERNS[gi]
    s = proj.shape[0]
    chunk = ATTN_BLOCK * dil
    assert s % chunk == 0 and window // dil <= ATTN_BLOCK
    w = LANES if dil > 1 else ATTN_GROUP_WIDTH
    split = ATTN_GROUP_WIDTH // w
    col = lambda tile0: (lambda j, i: (i, (tile0 + gi) * split + j))
    blk = (chunk, w)
    return pl.pallas_call(
        functools.partial(_attn_prompt_kernel, span=window // dil, dil=dil),
        grid=(split, s // chunk),
        in_specs=[pl.BlockSpec(blk, col(Q_TILE0)), pl.BlockSpec(blk, col(K_TILE0)),
                  pl.BlockSpec(blk, col(V_TILE0))],
        out_specs=[pl.BlockSpec(blk, lambda j, i: (i, j))] * 2,
        out_shape=[jax.ShapeDtypeStruct((s, ATTN_GROUP_WIDTH), F32)] * 2,
        scratch_shapes=[pltpu.VMEM(blk, F32), pltpu.VMEM(blk, F32)],
        compiler_params=pltpu.CompilerParams(
            dimension_semantics=("parallel", "arbitrary"), vmem_limit_bytes=VMEM_LIMIT),
        name=f"attn_prompt_g{gi}",
    )(proj, proj, proj)


def _attn_sample_kernel(q_ref, k_ref, v_ref, kc_ref, vc_ref, o_ref, lse_ref, ko_ref, vo_ref,
                        *, span, dil, pos0):
    l = SUBLANES
    nb = kc_ref.shape[0]
    nh, hd = HEADS_PER_GROUP, ATTN_HEAD_DIM
    rows = kc_ref.shape[3]
    w = nh * hd
    t_c = _iota((l, rows), 0)
    delta_c = rows + t_c - _iota((l, rows), 1)
    ok_c = ((delta_c <= span * dil) & ((delta_c & (dil - 1)) == 0) & (pos0 + t_c - delta_c >= 0))
    delta_n = _iota((l, l), 0) - _iota((l, l), 1)
    ok_n = (delta_n >= 0) & (delta_n <= span * dil) & ((delta_n & (dil - 1)) == 0)
    lane = _iota((hd, LANES), 1)
    hl = nh * l
    own = _shr(_iota((hl, w), 0), l) == _shr(_iota((hl, w), 1), hd)
    ok_c = jnp.concatenate([ok_c] * nh, axis=0)
    ok_n = jnp.concatenate([ok_n] * nh, axis=0)
    for bb in range(nb):
        rs = slice(bb * l, (bb + 1) * l)
        q, kn, vn = q_ref[rs, :], k_ref[rs, :], v_ref[rs, :]
        q_bd = jnp.where(own, jnp.concatenate([q] * nh, axis=0), 0.0)
        kt = kc_ref[bb].reshape(w, rows)
        vt = vc_ref[bb].reshape(w, rows)
        s_c = jnp.where(ok_c, _dot(q_bd, kt), -jnp.inf)
        s_n = jnp.where(ok_n, _dot_nt(q_bd, kn), -jnp.inf)
        m = jnp.maximum(jnp.max(s_c, axis=-1, keepdims=True), jnp.max(s_n, axis=-1, keepdims=True))
        p_c = jnp.exp(s_c - m)
        p_n = jnp.exp(s_n - m)
        den = jnp.sum(p_c, axis=-1, keepdims=True) + jnp.sum(p_n, axis=-1, keepdims=True)
        res = jnp.where(own, _dot_nt(p_c / den, vt) + _dot(p_n / den, vn), 0.0)
        lse = jnp.where(own, m + jnp.log(den), 0.0)
        o, lse_x = res[0:l], lse[0:l]
        for h in range(1, nh):
            o = o + res[h * l:(h + 1) * l]
            lse_x = lse_x + lse[h * l:(h + 1) * l]
        o_ref[rs, :] = o
        lse_ref[rs, :] = lse_x

        for new, c_ref, out_ref in ((kn, kc_ref, ko_ref), (vn, vc_ref, vo_ref)):
            padded = jnp.concatenate([jnp.zeros((LANES - l, w), F32), new], axis=0)
            tail = padded.T
            for h in range(nh):
                shifted = pltpu.roll(c_ref[bb, h], rows - l, axis=1)
                if rows > LANES:
                    out_ref[bb, h, :, 0:rows - LANES] = shifted[:, 0:rows - LANES]
                out_ref[bb, h, :, rows - LANES:rows] = jnp.where(
                    lane < LANES - l, shifted[:, rows - LANES:rows], tail[h * hd:(h + 1) * hd])


SAMPLE_CACHE_ROWS_PER_STEP = 2048
SAMPLE_MAX_SEQS_PER_STEP = 8


def _attn_sample(proj, k_cache, v_cache, gi, batch, pos0):
    window, dil = ATTN_PATTERNS[gi]
    rows = k_cache.shape[1]
    l = proj.shape[0] // batch
    assert l == SUBLANES and rows % LANES == 0
    nb = max(1, min(batch, SAMPLE_CACHE_ROWS_PER_STEP // rows, SAMPLE_MAX_SEQS_PER_STEP))
    assert batch % nb == 0
    tok = lambda tile0: pl.BlockSpec((nb * l, ATTN_GROUP_WIDTH), lambda b: (b, tile0 + gi))
    cache = pl.BlockSpec((nb, HEADS_PER_GROUP, ATTN_HEAD_DIM, rows), lambda b: (b, 0, 0, 0))
    out_tok = pl.BlockSpec((nb * l, ATTN_GROUP_WIDTH), lambda b: (b, 0))
    to_t = lambda c: jnp.transpose(c, (0, 2, 3, 1))
    t_shape = jax.ShapeDtypeStruct((batch, HEADS_PER_GROUP, ATTN_HEAD_DIM, rows), F32)
    o, lse, k_t, v_t = pl.pallas_call(
        functools.partial(_attn_sample_kernel, span=window // dil, dil=dil, pos0=pos0),
        grid=(batch // nb,),
        in_specs=[tok(Q_TILE0), tok(K_TILE0), tok(V_TILE0), cache, cache],
        out_specs=[out_tok, out_tok, cache, cache],
        out_shape=[jax.ShapeDtypeStruct((batch * l, ATTN_GROUP_WIDTH), F32)] * 2 + [t_shape] * 2,
        compiler_params=pltpu.CompilerParams(
            dimension_semantics=("parallel",), vmem_limit_bytes=VMEM_LIMIT),
        name=f"attn_sample_g{gi}",
    )(proj, proj, proj, to_t(k_cache), to_t(v_cache))
    from_t = lambda c: jnp.transpose(c, (0, 3, 1, 2))
    return o, lse, from_t(k_t), from_t(v_t)


def _merge_kernel(x_ref, ys_ref, o0_ref, o1_ref, o2_ref, l0_ref, l1_ref, l2_ref, g0_ref, g1_ref,
                  wbs_ref, wba_ref, wo_ref, nf_ref, wr1_ref, wr2_ref, br_ref,
                  x1_ref, xn_ref, lg_ref):
    l0, l1, l2 = l0_ref[...], l1_ref[...], l2_ref[...]
    m = jnp.maximum(jnp.maximum(l0, l1), l2)
    e0, e1, e2 = jnp.exp(l0 - m), jnp.exp(l1 - m), jnp.exp(l2 - m)
    den = e0 + e1 + e2
    y_attn = (e0 / den) * o0_ref[...] + (e1 / den) * o1_ref[...] + (e2 / den) * o2_ref[...]
    br_s = _dot(ys_ref[...].astype(BF16), wbs_ref[...])
    br_a = _dot(y_attn.astype(BF16), wba_ref[...])
    merged = jax.nn.sigmoid(g0_ref[...]) * br_s + jax.nn.sigmoid(g1_ref[...]) * br_a
    x1 = x_ref[...] + _dot(merged.astype(BF16), wo_ref[...])
    x1_ref[...] = x1
    xn = x1 * lax.rsqrt(jnp.mean(x1 * x1, axis=-1, keepdims=True) + EPS) * nf_ref[...]
    xn1 = xn.astype(BF16)
    xn_ref[...] = xn1
    xn2 = (xn - xn1.astype(F32)).astype(BF16)
    lg_ref[...] = (_dot(xn1, wr1_ref[...]) + _dot(xn2, wr1_ref[...]) + _dot(xn1, wr2_ref[...])
                   + br_ref[...])


def _merge(x, proj, y_ssm, attn, mp, tm):
    t = x.shape[0]
    row = lambda w: pl.BlockSpec((tm, w), lambda i: (i, 0))
    full = lambda a: pl.BlockSpec(a.shape, lambda i: (0,) * a.ndim)
    (o0, l0), (o1, l1), (o2, l2) = attn
    weights = [mp["w_br_ssm"], mp["w_br_attn"], mp["w_out"], mp["norm_ffn"], mp["w_r1"], mp["w_r2"], mp["b_r"]]
    return pl.pallas_call(
        _merge_kernel,
        grid=(t // tm,),
        in_specs=[row(D_MODEL), row(D_INNER)] + [row(ATTN_GROUP_WIDTH)] * 6 + [
            pl.BlockSpec((tm, D_MODEL), lambda i: (i, OFF_G // D_MODEL)),
            pl.BlockSpec((tm, D_MODEL), lambda i: (i, OFF_G // D_MODEL + 1)),
        ] + [full(w) for w in weights],
        out_specs=[row(D_MODEL), row(D_MODEL), row(ROUTER_WIDTH)],
        out_shape=[
            jax.ShapeDtypeStruct((t, D_MODEL), F32),
            jax.ShapeDtypeStruct((t, D_MODEL), BF16),
            jax.ShapeDtypeStruct((t, ROUTER_WIDTH), F32),
        ],
        compiler_params=pltpu.CompilerParams(
            dimension_semantics=("parallel",), vmem_limit_bytes=VMEM_LIMIT),
        name="merge",
    )(x, y_ssm, o0, o1, o2, l0, l1, l2, proj, proj, *weights)


def _prep_merge_params(w_br_ssm, w_br_attn, w_out, norm_ffn, w_router_group, b_router_group,
                       w_router_expert, b_router_expert):
    w_r = jnp.concatenate([w_router_group, w_router_expert.reshape(D_MODEL, N_EXPERTS)], axis=1)
    w_r = jnp.pad(w_r, ((0, 0), (0, ROUTER_WIDTH - w_r.shape[1])))
    w_r1 = w_r.astype(BF16)
    w_r2 = (w_r - w_r1.astype(F32)).astype(BF16)
    b_r = jnp.concatenate([b_router_group, b_router_expert.reshape(N_EXPERTS)])
    b_r = jnp.pad(b_r, (0, ROUTER_WIDTH - b_r.shape[0]))[None]
    return dict(w_br_ssm=w_br_ssm.astype(BF16), w_br_attn=w_br_attn.astype(BF16), w_out=w_out.astype(BF16),
                norm_ffn=norm_ffn[None], w_r1=w_r1, w_r2=w_r2, b_r=b_r)


def _route(logits):
    lane = _iota(logits.shape, 1)
    big = jnp.int32(2 ** 30)
    neg = -jnp.inf
    gl = jnp.where(lane < N_EXPERT_GROUPS, logits, neg)
    gmax = jnp.max(gl, axis=-1, keepdims=True)
    gidx = jnp.min(jnp.where(gl == gmax, lane, big), axis=-1, keepdims=True)
    g_w = 1.0 / jnp.sum(jnp.exp(gl - gmax), axis=-1, keepdims=True)
    lo = N_EXPERT_GROUPS + gidx * EXPERTS_PER_GROUP
    el = jnp.where((lane >= lo) & (lane < lo + EXPERTS_PER_GROUP), logits, neg)
    v1 = jnp.max(el, axis=-1, keepdims=True)
    i1 = jnp.min(jnp.where(el == v1, lane, big), axis=-1, keepdims=True)
    el2 = jnp.where(lane == i1, neg, el)
    v2 = jnp.max(el2, axis=-1, keepdims=True)
    i2 = jnp.min(jnp.where(el2 == v2, lane, big), axis=-1, keepdims=True)
    e2 = jnp.exp(v2 - v1)
    w1 = (1.0 / (1.0 + e2)) * g_w
    w2 = (e2 / (1.0 + e2)) * g_w
    return jnp.where(lane == i1, w1, jnp.where(lane == i2, w2, 0.0)), gidx


MOE_BLOCK = 256


def _moe_kernel(x1_ref, xn_ref, lg_ref, wg_ref, wu_ref, wd_ref, out_ref,
                comb_sc, oh_sc, cum_sc, oht_sc, cumt_sc):
    g = pl.program_id(1)
    tm = x1_ref.shape[0]
    blk = MOE_BLOCK

    @pl.when(g == 0)
    def _():
        comb, gidx = _route(lg_ref[...])
        c1, c2, c3 = _split3(comb)
        comb_sc[0] = c1
        comb_sc[1] = c2
        comb_sc[2] = c3
        onehot = (_iota((tm, LANES), 1) == gidx).astype(F32)
        oh_bf = onehot.astype(BF16)
        lower = (_iota((tm, tm), 0) > _iota((tm, tm), 1)).astype(BF16)
        upper = (_iota((tm, tm), 0) < _iota((tm, tm), 1)).astype(BF16)
        oh_sc[...] = onehot
        cum_sc[...] = _dot(lower, oh_bf)
        oht_sc[...] = onehot.T[0:SUBLANES]
        cumt_sc[...] = _dot_tn(oh_bf, upper)[0:SUBLANES]
        out_ref[...] = x1_ref[...]

    lane = _iota((tm, LANES), 1)
    pick = lane == g
    member_c = jnp.sum(jnp.where(pick, oh_sc[...], 0.0), axis=-1, keepdims=True)
    rank_c = jnp.sum(jnp.where(pick, cum_sc[...], 0.0), axis=-1, keepdims=True)
    member_r = oht_sc[pl.ds(g, 1), :]
    rank_r = cumt_sc[pl.ds(g, 1), :]
    count = jnp.sum(member_r).astype(jnp.int32)
    nblk = (count + (blk - 1)) // blk
    xn = xn_ref[...]
    wd = wd_ref[...].reshape(EXPERTS_PER_GROUP * D_EXPERT, D_MODEL)
    lane_b = _iota((blk, LANES), 1)

    def body(k, carry):
        base = (k * blk).astype(F32)
        want_r = base + _iota((blk, tm), 0).astype(F32)
        sel = jnp.where((rank_r == want_r) & (member_r > 0.0), 1.0, 0.0).astype(BF16)
        want_c = base + _iota((tm, blk), 1).astype(F32)
        sel_t = jnp.where((rank_c == want_c) & (member_c > 0.0), 1.0, 0.0).astype(BF16)
        xg = _dot(sel, xn).astype(BF16)
        cg = _dot(sel, comb_sc[0]) + _dot(sel, comb_sc[1]) + _dot(sel, comb_sc[2])
        hidden = []
        for e in range(EXPERTS_PER_GROUP):
            col = N_EXPERT_GROUPS + g * EXPERTS_PER_GROUP + e
            cw = jnp.sum(jnp.where(lane_b == col, cg, 0.0), axis=-1, keepdims=True)
            h = jax.nn.silu(_dot(xg, wg_ref[e])) * _dot(xg, wu_ref[e]) * cw
            hidden.append(h.astype(BF16))
        out = _dot(jnp.concatenate(hidden, axis=1), wd)
        o1 = out.astype(BF16)
        o2 = (out - o1.astype(F32)).astype(BF16)
        out_ref[...] += _dot(sel_t, o1) + _dot(sel_t, o2)
        return carry

    lax.fori_loop(0, nblk, body, 0)


def _moe(x1, xn, logits, wg, wu, wd, tm):
    t = x1.shape[0]
    ge = EXPERTS_PER_GROUP
    return pl.pallas_call(
        _moe_kernel,
        grid=(t // tm, N_EXPERT_GROUPS),
        in_specs=[
            pl.BlockSpec((tm, D_MODEL), lambda i, g: (i, 0), pipeline_mode=pl.Buffered(1)),
            pl.BlockSpec((tm, D_MODEL), lambda i, g: (i, 0)),
            pl.BlockSpec((tm, ROUTER_WIDTH), lambda i, g: (i, 0), pipeline_mode=pl.Buffered(1)),
            pl.BlockSpec((ge, D_MODEL, D_EXPERT), lambda i, g: (g, 0, 0)),
            pl.BlockSpec((ge, D_MODEL, D_EXPERT), lambda i, g: (g, 0, 0)),
            pl.BlockSpec((ge, D_EXPERT, D_MODEL), lambda i, g: (g, 0, 0)),
        ],
        out_specs=pl.BlockSpec((tm, D_MODEL), lambda i, g: (i, 0)),
        out_shape=jax.ShapeDtypeStruct((t, D_MODEL), F32),
        scratch_shapes=[
            pltpu.VMEM((3, tm, ROUTER_WIDTH), BF16),
            pltpu.VMEM((tm, LANES), F32),
            pltpu.VMEM((tm, LANES), F32),
            pltpu.VMEM((SUBLANES, tm), F32),
            pltpu.VMEM((SUBLANES, tm), F32),
        ],
        compiler_params=pltpu.CompilerParams(
            dimension_semantics=("parallel", "arbitrary"), vmem_limit_bytes=VMEM_LIMIT),
        name="moe",
    )(x1, xn, logits, wg, wu, wd)


def _moe_dense_kernel(x1_ref, xn_ref, lg_ref, wg_ref, wu_ref, wd_ref, out_ref, comb_sc, acc_sc):
    e = pl.program_id(1)

    @pl.when(e == 0)
    def _():
        comb_sc[...] = _route(lg_ref[...])[0]
        acc_sc[...] = jnp.zeros_like(acc_sc)

    lane = _iota(comb_sc.shape, 1)
    cw = jnp.sum(jnp.where(lane == e + N_EXPERT_GROUPS, comb_sc[...], 0.0), axis=-1, keepdims=True)
    xn = xn_ref[...]
    hidden = jax.nn.silu(_dot(xn, wg_ref[0])) * _dot(xn, wu_ref[0]) * cw
    acc_sc[...] += _dot(hidden.astype(BF16), wd_ref[0])

    @pl.when(e == pl.num_programs(1) - 1)
    def _():
        out_ref[...] = x1_ref[...] + acc_sc[...]


def _moe_dense(x1, xn, logits, wg, wu, wd, tm):
    t = x1.shape[0]
    return pl.pallas_call(
        _moe_dense_kernel,
        grid=(t // tm, N_EXPERTS),
        in_specs=[
            pl.BlockSpec((tm, D_MODEL), lambda i, e: (i, 0)),
            pl.BlockSpec((tm, D_MODEL), lambda i, e: (i, 0)),
            pl.BlockSpec((tm, ROUTER_WIDTH), lambda i, e: (i, 0)),
            pl.BlockSpec((1, D_MODEL, D_EXPERT), lambda i, e: (e, 0, 0)),
            pl.BlockSpec((1, D_MODEL, D_EXPERT), lambda i, e: (e, 0, 0)),
            pl.BlockSpec((1, D_EXPERT, D_MODEL), lambda i, e: (e, 0, 0)),
        ],
        out_specs=pl.BlockSpec((tm, D_MODEL), lambda i, e: (i, 0)),
        out_shape=jax.ShapeDtypeStruct((t, D_MODEL), F32),
        scratch_shapes=[pltpu.VMEM((tm, ROUTER_WIDTH), F32), pltpu.VMEM((tm, D_MODEL), F32)],
        compiler_params=pltpu.CompilerParams(
            dimension_semantics=("parallel", "arbitrary"), vmem_limit_bytes=VMEM_LIMIT),
        name="moe_dense",
    )(x1, xn, logits, wg, wu, wd)


def _tile_rows(t, cap):
    tm = min(t, cap)
    assert t % tm == 0
    return tm


def _decoder_layer(x, conv_buf, ssm_state, kv_bufs, pos0, pp, mp, ep):
    b, l, d = x.shape
    t = b * l
    x2 = x.reshape(t, d)
    proj, dt_raw = _in_proj(x2, pp["norm_mix"], pp["w_main"], pp["w_dt"], pp["qk_w"], _tile_rows(t, 2048))
    proj3 = proj.reshape(b, l, PROJ_WIDTH)
    h0 = ssm_state.reshape(b, D_INNER, SSM_STATE)
    if kv_bufs is None:
        assert b == 1 and l % SSD_CHUNK == 0
        y_ssm, h_new = _ssm_chunked(proj, dt_raw, conv_buf, h0, pp["ssm"], b, SSD_CHUNK)
        attn = [_attn_prompt(proj, gi) for gi in range(N_ATTN_GROUPS)]
        new_kv = []
        for gi, (window, _) in enumerate(ATTN_PATTERNS):
            keep = min(window, l)
            for off in (OFF_K, OFF_V):
                c0 = off + gi * ATTN_GROUP_WIDTH
                new_kv.append(proj3[:, l - keep:, c0:c0 + ATTN_GROUP_WIDTH]
                              .reshape(b, keep, HEADS_PER_GROUP, ATTN_HEAD_DIM))
    else:
        y_ssm, h_new = _ssm_step(proj, dt_raw, conv_buf, h0, pp["ssm"], b, l)
        attn, new_kv = [], []
        for gi in range(N_ATTN_GROUPS):
            kc, vc = kv_bufs[2 * gi], kv_bufs[2 * gi + 1]
            o, lse, kn, vn = _attn_sample(proj, kc, vc, gi, b, pos0)
            attn.append((o, lse))
            new_kv += [kn, vn]
    new_conv = proj3[:, l - (CONV_WIDTH - 1):, OFF_X:OFF_X + CONV_DIM]
    tm = _tile_rows(t, 512)
    x1, xn, logits = _merge(x2, proj, y_ssm, attn, mp, tm)
    y = _moe(x1, xn, logits, ep["wg"], ep["wu"], ep["wd"], _tile_rows(t, 1024))
    new_ssm = h_new.reshape(b, SSM_HEADS, SSM_HEAD_DIM, SSM_STATE)
    return y.reshape(b, l, d), (new_conv, new_ssm) + tuple(new_kv)


def kernel(x_prompt, x_sample, cache_conv, state_ssm, cache_k_w128, cache_v_w128, cache_k_w512, cache_v_w512,
           cache_k_w2048, cache_v_w2048, w_in, norm_mix, w_conv, b_conv, dt_bias, a_log, d_skip, w_ssm_norm,
           w_q_norm, w_k_norm, w_br_ssm, w_br_attn, w_out, norm_ffn, w_router_group, b_router_group,
           w_router_expert, b_router_expert, w_expert_gate, w_expert_up, w_expert_down):
    depth = w_in.shape[0]
    yp, ys = x_prompt, x_sample
    bp = x_prompt.shape[0]
    prompt_states, sample_states = [], []
    for i in range(depth):
        pp = _prep_params(w_in[i], norm_mix[i], w_conv[i], b_conv[i], dt_bias[i], a_log[i], d_skip[i],
                          w_ssm_norm[i], w_q_norm[i], w_k_norm[i])
        mp = _prep_merge_params(w_br_ssm[i], w_br_attn[i], w_out[i], norm_ffn[i], w_router_group[i],
                                b_router_group[i], w_router_expert[i], b_router_expert[i])
        ep = dict(wg=w_expert_gate[i].astype(BF16), wu=w_expert_up[i].astype(BF16),
                  wd=w_expert_down[i].astype(BF16))
        zero_conv = jnp.zeros((bp, CONV_WIDTH - 1, CONV_DIM), x_prompt.dtype)
        zero_ssm = jnp.zeros((bp, SSM_HEADS, SSM_HEAD_DIM, SSM_STATE), F32)
        yp, st_p = _decoder_layer(yp, zero_conv, zero_ssm, None, 0, pp, mp, ep)
        kv_bufs = (cache_k_w128[i], cache_v_w128[i], cache_k_w512[i], cache_v_w512[i],
                   cache_k_w2048[i], cache_v_w2048[i])
        ys, st_s = _decoder_layer(ys, cache_conv[i], state_ssm[i], kv_bufs, PAST_LEN, pp, mp, ep)
        prompt_states.append(st_p)
        sample_states.append(st_s)
    stack = lambda states: [jnp.stack([s[j] for s in states]) for j in range(8)]
    return (yp, ys) + tuple(stack(prompt_states)) + tuple(stack(sample_states))
```

```python
import functools
import math

import jax
import jax.numpy as jnp
from jax import lax
from jax.experimental import pallas as pl
from jax.experimental.pallas import tpu as pltpu

F32 = jnp.float32
BF16 = jnp.bfloat16

PAST_LEN = 8192
D_MODEL = 1024
D_INNER = 2048
SSM_HEAD_DIM = 64
SSM_HEADS = 32
SSM_GROUPS = 8
SSM_STATE = 128
CONV_WIDTH = 4
CONV_DIM = D_INNER + 2 * SSM_GROUPS * SSM_STATE
SSD_CHUNK = 128
ATTN_HEAD_DIM = 64
HEADS_PER_GROUP = 8
ATTN_PATTERNS = ((128, 1), (512, 4), (2048, 16))
N_ATTN_GROUPS = 3
ATTN_GROUP_WIDTH = HEADS_PER_GROUP * ATTN_HEAD_DIM
ATTN_WIDTH = N_ATTN_GROUPS * ATTN_GROUP_WIDTH
ATTN_BLOCK = 128
N_EXPERT_GROUPS = 4
EXPERTS_PER_GROUP = 8
N_EXPERTS = 32
D_EXPERT = 256
EPS = 1e-6

LANES = 128
SUBLANES = 8
VMEM_LIMIT = 56 * 1024 * 1024

PROJ_TILE = 512
OFF_Z = 0
OFF_X = 2048
OFF_B = 4096
OFF_C = 5120
OFF_G = 6144
OFF_Q = 8192
OFF_K = OFF_Q + ATTN_WIDTH
OFF_V = OFF_K + ATTN_WIDTH
PROJ_WIDTH = OFF_V + ATTN_WIDTH
N_PROJ_TILES = PROJ_WIDTH // PROJ_TILE
Q_TILE0 = OFF_Q // PROJ_TILE
K_TILE0 = OFF_K // PROJ_TILE
V_TILE0 = OFF_V // PROJ_TILE
ROUTER_WIDTH = LANES


def _split3(x):
    x1 = x.astype(BF16)
    r1 = x - x1.astype(F32)
    x2 = r1.astype(BF16)
    r2 = r1 - x2.astype(F32)
    return x1, x2, r2.astype(BF16)


def _dot(a, b):
    return jnp.dot(a, b, preferred_element_type=F32)


def _dot3(x, m):
    x1, x2, x3 = _split3(x)
    return _dot(x1, m) + _dot(x2, m) + _dot(x3, m)


def _dot3f(x, m):
    x1, x2, x3 = _split3(x)
    return _dot(x1.astype(F32), m) + _dot(x2.astype(F32), m) + _dot(x3.astype(F32), m)


def _dot_nt(a, b):
    return lax.dot_general(a, b, (((1,), (1,)), ((), ())), preferred_element_type=F32)


def _dot_tn(a, b):
    return lax.dot_general(a, b, (((0,), (0,)), ((), ())), preferred_element_type=F32)


def _iota(shape, dim):
    return lax.broadcasted_iota(jnp.int32, shape, dim)


def _shr(x, pow2):
    assert pow2 & (pow2 - 1) == 0
    return lax.shift_right_logical(x, jnp.int32(int(math.log2(pow2))))


def _in_proj_kernel(x_ref, nw_ref, w_ref, wdt_ref, qkw_ref, seg_ref, out_ref, dt_ref, h_sc):
    j = pl.program_id(1)

    @pl.when(j == 0)
    def _():
        xf = x_ref[...]
        y = xf * lax.rsqrt(jnp.mean(xf * xf, axis=-1, keepdims=True) + EPS)
        h = (y * nw_ref[...]).astype(BF16)
        h_sc[...] = h
        dt_ref[...] = _dot(h, wdt_ref[...])

    acc = _dot(h_sc[...], w_ref[...])
    is_qk = jnp.logical_and(j >= Q_TILE0, j < V_TILE0)

    @pl.when(jnp.logical_not(is_qk))
    def _():
        out_ref[...] = acc

    @pl.when(is_qk)
    def _():
        sq = acc * acc
        s1 = sq.astype(BF16)
        s2 = (sq - s1.astype(F32)).astype(BF16)
        ms = (_dot(s1, seg_ref[...]) + _dot(s2, seg_ref[...])) * (1.0 / ATTN_HEAD_DIM)
        scale = jnp.where(j < K_TILE0, ATTN_HEAD_DIM ** -0.5, 1.0).astype(F32)
        w = qkw_ref[0] * scale
        out_ref[...] = acc * lax.rsqrt(ms + EPS) * w


def _in_proj(x, norm_w, w_bf, wdt_bf, qk_w, tm):
    t = x.shape[0]
    seg = (_iota((PROJ_TILE, PROJ_TILE), 0) // ATTN_HEAD_DIM
           == _iota((PROJ_TILE, PROJ_TILE), 1) // ATTN_HEAD_DIM).astype(BF16)
    return pl.pallas_call(
        _in_proj_kernel,
        grid=(t // tm, N_PROJ_TILES),
        in_specs=[
            pl.BlockSpec((tm, D_MODEL), lambda i, j: (i, 0)),
            pl.BlockSpec((1, D_MODEL), lambda i, j: (0, 0)),
            pl.BlockSpec((D_MODEL, PROJ_TILE), lambda i, j: (0, j)),
            pl.BlockSpec((D_MODEL, LANES), lambda i, j: (0, 0)),
            pl.BlockSpec((1, 1, PROJ_TILE), lambda i, j: (j, 0, 0)),
            pl.BlockSpec((PROJ_TILE, PROJ_TILE), lambda i, j: (0, 0)),
        ],
        out_specs=[
            pl.BlockSpec((tm, PROJ_TILE), lambda i, j: (i, j)),
            pl.BlockSpec((tm, LANES), lambda i, j: (i, 0)),
        ],
        out_shape=[
            jax.ShapeDtypeStruct((t, PROJ_WIDTH), F32),
            jax.ShapeDtypeStruct((t, LANES), F32),
        ],
        scratch_shapes=[pltpu.VMEM((tm, D_MODEL), BF16)],
        compiler_params=pltpu.CompilerParams(
            dimension_semantics=("parallel", "arbitrary"), vmem_limit_bytes=VMEM_LIMIT),
        name="in_proj",
    )(x, norm_w, w_bf, wdt_bf, qk_w, seg)


def _softplus(x):
    return jnp.maximum(x, 0.0) + jnp.log1p(jnp.exp(-jnp.abs(x)))


def _conv_silu(raw, tail, w, b):
    q = raw.shape[0]
    up = jnp.concatenate([tail, raw], axis=0)
    acc = b + w[CONV_WIDTH - 1:CONV_WIDTH] * raw
    for tap in range(CONV_WIDTH - 1):
        shifted = pltpu.roll(up, CONV_WIDTH - 1 - tap, axis=0)[SUBLANES:SUBLANES + q]
        acc = acc + w[tap:tap + 1] * shifted
    return jax.nn.silu(acc)


def _ssm_chunk_kernel(z_ref, xs_ref, bm_ref, cm_ref, dt_ref, cbuf_ref, h0_ref, wconv_ref, bconv_ref,
                      dtb_ref, alog_ref, dskip_ref, wnorm_ref, expand_ref,
                      y_ref, hout_ref, tail_sc, xbc_sc, ht_sc):
    c = pl.program_id(1)
    nc = pl.num_programs(1)
    q = xs_ref.shape[0]
    gw = D_INNER // SSM_GROUPS
    n = SSM_STATE

    @pl.when(c == 0)
    def _():
        tail_sc[...] = jnp.zeros_like(tail_sc)
        tail_sc[SUBLANES - (CONV_WIDTH - 1):SUBLANES, :] = cbuf_ref[0]
        for g in range(SSM_GROUPS):
            ht_sc[g] = h0_ref[0, g * gw:(g + 1) * gw, :].T

    cw = PROJ_TILE
    for k in range(CONV_DIM // cw):
        if k < D_INNER // cw:
            raw = xs_ref[:, k * cw:(k + 1) * cw]
        elif k < (D_INNER + SSM_GROUPS * n) // cw:
            kk = k - D_INNER // cw
            raw = bm_ref[:, kk * cw:(kk + 1) * cw]
        else:
            kk = k - (D_INNER + SSM_GROUPS * n) // cw
            raw = cm_ref[:, kk * cw:(kk + 1) * cw]
        sl = slice(k * cw, (k + 1) * cw)
        xbc_sc[:, sl] = _conv_silu(raw, tail_sc[:, sl], wconv_ref[:, sl], bconv_ref[:, sl])
        tail_sc[:, sl] = raw[q - SUBLANES:q]

    dt = _softplus(dt_ref[...] + dtb_ref[...])
    a = -jnp.exp(alog_ref[...])
    dta = dt * a
    tri = (_iota((q, q), 0) >= _iota((q, q), 1)).astype(BF16)
    d1, d2, d3 = _split3(dta)
    acs = _dot(tri, d1) + _dot(tri, d2) + _dot(tri, d3)
    eye = (_iota((LANES, LANES), 0) == _iota((LANES, LANES), 1)).astype(BF16)
    c1, c2, c3 = _split3(acs)
    acs_t = _dot_nt(eye, c1) + _dot_nt(eye, c2) + _dot_nt(eye, c3)
    a_end = acs[q - 1:q, :]
    e_in = jnp.exp(acs)
    e_out = jnp.exp(a_end - acs)
    e_all = jnp.exp(a_end)

    ex = expand_ref[...]
    dt_x = _dot3(dt, ex)
    e_in_x = _dot3(e_in, ex)
    e_out_x = _dot3(e_out, ex)
    e_all_x = _dot3(jnp.broadcast_to(e_all, (2 * SUBLANES, LANES)), ex)[0:1]

    causal = _iota((q, q), 0) >= _iota((q, q), 1)
    lane_lo = _iota((q, LANES), 1) < SSM_HEAD_DIM

    for g in range(SSM_GROUPS):
        gs = slice(g * gw, (g + 1) * gw)
        xg = xbc_sc[:, gs]
        bg = xbc_sc[:, D_INNER + g * n:D_INNER + (g + 1) * n].astype(BF16)
        cg = xbc_sc[:, D_INNER + SSM_GROUPS * n + g * n:D_INNER + SSM_GROUPS * n + (g + 1) * n].astype(BF16)
        xdt = xg * dt_x[:, gs]
        cb = _dot_nt(cg, bg)
        ht = ht_sc[g]
        y_off = _dot(cg, ht.astype(BF16)) * e_in_x[:, gs]
        pieces = []
        for pair in range(2):
            ms = []
            for hh in range(2):
                h = g * 4 + pair * 2 + hh
                diff = acs[:, h:h + 1] - acs_t[h:h + 1, :]
                lmat = jnp.where(causal, jnp.exp(jnp.where(causal, diff, 0.0)), 0.0)
                ms.append((cb * lmat).astype(BF16))
            xp = xdt[:, pair * LANES:(pair + 1) * LANES]
            rhs = jnp.concatenate([jnp.where(lane_lo, xp, 0.0), jnp.where(lane_lo, 0.0, xp)],
                                  axis=0).astype(BF16)
            pieces.append(_dot(jnp.concatenate(ms, axis=1), rhs))
        y_diag = jnp.concatenate(pieces, axis=1)
        yg = y_diag + y_off + dskip_ref[:, gs] * xg
        zg = z_ref[:, gs]
        u = yg * jax.nn.silu(zg)
        u = u * lax.rsqrt(jnp.mean(u * u, axis=-1, keepdims=True) + EPS)
        y_ref[:, gs] = (u * wnorm_ref[:, gs]).astype(y_ref.dtype)
        st = _dot_tn(bg, (xdt * e_out_x[:, gs]).astype(BF16))
        ht_sc[g] = e_all_x[:, gs] * ht + st

    @pl.when(c == nc - 1)
    def _():
        for g in range(SSM_GROUPS):
            hout_ref[0, g * gw:(g + 1) * gw, :] = ht_sc[g].T


def _ssm_chunked(proj, dt_raw, conv_buf, h0, sp, batch, q):
    t = proj.shape[0]
    nc = t // batch // q
    row = lambda b, c: b * nc + c
    col = lambda off, w: off // w
    full = lambda shape: pl.BlockSpec(shape, lambda b, c: (0,) * len(shape))
    return pl.pallas_call(
        _ssm_chunk_kernel,
        grid=(batch, nc),
        in_specs=[
            pl.BlockSpec((q, D_INNER), lambda b, c: (row(b, c), col(OFF_Z, D_INNER))),
            pl.BlockSpec((q, D_INNER), lambda b, c: (row(b, c), col(OFF_X, D_INNER))),
            pl.BlockSpec((q, 1024), lambda b, c: (row(b, c), col(OFF_B, 1024))),
            pl.BlockSpec((q, 1024), lambda b, c: (row(b, c), col(OFF_C, 1024))),
            pl.BlockSpec((q, LANES), lambda b, c: (row(b, c), 0)),
            pl.BlockSpec((1, CONV_WIDTH - 1, CONV_DIM), lambda b, c: (b, 0, 0)),
            pl.BlockSpec((1, D_INNER, SSM_STATE), lambda b, c: (b, 0, 0)),
            full((CONV_WIDTH, CONV_DIM)),
            full((1, CONV_DIM)),
            full((1, LANES)),
            full((1, LANES)),
            full((1, D_INNER)),
            full((1, D_INNER)),
            full((LANES, D_INNER)),
        ],
        out_specs=[
            pl.BlockSpec((q, D_INNER), lambda b, c: (row(b, c), 0)),
            pl.BlockSpec((1, D_INNER, SSM_STATE), lambda b, c: (b, 0, 0)),
        ],
        out_shape=[
            jax.ShapeDtypeStruct((t, D_INNER), BF16),
            jax.ShapeDtypeStruct((batch, D_INNER, SSM_STATE), F32),
        ],
        scratch_shapes=[
            pltpu.VMEM((SUBLANES, CONV_DIM), F32),
            pltpu.VMEM((q, CONV_DIM), F32),
            pltpu.VMEM((SSM_GROUPS, SSM_STATE, D_INNER // SSM_GROUPS), F32),
        ],
        compiler_params=pltpu.CompilerParams(
            dimension_semantics=("parallel", "arbitrary"), vmem_limit_bytes=VMEM_LIMIT),
        name="ssm_chunked",
    )(proj, proj, proj, proj, dt_raw, conv_buf, h0, sp["w_conv"], sp["b_conv"], sp["dt_bias"],
      sp["a_log"], sp["d_skip"], sp["w_norm"], sp["expand"])


def _ssm_step_kernel(z_ref, xs_ref, bm_ref, cm_ref, dt_ref, cbuf_ref, h0_ref, wconv_ref, bconv_ref,
                     dtb_ref, alog_ref, dskip_ref, wnorm_ref, expand_ref, gsum_ref,
                     y_ref, hout_ref, tail_sc):
    q = xs_ref.shape[0]
    gw = D_INNER // SSM_GROUPS
    n = SSM_STATE
    tail_sc[...] = jnp.zeros_like(tail_sc)
    tail_sc[SUBLANES - (CONV_WIDTH - 1):SUBLANES, :] = cbuf_ref[0]
    xs = _conv_silu(xs_ref[...], tail_sc[:, 0:D_INNER], wconv_ref[:, 0:D_INNER], bconv_ref[:, 0:D_INNER])
    o_b, o_c = D_INNER, D_INNER + SSM_GROUPS * n
    bm = _conv_silu(bm_ref[...], tail_sc[:, o_b:o_c], wconv_ref[:, o_b:o_c], bconv_ref[:, o_b:o_c])
    cm = _conv_silu(cm_ref[...], tail_sc[:, o_c:], wconv_ref[:, o_c:], bconv_ref[:, o_c:])
    bm = bm.astype(BF16).astype(F32)
    cm = cm.astype(BF16).astype(F32)

    dt = _softplus(dt_ref[...] + dtb_ref[...])
    a = -jnp.exp(alog_ref[...])
    tri = (_iota((q, q), 0) >= _iota((q, q), 1)).astype(F32)
    d1, d2, d3 = _split3(dt * a)
    acs = _dot(tri, d1.astype(F32)) + _dot(tri, d2.astype(F32)) + _dot(tri, d3.astype(F32))
    a_end = acs[q - 1:q, :]

    row = _iota((q, LANES), 0)
    prods, lms = [], []
    for s in range(q):
        prods.append(cm * bm[s:s + 1, :])
        keep = row >= s
        lms.append(jnp.where(keep, jnp.exp(jnp.where(keep, acs - acs[s:s + 1, :], 0.0)), 0.0))
    qq = q * q
    cb3 = _dot(jnp.concatenate(_split3(jnp.concatenate(prods, axis=0)), axis=0), gsum_ref[...])
    cbs = cb3[0:qq] + cb3[qq:2 * qq] + cb3[2 * qq:3 * qq]
    gmat = (cbs * jnp.concatenate(lms, axis=0)).astype(BF16)
    quant = jnp.concatenate([dt, jnp.exp(acs), jnp.exp(a_end - acs),
                             jnp.broadcast_to(jnp.exp(a_end), (q, LANES))], axis=0)
    nq = 4 * q
    big = _dot(jnp.concatenate(_split3(quant) + (gmat,), axis=0), expand_ref[...])
    qx = big[0:nq] + big[nq:2 * nq] + big[2 * nq:3 * nq]
    dt_x, e_in_x, e_out_x, e_all_x = (qx[i * q:(i + 1) * q] for i in range(4))
    g_x = big[3 * nq:3 * nq + qq]
    xdt = xs * dt_x
    xdt_r = xdt.astype(BF16).astype(F32)
    y = dskip_ref[...] * xs
    for s in range(q):
        y = y + g_x[s * q:(s + 1) * q, :] * xdt_r[s:s + 1, :]
    xdte = xdt * e_out_x

    zpad = jnp.zeros((q, SSM_GROUPS * n), F32)
    cm16 = jnp.concatenate([cm, zpad], axis=0).astype(BF16)
    ones_lo = jnp.where(_iota((2 * q, n), 0) >= q, 1.0, 0.0)
    for g in range(SSM_GROUPS):
        gs = slice(g * gw, (g + 1) * gw)
        h0 = h0_ref[0, gs, :]
        yg = y[:, gs] + _dot_nt(cm16[:, g * n:(g + 1) * n], h0.astype(BF16))[0:q] * e_in_x[:, gs]
        u = yg * jax.nn.silu(z_ref[:, gs])
        u = u * lax.rsqrt(jnp.mean(u * u, axis=-1, keepdims=True) + EPS)
        y_ref[:, gs] = u * wnorm_ref[:, gs]
        e1, e2, e3 = _split3(e_all_x[:, gs])
        srow = _iota((q, gw), 0)
        e_rows = jnp.where(srow == 0, e1.astype(F32),
                           jnp.where(srow == 1, e2.astype(F32), jnp.where(srow == 2, e3.astype(F32), 0.0)))
        lhs = jnp.concatenate([xdte[:, gs].astype(BF16).astype(F32), e_rows], axis=0)
        rhs = jnp.concatenate([jnp.concatenate([bm[:, g * n:(g + 1) * n], zpad[:, 0:n]], axis=0), ones_lo],
                              axis=1)
        both = _dot_tn(lhs, rhs)
        hout_ref[0, gs, :] = both[:, n:2 * n] * h0 + both[:, 0:n]


def _ssm_step(proj, dt_raw, conv_buf, h0, sp, batch, q):
    t = proj.shape[0]
    assert t == batch * q and q == SUBLANES
    col = lambda off, w: off // w
    full = lambda shape: pl.BlockSpec(shape, lambda b: (0,) * len(shape))
    gsum = ((_iota((SSM_GROUPS * SSM_STATE, LANES), 0) // SSM_STATE
             == _iota((SSM_GROUPS * SSM_STATE, LANES), 1) // (SSM_HEADS // SSM_GROUPS))
            & (_iota((SSM_GROUPS * SSM_STATE, LANES), 1) < SSM_HEADS)).astype(BF16)
    return pl.pallas_call(
        _ssm_step_kernel,
        grid=(batch,),
        in_specs=[
            pl.BlockSpec((q, D_INNER), lambda b: (b, col(OFF_Z, D_INNER))),
            pl.BlockSpec((q, D_INNER), lambda b: (b, col(OFF_X, D_INNER))),
            pl.BlockSpec((q, 1024), lambda b: (b, col(OFF_B, 1024))),
            pl.BlockSpec((q, 1024), lambda b: (b, col(OFF_C, 1024))),
            pl.BlockSpec((q, LANES), lambda b: (b, 0)),
            pl.BlockSpec((1, CONV_WIDTH - 1, CONV_DIM), lambda b: (b, 0, 0)),
            pl.BlockSpec((1, D_INNER, SSM_STATE), lambda b: (b, 0, 0)),
            full((CONV_WIDTH, CONV_DIM)),
            full((1, CONV_DIM)),
            full((1, LANES)),
            full((1, LANES)),
            full((1, D_INNER)),
            full((1, D_INNER)),
            full((LANES, D_INNER)),
            full((SSM_GROUPS * SSM_STATE, LANES)),
        ],
        out_specs=[
            pl.BlockSpec((q, D_INNER), lambda b: (b, 0)),
            pl.BlockSpec((1, D_INNER, SSM_STATE), lambda b: (b, 0, 0)),
        ],
        out_shape=[
            jax.ShapeDtypeStruct((t, D_INNER), F32),
            jax.ShapeDtypeStruct((batch, D_INNER, SSM_STATE), F32),
        ],
        scratch_shapes=[pltpu.VMEM((SUBLANES, CONV_DIM), F32)],
        compiler_params=pltpu.CompilerParams(
            dimension_semantics=("parallel",), vmem_limit_bytes=VMEM_LIMIT),
        name="ssm_step",
    )(proj, proj, proj, proj, dt_raw, conv_buf, h0, sp["w_conv"], sp["b_conv"], sp["dt_bias"],
      sp["a_log"], sp["d_skip"], sp["w_norm"], sp["expand"], gsum)


def _prep_params(w_in, norm_mix, w_conv, b_conv, dt_bias, a_log, d_skip, w_ssm_norm, w_q_norm, w_k_norm):
    o_z, o_xbc = 0, D_INNER
    o_dt = o_xbc + CONV_DIM
    o_q = o_dt + SSM_HEADS
    o_k = o_q + ATTN_WIDTH
    o_v = o_k + ATTN_WIDTH
    o_g = o_v + ATTN_WIDTH
    w_main = jnp.concatenate(
        [w_in[:, o_z:o_dt], w_in[:, o_g:], w_in[:, o_q:o_g]], axis=1).astype(BF16)
    w_dt = jnp.pad(w_in[:, o_dt:o_q], ((0, 0), (0, LANES - SSM_HEADS))).astype(BF16)
    qk_w = jnp.ones((N_PROJ_TILES, 1, PROJ_TILE), F32)
    qk_w = qk_w.at[Q_TILE0:K_TILE0, 0].set(jnp.tile(w_q_norm, HEADS_PER_GROUP)[None])
    qk_w = qk_w.at[K_TILE0:V_TILE0, 0].set(jnp.tile(w_k_norm, HEADS_PER_GROUP)[None])
    pad_h = lambda v: jnp.pad(v, (0, LANES - SSM_HEADS))[None]
    expand = (_iota((LANES, D_INNER), 0) == _iota((LANES, D_INNER), 1) // SSM_HEAD_DIM).astype(BF16)
    ssm = dict(w_conv=w_conv, b_conv=b_conv[None], dt_bias=pad_h(dt_bias), a_log=pad_h(a_log),
               d_skip=jnp.repeat(d_skip, SSM_HEAD_DIM)[None], w_norm=w_ssm_norm[None], expand=expand)
    return dict(w_main=w_main, w_dt=w_dt, qk_w=qk_w, norm_mix=norm_mix[None], ssm=ssm)


def _attn_prompt_kernel(q_ref, k_ref, v_ref, o_ref, lse_ref, kp_sc, vp_sc, *, span, dil):
    i = pl.program_id(1)
    blk = ATTN_BLOCK
    pairs = q_ref.shape[1] // LANES

    @pl.when(i == 0)
    def _():
        kp_sc[...] = jnp.zeros_like(kp_sc)
        vp_sc[...] = jnp.zeros_like(vp_sc)

    qi = _iota((blk, 2 * blk), 0)
    kc = _iota((blk, 2 * blk), 1)
    dist = blk + qi - kc
    mask = (dist >= 0) & (dist <= span) & ((kc >= blk) | (i > 0))
    lo_q = _iota((blk, LANES), 1) < ATTN_HEAD_DIM
    lo_kv = _iota((2 * blk, LANES), 1) < ATTN_HEAD_DIM

    def residue(r, carry):
        rows = pl.ds(r, blk, stride=dil) if dil > 1 else slice(None)
        for pr in range(pairs):
            ls = slice(pr * LANES, (pr + 1) * LANES)
            q = q_ref[rows, ls].astype(BF16)
            k2 = jnp.concatenate([kp_sc[rows, ls], k_ref[rows, ls]], axis=0).astype(BF16)
            v2 = jnp.concatenate([vp_sc[rows, ls], v_ref[rows, ls]], axis=0).astype(BF16)
            ps, lses = [], []
            for keep in (lo_q, jnp.logical_not(lo_q)):
                s = jnp.where(mask, _dot_nt(jnp.where(keep, q, 0.0), k2), -jnp.inf)
                m = jnp.max(s, axis=-1, keepdims=True)
                p = jnp.exp(s - m)
                den = jnp.sum(p, axis=-1, keepdims=True)
                ps.append((p / den).astype(BF16))
                lses.append(m + jnp.log(den))
            v_bd = jnp.concatenate([jnp.where(lo_kv, v2, 0.0), jnp.where(lo_kv, 0.0, v2)], axis=0)
            o_ref[rows, ls] = _dot(jnp.concatenate(ps, axis=1), v_bd)
            lse_ref[rows, ls] = jnp.where(lo_q, lses[0], lses[1])
        return carry

    if dil > 1:
        lax.fori_loop(0, dil, residue, 0, unroll=4)
    else:
        residue(0, 0)
    kp_sc[...] = k_ref[...]
    vp_sc[...] = v_ref[...]


def _attn_prompt(proj, gi):
    window, dil = ATTN_PATTERNS[gi]
    s = proj.shape[0]
    chunk = ATTN_BLOCK * dil
    assert s % chunk == 0 and window // dil <= ATTN_BLOCK
    w = LANES if dil > 1 else ATTN_GROUP_WIDTH
    split = ATTN_GROUP_WIDTH // w
    col = lambda tile0: (lambda j, i: (i, (tile0 + gi) * split + j))
    blk = (chunk, w)
    return pl.pallas_call(
        functools.partial(_attn_prompt_kernel, span=window // dil, dil=dil),
        grid=(split, s // chunk),
        in_specs=[pl.BlockSpec(blk, col(Q_TILE0)), pl.BlockSpec(blk, col(K_TILE0)),
                  pl.BlockSpec(blk, col(V_TILE0))],
        out_specs=[pl.BlockSpec(blk, lambda j, i: (i, j))] * 2,
        out_shape=[jax.ShapeDtypeStruct((s, ATTN_GROUP_WIDTH), F32)] * 2,
        scratch_shapes=[pltpu.VMEM(blk, F32), pltpu.VMEM(blk, F32)],
        compiler_params=pltpu.CompilerParams(
            dimension_semantics=("parallel", "arbitrary"), vmem_limit_bytes=VMEM_LIMIT),
        name=f"attn_prompt_g{gi}",
    )(proj, proj, proj)


def _attn_sample_kernel(q_ref, k_ref, v_ref, kc_ref, vc_ref, o_ref, lse_ref, ko_ref, vo_ref,
                        *, span, dil, pos0):
    l = SUBLANES
    nb = kc_ref.shape[0]
    nh, hd = HEADS_PER_GROUP, ATTN_HEAD_DIM
    rows = kc_ref.shape[3]
    w = nh * hd
    t_c = _iota((l, rows), 0)
    delta_c = rows + t_c - _iota((l, rows), 1)
    ok_c = ((delta_c <= span * dil) & ((delta_c & (dil - 1)) == 0) & (pos0 + t_c - delta_c >= 0))
    delta_n = _iota((l, l), 0) - _iota((l, l), 1)
    ok_n = (delta_n >= 0) & (delta_n <= span * dil) & ((delta_n & (dil - 1)) == 0)
    lane = _iota((hd, LANES), 1)
    hl = nh * l
    own = _shr(_iota((hl, w), 0), l) == _shr(_iota((hl, w), 1), hd)
    ok_c = jnp.concatenate([ok_c] * nh, axis=0)
    ok_n = jnp.concatenate([ok_n] * nh, axis=0)
    for bb in range(nb):
        rs = slice(bb * l, (bb + 1) * l)
        q, kn, vn = q_ref[rs, :], k_ref[rs, :], v_ref[rs, :]
        q_bd = jnp.where(own, jnp.concatenate([q] * nh, axis=0), 0.0)
        kt = kc_ref[bb].reshape(w, rows)
        vt = vc_ref[bb].reshape(w, rows)
        s_c = jnp.where(ok_c, _dot(q_bd, kt), -jnp.inf)
        s_n = jnp.where(ok_n, _dot_nt(q_bd, kn), -jnp.inf)
        m = jnp.maximum(jnp.max(s_c, axis=-1, keepdims=True), jnp.max(s_n, axis=-1, keepdims=True))
        p_c = jnp.exp(s_c - m)
        p_n = jnp.exp(s_n - m)
        den = jnp.sum(p_c, axis=-1, keepdims=True) + jnp.sum(p_n, axis=-1, keepdims=True)
        res = jnp.where(own, _dot_nt(p_c / den, vt) + _dot(p_n / den, vn), 0.0)
        lse = jnp.where(own, m + jnp.log(den), 0.0)
        o, lse_x = res[0:l], lse[0:l]
        for h in range(1, nh):
            o = o + res[h * l:(h + 1) * l]
            lse_x = lse_x + lse[h * l:(h + 1) * l]
        o_ref[rs, :] = o
        lse_ref[rs, :] = lse_x

        for new, c_ref, out_ref in ((kn, kc_ref, ko_ref), (vn, vc_ref, vo_ref)):
            padded = jnp.concatenate([jnp.zeros((LANES - l, w), F32), new], axis=0)
            tail = padded.T
            for h in range(nh):
                shifted = pltpu.roll(c_ref[bb, h], rows - l, axis=1)
                if rows > LANES:
                    out_ref[bb, h, :, 0:rows - LANES] = shifted[:, 0:rows - LANES]
                out_ref[bb, h, :, rows - LANES:rows] = jnp.where(
                    lane < LANES - l, shifted[:, rows - LANES:rows], tail[h * hd:(h + 1) * hd])


SAMPLE_CACHE_ROWS_PER_STEP = 2048
SAMPLE_MAX_SEQS_PER_STEP = 8


def _attn_sample(proj, k_cache, v_cache, gi, batch, pos0):
    window, dil = ATTN_PATTERNS[gi]
    rows = k_cache.shape[1]
    l = proj.shape[0] // batch
    assert l == SUBLANES and rows % LANES == 0
    nb = max(1, min(batch, SAMPLE_CACHE_ROWS_PER_STEP // rows, SAMPLE_MAX_SEQS_PER_STEP))
    assert batch % nb == 0
    tok = lambda tile0: pl.BlockSpec((nb * l, ATTN_GROUP_WIDTH), lambda b: (b, tile0 + gi))
    cache = pl.BlockSpec((nb, HEADS_PER_GROUP, ATTN_HEAD_DIM, rows), lambda b: (b, 0, 0, 0))
    out_tok = pl.BlockSpec((nb * l, ATTN_GROUP_WIDTH), lambda b: (b, 0))
    to_t = lambda c: jnp.transpose(c, (0, 2, 3, 1))
    t_shape = jax.ShapeDtypeStruct((batch, HEADS_PER_GROUP, ATTN_HEAD_DIM, rows), F32)
    o, lse, k_t, v_t = pl.pallas_call(
        functools.partial(_attn_sample_kernel, span=window // dil, dil=dil, pos0=pos0),
        grid=(batch // nb,),
        in_specs=[tok(Q_TILE0), tok(K_TILE0), tok(V_TILE0), cache, cache],
        out_specs=[out_tok, out_tok, cache, cache],
        out_shape=[jax.ShapeDtypeStruct((batch * l, ATTN_GROUP_WIDTH), F32)] * 2 + [t_shape] * 2,
        compiler_params=pltpu.CompilerParams(
            dimension_semantics=("parallel",), vmem_limit_bytes=VMEM_LIMIT),
        name=f"attn_sample_g{gi}",
    )(proj, proj, proj, to_t(k_cache), to_t(v_cache))
    from_t = lambda c: jnp.transpose(c, (0, 3, 1, 2))
    return o, lse, from_t(k_t), from_t(v_t)


def _merge_kernel(x_ref, ys_ref, o0_ref, o1_ref, o2_ref, l0_ref, l1_ref, l2_ref, g0_ref, g1_ref,
                  wbs_ref, wba_ref, wo_ref, nf_ref, wr1_ref, wr2_ref, br_ref,
                  x1_ref, xn_ref, lg_ref):
    l0, l1, l2 = l0_ref[...], l1_ref[...], l2_ref[...]
    m = jnp.maximum(jnp.maximum(l0, l1), l2)
    e0, e1, e2 = jnp.exp(l0 - m), jnp.exp(l1 - m), jnp.exp(l2 - m)
    den = e0 + e1 + e2
    y_attn = (e0 / den) * o0_ref[...] + (e1 / den) * o1_ref[...] + (e2 / den) * o2_ref[...]
    br_s = _dot(ys_ref[...].astype(BF16), wbs_ref[...])
    br_a = _dot(y_attn.astype(BF16), wba_ref[...])
    merged = jax.nn.sigmoid(g0_ref[...]) * br_s + jax.nn.sigmoid(g1_ref[...]) * br_a
    x1 = x_ref[...] + _dot(merged.astype(BF16), wo_ref[...])
    x1_ref[...] = x1
    xn = x1 * lax.rsqrt(jnp.mean(x1 * x1, axis=-1, keepdims=True) + EPS) * nf_ref[...]
    xn1 = xn.astype(BF16)
    xn_ref[...] = xn1
    xn2 = (xn - xn1.astype(F32)).astype(BF16)
    lg_ref[...] = (_dot(xn1, wr1_ref[...]) + _dot(xn2, wr1_ref[...]) + _dot(xn1, wr2_ref[...])
                   + br_ref[...])


def _merge(x, proj, y_ssm, attn, mp, tm):
    t = x.shape[0]
    row = lambda w: pl.BlockSpec((tm, w), lambda i: (i, 0))
    full = lambda a: pl.BlockSpec(a.shape, lambda i: (0,) * a.ndim)
    (o0, l0), (o1, l1), (o2, l2) = attn
    weights = [mp["w_br_ssm"], mp["w_br_attn"], mp["w_out"], mp["norm_ffn"], mp["w_r1"], mp["w_r2"], mp["b_r"]]
    return pl.pallas_call(
        _merge_kernel,
        grid=(t // tm,),
        in_specs=[row(D_MODEL), row(D_INNER)] + [row(ATTN_GROUP_WIDTH)] * 6 + [
            pl.BlockSpec((tm, D_MODEL), lambda i: (i, OFF_G // D_MODEL)),
            pl.BlockSpec((tm, D_MODEL), lambda i: (i, OFF_G // D_MODEL + 1)),
        ] + [full(w) for w in weights],
        out_specs=[row(D_MODEL), row(D_MODEL), row(ROUTER_WIDTH)],
        out_shape=[
            jax.ShapeDtypeStruct((t, D_MODEL), F32),
            jax.ShapeDtypeStruct((t, D_MODEL), BF16),
            jax.ShapeDtypeStruct((t, ROUTER_WIDTH), F32),
        ],
        compiler_params=pltpu.CompilerParams(
            dimension_semantics=("parallel",), vmem_limit_bytes=VMEM_LIMIT),
        name="merge",
    )(x, y_ssm, o0, o1, o2, l0, l1, l2, proj, proj, *weights)


def _prep_merge_params(w_br_ssm, w_br_attn, w_out, norm_ffn, w_router_group, b_router_group,
                       w_router_expert, b_router_expert):
    w_r = jnp.concatenate([w_router_group, w_router_expert.reshape(D_MODEL, N_EXPERTS)], axis=1)
    w_r = jnp.pad(w_r, ((0, 0), (0, ROUTER_WIDTH - w_r.shape[1])))
    w_r1 = w_r.astype(BF16)
    w_r2 = (w_r - w_r1.astype(F32)).astype(BF16)
    b_r = jnp.concatenate([b_router_group, b_router_expert.reshape(N_EXPERTS)])
    b_r = jnp.pad(b_r, (0, ROUTER_WIDTH - b_r.shape[0]))[None]
    return dict(w_br_ssm=w_br_ssm.astype(BF16), w_br_attn=w_br_attn.astype(BF16), w_out=w_out.astype(BF16),
                norm_ffn=norm_ffn[None], w_r1=w_r1, w_r2=w_r2, b_r=b_r)


def _route(logits):
    lane = _iota(logits.shape, 1)
    big = jnp.int32(2 ** 30)
    neg = -jnp.inf
    gl = jnp.where(lane < N_EXPERT_GROUPS, logits, neg)
    gmax = jnp.max(gl, axis=-1, keepdims=True)
    gidx = jnp.min(jnp.where(gl == gmax, lane, big), axis=-1, keepdims=True)
    g_w = 1.0 / jnp.sum(jnp.exp(gl - gmax), axis=-1, keepdims=True)
    lo = N_EXPERT_GROUPS + gidx * EXPERTS_PER_GROUP
    el = jnp.where((lane >= lo) & (lane < lo + EXPERTS_PER_GROUP), logits, neg)
    v1 = jnp.max(el, axis=-1, keepdims=True)
    i1 = jnp.min(jnp.where(el == v1, lane, big), axis=-1, keepdims=True)
    el2 = jnp.where(lane == i1, neg, el)
    v2 = jnp.max(el2, axis=-1, keepdims=True)
    i2 = jnp.min(jnp.where(el2 == v2, lane, big), axis=-1, keepdims=True)
    e2 = jnp.exp(v2 - v1)
    w1 = (1.0 / (1.0 + e2)) * g_w
    w2 = (e2 / (1.0 + e2)) * g_w
    return jnp.where(lane == i1, w1, jnp.where(lane == i2, w2, 0.0)), gidx


MOE_BLOCK = 256
MOE_SMALL_BLOCK = 64


def _moe_kernel(x1_ref, xn_ref, lg_ref, wg_ref, wu_ref, wd_ref, out_ref,
                comb_sc, oh_sc, cum_sc, oht_sc, cumt_sc):
    g = pl.program_id(1)
    tm = x1_ref.shape[0]
    blk = MOE_BLOCK

    @pl.when(g == 0)
    def _():
        comb, gidx = _route(lg_ref[...])
        c1, c2, c3 = _split3(comb)
        comb_sc[0] = c1
        comb_sc[1] = c2
        comb_sc[2] = c3
        onehot = (_iota((tm, LANES), 1) == gidx).astype(F32)
        oh_bf = onehot.astype(BF16)
        lower = (_iota((tm, tm), 0) > _iota((tm, tm), 1)).astype(BF16)
        upper = (_iota((tm, tm), 0) < _iota((tm, tm), 1)).astype(BF16)
        oh_sc[...] = onehot
        cum_sc[...] = _dot(lower, oh_bf)
        oht_sc[...] = onehot.T[0:SUBLANES]
        cumt_sc[...] = _dot_tn(oh_bf, upper)[0:SUBLANES]
        out_ref[...] = x1_ref[...]

    lane = _iota((tm, LANES), 1)
    pick = lane == g
    member_c = jnp.sum(jnp.where(pick, oh_sc[...], 0.0), axis=-1, keepdims=True)
    rank_c = jnp.sum(jnp.where(pick, cum_sc[...], 0.0), axis=-1, keepdims=True)
    member_r = oht_sc[pl.ds(g, 1), :]
    rank_r = cumt_sc[pl.ds(g, 1), :]
    count = jnp.sum(member_r).astype(jnp.int32)
    small = MOE_SMALL_BLOCK
    nfull = count // blk
    rem = count - nfull * blk
    use_small = rem <= blk // 2
    n_big = nfull + jnp.where(use_small, 0, 1)
    n_small = jnp.where(use_small, (rem + (small - 1)) // small, 0)
    xn = xn_ref[...]
    wd = wd_ref[...].reshape(EXPERTS_PER_GROUP * D_EXPERT, D_MODEL)

    def run_block(first_row, blk):
        lane_b = _iota((blk, LANES), 1)
        base = first_row.astype(F32)
        want_r = base + _iota((blk, tm), 0).astype(F32)
        sel = jnp.where((rank_r == want_r) & (member_r > 0.0), 1.0, 0.0).astype(BF16)
        want_c = base + _iota((tm, blk), 1).astype(F32)
        sel_t = jnp.where((rank_c == want_c) & (member_c > 0.0), 1.0, 0.0).astype(BF16)
        xg = _dot(sel, xn).astype(BF16)
        cg = _dot(sel, comb_sc[0]) + _dot(sel, comb_sc[1]) + _dot(sel, comb_sc[2])
        hidden = []
        for e in range(EXPERTS_PER_GROUP):
            col = N_EXPERT_GROUPS + g * EXPERTS_PER_GROUP + e
            cw = jnp.sum(jnp.where(lane_b == col, cg, 0.0), axis=-1, keepdims=True)
            h = jax.nn.silu(_dot(xg, wg_ref[e])) * _dot(xg, wu_ref[e]) * cw
            hidden.append(h.astype(BF16))
        out = _dot(jnp.concatenate(hidden, axis=1), wd)
        o1 = out.astype(BF16)
        o2 = (out - o1.astype(F32)).astype(BF16)
        out_ref[...] += _dot(sel_t, o1) + _dot(sel_t, o2)

    def big(k, carry):
        run_block(k * blk, blk)
        return carry

    def little(k, carry):
        run_block(nfull * blk + k * small, small)
        return carry

    lax.fori_loop(0, n_big, big, 0)
    lax.fori_loop(0, n_small, little, 0)


def _moe(x1, xn, logits, wg, wu, wd, tm):
    t = x1.shape[0]
    ge = EXPERTS_PER_GROUP
    return pl.pallas_call(
        _moe_kernel,
        grid=(t // tm, N_EXPERT_GROUPS),
        in_specs=[
            pl.BlockSpec((tm, D_MODEL), lambda i, g: (i, 0), pipeline_mode=pl.Buffered(1)),
            pl.BlockSpec((tm, D_MODEL), lambda i, g: (i, 0)),
            pl.BlockSpec((tm, ROUTER_WIDTH), lambda i, g: (i, 0), pipeline_mode=pl.Buffered(1)),
            pl.BlockSpec((ge, D_MODEL, D_EXPERT), lambda i, g: (g, 0, 0)),
            pl.BlockSpec((ge, D_MODEL, D_EXPERT), lambda i, g: (g, 0, 0)),
            pl.BlockSpec((ge, D_EXPERT, D_MODEL), lambda i, g: (g, 0, 0)),
        ],
        out_specs=pl.BlockSpec((tm, D_MODEL), lambda i, g: (i, 0)),
        out_shape=jax.ShapeDtypeStruct((t, D_MODEL), F32),
        scratch_shapes=[
            pltpu.VMEM((3, tm, ROUTER_WIDTH), BF16),
            pltpu.VMEM((tm, LANES), F32),
            pltpu.VMEM((tm, LANES), F32),
            pltpu.VMEM((SUBLANES, tm), F32),
            pltpu.VMEM((SUBLANES, tm), F32),
        ],
        compiler_params=pltpu.CompilerParams(
            dimension_semantics=("parallel", "arbitrary"), vmem_limit_bytes=VMEM_LIMIT),
        name="moe",
    )(x1, xn, logits, wg, wu, wd)


def _moe_dense_kernel(x1_ref, xn_ref, lg_ref, wg_ref, wu_ref, wd_ref, out_ref, comb_sc, acc_sc):
    e = pl.program_id(1)

    @pl.when(e == 0)
    def _():
        comb_sc[...] = _route(lg_ref[...])[0]
        acc_sc[...] = jnp.zeros_like(acc_sc)

    lane = _iota(comb_sc.shape, 1)
    cw = jnp.sum(jnp.where(lane == e + N_EXPERT_GROUPS, comb_sc[...], 0.0), axis=-1, keepdims=True)
    xn = xn_ref[...]
    hidden = jax.nn.silu(_dot(xn, wg_ref[0])) * _dot(xn, wu_ref[0]) * cw
    acc_sc[...] += _dot(hidden.astype(BF16), wd_ref[0])

    @pl.when(e == pl.num_programs(1) - 1)
    def _():
        out_ref[...] = x1_ref[...] + acc_sc[...]


def _moe_dense(x1, xn, logits, wg, wu, wd, tm):
    t = x1.shape[0]
    return pl.pallas_call(
        _moe_dense_kernel,
        grid=(t // tm, N_EXPERTS),
        in_specs=[
            pl.BlockSpec((tm, D_MODEL), lambda i, e: (i, 0)),
            pl.BlockSpec((tm, D_MODEL), lambda i, e: (i, 0)),
            pl.BlockSpec((tm, ROUTER_WIDTH), lambda i, e: (i, 0)),
            pl.BlockSpec((1, D_MODEL, D_EXPERT), lambda i, e: (e, 0, 0)),
            pl.BlockSpec((1, D_MODEL, D_EXPERT), lambda i, e: (e, 0, 0)),
            pl.BlockSpec((1, D_EXPERT, D_MODEL), lambda i, e: (e, 0, 0)),
        ],
        out_specs=pl.BlockSpec((tm, D_MODEL), lambda i, e: (i, 0)),
        out_shape=jax.ShapeDtypeStruct((t, D_MODEL), F32),
        scratch_shapes=[pltpu.VMEM((tm, ROUTER_WIDTH), F32), pltpu.VMEM((tm, D_MODEL), F32)],
        compiler_params=pltpu.CompilerParams(
            dimension_semantics=("parallel", "arbitrary"), vmem_limit_bytes=VMEM_LIMIT),
        name="moe_dense",
    )(x1, xn, logits, wg, wu, wd)


def _tile_rows(t, cap):
    tm = min(t, cap)
    assert t % tm == 0
    return tm


def _decoder_layer(x, conv_buf, ssm_state, kv_bufs, pos0, pp, mp, ep):
    b, l, d = x.shape
    t = b * l
    x2 = x.reshape(t, d)
    proj, dt_raw = _in_proj(x2, pp["norm_mix"], pp["w_main"], pp["w_dt"], pp["qk_w"], _tile_rows(t, 2048))
    proj3 = proj.reshape(b, l, PROJ_WIDTH)
    h0 = ssm_state.reshape(b, D_INNER, SSM_STATE)
    if kv_bufs is None:
        assert b == 1 and l % SSD_CHUNK == 0
        y_ssm, h_new = _ssm_chunked(proj, dt_raw, conv_buf, h0, pp["ssm"], b, SSD_CHUNK)
        attn = [_attn_prompt(proj, gi) for gi in range(N_ATTN_GROUPS)]
        new_kv = []
        for gi, (window, _) in enumerate(ATTN_PATTERNS):
            keep = min(window, l)
            for off in (OFF_K, OFF_V):
                c0 = off + gi * ATTN_GROUP_WIDTH
                new_kv.append(proj3[:, l - keep:, c0:c0 + ATTN_GROUP_WIDTH]
                              .reshape(b, keep, HEADS_PER_GROUP, ATTN_HEAD_DIM))
    else:
        y_ssm, h_new = _ssm_step(proj, dt_raw, conv_buf, h0, pp["ssm"], b, l)
        attn, new_kv = [], []
        for gi in range(N_ATTN_GROUPS):
            kc, vc = kv_bufs[2 * gi], kv_bufs[2 * gi + 1]
            o, lse, kn, vn = _attn_sample(proj, kc, vc, gi, b, pos0)
            attn.append((o, lse))
            new_kv += [kn, vn]
    new_conv = proj3[:, l - (CONV_WIDTH - 1):, OFF_X:OFF_X + CONV_DIM]
    tm = _tile_rows(t, 512)
    x1, xn, logits = _merge(x2, proj, y_ssm, attn, mp, tm)
    y = _moe(x1, xn, logits, ep["wg"], ep["wu"], ep["wd"], _tile_rows(t, 1024))
    new_ssm = h_new.reshape(b, SSM_HEADS, SSM_HEAD_DIM, SSM_STATE)
    return y.reshape(b, l, d), (new_conv, new_ssm) + tuple(new_kv)


def kernel(x_prompt, x_sample, cache_conv, state_ssm, cache_k_w128, cache_v_w128, cache_k_w512, cache_v_w512,
           cache_k_w2048, cache_v_w2048, w_in, norm_mix, w_conv, b_conv, dt_bias, a_log, d_skip, w_ssm_norm,
           w_q_norm, w_k_norm, w_br_ssm, w_br_attn, w_out, norm_ffn, w_router_group, b_router_group,
           w_router_expert, b_router_expert, w_expert_gate, w_expert_up, w_expert_down):
    depth = w_in.shape[0]
    yp, ys = x_prompt, x_sample
    bp = x_prompt.shape[0]
    prompt_states, sample_states = [], []
    for i in range(depth):
        pp = _prep_params(w_in[i], norm_mix[i], w_conv[i], b_conv[i], dt_bias[i], a_log[i], d_skip[i],
                          w_ssm_norm[i], w_q_norm[i], w_k_norm[i])
        mp = _prep_merge_params(w_br_ssm[i], w_br_attn[i], w_out[i], norm_ffn[i], w_router_group[i],
                                b_router_group[i], w_router_expert[i], b_router_expert[i])
        ep = dict(wg=w_expert_gate[i].astype(BF16), wu=w_expert_up[i].astype(BF16),
                  wd=w_expert_down[i].astype(BF16))
        zero_conv = jnp.zeros((bp, CONV_WIDTH - 1, CONV_DIM), x_prompt.dtype)
        zero_ssm = jnp.zeros((bp, SSM_HEADS, SSM_HEAD_DIM, SSM_STATE), F32)
        yp, st_p = _decoder_layer(yp, zero_conv, zero_ssm, None, 0, pp, mp, ep)
        kv_bufs = (cache_k_w128[i], cache_v_w128[i], cache_k_w512[i], cache_v_w512[i],
                   cache_k_w2048[i], cache_v_w2048[i])
        ys, st_s = _decoder_layer(ys, cache_conv[i], state_ssm[i], kv_bufs, PAST_LEN, pp, mp, ep)
        prompt_states.append(st_p)
        sample_states.append(st_s)
    stack = lambda states: [jnp.stack([s[j] for s in states]) for j in range(8)]
    return (yp, ys) + tuple(stack(prompt_states)) + tuple(stack(sample_states))
```

```python
import functools
import math

import jax
import jax.numpy as jnp
from jax import lax
from jax.experimental import pallas as pl
from jax.experimental.pallas import tpu as pltpu

F32 = jnp.float32
BF16 = jnp.bfloat16

PAST_LEN = 8192
D_MODEL = 1024
D_INNER = 2048
SSM_HEAD_DIM = 64
SSM_HEADS = 32
SSM_GROUPS = 8
SSM_STATE = 128
CONV_WIDTH = 4
CONV_DIM = D_INNER + 2 * SSM_GROUPS * SSM_STATE
SSD_CHUNK = 128
ATTN_HEAD_DIM = 64
HEADS_PER_GROUP = 8
ATTN_PATTERNS = ((128, 1), (512, 4), (2048, 16))
N_ATTN_GROUPS = 3
ATTN_GROUP_WIDTH = HEADS_PER_GROUP * ATTN_HEAD_DIM
ATTN_WIDTH = N_ATTN_GROUPS * ATTN_GROUP_WIDTH
ATTN_BLOCK = 128
N_EXPERT_GROUPS = 4
EXPERTS_PER_GROUP = 8
N_EXPERTS = 32
D_EXPERT = 256
EPS = 1e-6

LANES = 128
SUBLANES = 8
VMEM_LIMIT = 56 * 1024 * 1024

PROJ_TILE = 512
OFF_Z = 0
OFF_X = 2048
OFF_B = 4096
OFF_C = 5120
OFF_G = 6144
OFF_Q = 8192
OFF_K = OFF_Q + ATTN_WIDTH
OFF_V = OFF_K + ATTN_WIDTH
PROJ_WIDTH = OFF_V + ATTN_WIDTH
N_PROJ_TILES = PROJ_WIDTH // PROJ_TILE
Q_TILE0 = OFF_Q // PROJ_TILE
K_TILE0 = OFF_K // PROJ_TILE
V_TILE0 = OFF_V // PROJ_TILE
ROUTER_WIDTH = LANES


def _split3(x):
    x1 = x.astype(BF16)
    r1 = x - x1.astype(F32)
    x2 = r1.astype(BF16)
    r2 = r1 - x2.astype(F32)
    return x1, x2, r2.astype(BF16)


def _dot(a, b):
    return jnp.dot(a, b, preferred_element_type=F32)


def _dot3(x, m):
    x1, x2, x3 = _split3(x)
    return _dot(x1, m) + _dot(x2, m) + _dot(x3, m)


def _dot3f(x, m):
    x1, x2, x3 = _split3(x)
    return _dot(x1.astype(F32), m) + _dot(x2.astype(F32), m) + _dot(x3.astype(F32), m)


def _dot_nt(a, b):
    return lax.dot_general(a, b, (((1,), (1,)), ((), ())), preferred_element_type=F32)


def _dot_tn(a, b):
    return lax.dot_general(a, b, (((0,), (0,)), ((), ())), preferred_element_type=F32)


def _iota(shape, dim):
    return lax.broadcasted_iota(jnp.int32, shape, dim)


def _shr(x, pow2):
    assert pow2 & (pow2 - 1) == 0
    return lax.shift_right_logical(x, jnp.int32(int(math.log2(pow2))))


def _in_proj_kernel(x_ref, nw_ref, w_ref, wdt_ref, qkw_ref, seg_ref, out_ref, dt_ref, h_sc):
    j = pl.program_id(1)

    @pl.when(j == 0)
    def _():
        xf = x_ref[...]
        y = xf * lax.rsqrt(jnp.mean(xf * xf, axis=-1, keepdims=True) + EPS)
        h = (y * nw_ref[...]).astype(BF16)
        h_sc[...] = h
        dt_ref[...] = _dot(h, wdt_ref[...])

    acc = _dot(h_sc[...], w_ref[...])
    is_qk = jnp.logical_and(j >= Q_TILE0, j < V_TILE0)

    @pl.when(jnp.logical_not(is_qk))
    def _():
        out_ref[...] = acc

    @pl.when(is_qk)
    def _():
        sq = acc * acc
        s1 = sq.astype(BF16)
        s2 = (sq - s1.astype(F32)).astype(BF16)
        ms = (_dot(s1, seg_ref[...]) + _dot(s2, seg_ref[...])) * (1.0 / ATTN_HEAD_DIM)
        scale = jnp.where(j < K_TILE0, ATTN_HEAD_DIM ** -0.5, 1.0).astype(F32)
        w = qkw_ref[0] * scale
        out_ref[...] = acc * lax.rsqrt(ms + EPS) * w


def _in_proj(x, norm_w, w_bf, wdt_bf, qk_w, tm):
    t = x.shape[0]
    seg = (_iota((PROJ_TILE, PROJ_TILE), 0) // ATTN_HEAD_DIM
           == _iota((PROJ_TILE, PROJ_TILE), 1) // ATTN_HEAD_DIM).astype(BF16)
    return pl.pallas_call(
        _in_proj_kernel,
        grid=(t // tm, N_PROJ_TILES),
        in_specs=[
            pl.BlockSpec((tm, D_MODEL), lambda i, j: (i, 0)),
            pl.BlockSpec((1, D_MODEL), lambda i, j: (0, 0)),
            pl.BlockSpec((D_MODEL, PROJ_TILE), lambda i, j: (0, j)),
            pl.BlockSpec((D_MODEL, LANES), lambda i, j: (0, 0)),
            pl.BlockSpec((1, 1, PROJ_TILE), lambda i, j: (j, 0, 0)),
            pl.BlockSpec((PROJ_TILE, PROJ_TILE), lambda i, j: (0, 0)),
        ],
        out_specs=[
            pl.BlockSpec((tm, PROJ_TILE), lambda i, j: (i, j)),
            pl.BlockSpec((tm, LANES), lambda i, j: (i, 0)),
        ],
        out_shape=[
            jax.ShapeDtypeStruct((t, PROJ_WIDTH), F32),
            jax.ShapeDtypeStruct((t, LANES), F32),
        ],
        scratch_shapes=[pltpu.VMEM((tm, D_MODEL), BF16)],
        compiler_params=pltpu.CompilerParams(
            dimension_semantics=("parallel", "arbitrary"), vmem_limit_bytes=VMEM_LIMIT),
        name="in_proj",
    )(x, norm_w, w_bf, wdt_bf, qk_w, seg)


def _softplus(x):
    return jnp.maximum(x, 0.0) + jnp.log1p(jnp.exp(-jnp.abs(x)))


def _conv_silu(raw, tail, w, b):
    q = raw.shape[0]
    up = jnp.concatenate([tail, raw], axis=0)
    acc = b + w[CONV_WIDTH - 1:CONV_WIDTH] * raw
    for tap in range(CONV_WIDTH - 1):
        shifted = pltpu.roll(up, CONV_WIDTH - 1 - tap, axis=0)[SUBLANES:SUBLANES + q]
        acc = acc + w[tap:tap + 1] * shifted
    return jax.nn.silu(acc)


def _ssm_chunk_kernel(z_ref, xs_ref, bm_ref, cm_ref, dt_ref, cbuf_ref, h0_ref, wconv_ref, bconv_ref,
                      dtb_ref, alog_ref, dskip_ref, wnorm_ref, expand_ref,
                      y_ref, hout_ref, tail_sc, xbc_sc, ht_sc):
    c = pl.program_id(1)
    nc = pl.num_programs(1)
    q = xs_ref.shape[0]
    gw = D_INNER // SSM_GROUPS
    n = SSM_STATE

    @pl.when(c == 0)
    def _():
        tail_sc[...] = jnp.zeros_like(tail_sc)
        tail_sc[SUBLANES - (CONV_WIDTH - 1):SUBLANES, :] = cbuf_ref[0]
        for g in range(SSM_GROUPS):
            ht_sc[g] = h0_ref[0, g * gw:(g + 1) * gw, :].T

    cw = PROJ_TILE
    for k in range(CONV_DIM // cw):
        if k < D_INNER // cw:
            raw = xs_ref[:, k * cw:(k + 1) * cw]
        elif k < (D_INNER + SSM_GROUPS * n) // cw:
            kk = k - D_INNER // cw
            raw = bm_ref[:, kk * cw:(kk + 1) * cw]
        else:
            kk = k - (D_INNER + SSM_GROUPS * n) // cw
            raw = cm_ref[:, kk * cw:(kk + 1) * cw]
        sl = slice(k * cw, (k + 1) * cw)
        xbc_sc[:, sl] = _conv_silu(raw, tail_sc[:, sl], wconv_ref[:, sl], bconv_ref[:, sl])
        tail_sc[:, sl] = raw[q - SUBLANES:q]

    dt = _softplus(dt_ref[...] + dtb_ref[...])
    a = -jnp.exp(alog_ref[...])
    dta = dt * a
    tri = (_iota((q, q), 0) >= _iota((q, q), 1)).astype(BF16)
    d1, d2, d3 = _split3(dta)
    acs = _dot(tri, d1) + _dot(tri, d2) + _dot(tri, d3)
    eye = (_iota((LANES, LANES), 0) == _iota((LANES, LANES), 1)).astype(BF16)
    c1, c2, c3 = _split3(acs)
    acs_t = _dot_nt(eye, c1) + _dot_nt(eye, c2) + _dot_nt(eye, c3)
    a_end = acs[q - 1:q, :]
    e_in = jnp.exp(acs)
    e_out = jnp.exp(a_end - acs)
    e_all = jnp.exp(a_end)

    ex = expand_ref[...]
    dt_x = _dot3(dt, ex)
    e_in_x = _dot3(e_in, ex)
    e_out_x = _dot3(e_out, ex)
    e_all_x = _dot3(jnp.broadcast_to(e_all, (2 * SUBLANES, LANES)), ex)[0:1]

    causal = _iota((q, q), 0) >= _iota((q, q), 1)
    lane_lo = _iota((q, LANES), 1) < SSM_HEAD_DIM

    for g in range(SSM_GROUPS):
        gs = slice(g * gw, (g + 1) * gw)
        xg = xbc_sc[:, gs]
        bg = xbc_sc[:, D_INNER + g * n:D_INNER + (g + 1) * n].astype(BF16)
        cg = xbc_sc[:, D_INNER + SSM_GROUPS * n + g * n:D_INNER + SSM_GROUPS * n + (g + 1) * n].astype(BF16)
        xdt = xg * dt_x[:, gs]
        cb = _dot_nt(cg, bg)
        ht = ht_sc[g]
        y_off = _dot(cg, ht.astype(BF16)) * e_in_x[:, gs]
        pieces = []
        for pair in range(2):
            ms = []
            for hh in range(2):
                h = g * 4 + pair * 2 + hh
                diff = acs[:, h:h + 1] - acs_t[h:h + 1, :]
                lmat = jnp.where(causal, jnp.exp(jnp.where(causal, diff, 0.0)), 0.0)
                ms.append((cb * lmat).astype(BF16))
            xp = xdt[:, pair * LANES:(pair + 1) * LANES]
            rhs = jnp.concatenate([jnp.where(lane_lo, xp, 0.0), jnp.where(lane_lo, 0.0, xp)],
                                  axis=0).astype(BF16)
            pieces.append(_dot(jnp.concatenate(ms, axis=1), rhs))
        y_diag = jnp.concatenate(pieces, axis=1)
        yg = y_diag + y_off + dskip_ref[:, gs] * xg
        zg = z_ref[:, gs]
        u = yg * jax.nn.silu(zg)
        u = u * lax.rsqrt(jnp.mean(u * u, axis=-1, keepdims=True) + EPS)
        y_ref[:, gs] = (u * wnorm_ref[:, gs]).astype(y_ref.dtype)
        st = _dot_tn(bg, (xdt * e_out_x[:, gs]).astype(BF16))
        ht_sc[g] = e_all_x[:, gs] * ht + st

    @pl.when(c == nc - 1)
    def _():
        for g in range(SSM_GROUPS):
            hout_ref[0, g * gw:(g + 1) * gw, :] = ht_sc[g].T


def _ssm_chunked(proj, dt_raw, conv_buf, h0, sp, batch, q, rider=None):
    t = proj.shape[0]
    nc = t // batch // q
    row = lambda b, c: b * nc + c
    col = lambda off, w: off // w
    full = lambda shape: pl.BlockSpec(shape, lambda b, c: (0,) * len(shape))
    n_in, n_out = 14, 2
    rider = rider or dict(kernel=None, in_specs=[], out_specs=[], out_shape=[], args=())
    r_in, r_out = len(rider["in_specs"]), len(rider["out_specs"])

    def body(*refs):
        ins, refs = refs[:n_in], refs[n_in:]
        r_ins, refs = refs[:r_in], refs[r_in:]
        outs, refs = refs[:n_out], refs[n_out:]
        r_outs, scratch = refs[:r_out], refs[r_out:]
        _ssm_chunk_kernel(*ins, *outs, *scratch)
        if rider["kernel"] is not None:
            rider["kernel"](*r_ins, *r_outs)

    outs = pl.pallas_call(
        body,
        grid=(batch, nc),
        in_specs=[
            pl.BlockSpec((q, D_INNER), lambda b, c: (row(b, c), col(OFF_Z, D_INNER))),
            pl.BlockSpec((q, D_INNER), lambda b, c: (row(b, c), col(OFF_X, D_INNER))),
            pl.BlockSpec((q, 1024), lambda b, c: (row(b, c), col(OFF_B, 1024))),
            pl.BlockSpec((q, 1024), lambda b, c: (row(b, c), col(OFF_C, 1024))),
            pl.BlockSpec((q, LANES), lambda b, c: (row(b, c), 0)),
            pl.BlockSpec((1, CONV_WIDTH - 1, CONV_DIM), lambda b, c: (b, 0, 0)),
            pl.BlockSpec((1, D_INNER, SSM_STATE), lambda b, c: (b, 0, 0)),
            full((CONV_WIDTH, CONV_DIM)),
            full((1, CONV_DIM)),
            full((1, LANES)),
            full((1, LANES)),
            full((1, D_INNER)),
            full((1, D_INNER)),
            full((LANES, D_INNER)),
        ] + list(rider["in_specs"]),
        out_specs=[
            pl.BlockSpec((q, D_INNER), lambda b, c: (row(b, c), 0)),
            pl.BlockSpec((1, D_INNER, SSM_STATE), lambda b, c: (b, 0, 0)),
        ] + list(rider["out_specs"]),
        out_shape=[
            jax.ShapeDtypeStruct((t, D_INNER), BF16),
            jax.ShapeDtypeStruct((batch, D_INNER, SSM_STATE), F32),
        ] + list(rider["out_shape"]),
        scratch_shapes=[
            pltpu.VMEM((SUBLANES, CONV_DIM), F32),
            pltpu.VMEM((q, CONV_DIM), F32),
            pltpu.VMEM((SSM_GROUPS, SSM_STATE, D_INNER // SSM_GROUPS), F32),
        ],
        compiler_params=pltpu.CompilerParams(
            dimension_semantics=("parallel", "arbitrary"), vmem_limit_bytes=VMEM_LIMIT),
        name="ssm_chunked",
    )(proj, proj, proj, proj, dt_raw, conv_buf, h0, sp["w_conv"], sp["b_conv"], sp["dt_bias"],
      sp["a_log"], sp["d_skip"], sp["w_norm"], sp["expand"], *rider["args"])
    assert len(outs) == n_out + r_out
    return tuple(outs[:n_out]), tuple(outs[n_out:])


def _ssm_step_kernel(z_ref, xs_ref, bm_ref, cm_ref, dt_ref, cbuf_ref, h0_ref, wconv_ref, bconv_ref,
                     dtb_ref, alog_ref, dskip_ref, wnorm_ref, expand_ref, gsum_ref,
                     y_ref, hout_ref, tail_sc):
    q = xs_ref.shape[0]
    gw = D_INNER // SSM_GROUPS
    n = SSM_STATE
    tail_sc[...] = jnp.zeros_like(tail_sc)
    tail_sc[SUBLANES - (CONV_WIDTH - 1):SUBLANES, :] = cbuf_ref[0]
    xs = _conv_silu(xs_ref[...], tail_sc[:, 0:D_INNER], wconv_ref[:, 0:D_INNER], bconv_ref[:, 0:D_INNER])
    o_b, o_c = D_INNER, D_INNER + SSM_GROUPS * n
    bm = _conv_silu(bm_ref[...], tail_sc[:, o_b:o_c], wconv_ref[:, o_b:o_c], bconv_ref[:, o_b:o_c])
    cm = _conv_silu(cm_ref[...], tail_sc[:, o_c:], wconv_ref[:, o_c:], bconv_ref[:, o_c:])
    bm = bm.astype(BF16).astype(F32)
    cm = cm.astype(BF16).astype(F32)

    dt = _softplus(dt_ref[...] + dtb_ref[...])
    a = -jnp.exp(alog_ref[...])
    tri = (_iota((q, q), 0) >= _iota((q, q), 1)).astype(F32)
    d1, d2, d3 = _split3(dt * a)
    acs = _dot(tri, d1.astype(F32)) + _dot(tri, d2.astype(F32)) + _dot(tri, d3.astype(F32))
    a_end = acs[q - 1:q, :]

    row = _iota((q, LANES), 0)
    prods, lms = [], []
    for s in range(q):
        prods.append(cm * bm[s:s + 1, :])
        keep = row >= s
        lms.append(jnp.where(keep, jnp.exp(jnp.where(keep, acs - acs[s:s + 1, :], 0.0)), 0.0))
    qq = q * q
    cb3 = _dot(jnp.concatenate(_split3(jnp.concatenate(prods, axis=0)), axis=0), gsum_ref[...])
    cbs = cb3[0:qq] + cb3[qq:2 * qq] + cb3[2 * qq:3 * qq]
    gmat = (cbs * jnp.concatenate(lms, axis=0)).astype(BF16)
    quant = jnp.concatenate([dt, jnp.exp(acs), jnp.exp(a_end - acs),
                             jnp.broadcast_to(jnp.exp(a_end), (q, LANES))], axis=0)
    nq = 4 * q
    big = _dot(jnp.concatenate(_split3(quant) + (gmat,), axis=0), expand_ref[...])
    qx = big[0:nq] + big[nq:2 * nq] + big[2 * nq:3 * nq]
    dt_x, e_in_x, e_out_x, e_all_x = (qx[i * q:(i + 1) * q] for i in range(4))
    g_x = big[3 * nq:3 * nq + qq]
    xdt = xs * dt_x
    xdt_r = xdt.astype(BF16).astype(F32)
    y = dskip_ref[...] * xs
    for s in range(q):
        y = y + g_x[s * q:(s + 1) * q, :] * xdt_r[s:s + 1, :]
    xdte = xdt * e_out_x

    zpad = jnp.zeros((q, SSM_GROUPS * n), F32)
    cm16 = jnp.concatenate([cm, zpad], axis=0).astype(BF16)
    ones_lo = jnp.where(_iota((2 * q, n), 0) >= q, 1.0, 0.0)
    for g in range(SSM_GROUPS):
        gs = slice(g * gw, (g + 1) * gw)
        h0 = h0_ref[0, gs, :]
        yg = y[:, gs] + _dot_nt(cm16[:, g * n:(g + 1) * n], h0.astype(BF16))[0:q] * e_in_x[:, gs]
        u = yg * jax.nn.silu(z_ref[:, gs])
        u = u * lax.rsqrt(jnp.mean(u * u, axis=-1, keepdims=True) + EPS)
        y_ref[:, gs] = u * wnorm_ref[:, gs]
        e1, e2, e3 = _split3(e_all_x[:, gs])
        srow = _iota((q, gw), 0)
        e_rows = jnp.where(srow == 0, e1.astype(F32),
                           jnp.where(srow == 1, e2.astype(F32), jnp.where(srow == 2, e3.astype(F32), 0.0)))
        lhs = jnp.concatenate([xdte[:, gs].astype(BF16).astype(F32), e_rows], axis=0)
        rhs = jnp.concatenate([jnp.concatenate([bm[:, g * n:(g + 1) * n], zpad[:, 0:n]], axis=0), ones_lo],
                              axis=1)
        both = _dot_tn(lhs, rhs)
        hout_ref[0, gs, :] = both[:, n:2 * n] * h0 + both[:, 0:n]


def _ssm_step(proj, dt_raw, conv_buf, h0, sp, batch, q):
    t = proj.shape[0]
    assert t == batch * q and q == SUBLANES
    col = lambda off, w: off // w
    full = lambda shape: pl.BlockSpec(shape, lambda b: (0,) * len(shape))
    gsum = ((_iota((SSM_GROUPS * SSM_STATE, LANES), 0) // SSM_STATE
             == _iota((SSM_GROUPS * SSM_STATE, LANES), 1) // (SSM_HEADS // SSM_GROUPS))
            & (_iota((SSM_GROUPS * SSM_STATE, LANES), 1) < SSM_HEADS)).astype(BF16)
    return pl.pallas_call(
        _ssm_step_kernel,
        grid=(batch,),
        in_specs=[
            pl.BlockSpec((q, D_INNER), lambda b: (b, col(OFF_Z, D_INNER))),
            pl.BlockSpec((q, D_INNER), lambda b: (b, col(OFF_X, D_INNER))),
            pl.BlockSpec((q, 1024), lambda b: (b, col(OFF_B, 1024))),
            pl.BlockSpec((q, 1024), lambda b: (b, col(OFF_C, 1024))),
            pl.BlockSpec((q, LANES), lambda b: (b, 0)),
            pl.BlockSpec((1, CONV_WIDTH - 1, CONV_DIM), lambda b: (b, 0, 0)),
            pl.BlockSpec((1, D_INNER, SSM_STATE), lambda b: (b, 0, 0)),
            full((CONV_WIDTH, CONV_DIM)),
            full((1, CONV_DIM)),
            full((1, LANES)),
            full((1, LANES)),
            full((1, D_INNER)),
            full((1, D_INNER)),
            full((LANES, D_INNER)),
            full((SSM_GROUPS * SSM_STATE, LANES)),
        ],
        out_specs=[
            pl.BlockSpec((q, D_INNER), lambda b: (b, 0)),
            pl.BlockSpec((1, D_INNER, SSM_STATE), lambda b: (b, 0, 0)),
        ],
        out_shape=[
            jax.ShapeDtypeStruct((t, D_INNER), F32),
            jax.ShapeDtypeStruct((batch, D_INNER, SSM_STATE), F32),
        ],
        scratch_shapes=[pltpu.VMEM((SUBLANES, CONV_DIM), F32)],
        compiler_params=pltpu.CompilerParams(
            dimension_semantics=("parallel",), vmem_limit_bytes=VMEM_LIMIT),
        name="ssm_step",
    )(proj, proj, proj, proj, dt_raw, conv_buf, h0, sp["w_conv"], sp["b_conv"], sp["dt_bias"],
      sp["a_log"], sp["d_skip"], sp["w_norm"], sp["expand"], gsum)


def _prep_params(w_in, norm_mix, w_conv, b_conv, dt_bias, a_log, d_skip, w_ssm_norm, w_q_norm, w_k_norm):
    o_z, o_xbc = 0, D_INNER
    o_dt = o_xbc + CONV_DIM
    o_q = o_dt + SSM_HEADS
    o_k = o_q + ATTN_WIDTH
    o_v = o_k + ATTN_WIDTH
    o_g = o_v + ATTN_WIDTH
    w_main = jnp.concatenate(
        [w_in[:, o_z:o_dt], w_in[:, o_g:], w_in[:, o_q:o_g]], axis=1).astype(BF16)
    w_dt = jnp.pad(w_in[:, o_dt:o_q], ((0, 0), (0, LANES - SSM_HEADS))).astype(BF16)
    qk_w = jnp.ones((N_PROJ_TILES, 1, PROJ_TILE), F32)
    qk_w = qk_w.at[Q_TILE0:K_TILE0, 0].set(jnp.tile(w_q_norm, HEADS_PER_GROUP)[None])
    qk_w = qk_w.at[K_TILE0:V_TILE0, 0].set(jnp.tile(w_k_norm, HEADS_PER_GROUP)[None])
    pad_h = lambda v: jnp.pad(v, (0, LANES - SSM_HEADS))[None]
    expand = (_iota((LANES, D_INNER), 0) == _iota((LANES, D_INNER), 1) // SSM_HEAD_DIM).astype(BF16)
    ssm = dict(w_conv=w_conv, b_conv=b_conv[None], dt_bias=pad_h(dt_bias), a_log=pad_h(a_log),
               d_skip=jnp.repeat(d_skip, SSM_HEAD_DIM)[None], w_norm=w_ssm_norm[None], expand=expand)
    return dict(w_main=w_main, w_dt=w_dt, qk_w=qk_w, norm_mix=norm_mix[None], ssm=ssm)


def _attn_prompt_kernel(q_ref, k_ref, v_ref, o_ref, lse_ref, kp_sc, vp_sc, *, span, dil):
    i = pl.program_id(1)
    blk = ATTN_BLOCK
    pairs = q_ref.shape[1] // LANES

    @pl.when(i == 0)
    def _():
        kp_sc[...] = jnp.zeros_like(kp_sc)
        vp_sc[...] = jnp.zeros_like(vp_sc)

    qi = _iota((blk, 2 * blk), 0)
    kc = _iota((blk, 2 * blk), 1)
    dist = blk + qi - kc
    mask = (dist >= 0) & (dist <= span) & ((kc >= blk) | (i > 0))
    lo_q = _iota((blk, LANES), 1) < ATTN_HEAD_DIM
    lo_kv = _iota((2 * blk, LANES), 1) < ATTN_HEAD_DIM

    def residue(r, carry):
        rows = pl.ds(r, blk, stride=dil) if dil > 1 else slice(None)
        for pr in range(pairs):
            ls = slice(pr * LANES, (pr + 1) * LANES)
            q = q_ref[rows, ls].astype(BF16)
            k2 = jnp.concatenate([kp_sc[rows, ls], k_ref[rows, ls]], axis=0).astype(BF16)
            v2 = jnp.concatenate([vp_sc[rows, ls], v_ref[rows, ls]], axis=0).astype(BF16)
            ps, lses = [], []
            for keep in (lo_q, jnp.logical_not(lo_q)):
                s = jnp.where(mask, _dot_nt(jnp.where(keep, q, 0.0), k2), -jnp.inf)
                m = jnp.max(s, axis=-1, keepdims=True)
                p = jnp.exp(s - m)
                den = jnp.sum(p, axis=-1, keepdims=True)
                ps.append((p / den).astype(BF16))
                lses.append(m + jnp.log(den))
            v_bd = jnp.concatenate([jnp.where(lo_kv, v2, 0.0), jnp.where(lo_kv, 0.0, v2)], axis=0)
            o_ref[rows, ls] = _dot(jnp.concatenate(ps, axis=1), v_bd)
            lse_ref[rows, ls] = jnp.where(lo_q, lses[0], lses[1])
        return carry

    if dil > 1:
        lax.fori_loop(0, dil, residue, 0, unroll=4)
    else:
        residue(0, 0)
    kp_sc[...] = k_ref[...]
    vp_sc[...] = v_ref[...]


def _attn_prompt(proj, gi):
    window, dil = ATTN_PATTERNS[gi]
    s = proj.shape[0]
    chunk = ATTN_BLOCK * dil
    assert s % chunk == 0 and window // dil <= ATTN_BLOCK
    w = LANES if dil > 1 else ATTN_GROUP_WIDTH
    split = ATTN_GROUP_WIDTH // w
    col = lambda tile0: (lambda j, i: (i, (tile0 + gi) * split + j))
    blk = (chunk, w)
    return pl.pallas_call(
        functools.partial(_attn_prompt_kernel, span=window // dil, dil=dil),
        grid=(split, s // chunk),
        in_specs=[pl.BlockSpec(blk, col(Q_TILE0)), pl.BlockSpec(blk, col(K_TILE0)),
                  pl.BlockSpec(blk, col(V_TILE0))],
        out_specs=[pl.BlockSpec(blk, lambda j, i: (i, j))] * 2,
        out_shape=[jax.ShapeDtypeStruct((s, ATTN_GROUP_WIDTH), F32)] * 2,
        scratch_shapes=[pltpu.VMEM(blk, F32), pltpu.VMEM(blk, F32)],
        compiler_params=pltpu.CompilerParams(
            dimension_semantics=("parallel", "arbitrary"), vmem_limit_bytes=VMEM_LIMIT),
        name=f"attn_prompt_g{gi}",
    )(proj, proj, proj)


def _attn_sample_kernel(q_ref, k_ref, v_ref, kc_ref, vc_ref, o_ref, lse_ref, ko_ref, vo_ref,
                        *, span, dil, pos0):
    l = SUBLANES
    nb = kc_ref.shape[0]
    nh, hd = HEADS_PER_GROUP, ATTN_HEAD_DIM
    rows = kc_ref.shape[3]
    w = nh * hd
    t_c = _iota((l, rows), 0)
    delta_c = rows + t_c - _iota((l, rows), 1)
    ok_c = ((delta_c <= span * dil) & ((delta_c & (dil - 1)) == 0) & (pos0 + t_c - delta_c >= 0))
    delta_n = _iota((l, l), 0) - _iota((l, l), 1)
    ok_n = (delta_n >= 0) & (delta_n <= span * dil) & ((delta_n & (dil - 1)) == 0)
    lane = _iota((hd, LANES), 1)
    hl = nh * l
    own = _shr(_iota((hl, w), 0), l) == _shr(_iota((hl, w), 1), hd)
    ok_c = jnp.concatenate([ok_c] * nh, axis=0)
    ok_n = jnp.concatenate([ok_n] * nh, axis=0)
    for bb in range(nb):
        rs = slice(bb * l, (bb + 1) * l)
        q, kn, vn = q_ref[rs, :], k_ref[rs, :], v_ref[rs, :]
        q_bd = jnp.where(own, jnp.concatenate([q] * nh, axis=0), 0.0)
        kt = kc_ref[bb].reshape(w, rows)
        vt = vc_ref[bb].reshape(w, rows)
        s_c = jnp.where(ok_c, _dot(q_bd, kt), -jnp.inf)
        s_n = jnp.where(ok_n, _dot_nt(q_bd, kn), -jnp.inf)
        m = jnp.maximum(jnp.max(s_c, axis=-1, keepdims=True), jnp.max(s_n, axis=-1, keepdims=True))
        p_c = jnp.exp(s_c - m)
        p_n = jnp.exp(s_n - m)
        den = jnp.sum(p_c, axis=-1, keepdims=True) + jnp.sum(p_n, axis=-1, keepdims=True)
        res = jnp.where(own, _dot_nt(p_c / den, vt) + _dot(p_n / den, vn), 0.0)
        lse = jnp.where(own, m + jnp.log(den), 0.0)
        o, lse_x = res[0:l], lse[0:l]
        for h in range(1, nh):
            o = o + res[h * l:(h + 1) * l]
            lse_x = lse_x + lse[h * l:(h + 1) * l]
        o_ref[rs, :] = o
        lse_ref[rs, :] = lse_x

        for new, c_ref, out_ref in ((kn, kc_ref, ko_ref), (vn, vc_ref, vo_ref)):
            padded = jnp.concatenate([jnp.zeros((LANES - l, w), F32), new], axis=0)
            tail = padded.T
            for h in range(nh):
                shifted = pltpu.roll(c_ref[bb, h], rows - l, axis=1)
                if rows > LANES:
                    out_ref[bb, h, :, 0:rows - LANES] = shifted[:, 0:rows - LANES]
                out_ref[bb, h, :, rows - LANES:rows] = jnp.where(
                    lane < LANES - l, shifted[:, rows - LANES:rows], tail[h * hd:(h + 1) * hd])


SAMPLE_CACHE_ROWS_PER_STEP = 2048
SAMPLE_MAX_SEQS_PER_STEP = 8


def _attn_sample_steps(rows, batch):
    nb = max(1, min(batch, SAMPLE_CACHE_ROWS_PER_STEP // rows, SAMPLE_MAX_SEQS_PER_STEP))
    assert batch % nb == 0
    return nb, batch // nb


def _attn_sample_parts(proj, k_cache, v_cache, gi, batch, pos0, step_of):
    window, dil = ATTN_PATTERNS[gi]
    rows = k_cache.shape[1]
    l = proj.shape[0] // batch
    assert l == SUBLANES and rows % LANES == 0
    nb, _ = _attn_sample_steps(rows, batch)
    tok = lambda tile0: pl.BlockSpec((nb * l, ATTN_GROUP_WIDTH), lambda *g: (step_of(*g), tile0 + gi))
    cache = pl.BlockSpec((nb, HEADS_PER_GROUP, ATTN_HEAD_DIM, rows), lambda *g: (step_of(*g), 0, 0, 0))
    out_tok = pl.BlockSpec((nb * l, ATTN_GROUP_WIDTH), lambda *g: (step_of(*g), 0))
    to_t = lambda c: jnp.transpose(c, (0, 2, 3, 1))
    t_shape = jax.ShapeDtypeStruct((batch, HEADS_PER_GROUP, ATTN_HEAD_DIM, rows), F32)
    return dict(
        kernel=functools.partial(_attn_sample_kernel, span=window // dil, dil=dil, pos0=pos0),
        in_specs=[tok(Q_TILE0), tok(K_TILE0), tok(V_TILE0), cache, cache],
        out_specs=[out_tok, out_tok, cache, cache],
        out_shape=[jax.ShapeDtypeStruct((batch * l, ATTN_GROUP_WIDTH), F32)] * 2 + [t_shape] * 2,
        args=(proj, proj, proj, to_t(k_cache), to_t(v_cache)),
    )


def _attn_sample_finish(o, lse, k_t, v_t):
    from_t = lambda c: jnp.transpose(c, (0, 3, 1, 2))
    return o, lse, from_t(k_t), from_t(v_t)


def _attn_sample(proj, k_cache, v_cache, gi, batch, pos0):
    parts = _attn_sample_parts(proj, k_cache, v_cache, gi, batch, pos0, lambda b: b)
    _, steps = _attn_sample_steps(k_cache.shape[1], batch)
    outs = pl.pallas_call(
        parts["kernel"],
        grid=(steps,),
        in_specs=parts["in_specs"],
        out_specs=parts["out_specs"],
        out_shape=parts["out_shape"],
        compiler_params=pltpu.CompilerParams(
            dimension_semantics=("parallel",), vmem_limit_bytes=VMEM_LIMIT),
        name=f"attn_sample_g{gi}",
    )(*parts["args"])
    return _attn_sample_finish(*outs)


def _merge_kernel(x_ref, ys_ref, o0_ref, o1_ref, o2_ref, l0_ref, l1_ref, l2_ref, g0_ref, g1_ref,
                  wbs_ref, wba_ref, wo_ref, nf_ref, wr1_ref, wr2_ref, br_ref,
                  x1_ref, xn_ref, lg_ref):
    l0, l1, l2 = l0_ref[...], l1_ref[...], l2_ref[...]
    m = jnp.maximum(jnp.maximum(l0, l1), l2)
    e0, e1, e2 = jnp.exp(l0 - m), jnp.exp(l1 - m), jnp.exp(l2 - m)
    den = e0 + e1 + e2
    y_attn = (e0 / den) * o0_ref[...] + (e1 / den) * o1_ref[...] + (e2 / den) * o2_ref[...]
    br_s = _dot(ys_ref[...].astype(BF16), wbs_ref[...])
    br_a = _dot(y_attn.astype(BF16), wba_ref[...])
    merged = jax.nn.sigmoid(g0_ref[...]) * br_s + jax.nn.sigmoid(g1_ref[...]) * br_a
    x1 = x_ref[...] + _dot(merged.astype(BF16), wo_ref[...])
    x1_ref[...] = x1
    xn = x1 * lax.rsqrt(jnp.mean(x1 * x1, axis=-1, keepdims=True) + EPS) * nf_ref[...]
    xn1 = xn.astype(BF16)
    xn_ref[...] = xn1
    xn2 = (xn - xn1.astype(F32)).astype(BF16)
    lg_ref[...] = (_dot(xn1, wr1_ref[...]) + _dot(xn2, wr1_ref[...]) + _dot(xn1, wr2_ref[...])
                   + br_ref[...])


def _merge(x, proj, y_ssm, attn, mp, tm):
    t = x.shape[0]
    row = lambda w: pl.BlockSpec((tm, w), lambda i: (i, 0))
    full = lambda a: pl.BlockSpec(a.shape, lambda i: (0,) * a.ndim)
    (o0, l0), (o1, l1), (o2, l2) = attn
    weights = [mp["w_br_ssm"], mp["w_br_attn"], mp["w_out"], mp["norm_ffn"], mp["w_r1"], mp["w_r2"], mp["b_r"]]
    return pl.pallas_call(
        _merge_kernel,
        grid=(t // tm,),
        in_specs=[row(D_MODEL), row(D_INNER)] + [row(ATTN_GROUP_WIDTH)] * 6 + [
            pl.BlockSpec((tm, D_MODEL), lambda i: (i, OFF_G // D_MODEL)),
            pl.BlockSpec((tm, D_MODEL), lambda i: (i, OFF_G // D_MODEL + 1)),
        ] + [full(w) for w in weights],
        out_specs=[row(D_MODEL), row(D_MODEL), row(ROUTER_WIDTH)],
        out_shape=[
            jax.ShapeDtypeStruct((t, D_MODEL), F32),
            jax.ShapeDtypeStruct((t, D_MODEL), BF16),
            jax.ShapeDtypeStruct((t, ROUTER_WIDTH), F32),
        ],
        compiler_params=pltpu.CompilerParams(
            dimension_semantics=("parallel",), vmem_limit_bytes=VMEM_LIMIT),
        name="merge",
    )(x, y_ssm, o0, o1, o2, l0, l1, l2, proj, proj, *weights)


def _prep_merge_params(w_br_ssm, w_br_attn, w_out, norm_ffn, w_router_group, b_router_group,
                       w_router_expert, b_router_expert):
    w_r = jnp.concatenate([w_router_group, w_router_expert.reshape(D_MODEL, N_EXPERTS)], axis=1)
    w_r = jnp.pad(w_r, ((0, 0), (0, ROUTER_WIDTH - w_r.shape[1])))
    w_r1 = w_r.astype(BF16)
    w_r2 = (w_r - w_r1.astype(F32)).astype(BF16)
    b_r = jnp.concatenate([b_router_group, b_router_expert.reshape(N_EXPERTS)])
    b_r = jnp.pad(b_r, (0, ROUTER_WIDTH - b_r.shape[0]))[None]
    return dict(w_br_ssm=w_br_ssm.astype(BF16), w_br_attn=w_br_attn.astype(BF16), w_out=w_out.astype(BF16),
                norm_ffn=norm_ffn[None], w_r1=w_r1, w_r2=w_r2, b_r=b_r)


def _route(logits):
    lane = _iota(logits.shape, 1)
    big = jnp.int32(2 ** 30)
    neg = -jnp.inf
    gl = jnp.where(lane < N_EXPERT_GROUPS, logits, neg)
    gmax = jnp.max(gl, axis=-1, keepdims=True)
    gidx = jnp.min(jnp.where(gl == gmax, lane, big), axis=-1, keepdims=True)
    g_w = 1.0 / jnp.sum(jnp.exp(gl - gmax), axis=-1, keepdims=True)
    lo = N_EXPERT_GROUPS + gidx * EXPERTS_PER_GROUP
    el = jnp.where((lane >= lo) & (lane < lo + EXPERTS_PER_GROUP), logits, neg)
    v1 = jnp.max(el, axis=-1, keepdims=True)
    i1 = jnp.min(jnp.where(el == v1, lane, big), axis=-1, keepdims=True)
    el2 = jnp.where(lane == i1, neg, el)
    v2 = jnp.max(el2, axis=-1, keepdims=True)
    i2 = jnp.min(jnp.where(el2 == v2, lane, big), axis=-1, keepdims=True)
    e2 = jnp.exp(v2 - v1)
    w1 = (1.0 / (1.0 + e2)) * g_w
    w2 = (e2 / (1.0 + e2)) * g_w
    return jnp.where(lane == i1, w1, jnp.where(lane == i2, w2, 0.0)), gidx


MOE_BLOCK = 256
MOE_SMALL_BLOCK = 64


def _moe_kernel(x1_ref, xn_ref, lg_ref, wg_ref, wu_ref, wd_ref, out_ref,
                comb_sc, oh_sc, cum_sc, oht_sc, cumt_sc):
    g = pl.program_id(1)
    tm = x1_ref.shape[0]
    blk = MOE_BLOCK

    @pl.when(g == 0)
    def _():
        comb, gidx = _route(lg_ref[...])
        c1, c2, c3 = _split3(comb)
        comb_sc[0] = c1
        comb_sc[1] = c2
        comb_sc[2] = c3
        onehot = (_iota((tm, LANES), 1) == gidx).astype(F32)
        oh_bf = onehot.astype(BF16)
        lower = (_iota((tm, tm), 0) > _iota((tm, tm), 1)).astype(BF16)
        upper = (_iota((tm, tm), 0) < _iota((tm, tm), 1)).astype(BF16)
        oh_sc[...] = onehot
        cum_sc[...] = _dot(lower, oh_bf)
        oht_sc[...] = onehot.T[0:SUBLANES]
        cumt_sc[...] = _dot_tn(oh_bf, upper)[0:SUBLANES]
        out_ref[...] = x1_ref[...]

    lane = _iota((tm, LANES), 1)
    pick = lane == g
    member_c = jnp.sum(jnp.where(pick, oh_sc[...], 0.0), axis=-1, keepdims=True)
    rank_c = jnp.sum(jnp.where(pick, cum_sc[...], 0.0), axis=-1, keepdims=True)
    member_r = oht_sc[pl.ds(g, 1), :]
    rank_r = cumt_sc[pl.ds(g, 1), :]
    count = jnp.sum(member_r).astype(jnp.int32)
    small = MOE_SMALL_BLOCK
    nfull = count // blk
    rem = count - nfull * blk
    use_small = rem <= blk // 2
    n_big = nfull + jnp.where(use_small, 0, 1)
    n_small = jnp.where(use_small, (rem + (small - 1)) // small, 0)
    xn = xn_ref[...]
    wd = wd_ref[...].reshape(EXPERTS_PER_GROUP * D_EXPERT, D_MODEL)

    def run_block(first_row, blk):
        lane_b = _iota((blk, LANES), 1)
        base = first_row.astype(F32)
        want_r = base + _iota((blk, tm), 0).astype(F32)
        sel = jnp.where((rank_r == want_r) & (member_r > 0.0), 1.0, 0.0).astype(BF16)
        want_c = base + _iota((tm, blk), 1).astype(F32)
        sel_t = jnp.where((rank_c == want_c) & (member_c > 0.0), 1.0, 0.0).astype(BF16)
        xg = _dot(sel, xn).astype(BF16)
        cg = _dot(sel, comb_sc[0]) + _dot(sel, comb_sc[1]) + _dot(sel, comb_sc[2])
        hidden = []
        for e in range(EXPERTS_PER_GROUP):
            col = N_EXPERT_GROUPS + g * EXPERTS_PER_GROUP + e
            cw = jnp.sum(jnp.where(lane_b == col, cg, 0.0), axis=-1, keepdims=True)
            h = jax.nn.silu(_dot(xg, wg_ref[e])) * _dot(xg, wu_ref[e]) * cw
            hidden.append(h.astype(BF16))
        out = _dot(jnp.concatenate(hidden, axis=1), wd)
        o1 = out.astype(BF16)
        o2 = (out - o1.astype(F32)).astype(BF16)
        out_ref[...] += _dot(sel_t, o1) + _dot(sel_t, o2)

    def big(k, carry):
        run_block(k * blk, blk)
        return carry

    def little(k, carry):
        run_block(nfull * blk + k * small, small)
        return carry

    lax.fori_loop(0, n_big, big, 0)
    lax.fori_loop(0, n_small, little, 0)


def _moe(x1, xn, logits, wg, wu, wd, tm):
    t = x1.shape[0]
    ge = EXPERTS_PER_GROUP
    return pl.pallas_call(
        _moe_kernel,
        grid=(t // tm, N_EXPERT_GROUPS),
        in_specs=[
            pl.BlockSpec((tm, D_MODEL), lambda i, g: (i, 0), pipeline_mode=pl.Buffered(1)),
            pl.BlockSpec((tm, D_MODEL), lambda i, g: (i, 0)),
            pl.BlockSpec((tm, ROUTER_WIDTH), lambda i, g: (i, 0), pipeline_mode=pl.Buffered(1)),
            pl.BlockSpec((ge, D_MODEL, D_EXPERT), lambda i, g: (g, 0, 0)),
            pl.BlockSpec((ge, D_MODEL, D_EXPERT), lambda i, g: (g, 0, 0)),
            pl.BlockSpec((ge, D_EXPERT, D_MODEL), lambda i, g: (g, 0, 0)),
        ],
        out_specs=pl.BlockSpec((tm, D_MODEL), lambda i, g: (i, 0)),
        out_shape=jax.ShapeDtypeStruct((t, D_MODEL), F32),
        scratch_shapes=[
            pltpu.VMEM((3, tm, ROUTER_WIDTH), BF16),
            pltpu.VMEM((tm, LANES), F32),
            pltpu.VMEM((tm, LANES), F32),
            pltpu.VMEM((SUBLANES, tm), F32),
            pltpu.VMEM((SUBLANES, tm), F32),
        ],
        compiler_params=pltpu.CompilerParams(
            dimension_semantics=("parallel", "arbitrary"), vmem_limit_bytes=VMEM_LIMIT),
        name="moe",
    )(x1, xn, logits, wg, wu, wd)


def _moe_dense_kernel(x1_ref, xn_ref, lg_ref, wg_ref, wu_ref, wd_ref, out_ref, comb_sc, acc_sc):
    e = pl.program_id(1)

    @pl.when(e == 0)
    def _():
        comb_sc[...] = _route(lg_ref[...])[0]
        acc_sc[...] = jnp.zeros_like(acc_sc)

    lane = _iota(comb_sc.shape, 1)
    cw = jnp.sum(jnp.where(lane == e + N_EXPERT_GROUPS, comb_sc[...], 0.0), axis=-1, keepdims=True)
    xn = xn_ref[...]
    hidden = jax.nn.silu(_dot(xn, wg_ref[0])) * _dot(xn, wu_ref[0]) * cw
    acc_sc[...] += _dot(hidden.astype(BF16), wd_ref[0])

    @pl.when(e == pl.num_programs(1) - 1)
    def _():
        out_ref[...] = x1_ref[...] + acc_sc[...]


def _moe_dense(x1, xn, logits, wg, wu, wd, tm):
    t = x1.shape[0]
    return pl.pallas_call(
        _moe_dense_kernel,
        grid=(t // tm, N_EXPERTS),
        in_specs=[
            pl.BlockSpec((tm, D_MODEL), lambda i, e: (i, 0)),
            pl.BlockSpec((tm, D_MODEL), lambda i, e: (i, 0)),
            pl.BlockSpec((tm, ROUTER_WIDTH), lambda i, e: (i, 0)),
            pl.BlockSpec((1, D_MODEL, D_EXPERT), lambda i, e: (e, 0, 0)),
            pl.BlockSpec((1, D_MODEL, D_EXPERT), lambda i, e: (e, 0, 0)),
            pl.BlockSpec((1, D_EXPERT, D_MODEL), lambda i, e: (e, 0, 0)),
        ],
        out_specs=pl.BlockSpec((tm, D_MODEL), lambda i, e: (i, 0)),
        out_shape=jax.ShapeDtypeStruct((t, D_MODEL), F32),
        scratch_shapes=[pltpu.VMEM((tm, ROUTER_WIDTH), F32), pltpu.VMEM((tm, D_MODEL), F32)],
        compiler_params=pltpu.CompilerParams(
            dimension_semantics=("parallel", "arbitrary"), vmem_limit_bytes=VMEM_LIMIT),
        name="moe_dense",
    )(x1, xn, logits, wg, wu, wd)


def _tile_rows(t, cap):
    tm = min(t, cap)
    assert t % tm == 0
    return tm


def _finish_layer(x2, proj, y_ssm, attn, h_new, new_kv, b, l, mp, ep):
    t, d = x2.shape
    new_conv = proj.reshape(b, l, PROJ_WIDTH)[:, l - (CONV_WIDTH - 1):, OFF_X:OFF_X + CONV_DIM]
    x1, xn, logits = _merge(x2, proj, y_ssm, attn, mp, _tile_rows(t, 512))
    y = _moe(x1, xn, logits, ep["wg"], ep["wu"], ep["wd"], _tile_rows(t, 1024))
    new_ssm = h_new.reshape(b, SSM_HEADS, SSM_HEAD_DIM, SSM_STATE)
    return y.reshape(b, l, d), (new_conv, new_ssm) + tuple(new_kv)


def _layer_pair(x_prompt, x_sample, conv_s, ssm_s, kv_s, pos0, pp, mp, ep):
    bp, lp, d = x_prompt.shape
    bs, ls, _ = x_sample.shape
    assert bp == 1 and lp % SSD_CHUNK == 0
    tp, ts = bp * lp, bs * ls
    xp2, xs2 = x_prompt.reshape(tp, d), x_sample.reshape(ts, d)
    proj_p, dt_p = _in_proj(xp2, pp["norm_mix"], pp["w_main"], pp["w_dt"], pp["qk_w"], _tile_rows(tp, 2048))
    proj_s, dt_s = _in_proj(xs2, pp["norm_mix"], pp["w_main"], pp["w_dt"], pp["qk_w"], _tile_rows(ts, 2048))

    nc = lp // SSD_CHUNK
    rider_gi = next((gi for gi in reversed(range(N_ATTN_GROUPS))
                     if _attn_sample_steps(kv_s[2 * gi].shape[1], bs)[1] == nc), None)
    rider = None
    if rider_gi is not None:
        rider = _attn_sample_parts(proj_s, kv_s[2 * rider_gi], kv_s[2 * rider_gi + 1], rider_gi, bs, pos0,
                                   lambda b, c: c)
    zero_conv = jnp.zeros((bp, CONV_WIDTH - 1, CONV_DIM), F32)
    zero_h0 = jnp.zeros((bp, D_INNER, SSM_STATE), F32)
    (y_ssm_p, h_p), rider_out = _ssm_chunked(proj_p, dt_p, zero_conv, zero_h0, pp["ssm"], bp, SSD_CHUNK, rider)

    attn_p = [_attn_prompt(proj_p, gi) for gi in range(N_ATTN_GROUPS)]
    proj_p3 = proj_p.reshape(bp, lp, PROJ_WIDTH)
    kv_p = []
    for gi, (window, _) in enumerate(ATTN_PATTERNS):
        keep = min(window, lp)
        for off in (OFF_K, OFF_V):
            c0 = off + gi * ATTN_GROUP_WIDTH
            kv_p.append(proj_p3[:, lp - keep:, c0:c0 + ATTN_GROUP_WIDTH]
                        .reshape(bp, keep, HEADS_PER_GROUP, ATTN_HEAD_DIM))

    y_ssm_s, h_s = _ssm_step(proj_s, dt_s, conv_s, ssm_s.reshape(bs, D_INNER, SSM_STATE), pp["ssm"], bs, ls)
    attn_s, kv_new = [], []
    for gi in range(N_ATTN_GROUPS):
        if gi == rider_gi:
            o, lse, kn, vn = _attn_sample_finish(*rider_out)
        else:
            o, lse, kn, vn = _attn_sample(proj_s, kv_s[2 * gi], kv_s[2 * gi + 1], gi, bs, pos0)
        attn_s.append((o, lse))
        kv_new += [kn, vn]

    yp, st_p = _finish_layer(xp2, proj_p, y_ssm_p, attn_p, h_p, kv_p, bp, lp, mp, ep)
    ys, st_s = _finish_layer(xs2, proj_s, y_ssm_s, attn_s, h_s, kv_new, bs, ls, mp, ep)
    return yp, st_p, ys, st_s


def kernel(x_prompt, x_sample, cache_conv, state_ssm, cache_k_w128, cache_v_w128, cache_k_w512, cache_v_w512,
           cache_k_w2048, cache_v_w2048, w_in, norm_mix, w_conv, b_conv, dt_bias, a_log, d_skip, w_ssm_norm,
           w_q_norm, w_k_norm, w_br_ssm, w_br_attn, w_out, norm_ffn, w_router_group, b_router_group,
           w_router_expert, b_router_expert, w_expert_gate, w_expert_up, w_expert_down):
    depth = w_in.shape[0]
    yp, ys = x_prompt, x_sample
    prompt_states, sample_states = [], []
    for i in range(depth):
        pp = _prep_params(w_in[i], norm_mix[i], w_conv[i], b_conv[i], dt_bias[i], a_log[i], d_skip[i],
                          w_ssm_norm[i], w_q_norm[i], w_k_norm[i])
        mp = _prep_merge_params(w_br_ssm[i], w_br_attn[i], w_out[i], norm_ffn[i], w_router_group[i],
                                b_router_group[i], w_router_expert[i], b_router_expert[i])
        ep = dict(wg=w_expert_gate[i].astype(BF16), wu=w_expert_up[i].astype(BF16),
                  wd=w_expert_down[i].astype(BF16))
        kv_bufs = (cache_k_w128[i], cache_v_w128[i], cache_k_w512[i], cache_v_w512[i],
                   cache_k_w2048[i], cache_v_w2048[i])
        yp, st_p, ys, st_s = _layer_pair(yp, ys, cache_conv[i], state_ssm[i], kv_bufs, PAST_LEN, pp, mp, ep)
        prompt_states.append(st_p)
        sample_states.append(st_s)
    stack = lambda states: [jnp.stack([s[j] for s in states]) for j in range(8)]
    return (yp, ys) + tuple(stack(prompt_states)) + tuple(stack(sample_states))
```

```python
import functools
import math

import jax
import jax.numpy as jnp
from jax import lax
from jax.experimental import pallas as pl
from jax.experimental.pallas import tpu as pltpu

F32 = jnp.float32
BF16 = jnp.bfloat16

PAST_LEN = 8192
D_MODEL = 1024
D_INNER = 2048
SSM_HEAD_DIM = 64
SSM_HEADS = 32
SSM_GROUPS = 8
SSM_STATE = 128
CONV_WIDTH = 4
CONV_DIM = D_INNER + 2 * SSM_GROUPS * SSM_STATE
SSD_CHUNK = 128
ATTN_HEAD_DIM = 64
HEADS_PER_GROUP = 8
ATTN_PATTERNS = ((128, 1), (512, 4), (2048, 16))
N_ATTN_GROUPS = 3
ATTN_GROUP_WIDTH = HEADS_PER_GROUP * ATTN_HEAD_DIM
ATTN_WIDTH = N_ATTN_GROUPS * ATTN_GROUP_WIDTH
ATTN_BLOCK = 128
N_EXPERT_GROUPS = 4
EXPERTS_PER_GROUP = 8
N_EXPERTS = 32
D_EXPERT = 256
EPS = 1e-6

LANES = 128
SUBLANES = 8
VMEM_LIMIT = 56 * 1024 * 1024

PROJ_TILE = 512
OFF_Z = 0
OFF_X = 2048
OFF_B = 4096
OFF_C = 5120
OFF_G = 6144
OFF_Q = 8192
OFF_K = OFF_Q + ATTN_WIDTH
OFF_V = OFF_K + ATTN_WIDTH
PROJ_WIDTH = OFF_V + ATTN_WIDTH
N_PROJ_TILES = PROJ_WIDTH // PROJ_TILE
Q_TILE0 = OFF_Q // PROJ_TILE
K_TILE0 = OFF_K // PROJ_TILE
V_TILE0 = OFF_V // PROJ_TILE
ROUTER_WIDTH = LANES


def _split3(x):
    x1 = x.astype(BF16)
    r1 = x - x1.astype(F32)
    x2 = r1.astype(BF16)
    r2 = r1 - x2.astype(F32)
    return x1, x2, r2.astype(BF16)


def _dot(a, b):
    return jnp.dot(a, b, preferred_element_type=F32)


def _dot3(x, m):
    x1, x2, x3 = _split3(x)
    return _dot(x1, m) + _dot(x2, m) + _dot(x3, m)


def _dot3f(x, m):
    x1, x2, x3 = _split3(x)
    return _dot(x1.astype(F32), m) + _dot(x2.astype(F32), m) + _dot(x3.astype(F32), m)


def _dot_nt(a, b):
    return lax.dot_general(a, b, (((1,), (1,)), ((), ())), preferred_element_type=F32)


def _dot_tn(a, b):
    return lax.dot_general(a, b, (((0,), (0,)), ((), ())), preferred_element_type=F32)


def _iota(shape, dim):
    return lax.broadcasted_iota(jnp.int32, shape, dim)


def _shr(x, pow2):
    assert pow2 & (pow2 - 1) == 0
    return lax.shift_right_logical(x, jnp.int32(int(math.log2(pow2))))


def _in_proj_kernel(x_ref, nw_ref, w_ref, wdt_ref, qkw_ref, seg_ref, out_ref, dt_ref, h_sc):
    j = pl.program_id(1)

    @pl.when(j == 0)
    def _():
        xf = x_ref[...]
        y = xf * lax.rsqrt(jnp.mean(xf * xf, axis=-1, keepdims=True) + EPS)
        h = (y * nw_ref[...]).astype(BF16)
        h_sc[...] = h
        dt_ref[...] = _dot(h, wdt_ref[...])

    acc = _dot(h_sc[...], w_ref[...])
    is_qk = jnp.logical_and(j >= Q_TILE0, j < V_TILE0)

    @pl.when(jnp.logical_not(is_qk))
    def _():
        out_ref[...] = acc

    @pl.when(is_qk)
    def _():
        sq = acc * acc
        s1 = sq.astype(BF16)
        s2 = (sq - s1.astype(F32)).astype(BF16)
        ms = (_dot(s1, seg_ref[...]) + _dot(s2, seg_ref[...])) * (1.0 / ATTN_HEAD_DIM)
        scale = jnp.where(j < K_TILE0, ATTN_HEAD_DIM ** -0.5, 1.0).astype(F32)
        w = qkw_ref[0] * scale
        out_ref[...] = acc * lax.rsqrt(ms + EPS) * w


def _in_proj(x, norm_w, w_bf, wdt_bf, qk_w, tm):
    t = x.shape[0]
    seg = (_iota((PROJ_TILE, PROJ_TILE), 0) // ATTN_HEAD_DIM
           == _iota((PROJ_TILE, PROJ_TILE), 1) // ATTN_HEAD_DIM).astype(BF16)
    return pl.pallas_call(
        _in_proj_kernel,
        grid=(t // tm, N_PROJ_TILES),
        in_specs=[
            pl.BlockSpec((tm, D_MODEL), lambda i, j: (i, 0)),
            pl.BlockSpec((1, D_MODEL), lambda i, j: (0, 0)),
            pl.BlockSpec((D_MODEL, PROJ_TILE), lambda i, j: (0, j)),
            pl.BlockSpec((D_MODEL, LANES), lambda i, j: (0, 0)),
            pl.BlockSpec((1, 1, PROJ_TILE), lambda i, j: (j, 0, 0)),
            pl.BlockSpec((PROJ_TILE, PROJ_TILE), lambda i, j: (0, 0)),
        ],
        out_specs=[
            pl.BlockSpec((tm, PROJ_TILE), lambda i, j: (i, j)),
            pl.BlockSpec((tm, LANES), lambda i, j: (i, 0)),
        ],
        out_shape=[
            jax.ShapeDtypeStruct((t, PROJ_WIDTH), F32),
            jax.ShapeDtypeStruct((t, LANES), F32),
        ],
        scratch_shapes=[pltpu.VMEM((tm, D_MODEL), BF16)],
        compiler_params=pltpu.CompilerParams(
            dimension_semantics=("parallel", "arbitrary"), vmem_limit_bytes=VMEM_LIMIT),
        name="in_proj",
    )(x, norm_w, w_bf, wdt_bf, qk_w, seg)


def _softplus(x):
    return jnp.maximum(x, 0.0) + jnp.log1p(jnp.exp(-jnp.abs(x)))


def _conv_silu(raw, tail, w, b):
    q = raw.shape[0]
    up = jnp.concatenate([tail, raw], axis=0)
    acc = b + w[CONV_WIDTH - 1:CONV_WIDTH] * raw
    for tap in range(CONV_WIDTH - 1):
        shifted = pltpu.roll(up, CONV_WIDTH - 1 - tap, axis=0)[SUBLANES:SUBLANES + q]
        acc = acc + w[tap:tap + 1] * shifted
    return jax.nn.silu(acc)


def _ssm_chunk_kernel(z_ref, xs_ref, bm_ref, cm_ref, dt_ref, cbuf_ref, h0_ref, wconv_ref, bconv_ref,
                      dtb_ref, alog_ref, dskip_ref, wnorm_ref, expand_ref,
                      y_ref, hout_ref, tail_sc, xbc_sc, ht_sc):
    c = pl.program_id(1)
    nc = pl.num_programs(1)
    q = xs_ref.shape[0]
    gw = D_INNER // SSM_GROUPS
    n = SSM_STATE

    @pl.when(c == 0)
    def _():
        tail_sc[...] = jnp.zeros_like(tail_sc)
        tail_sc[SUBLANES - (CONV_WIDTH - 1):SUBLANES, :] = cbuf_ref[0]
        for g in range(SSM_GROUPS):
            ht_sc[g] = h0_ref[0, g * gw:(g + 1) * gw, :].T

    cw = PROJ_TILE
    for k in range(CONV_DIM // cw):
        if k < D_INNER // cw:
            raw = xs_ref[:, k * cw:(k + 1) * cw]
        elif k < (D_INNER + SSM_GROUPS * n) // cw:
            kk = k - D_INNER // cw
            raw = bm_ref[:, kk * cw:(kk + 1) * cw]
        else:
            kk = k - (D_INNER + SSM_GROUPS * n) // cw
            raw = cm_ref[:, kk * cw:(kk + 1) * cw]
        sl = slice(k * cw, (k + 1) * cw)
        xbc_sc[:, sl] = _conv_silu(raw, tail_sc[:, sl], wconv_ref[:, sl], bconv_ref[:, sl])
        tail_sc[:, sl] = raw[q - SUBLANES:q]

    dt = _softplus(dt_ref[...] + dtb_ref[...])
    a = -jnp.exp(alog_ref[...])
    dta = dt * a
    tri = (_iota((q, q), 0) >= _iota((q, q), 1)).astype(BF16)
    d1, d2, d3 = _split3(dta)
    acs = _dot(tri, d1) + _dot(tri, d2) + _dot(tri, d3)
    eye = (_iota((LANES, LANES), 0) == _iota((LANES, LANES), 1)).astype(BF16)
    c1, c2, c3 = _split3(acs)
    acs_t = _dot_nt(eye, c1) + _dot_nt(eye, c2) + _dot_nt(eye, c3)
    a_end = acs[q - 1:q, :]
    e_in = jnp.exp(acs)
    e_out = jnp.exp(a_end - acs)
    e_all = jnp.exp(a_end)

    ex = expand_ref[...]
    dt_x = _dot3(dt, ex)
    e_in_x = _dot3(e_in, ex)
    e_out_x = _dot3(e_out, ex)
    e_all_x = _dot3(jnp.broadcast_to(e_all, (2 * SUBLANES, LANES)), ex)[0:1]

    causal = _iota((q, q), 0) >= _iota((q, q), 1)
    lane_lo = _iota((q, LANES), 1) < SSM_HEAD_DIM

    for g in range(SSM_GROUPS):
        gs = slice(g * gw, (g + 1) * gw)
        xg = xbc_sc[:, gs]
        bg = xbc_sc[:, D_INNER + g * n:D_INNER + (g + 1) * n].astype(BF16)
        cg = xbc_sc[:, D_INNER + SSM_GROUPS * n + g * n:D_INNER + SSM_GROUPS * n + (g + 1) * n].astype(BF16)
        xdt = xg * dt_x[:, gs]
        cb = _dot_nt(cg, bg)
        ht = ht_sc[g]
        y_off = _dot(cg, ht.astype(BF16)) * e_in_x[:, gs]
        pieces = []
        for pair in range(2):
            ms = []
            for hh in range(2):
                h = g * 4 + pair * 2 + hh
                diff = acs[:, h:h + 1] - acs_t[h:h + 1, :]
                lmat = jnp.where(causal, jnp.exp(jnp.where(causal, diff, 0.0)), 0.0)
                ms.append((cb * lmat).astype(BF16))
            xp = xdt[:, pair * LANES:(pair + 1) * LANES]
            rhs = jnp.concatenate([jnp.where(lane_lo, xp, 0.0), jnp.where(lane_lo, 0.0, xp)],
                                  axis=0).astype(BF16)
            pieces.append(_dot(jnp.concatenate(ms, axis=1), rhs))
        y_diag = jnp.concatenate(pieces, axis=1)
        yg = y_diag + y_off + dskip_ref[:, gs] * xg
        zg = z_ref[:, gs]
        u = yg * jax.nn.silu(zg)
        u = u * lax.rsqrt(jnp.mean(u * u, axis=-1, keepdims=True) + EPS)
        y_ref[:, gs] = (u * wnorm_ref[:, gs]).astype(y_ref.dtype)
        st = _dot_tn(bg, (xdt * e_out_x[:, gs]).astype(BF16))
        ht_sc[g] = e_all_x[:, gs] * ht + st

    @pl.when(c == nc - 1)
    def _():
        for g in range(SSM_GROUPS):
            hout_ref[0, g * gw:(g + 1) * gw, :] = ht_sc[g].T


def _ssm_chunked(proj, dt_raw, conv_buf, h0, sp, batch, q, rider=None):
    t = proj.shape[0]
    nc = t // batch // q
    row = lambda b, c: b * nc + c
    col = lambda off, w: off // w
    full = lambda shape: pl.BlockSpec(shape, lambda b, c: (0,) * len(shape))
    n_in, n_out = 14, 2
    rider = rider or dict(kernel=None, in_specs=[], out_specs=[], out_shape=[], args=())
    r_in, r_out = len(rider["in_specs"]), len(rider["out_specs"])

    def body(*refs):
        ins, refs = refs[:n_in], refs[n_in:]
        r_ins, refs = refs[:r_in], refs[r_in:]
        outs, refs = refs[:n_out], refs[n_out:]
        r_outs, scratch = refs[:r_out], refs[r_out:]
        _ssm_chunk_kernel(*ins, *outs, *scratch)
        if rider["kernel"] is not None:
            rider["kernel"](*r_ins, *r_outs)

    outs = pl.pallas_call(
        body,
        grid=(batch, nc),
        in_specs=[
            pl.BlockSpec((q, D_INNER), lambda b, c: (row(b, c), col(OFF_Z, D_INNER))),
            pl.BlockSpec((q, D_INNER), lambda b, c: (row(b, c), col(OFF_X, D_INNER))),
            pl.BlockSpec((q, 1024), lambda b, c: (row(b, c), col(OFF_B, 1024))),
            pl.BlockSpec((q, 1024), lambda b, c: (row(b, c), col(OFF_C, 1024))),
            pl.BlockSpec((q, LANES), lambda b, c: (row(b, c), 0)),
            pl.BlockSpec((1, CONV_WIDTH - 1, CONV_DIM), lambda b, c: (b, 0, 0)),
            pl.BlockSpec((1, D_INNER, SSM_STATE), lambda b, c: (b, 0, 0)),
            full((CONV_WIDTH, CONV_DIM)),
            full((1, CONV_DIM)),
            full((1, LANES)),
            full((1, LANES)),
            full((1, D_INNER)),
            full((1, D_INNER)),
            full((LANES, D_INNER)),
        ] + list(rider["in_specs"]),
        out_specs=[
            pl.BlockSpec((q, D_INNER), lambda b, c: (row(b, c), 0)),
            pl.BlockSpec((1, D_INNER, SSM_STATE), lambda b, c: (b, 0, 0)),
        ] + list(rider["out_specs"]),
        out_shape=[
            jax.ShapeDtypeStruct((t, D_INNER), BF16),
            jax.ShapeDtypeStruct((batch, D_INNER, SSM_STATE), F32),
        ] + list(rider["out_shape"]),
        scratch_shapes=[
            pltpu.VMEM((SUBLANES, CONV_DIM), F32),
            pltpu.VMEM((q, CONV_DIM), F32),
            pltpu.VMEM((SSM_GROUPS, SSM_STATE, D_INNER // SSM_GROUPS), F32),
        ],
        compiler_params=pltpu.CompilerParams(
            dimension_semantics=("parallel", "arbitrary"), vmem_limit_bytes=VMEM_LIMIT),
        name="ssm_chunked",
    )(proj, proj, proj, proj, dt_raw, conv_buf, h0, sp["w_conv"], sp["b_conv"], sp["dt_bias"],
      sp["a_log"], sp["d_skip"], sp["w_norm"], sp["expand"], *rider["args"])
    assert len(outs) == n_out + r_out
    return tuple(outs[:n_out]), tuple(outs[n_out:])


def _ssm_step_kernel(z_ref, xs_ref, bm_ref, cm_ref, dt_ref, cbuf_ref, h0_ref, wconv_ref, bconv_ref,
                     dtb_ref, alog_ref, dskip_ref, wnorm_ref, expand_ref, gsum_ref,
                     y_ref, hout_ref, tail_sc):
    q = xs_ref.shape[0]
    gw = D_INNER // SSM_GROUPS
    n = SSM_STATE
    tail_sc[...] = jnp.zeros_like(tail_sc)
    tail_sc[SUBLANES - (CONV_WIDTH - 1):SUBLANES, :] = cbuf_ref[0]
    xs = _conv_silu(xs_ref[...], tail_sc[:, 0:D_INNER], wconv_ref[:, 0:D_INNER], bconv_ref[:, 0:D_INNER])
    o_b, o_c = D_INNER, D_INNER + SSM_GROUPS * n
    bm = _conv_silu(bm_ref[...], tail_sc[:, o_b:o_c], wconv_ref[:, o_b:o_c], bconv_ref[:, o_b:o_c])
    cm = _conv_silu(cm_ref[...], tail_sc[:, o_c:], wconv_ref[:, o_c:], bconv_ref[:, o_c:])
    bm = bm.astype(BF16).astype(F32)
    cm = cm.astype(BF16).astype(F32)

    dt = _softplus(dt_ref[...] + dtb_ref[...])
    a = -jnp.exp(alog_ref[...])
    tri = (_iota((q, q), 0) >= _iota((q, q), 1)).astype(F32)
    d1, d2, d3 = _split3(dt * a)
    acs = _dot(tri, d1.astype(F32)) + _dot(tri, d2.astype(F32)) + _dot(tri, d3.astype(F32))
    a_end = acs[q - 1:q, :]

    row = _iota((q, LANES), 0)
    prods, lms = [], []
    for s in range(q):
        prods.append(cm * bm[s:s + 1, :])
        keep = row >= s
        lms.append(jnp.where(keep, jnp.exp(jnp.where(keep, acs - acs[s:s + 1, :], 0.0)), 0.0))
    qq = q * q
    cb3 = _dot(jnp.concatenate(_split3(jnp.concatenate(prods, axis=0)), axis=0), gsum_ref[...])
    cbs = cb3[0:qq] + cb3[qq:2 * qq] + cb3[2 * qq:3 * qq]
    gmat = (cbs * jnp.concatenate(lms, axis=0)).astype(BF16)
    quant = jnp.concatenate([dt, jnp.exp(acs), jnp.exp(a_end - acs),
                             jnp.broadcast_to(jnp.exp(a_end), (q, LANES))], axis=0)
    nq = 4 * q
    big = _dot(jnp.concatenate(_split3(quant) + (gmat,), axis=0), expand_ref[...])
    qx = big[0:nq] + big[nq:2 * nq] + big[2 * nq:3 * nq]
    dt_x, e_in_x, e_out_x, e_all_x = (qx[i * q:(i + 1) * q] for i in range(4))
    g_x = big[3 * nq:3 * nq + qq]
    xdt = xs * dt_x
    xdt_r = xdt.astype(BF16).astype(F32)
    y = dskip_ref[...] * xs
    for s in range(q):
        y = y + g_x[s * q:(s + 1) * q, :] * xdt_r[s:s + 1, :]
    xdte = xdt * e_out_x

    zpad = jnp.zeros((q, SSM_GROUPS * n), F32)
    cm16 = jnp.concatenate([cm, zpad], axis=0).astype(BF16)
    ones_lo = jnp.where(_iota((2 * q, n), 0) >= q, 1.0, 0.0)
    for g in range(SSM_GROUPS):
        gs = slice(g * gw, (g + 1) * gw)
        h0 = h0_ref[0, gs, :]
        yg = y[:, gs] + _dot_nt(cm16[:, g * n:(g + 1) * n], h0.astype(BF16))[0:q] * e_in_x[:, gs]
        u = yg * jax.nn.silu(z_ref[:, gs])
        u = u * lax.rsqrt(jnp.mean(u * u, axis=-1, keepdims=True) + EPS)
        y_ref[:, gs] = u * wnorm_ref[:, gs]
        e1, e2, e3 = _split3(e_all_x[:, gs])
        srow = _iota((q, gw), 0)
        e_rows = jnp.where(srow == 0, e1.astype(F32),
                           jnp.where(srow == 1, e2.astype(F32), jnp.where(srow == 2, e3.astype(F32), 0.0)))
        lhs = jnp.concatenate([xdte[:, gs].astype(BF16).astype(F32), e_rows], axis=0)
        rhs = jnp.concatenate([jnp.concatenate([bm[:, g * n:(g + 1) * n], zpad[:, 0:n]], axis=0), ones_lo],
                              axis=1)
        both = _dot_tn(lhs, rhs)
        hout_ref[0, gs, :] = both[:, n:2 * n] * h0 + both[:, 0:n]


def _ssm_step(proj, dt_raw, conv_buf, h0, sp, batch, q):
    t = proj.shape[0]
    assert t == batch * q and q == SUBLANES
    col = lambda off, w: off // w
    full = lambda shape: pl.BlockSpec(shape, lambda b: (0,) * len(shape))
    gsum = ((_iota((SSM_GROUPS * SSM_STATE, LANES), 0) // SSM_STATE
             == _iota((SSM_GROUPS * SSM_STATE, LANES), 1) // (SSM_HEADS // SSM_GROUPS))
            & (_iota((SSM_GROUPS * SSM_STATE, LANES), 1) < SSM_HEADS)).astype(BF16)
    return pl.pallas_call(
        _ssm_step_kernel,
        grid=(batch,),
        in_specs=[
            pl.BlockSpec((q, D_INNER), lambda b: (b, col(OFF_Z, D_INNER))),
            pl.BlockSpec((q, D_INNER), lambda b: (b, col(OFF_X, D_INNER))),
            pl.BlockSpec((q, 1024), lambda b: (b, col(OFF_B, 1024))),
            pl.BlockSpec((q, 1024), lambda b: (b, col(OFF_C, 1024))),
            pl.BlockSpec((q, LANES), lambda b: (b, 0)),
            pl.BlockSpec((1, CONV_WIDTH - 1, CONV_DIM), lambda b: (b, 0, 0)),
            pl.BlockSpec((1, D_INNER, SSM_STATE), lambda b: (b, 0, 0)),
            full((CONV_WIDTH, CONV_DIM)),
            full((1, CONV_DIM)),
            full((1, LANES)),
            full((1, LANES)),
            full((1, D_INNER)),
            full((1, D_INNER)),
            full((LANES, D_INNER)),
            full((SSM_GROUPS * SSM_STATE, LANES)),
        ],
        out_specs=[
            pl.BlockSpec((q, D_INNER), lambda b: (b, 0)),
            pl.BlockSpec((1, D_INNER, SSM_STATE), lambda b: (b, 0, 0)),
        ],
        out_shape=[
            jax.ShapeDtypeStruct((t, D_INNER), F32),
            jax.ShapeDtypeStruct((batch, D_INNER, SSM_STATE), F32),
        ],
        scratch_shapes=[pltpu.VMEM((SUBLANES, CONV_DIM), F32)],
        compiler_params=pltpu.CompilerParams(
            dimension_semantics=("parallel",), vmem_limit_bytes=VMEM_LIMIT),
        name="ssm_step",
    )(proj, proj, proj, proj, dt_raw, conv_buf, h0, sp["w_conv"], sp["b_conv"], sp["dt_bias"],
      sp["a_log"], sp["d_skip"], sp["w_norm"], sp["expand"], gsum)


def _prep_params(w_in, norm_mix, w_conv, b_conv, dt_bias, a_log, d_skip, w_ssm_norm, w_q_norm, w_k_norm):
    o_z, o_xbc = 0, D_INNER
    o_dt = o_xbc + CONV_DIM
    o_q = o_dt + SSM_HEADS
    o_k = o_q + ATTN_WIDTH
    o_v = o_k + ATTN_WIDTH
    o_g = o_v + ATTN_WIDTH
    w_main = jnp.concatenate(
        [w_in[:, o_z:o_dt], w_in[:, o_g:], w_in[:, o_q:o_g]], axis=1).astype(BF16)
    w_dt = jnp.pad(w_in[:, o_dt:o_q], ((0, 0), (0, LANES - SSM_HEADS))).astype(BF16)
    qk_w = jnp.ones((N_PROJ_TILES, 1, PROJ_TILE), F32)
    qk_w = qk_w.at[Q_TILE0:K_TILE0, 0].set(jnp.tile(w_q_norm, HEADS_PER_GROUP)[None])
    qk_w = qk_w.at[K_TILE0:V_TILE0, 0].set(jnp.tile(w_k_norm, HEADS_PER_GROUP)[None])
    pad_h = lambda v: jnp.pad(v, (0, LANES - SSM_HEADS))[None]
    expand = (_iota((LANES, D_INNER), 0) == _iota((LANES, D_INNER), 1) // SSM_HEAD_DIM).astype(BF16)
    ssm = dict(w_conv=w_conv, b_conv=b_conv[None], dt_bias=pad_h(dt_bias), a_log=pad_h(a_log),
               d_skip=jnp.repeat(d_skip, SSM_HEAD_DIM)[None], w_norm=w_ssm_norm[None], expand=expand)
    return dict(w_main=w_main, w_dt=w_dt, qk_w=qk_w, norm_mix=norm_mix[None], ssm=ssm)


def _attn_prompt_kernel(q_ref, k_ref, v_ref, o_ref, lse_ref, kp_sc, vp_sc, *, span, dil):
    i = pl.program_id(1)
    blk = ATTN_BLOCK
    pairs = q_ref.shape[1] // LANES

    @pl.when(i == 0)
    def _():
        kp_sc[...] = jnp.zeros_like(kp_sc)
        vp_sc[...] = jnp.zeros_like(vp_sc)

    qi = _iota((blk, 2 * blk), 0)
    kc = _iota((blk, 2 * blk), 1)
    dist = blk + qi - kc
    mask = (dist >= 0) & (dist <= span) & ((kc >= blk) | (i > 0))
    lo_q = _iota((blk, LANES), 1) < ATTN_HEAD_DIM
    lo_kv = _iota((2 * blk, LANES), 1) < ATTN_HEAD_DIM

    def residue(r, carry):
        rows = pl.ds(r, blk, stride=dil) if dil > 1 else slice(None)
        for pr in range(pairs):
            ls = slice(pr * LANES, (pr + 1) * LANES)
            q = q_ref[rows, ls].astype(BF16)
            k2 = jnp.concatenate([kp_sc[rows, ls], k_ref[rows, ls]], axis=0).astype(BF16)
            v2 = jnp.concatenate([vp_sc[rows, ls], v_ref[rows, ls]], axis=0).astype(BF16)
            ps, lses = [], []
            for keep in (lo_q, jnp.logical_not(lo_q)):
                s = jnp.where(mask, _dot_nt(jnp.where(keep, q, 0.0), k2), -jnp.inf)
                m = jnp.max(s, axis=-1, keepdims=True)
                p = jnp.exp(s - m)
                den = jnp.sum(p, axis=-1, keepdims=True)
                ps.append((p / den).astype(BF16))
                lses.append(m + jnp.log(den))
            v_bd = jnp.concatenate([jnp.where(lo_kv, v2, 0.0), jnp.where(lo_kv, 0.0, v2)], axis=0)
            o_ref[rows, ls] = _dot(jnp.concatenate(ps, axis=1), v_bd)
            lse_ref[rows, ls] = jnp.where(lo_q, lses[0], lses[1])
        return carry

    if dil > 1:
        lax.fori_loop(0, dil, residue, 0, unroll=4)
    else:
        residue(0, 0)
    kp_sc[...] = k_ref[...]
    vp_sc[...] = v_ref[...]


def _attn_prompt_grid(s, gi):
    window, dil = ATTN_PATTERNS[gi]
    chunk = ATTN_BLOCK * dil
    assert s % chunk == 0 and window // dil <= ATTN_BLOCK
    w = LANES if dil > 1 else ATTN_GROUP_WIDTH
    return w, ATTN_GROUP_WIDTH // w, s // chunk


def _attn_prompt(proj, gi, rider=None):
    window, dil = ATTN_PATTERNS[gi]
    s = proj.shape[0]
    w, split, chunks = _attn_prompt_grid(s, gi)
    col = lambda tile0: (lambda j, i: (i, (tile0 + gi) * split + j))
    blk = (ATTN_BLOCK * dil, w)
    n_in, n_out = 3, 2
    rider = rider or dict(kernel=None, in_specs=[], out_specs=[], out_shape=[], args=())
    r_in, r_out = len(rider["in_specs"]), len(rider["out_specs"])

    def body(*refs):
        ins, refs = refs[:n_in], refs[n_in:]
        r_ins, refs = refs[:r_in], refs[r_in:]
        outs, refs = refs[:n_out], refs[n_out:]
        r_outs, scratch = refs[:r_out], refs[r_out:]
        _attn_prompt_kernel(*ins, *outs, *scratch, span=window // dil, dil=dil)
        if rider["kernel"] is not None:
            rider["kernel"](*r_ins, *r_outs)

    outs = pl.pallas_call(
        body,
        grid=(split, chunks),
        in_specs=[pl.BlockSpec(blk, col(Q_TILE0)), pl.BlockSpec(blk, col(K_TILE0)),
                  pl.BlockSpec(blk, col(V_TILE0))] + list(rider["in_specs"]),
        out_specs=[pl.BlockSpec(blk, lambda j, i: (i, j))] * 2 + list(rider["out_specs"]),
        out_shape=[jax.ShapeDtypeStruct((s, ATTN_GROUP_WIDTH), F32)] * 2 + list(rider["out_shape"]),
        scratch_shapes=[pltpu.VMEM(blk, F32), pltpu.VMEM(blk, F32)],
        compiler_params=pltpu.CompilerParams(
            dimension_semantics=("parallel", "arbitrary"), vmem_limit_bytes=VMEM_LIMIT),
        name=f"attn_prompt_g{gi}",
    )(proj, proj, proj, *rider["args"])
    return tuple(outs[:n_out]), tuple(outs[n_out:])


def _attn_sample_kernel(q_ref, k_ref, v_ref, kc_ref, vc_ref, o_ref, lse_ref, ko_ref, vo_ref,
                        *, span, dil, pos0):
    l = SUBLANES
    nb = kc_ref.shape[0]
    nh, hd = HEADS_PER_GROUP, ATTN_HEAD_DIM
    rows = kc_ref.shape[3]
    w = nh * hd
    t_c = _iota((l, rows), 0)
    delta_c = rows + t_c - _iota((l, rows), 1)
    ok_c = ((delta_c <= span * dil) & ((delta_c & (dil - 1)) == 0) & (pos0 + t_c - delta_c >= 0))
    delta_n = _iota((l, l), 0) - _iota((l, l), 1)
    ok_n = (delta_n >= 0) & (delta_n <= span * dil) & ((delta_n & (dil - 1)) == 0)
    lane = _iota((hd, LANES), 1)
    hl = nh * l
    own = _shr(_iota((hl, w), 0), l) == _shr(_iota((hl, w), 1), hd)
    ok_c = jnp.concatenate([ok_c] * nh, axis=0)
    ok_n = jnp.concatenate([ok_n] * nh, axis=0)
    for bb in range(nb):
        rs = slice(bb * l, (bb + 1) * l)
        q, kn, vn = q_ref[rs, :], k_ref[rs, :], v_ref[rs, :]
        q_bd = jnp.where(own, jnp.concatenate([q] * nh, axis=0), 0.0)
        kt = kc_ref[bb].reshape(w, rows)
        vt = vc_ref[bb].reshape(w, rows)
        s_c = jnp.where(ok_c, _dot(q_bd, kt), -jnp.inf)
        s_n = jnp.where(ok_n, _dot_nt(q_bd, kn), -jnp.inf)
        m = jnp.maximum(jnp.max(s_c, axis=-1, keepdims=True), jnp.max(s_n, axis=-1, keepdims=True))
        p_c = jnp.exp(s_c - m)
        p_n = jnp.exp(s_n - m)
        den = jnp.sum(p_c, axis=-1, keepdims=True) + jnp.sum(p_n, axis=-1, keepdims=True)
        res = jnp.where(own, _dot_nt(p_c / den, vt) + _dot(p_n / den, vn), 0.0)
        lse = jnp.where(own, m + jnp.log(den), 0.0)
        o, lse_x = res[0:l], lse[0:l]
        for h in range(1, nh):
            o = o + res[h * l:(h + 1) * l]
            lse_x = lse_x + lse[h * l:(h + 1) * l]
        o_ref[rs, :] = o
        lse_ref[rs, :] = lse_x

        for new, c_ref, out_ref in ((kn, kc_ref, ko_ref), (vn, vc_ref, vo_ref)):
            padded = jnp.concatenate([jnp.zeros((LANES - l, w), F32), new], axis=0)
            tail = padded.T
            for h in range(nh):
                shifted = pltpu.roll(c_ref[bb, h], rows - l, axis=1)
                if rows > LANES:
                    out_ref[bb, h, :, 0:rows - LANES] = shifted[:, 0:rows - LANES]
                out_ref[bb, h, :, rows - LANES:rows] = jnp.where(
                    lane < LANES - l, shifted[:, rows - LANES:rows], tail[h * hd:(h + 1) * hd])


SAMPLE_CACHE_ROWS_PER_STEP = 2048
SAMPLE_MAX_SEQS_PER_STEP = 8


def _attn_sample_steps(rows, batch):
    nb = max(1, min(batch, SAMPLE_CACHE_ROWS_PER_STEP // rows, SAMPLE_MAX_SEQS_PER_STEP))
    assert batch % nb == 0
    return nb, batch // nb


def _attn_sample_parts(proj, k_cache, v_cache, gi, batch, pos0, step_of):
    window, dil = ATTN_PATTERNS[gi]
    rows = k_cache.shape[1]
    l = proj.shape[0] // batch
    assert l == SUBLANES and rows % LANES == 0
    nb, _ = _attn_sample_steps(rows, batch)
    tok = lambda tile0: pl.BlockSpec((nb * l, ATTN_GROUP_WIDTH), lambda *g: (step_of(*g), tile0 + gi))
    cache = pl.BlockSpec((nb, HEADS_PER_GROUP, ATTN_HEAD_DIM, rows), lambda *g: (step_of(*g), 0, 0, 0))
    out_tok = pl.BlockSpec((nb * l, ATTN_GROUP_WIDTH), lambda *g: (step_of(*g), 0))
    to_t = lambda c: jnp.transpose(c, (0, 2, 3, 1))
    t_shape = jax.ShapeDtypeStruct((batch, HEADS_PER_GROUP, ATTN_HEAD_DIM, rows), F32)
    return dict(
        kernel=functools.partial(_attn_sample_kernel, span=window // dil, dil=dil, pos0=pos0),
        in_specs=[tok(Q_TILE0), tok(K_TILE0), tok(V_TILE0), cache, cache],
        out_specs=[out_tok, out_tok, cache, cache],
        out_shape=[jax.ShapeDtypeStruct((batch * l, ATTN_GROUP_WIDTH), F32)] * 2 + [t_shape] * 2,
        args=(proj, proj, proj, to_t(k_cache), to_t(v_cache)),
    )


def _attn_sample_finish(o, lse, k_t, v_t):
    from_t = lambda c: jnp.transpose(c, (0, 3, 1, 2))
    return o, lse, from_t(k_t), from_t(v_t)


def _attn_sample(proj, k_cache, v_cache, gi, batch, pos0):
    parts = _attn_sample_parts(proj, k_cache, v_cache, gi, batch, pos0, lambda b: b)
    _, steps = _attn_sample_steps(k_cache.shape[1], batch)
    outs = pl.pallas_call(
        parts["kernel"],
        grid=(steps,),
        in_specs=parts["in_specs"],
        out_specs=parts["out_specs"],
        out_shape=parts["out_shape"],
        compiler_params=pltpu.CompilerParams(
            dimension_semantics=("parallel",), vmem_limit_bytes=VMEM_LIMIT),
        name=f"attn_sample_g{gi}",
    )(*parts["args"])
    return _attn_sample_finish(*outs)


def _merge_kernel(x_ref, ys_ref, o0_ref, o1_ref, o2_ref, l0_ref, l1_ref, l2_ref, g0_ref, g1_ref,
                  wbs_ref, wba_ref, wo_ref, nf_ref, wr1_ref, wr2_ref, br_ref,
                  x1_ref, xn_ref, lg_ref):
    l0, l1, l2 = l0_ref[...], l1_ref[...], l2_ref[...]
    m = jnp.maximum(jnp.maximum(l0, l1), l2)
    e0, e1, e2 = jnp.exp(l0 - m), jnp.exp(l1 - m), jnp.exp(l2 - m)
    den = e0 + e1 + e2
    y_attn = (e0 / den) * o0_ref[...] + (e1 / den) * o1_ref[...] + (e2 / den) * o2_ref[...]
    br_s = _dot(ys_ref[...].astype(BF16), wbs_ref[...])
    br_a = _dot(y_attn.astype(BF16), wba_ref[...])
    merged = jax.nn.sigmoid(g0_ref[...]) * br_s + jax.nn.sigmoid(g1_ref[...]) * br_a
    x1 = x_ref[...] + _dot(merged.astype(BF16), wo_ref[...])
    x1_ref[...] = x1
    xn = x1 * lax.rsqrt(jnp.mean(x1 * x1, axis=-1, keepdims=True) + EPS) * nf_ref[...]
    xn1 = xn.astype(BF16)
    xn_ref[...] = xn1
    xn2 = (xn - xn1.astype(F32)).astype(BF16)
    lg_ref[...] = (_dot(xn1, wr1_ref[...]) + _dot(xn2, wr1_ref[...]) + _dot(xn1, wr2_ref[...])
                   + br_ref[...])


def _merge(x, proj, y_ssm, attn, mp, tm):
    t = x.shape[0]
    row = lambda w: pl.BlockSpec((tm, w), lambda i: (i, 0))
    full = lambda a: pl.BlockSpec(a.shape, lambda i: (0,) * a.ndim)
    (o0, l0), (o1, l1), (o2, l2) = attn
    weights = [mp["w_br_ssm"], mp["w_br_attn"], mp["w_out"], mp["norm_ffn"], mp["w_r1"], mp["w_r2"], mp["b_r"]]
    return pl.pallas_call(
        _merge_kernel,
        grid=(t // tm,),
        in_specs=[row(D_MODEL), row(D_INNER)] + [row(ATTN_GROUP_WIDTH)] * 6 + [
            pl.BlockSpec((tm, D_MODEL), lambda i: (i, OFF_G // D_MODEL)),
            pl.BlockSpec((tm, D_MODEL), lambda i: (i, OFF_G // D_MODEL + 1)),
        ] + [full(w) for w in weights],
        out_specs=[row(D_MODEL), row(D_MODEL), row(ROUTER_WIDTH)],
        out_shape=[
            jax.ShapeDtypeStruct((t, D_MODEL), F32),
            jax.ShapeDtypeStruct((t, D_MODEL), BF16),
            jax.ShapeDtypeStruct((t, ROUTER_WIDTH), F32),
        ],
        compiler_params=pltpu.CompilerParams(
            dimension_semantics=("parallel",), vmem_limit_bytes=VMEM_LIMIT),
        name="merge",
    )(x, y_ssm, o0, o1, o2, l0, l1, l2, proj, proj, *weights)


def _prep_merge_params(w_br_ssm, w_br_attn, w_out, norm_ffn, w_router_group, b_router_group,
                       w_router_expert, b_router_expert):
    w_r = jnp.concatenate([w_router_group, w_router_expert.reshape(D_MODEL, N_EXPERTS)], axis=1)
    w_r = jnp.pad(w_r, ((0, 0), (0, ROUTER_WIDTH - w_r.shape[1])))
    w_r1 = w_r.astype(BF16)
    w_r2 = (w_r - w_r1.astype(F32)).astype(BF16)
    b_r = jnp.concatenate([b_router_group, b_router_expert.reshape(N_EXPERTS)])
    b_r = jnp.pad(b_r, (0, ROUTER_WIDTH - b_r.shape[0]))[None]
    return dict(w_br_ssm=w_br_ssm.astype(BF16), w_br_attn=w_br_attn.astype(BF16), w_out=w_out.astype(BF16),
                norm_ffn=norm_ffn[None], w_r1=w_r1, w_r2=w_r2, b_r=b_r)


def _route(logits):
    lane = _iota(logits.shape, 1)
    big = jnp.int32(2 ** 30)
    neg = -jnp.inf
    gl = jnp.where(lane < N_EXPERT_GROUPS, logits, neg)
    gmax = jnp.max(gl, axis=-1, keepdims=True)
    gidx = jnp.min(jnp.where(gl == gmax, lane, big), axis=-1, keepdims=True)
    g_w = 1.0 / jnp.sum(jnp.exp(gl - gmax), axis=-1, keepdims=True)
    lo = N_EXPERT_GROUPS + gidx * EXPERTS_PER_GROUP
    el = jnp.where((lane >= lo) & (lane < lo + EXPERTS_PER_GROUP), logits, neg)
    v1 = jnp.max(el, axis=-1, keepdims=True)
    i1 = jnp.min(jnp.where(el == v1, lane, big), axis=-1, keepdims=True)
    el2 = jnp.where(lane == i1, neg, el)
    v2 = jnp.max(el2, axis=-1, keepdims=True)
    i2 = jnp.min(jnp.where(el2 == v2, lane, big), axis=-1, keepdims=True)
    e2 = jnp.exp(v2 - v1)
    w1 = (1.0 / (1.0 + e2)) * g_w
    w2 = (e2 / (1.0 + e2)) * g_w
    return jnp.where(lane == i1, w1, jnp.where(lane == i2, w2, 0.0)), gidx


MOE_BLOCK = 256
MOE_SMALL_BLOCK = 64


def _moe_kernel(x1_ref, xn_ref, lg_ref, wg_ref, wu_ref, wd_ref, out_ref,
                comb_sc, oh_sc, cum_sc, oht_sc, cumt_sc):
    g = pl.program_id(1)
    tm = x1_ref.shape[0]
    blk = MOE_BLOCK

    @pl.when(g == 0)
    def _():
        comb, gidx = _route(lg_ref[...])
        c1, c2, c3 = _split3(comb)
        comb_sc[0] = c1
        comb_sc[1] = c2
        comb_sc[2] = c3
        onehot = (_iota((tm, LANES), 1) == gidx).astype(F32)
        oh_bf = onehot.astype(BF16)
        lower = (_iota((tm, tm), 0) > _iota((tm, tm), 1)).astype(BF16)
        upper = (_iota((tm, tm), 0) < _iota((tm, tm), 1)).astype(BF16)
        oh_sc[...] = onehot
        cum_sc[...] = _dot(lower, oh_bf)
        oht_sc[...] = onehot.T[0:SUBLANES]
        cumt_sc[...] = _dot_tn(oh_bf, upper)[0:SUBLANES]
        out_ref[...] = x1_ref[...]

    lane = _iota((tm, LANES), 1)
    pick = lane == g
    member_c = jnp.sum(jnp.where(pick, oh_sc[...], 0.0), axis=-1, keepdims=True)
    rank_c = jnp.sum(jnp.where(pick, cum_sc[...], 0.0), axis=-1, keepdims=True)
    member_r = oht_sc[pl.ds(g, 1), :]
    rank_r = cumt_sc[pl.ds(g, 1), :]
    count = jnp.sum(member_r).astype(jnp.int32)
    small = MOE_SMALL_BLOCK
    nfull = count // blk
    rem = count - nfull * blk
    use_small = rem <= blk // 2
    n_big = nfull + jnp.where(use_small, 0, 1)
    n_small = jnp.where(use_small, (rem + (small - 1)) // small, 0)
    xn = xn_ref[...]
    wd = wd_ref[...].reshape(EXPERTS_PER_GROUP * D_EXPERT, D_MODEL)

    def run_block(first_row, blk):
        lane_b = _iota((blk, LANES), 1)
        base = first_row.astype(F32)
        want_r = base + _iota((blk, tm), 0).astype(F32)
        sel = jnp.where((rank_r == want_r) & (member_r > 0.0), 1.0, 0.0).astype(BF16)
        want_c = base + _iota((tm, blk), 1).astype(F32)
        sel_t = jnp.where((rank_c == want_c) & (member_c > 0.0), 1.0, 0.0).astype(BF16)
        xg = _dot(sel, xn).astype(BF16)
        cg = _dot(sel, comb_sc[0]) + _dot(sel, comb_sc[1]) + _dot(sel, comb_sc[2])
        hidden = []
        for e in range(EXPERTS_PER_GROUP):
            col = N_EXPERT_GROUPS + g * EXPERTS_PER_GROUP + e
            cw = jnp.sum(jnp.where(lane_b == col, cg, 0.0), axis=-1, keepdims=True)
            h = jax.nn.silu(_dot(xg, wg_ref[e])) * _dot(xg, wu_ref[e]) * cw
            hidden.append(h.astype(BF16))
        out = _dot(jnp.concatenate(hidden, axis=1), wd)
        o1 = out.astype(BF16)
        o2 = (out - o1.astype(F32)).astype(BF16)
        out_ref[...] += _dot(sel_t, o1) + _dot(sel_t, o2)

    def big(k, carry):
        run_block(k * blk, blk)
        return carry

    def little(k, carry):
        run_block(nfull * blk + k * small, small)
        return carry

    lax.fori_loop(0, n_big, big, 0)
    lax.fori_loop(0, n_small, little, 0)


def _moe(x1, xn, logits, wg, wu, wd, tm):
    t = x1.shape[0]
    ge = EXPERTS_PER_GROUP
    return pl.pallas_call(
        _moe_kernel,
        grid=(t // tm, N_EXPERT_GROUPS),
        in_specs=[
            pl.BlockSpec((tm, D_MODEL), lambda i, g: (i, 0), pipeline_mode=pl.Buffered(1)),
            pl.BlockSpec((tm, D_MODEL), lambda i, g: (i, 0)),
            pl.BlockSpec((tm, ROUTER_WIDTH), lambda i, g: (i, 0), pipeline_mode=pl.Buffered(1)),
            pl.BlockSpec((ge, D_MODEL, D_EXPERT), lambda i, g: (g, 0, 0)),
            pl.BlockSpec((ge, D_MODEL, D_EXPERT), lambda i, g: (g, 0, 0)),
            pl.BlockSpec((ge, D_EXPERT, D_MODEL), lambda i, g: (g, 0, 0)),
        ],
        out_specs=pl.BlockSpec((tm, D_MODEL), lambda i, g: (i, 0)),
        out_shape=jax.ShapeDtypeStruct((t, D_MODEL), F32),
        scratch_shapes=[
            pltpu.VMEM((3, tm, ROUTER_WIDTH), BF16),
            pltpu.VMEM((tm, LANES), F32),
            pltpu.VMEM((tm, LANES), F32),
            pltpu.VMEM((SUBLANES, tm), F32),
            pltpu.VMEM((SUBLANES, tm), F32),
        ],
        compiler_params=pltpu.CompilerParams(
            dimension_semantics=("parallel", "arbitrary"), vmem_limit_bytes=VMEM_LIMIT),
        name="moe",
    )(x1, xn, logits, wg, wu, wd)


def _moe_dense_kernel(x1_ref, xn_ref, lg_ref, wg_ref, wu_ref, wd_ref, out_ref, comb_sc, acc_sc):
    e = pl.program_id(1)

    @pl.when(e == 0)
    def _():
        comb_sc[...] = _route(lg_ref[...])[0]
        acc_sc[...] = jnp.zeros_like(acc_sc)

    lane = _iota(comb_sc.shape, 1)
    cw = jnp.sum(jnp.where(lane == e + N_EXPERT_GROUPS, comb_sc[...], 0.0), axis=-1, keepdims=True)
    xn = xn_ref[...]
    hidden = jax.nn.silu(_dot(xn, wg_ref[0])) * _dot(xn, wu_ref[0]) * cw
    acc_sc[...] += _dot(hidden.astype(BF16), wd_ref[0])

    @pl.when(e == pl.num_programs(1) - 1)
    def _():
        out_ref[...] = x1_ref[...] + acc_sc[...]


def _moe_dense(x1, xn, logits, wg, wu, wd, tm):
    t = x1.shape[0]
    return pl.pallas_call(
        _moe_dense_kernel,
        grid=(t // tm, N_EXPERTS),
        in_specs=[
            pl.BlockSpec((tm, D_MODEL), lambda i, e: (i, 0)),
            pl.BlockSpec((tm, D_MODEL), lambda i, e: (i, 0)),
            pl.BlockSpec((tm, ROUTER_WIDTH), lambda i, e: (i, 0)),
            pl.BlockSpec((1, D_MODEL, D_EXPERT), lambda i, e: (e, 0, 0)),
            pl.BlockSpec((1, D_MODEL, D_EXPERT), lambda i, e: (e, 0, 0)),
            pl.BlockSpec((1, D_EXPERT, D_MODEL), lambda i, e: (e, 0, 0)),
        ],
        out_specs=pl.BlockSpec((tm, D_MODEL), lambda i, e: (i, 0)),
        out_shape=jax.ShapeDtypeStruct((t, D_MODEL), F32),
        scratch_shapes=[pltpu.VMEM((tm, ROUTER_WIDTH), F32), pltpu.VMEM((tm, D_MODEL), F32)],
        compiler_params=pltpu.CompilerParams(
            dimension_semantics=("parallel", "arbitrary"), vmem_limit_bytes=VMEM_LIMIT),
        name="moe_dense",
    )(x1, xn, logits, wg, wu, wd)


def _tile_rows(t, cap):
    tm = min(t, cap)
    assert t % tm == 0
    return tm


def _finish_layer(x2, proj, y_ssm, attn, h_new, new_kv, b, l, mp, ep):
    t, d = x2.shape
    new_conv = proj.reshape(b, l, PROJ_WIDTH)[:, l - (CONV_WIDTH - 1):, OFF_X:OFF_X + CONV_DIM]
    x1, xn, logits = _merge(x2, proj, y_ssm, attn, mp, _tile_rows(t, 512))
    y = _moe(x1, xn, logits, ep["wg"], ep["wu"], ep["wd"], _tile_rows(t, 1024))
    new_ssm = h_new.reshape(b, SSM_HEADS, SSM_HEAD_DIM, SSM_STATE)
    return y.reshape(b, l, d), (new_conv, new_ssm) + tuple(new_kv)


def _layer_pair(x_prompt, x_sample, conv_s, ssm_s, kv_s, pos0, pp, mp, ep):
    bp, lp, d = x_prompt.shape
    bs, ls, _ = x_sample.shape
    assert bp == 1 and lp % SSD_CHUNK == 0
    tp, ts = bp * lp, bs * ls
    xp2, xs2 = x_prompt.reshape(tp, d), x_sample.reshape(ts, d)
    proj_p, dt_p = _in_proj(xp2, pp["norm_mix"], pp["w_main"], pp["w_dt"], pp["qk_w"], _tile_rows(tp, 2048))
    proj_s, dt_s = _in_proj(xs2, pp["norm_mix"], pp["w_main"], pp["w_dt"], pp["qk_w"], _tile_rows(ts, 2048))

    nc = lp // SSD_CHUNK
    rider_gi = next((gi for gi in reversed(range(N_ATTN_GROUPS))
                     if _attn_sample_steps(kv_s[2 * gi].shape[1], bs)[1] == nc), None)
    rider = None
    if rider_gi is not None:
        rider = _attn_sample_parts(proj_s, kv_s[2 * rider_gi], kv_s[2 * rider_gi + 1], rider_gi, bs, pos0,
                                   lambda b, c: c)
    zero_conv = jnp.zeros((bp, CONV_WIDTH - 1, CONV_DIM), F32)
    zero_h0 = jnp.zeros((bp, D_INNER, SSM_STATE), F32)
    (y_ssm_p, h_p), rider_out = _ssm_chunked(proj_p, dt_p, zero_conv, zero_h0, pp["ssm"], bp, SSD_CHUNK, rider)

    riders_out = {rider_gi: rider_out}
    attn_p = []
    for pg in range(N_ATTN_GROUPS):
        _, split, chunks = _attn_prompt_grid(tp, pg)
        sg = next((g for g in range(N_ATTN_GROUPS) if g not in riders_out
                   and _attn_sample_steps(kv_s[2 * g].shape[1], bs)[1] == split * chunks), None)
        parts = None
        if sg is not None:
            parts = _attn_sample_parts(proj_s, kv_s[2 * sg], kv_s[2 * sg + 1], sg, bs, pos0,
                                       lambda j, i, n=chunks: j * n + i)
        o_lse, out = _attn_prompt(proj_p, pg, parts)
        attn_p.append(o_lse)
        if sg is not None:
            riders_out[sg] = out
    proj_p3 = proj_p.reshape(bp, lp, PROJ_WIDTH)
    kv_p = []
    for gi, (window, _) in enumerate(ATTN_PATTERNS):
        keep = min(window, lp)
        for off in (OFF_K, OFF_V):
            c0 = off + gi * ATTN_GROUP_WIDTH
            kv_p.append(proj_p3[:, lp - keep:, c0:c0 + ATTN_GROUP_WIDTH]
                        .reshape(bp, keep, HEADS_PER_GROUP, ATTN_HEAD_DIM))

    y_ssm_s, h_s = _ssm_step(proj_s, dt_s, conv_s, ssm_s.reshape(bs, D_INNER, SSM_STATE), pp["ssm"], bs, ls)
    attn_s, kv_new = [], []
    for gi in range(N_ATTN_GROUPS):
        if riders_out.get(gi):
            o, lse, kn, vn = _attn_sample_finish(*riders_out[gi])
        else:
            o, lse, kn, vn = _attn_sample(proj_s, kv_s[2 * gi], kv_s[2 * gi + 1], gi, bs, pos0)
        attn_s.append((o, lse))
        kv_new += [kn, vn]

    yp, st_p = _finish_layer(xp2, proj_p, y_ssm_p, attn_p, h_p, kv_p, bp, lp, mp, ep)
    ys, st_s = _finish_layer(xs2, proj_s, y_ssm_s, attn_s, h_s, kv_new, bs, ls, mp, ep)
    return yp, st_p, ys, st_s


def kernel(x_prompt, x_sample, cache_conv, state_ssm, cache_k_w128, cache_v_w128, cache_k_w512, cache_v_w512,
           cache_k_w2048, cache_v_w2048, w_in, norm_mix, w_conv, b_conv, dt_bias, a_log, d_skip, w_ssm_norm,
           w_q_norm, w_k_norm, w_br_ssm, w_br_attn, w_out, norm_ffn, w_router_group, b_router_group,
           w_router_expert, b_router_expert, w_expert_gate, w_expert_up, w_expert_down):
    depth = w_in.shape[0]
    yp, ys = x_prompt, x_sample
    prompt_states, sample_states = [], []
    for i in range(depth):
        pp = _prep_params(w_in[i], norm_mix[i], w_conv[i], b_conv[i], dt_bias[i], a_log[i], d_skip[i],
                          w_ssm_norm[i], w_q_norm[i], w_k_norm[i])
        mp = _prep_merge_params(w_br_ssm[i], w_br_attn[i], w_out[i], norm_ffn[i], w_router_group[i],
                                b_router_group[i], w_router_expert[i], b_router_expert[i])
        ep = dict(wg=w_expert_gate[i].astype(BF16), wu=w_expert_up[i].astype(BF16),
                  wd=w_expert_down[i].astype(BF16))
        kv_bufs = (cache_k_w128[i], cache_v_w128[i], cache_k_w512[i], cache_v_w512[i],
                   cache_k_w2048[i], cache_v_w2048[i])
        yp, st_p, ys, st_s = _layer_pair(yp, ys, cache_conv[i], state_ssm[i], kv_bufs, PAST_LEN, pp, mp, ep)
        prompt_states.append(st_p)
        sample_states.append(st_s)
    stack = lambda states: [jnp.stack([s[j] for s in states]) for j in range(8)]
    return (yp, ys) + tuple(stack(prompt_states)) + tuple(stack(sample_states))
```

```python
import functools
import math

import jax
import jax.numpy as jnp
from jax import lax
from jax.experimental import pallas as pl
from jax.experimental.pallas import tpu as pltpu

F32 = jnp.float32
BF16 = jnp.bfloat16

PAST_LEN = 8192
D_MODEL = 1024
D_INNER = 2048
SSM_HEAD_DIM = 64
SSM_HEADS = 32
SSM_GROUPS = 8
SSM_STATE = 128
CONV_WIDTH = 4
CONV_DIM = D_INNER + 2 * SSM_GROUPS * SSM_STATE
SSD_CHUNK = 128
ATTN_HEAD_DIM = 64
HEADS_PER_GROUP = 8
ATTN_PATTERNS = ((128, 1), (512, 4), (2048, 16))
N_ATTN_GROUPS = 3
ATTN_GROUP_WIDTH = HEADS_PER_GROUP * ATTN_HEAD_DIM
ATTN_WIDTH = N_ATTN_GROUPS * ATTN_GROUP_WIDTH
ATTN_BLOCK = 128
N_EXPERT_GROUPS = 4
EXPERTS_PER_GROUP = 8
N_EXPERTS = 32
D_EXPERT = 256
EPS = 1e-6

LANES = 128
SUBLANES = 8
VMEM_LIMIT = 56 * 1024 * 1024

PROJ_TILE = 512
OFF_Z = 0
OFF_X = 2048
OFF_B = 4096
OFF_C = 5120
OFF_G = 6144
OFF_Q = 8192
OFF_K = OFF_Q + ATTN_WIDTH
OFF_V = OFF_K + ATTN_WIDTH
PROJ_WIDTH = OFF_V + ATTN_WIDTH
N_PROJ_TILES = PROJ_WIDTH // PROJ_TILE
Q_TILE0 = OFF_Q // PROJ_TILE
K_TILE0 = OFF_K // PROJ_TILE
V_TILE0 = OFF_V // PROJ_TILE
ROUTER_WIDTH = LANES


def _split3(x):
    x1 = x.astype(BF16)
    r1 = x - x1.astype(F32)
    x2 = r1.astype(BF16)
    r2 = r1 - x2.astype(F32)
    return x1, x2, r2.astype(BF16)


def _dot(a, b):
    return jnp.dot(a, b, preferred_element_type=F32)


def _dot3(x, m):
    x1, x2, x3 = _split3(x)
    return _dot(x1, m) + _dot(x2, m) + _dot(x3, m)


def _dot3f(x, m):
    x1, x2, x3 = _split3(x)
    return _dot(x1.astype(F32), m) + _dot(x2.astype(F32), m) + _dot(x3.astype(F32), m)


def _dot_nt(a, b):
    return lax.dot_general(a, b, (((1,), (1,)), ((), ())), preferred_element_type=F32)


def _dot_tn(a, b):
    return lax.dot_general(a, b, (((0,), (0,)), ((), ())), preferred_element_type=F32)


def _iota(shape, dim):
    return lax.broadcasted_iota(jnp.int32, shape, dim)


def _shr(x, pow2):
    assert pow2 & (pow2 - 1) == 0
    return lax.shift_right_logical(x, jnp.int32(int(math.log2(pow2))))


def _in_proj_kernel(x_ref, nw_ref, w_ref, wdt_ref, qkw_ref, seg_ref, out_ref, dt_ref, h_sc):
    j = pl.program_id(1)

    @pl.when(j == 0)
    def _():
        xf = x_ref[...]
        y = xf * lax.rsqrt(jnp.mean(xf * xf, axis=-1, keepdims=True) + EPS)
        h = (y * nw_ref[...]).astype(BF16)
        h_sc[...] = h
        dt_ref[...] = _dot(h, wdt_ref[...])

    acc = _dot(h_sc[...], w_ref[...])
    is_qk = jnp.logical_and(j >= Q_TILE0, j < V_TILE0)

    @pl.when(jnp.logical_not(is_qk))
    def _():
        out_ref[...] = acc

    @pl.when(is_qk)
    def _():
        sq = acc * acc
        s1 = sq.astype(BF16)
        s2 = (sq - s1.astype(F32)).astype(BF16)
        ms = (_dot(s1, seg_ref[...]) + _dot(s2, seg_ref[...])) * (1.0 / ATTN_HEAD_DIM)
        scale = jnp.where(j < K_TILE0, ATTN_HEAD_DIM ** -0.5, 1.0).astype(F32)
        w = qkw_ref[0] * scale
        out_ref[...] = acc * lax.rsqrt(ms + EPS) * w


def _in_proj(x, norm_w, w_bf, wdt_bf, qk_w, tm):
    t = x.shape[0]
    seg = (_iota((PROJ_TILE, PROJ_TILE), 0) // ATTN_HEAD_DIM
           == _iota((PROJ_TILE, PROJ_TILE), 1) // ATTN_HEAD_DIM).astype(BF16)
    return pl.pallas_call(
        _in_proj_kernel,
        grid=(t // tm, N_PROJ_TILES),
        in_specs=[
            pl.BlockSpec((tm, D_MODEL), lambda i, j: (i, 0)),
            pl.BlockSpec((1, D_MODEL), lambda i, j: (0, 0)),
            pl.BlockSpec((D_MODEL, PROJ_TILE), lambda i, j: (0, j)),
            pl.BlockSpec((D_MODEL, LANES), lambda i, j: (0, 0)),
            pl.BlockSpec((1, 1, PROJ_TILE), lambda i, j: (j, 0, 0)),
            pl.BlockSpec((PROJ_TILE, PROJ_TILE), lambda i, j: (0, 0)),
        ],
        out_specs=[
            pl.BlockSpec((tm, PROJ_TILE), lambda i, j: (i, j)),
            pl.BlockSpec((tm, LANES), lambda i, j: (i, 0)),
        ],
        out_shape=[
            jax.ShapeDtypeStruct((t, PROJ_WIDTH), F32),
            jax.ShapeDtypeStruct((t, LANES), F32),
        ],
        scratch_shapes=[pltpu.VMEM((tm, D_MODEL), BF16)],
        compiler_params=pltpu.CompilerParams(
            dimension_semantics=("parallel", "arbitrary"), vmem_limit_bytes=VMEM_LIMIT),
        name="in_proj",
    )(x, norm_w, w_bf, wdt_bf, qk_w, seg)


def _softplus(x):
    return jnp.maximum(x, 0.0) + jnp.log1p(jnp.exp(-jnp.abs(x)))


def _conv_silu(raw, tail, w, b):
    q = raw.shape[0]
    up = jnp.concatenate([tail, raw], axis=0)
    acc = b + w[CONV_WIDTH - 1:CONV_WIDTH] * raw
    for tap in range(CONV_WIDTH - 1):
        shifted = pltpu.roll(up, CONV_WIDTH - 1 - tap, axis=0)[SUBLANES:SUBLANES + q]
        acc = acc + w[tap:tap + 1] * shifted
    return jax.nn.silu(acc)


def _ssm_chunk_kernel(z_ref, xs_ref, bm_ref, cm_ref, dt_ref, cbuf_ref, h0_ref, wconv_ref, bconv_ref,
                      dtb_ref, alog_ref, dskip_ref, wnorm_ref, expand_ref,
                      y_ref, hout_ref, tail_sc, xbc_sc, ht_sc):
    c = pl.program_id(1)
    nc = pl.num_programs(1)
    q = xs_ref.shape[0]
    gw = D_INNER // SSM_GROUPS
    n = SSM_STATE

    @pl.when(c == 0)
    def _():
        tail_sc[...] = jnp.zeros_like(tail_sc)
        tail_sc[SUBLANES - (CONV_WIDTH - 1):SUBLANES, :] = cbuf_ref[0]
        for g in range(SSM_GROUPS):
            ht_sc[g] = h0_ref[0, g * gw:(g + 1) * gw, :].T

    cw = PROJ_TILE
    for k in range(CONV_DIM // cw):
        if k < D_INNER // cw:
            raw = xs_ref[:, k * cw:(k + 1) * cw]
        elif k < (D_INNER + SSM_GROUPS * n) // cw:
            kk = k - D_INNER // cw
            raw = bm_ref[:, kk * cw:(kk + 1) * cw]
        else:
            kk = k - (D_INNER + SSM_GROUPS * n) // cw
            raw = cm_ref[:, kk * cw:(kk + 1) * cw]
        sl = slice(k * cw, (k + 1) * cw)
        xbc_sc[:, sl] = _conv_silu(raw, tail_sc[:, sl], wconv_ref[:, sl], bconv_ref[:, sl])
        tail_sc[:, sl] = raw[q - SUBLANES:q]

    dt = _softplus(dt_ref[...] + dtb_ref[...])
    a = -jnp.exp(alog_ref[...])
    dta = dt * a
    tri = (_iota((q, q), 0) >= _iota((q, q), 1)).astype(BF16)
    d1, d2, d3 = _split3(dta)
    acs = _dot(tri, d1) + _dot(tri, d2) + _dot(tri, d3)
    eye = (_iota((LANES, LANES), 0) == _iota((LANES, LANES), 1)).astype(BF16)
    c1, c2, c3 = _split3(acs)
    acs_t = _dot_nt(eye, c1) + _dot_nt(eye, c2) + _dot_nt(eye, c3)
    a_end = acs[q - 1:q, :]
    e_in = jnp.exp(acs)
    e_out = jnp.exp(a_end - acs)
    e_all = jnp.exp(a_end)

    ex = expand_ref[...]
    dt_x = _dot3(dt, ex)
    e_in_x = _dot3(e_in, ex)
    e_out_x = _dot3(e_out, ex)
    e_all_x = _dot3(jnp.broadcast_to(e_all, (2 * SUBLANES, LANES)), ex)[0:1]

    causal = _iota((q, q), 0) >= _iota((q, q), 1)
    lane_lo = _iota((q, LANES), 1) < SSM_HEAD_DIM

    for g in range(SSM_GROUPS):
        gs = slice(g * gw, (g + 1) * gw)
        xg = xbc_sc[:, gs]
        bg = xbc_sc[:, D_INNER + g * n:D_INNER + (g + 1) * n].astype(BF16)
        cg = xbc_sc[:, D_INNER + SSM_GROUPS * n + g * n:D_INNER + SSM_GROUPS * n + (g + 1) * n].astype(BF16)
        xdt = xg * dt_x[:, gs]
        cb = _dot_nt(cg, bg)
        ht = ht_sc[g]
        y_off = _dot(cg, ht.astype(BF16)) * e_in_x[:, gs]
        pieces = []
        for pair in range(2):
            ms = []
            for hh in range(2):
                h = g * 4 + pair * 2 + hh
                diff = acs[:, h:h + 1] - acs_t[h:h + 1, :]
                lmat = jnp.where(causal, jnp.exp(jnp.where(causal, diff, 0.0)), 0.0)
                ms.append((cb * lmat).astype(BF16))
            xp = xdt[:, pair * LANES:(pair + 1) * LANES]
            rhs = jnp.concatenate([jnp.where(lane_lo, xp, 0.0), jnp.where(lane_lo, 0.0, xp)],
                                  axis=0).astype(BF16)
            pieces.append(_dot(jnp.concatenate(ms, axis=1), rhs))
        y_diag = jnp.concatenate(pieces, axis=1)
        yg = y_diag + y_off + dskip_ref[:, gs] * xg
        zg = z_ref[:, gs]
        u = yg * jax.nn.silu(zg)
        u = u * lax.rsqrt(jnp.mean(u * u, axis=-1, keepdims=True) + EPS)
        y_ref[:, gs] = (u * wnorm_ref[:, gs]).astype(y_ref.dtype)
        st = _dot_tn(bg, (xdt * e_out_x[:, gs]).astype(BF16))
        ht_sc[g] = e_all_x[:, gs] * ht + st

    @pl.when(c == nc - 1)
    def _():
        for g in range(SSM_GROUPS):
            hout_ref[0, g * gw:(g + 1) * gw, :] = ht_sc[g].T


def _ssm_chunked(proj, dt_raw, conv_buf, h0, sp, batch, q, rider=None):
    t = proj.shape[0]
    nc = t // batch // q
    row = lambda b, c: b * nc + c
    col = lambda off, w: off // w
    full = lambda shape: pl.BlockSpec(shape, lambda b, c: (0,) * len(shape))
    n_in, n_out = 14, 2
    rider = rider or dict(kernel=None, in_specs=[], out_specs=[], out_shape=[], args=())
    r_in, r_out = len(rider["in_specs"]), len(rider["out_specs"])

    def body(*refs):
        ins, refs = refs[:n_in], refs[n_in:]
        r_ins, refs = refs[:r_in], refs[r_in:]
        outs, refs = refs[:n_out], refs[n_out:]
        r_outs, scratch = refs[:r_out], refs[r_out:]
        _ssm_chunk_kernel(*ins, *outs, *scratch)
        if rider["kernel"] is not None:
            rider["kernel"](*r_ins, *r_outs)

    outs = pl.pallas_call(
        body,
        grid=(batch, nc),
        in_specs=[
            pl.BlockSpec((q, D_INNER), lambda b, c: (row(b, c), col(OFF_Z, D_INNER))),
            pl.BlockSpec((q, D_INNER), lambda b, c: (row(b, c), col(OFF_X, D_INNER))),
            pl.BlockSpec((q, 1024), lambda b, c: (row(b, c), col(OFF_B, 1024))),
            pl.BlockSpec((q, 1024), lambda b, c: (row(b, c), col(OFF_C, 1024))),
            pl.BlockSpec((q, LANES), lambda b, c: (row(b, c), 0)),
            pl.BlockSpec((1, CONV_WIDTH - 1, CONV_DIM), lambda b, c: (b, 0, 0)),
            pl.BlockSpec((1, D_INNER, SSM_STATE), lambda b, c: (b, 0, 0)),
            full((CONV_WIDTH, CONV_DIM)),
            full((1, CONV_DIM)),
            full((1, LANES)),
            full((1, LANES)),
            full((1, D_INNER)),
            full((1, D_INNER)),
            full((LANES, D_INNER)),
        ] + list(rider["in_specs"]),
        out_specs=[
            pl.BlockSpec((q, D_INNER), lambda b, c: (row(b, c), 0)),
            pl.BlockSpec((1, D_INNER, SSM_STATE), lambda b, c: (b, 0, 0)),
        ] + list(rider["out_specs"]),
        out_shape=[
            jax.ShapeDtypeStruct((t, D_INNER), BF16),
            jax.ShapeDtypeStruct((batch, D_INNER, SSM_STATE), F32),
        ] + list(rider["out_shape"]),
        scratch_shapes=[
            pltpu.VMEM((SUBLANES, CONV_DIM), F32),
            pltpu.VMEM((q, CONV_DIM), F32),
            pltpu.VMEM((SSM_GROUPS, SSM_STATE, D_INNER // SSM_GROUPS), F32),
        ],
        compiler_params=pltpu.CompilerParams(
            dimension_semantics=("parallel", "arbitrary"), vmem_limit_bytes=VMEM_LIMIT),
        name="ssm_chunked",
    )(proj, proj, proj, proj, dt_raw, conv_buf, h0, sp["w_conv"], sp["b_conv"], sp["dt_bias"],
      sp["a_log"], sp["d_skip"], sp["w_norm"], sp["expand"], *rider["args"])
    assert len(outs) == n_out + r_out
    return tuple(outs[:n_out]), tuple(outs[n_out:])


def _ssm_step_kernel(z_ref, xs_ref, bm_ref, cm_ref, dt_ref, cbuf_ref, h0_ref, wconv_ref, bconv_ref,
                     dtb_ref, alog_ref, dskip_ref, wnorm_ref, expand_ref, gsum_ref,
                     y_ref, hout_ref, tail_sc):
    q = xs_ref.shape[0]
    gw = D_INNER // SSM_GROUPS
    n = SSM_STATE
    tail_sc[...] = jnp.zeros_like(tail_sc)
    tail_sc[SUBLANES - (CONV_WIDTH - 1):SUBLANES, :] = cbuf_ref[0]
    xs = _conv_silu(xs_ref[...], tail_sc[:, 0:D_INNER], wconv_ref[:, 0:D_INNER], bconv_ref[:, 0:D_INNER])
    o_b, o_c = D_INNER, D_INNER + SSM_GROUPS * n
    bm = _conv_silu(bm_ref[...], tail_sc[:, o_b:o_c], wconv_ref[:, o_b:o_c], bconv_ref[:, o_b:o_c])
    cm = _conv_silu(cm_ref[...], tail_sc[:, o_c:], wconv_ref[:, o_c:], bconv_ref[:, o_c:])
    bm = bm.astype(BF16).astype(F32)
    cm = cm.astype(BF16).astype(F32)

    dt = _softplus(dt_ref[...] + dtb_ref[...])
    a = -jnp.exp(alog_ref[...])
    tri = (_iota((q, q), 0) >= _iota((q, q), 1)).astype(F32)
    d1, d2, d3 = _split3(dt * a)
    acs = _dot(tri, d1.astype(F32)) + _dot(tri, d2.astype(F32)) + _dot(tri, d3.astype(F32))
    a_end = acs[q - 1:q, :]

    row = _iota((q, LANES), 0)
    prods, lms = [], []
    for s in range(q):
        prods.append(cm * bm[s:s + 1, :])
        keep = row >= s
        lms.append(jnp.where(keep, jnp.exp(jnp.where(keep, acs - acs[s:s + 1, :], 0.0)), 0.0))
    qq = q * q
    cb3 = _dot(jnp.concatenate(_split3(jnp.concatenate(prods, axis=0)), axis=0), gsum_ref[...])
    cbs = cb3[0:qq] + cb3[qq:2 * qq] + cb3[2 * qq:3 * qq]
    gmat = (cbs * jnp.concatenate(lms, axis=0)).astype(BF16)
    quant = jnp.concatenate([dt, jnp.exp(acs), jnp.exp(a_end - acs),
                             jnp.broadcast_to(jnp.exp(a_end), (q, LANES))], axis=0)
    nq = 4 * q
    big = _dot(jnp.concatenate(_split3(quant) + (gmat,), axis=0), expand_ref[...])
    qx = big[0:nq] + big[nq:2 * nq] + big[2 * nq:3 * nq]
    dt_x, e_in_x, e_out_x, e_all_x = (qx[i * q:(i + 1) * q] for i in range(4))
    g_x = big[3 * nq:3 * nq + qq]
    xdt = xs * dt_x
    xdt_r = xdt.astype(BF16).astype(F32)
    y = dskip_ref[...] * xs
    for s in range(q):
        y = y + g_x[s * q:(s + 1) * q, :] * xdt_r[s:s + 1, :]
    xdte = xdt * e_out_x

    zpad = jnp.zeros((q, SSM_GROUPS * n), F32)
    cm16 = jnp.concatenate([cm, zpad], axis=0).astype(BF16)
    ones_lo = jnp.where(_iota((2 * q, n), 0) >= q, 1.0, 0.0)
    for g in range(SSM_GROUPS):
        gs = slice(g * gw, (g + 1) * gw)
        h0 = h0_ref[0, gs, :]
        yg = y[:, gs] + _dot_nt(cm16[:, g * n:(g + 1) * n], h0.astype(BF16))[0:q] * e_in_x[:, gs]
        u = yg * jax.nn.silu(z_ref[:, gs])
        u = u * lax.rsqrt(jnp.mean(u * u, axis=-1, keepdims=True) + EPS)
        y_ref[:, gs] = u * wnorm_ref[:, gs]
        e1, e2, e3 = _split3(e_all_x[:, gs])
        srow = _iota((q, gw), 0)
        e_rows = jnp.where(srow == 0, e1.astype(F32),
                           jnp.where(srow == 1, e2.astype(F32), jnp.where(srow == 2, e3.astype(F32), 0.0)))
        lhs = jnp.concatenate([xdte[:, gs].astype(BF16).astype(F32), e_rows], axis=0)
        rhs = jnp.concatenate([jnp.concatenate([bm[:, g * n:(g + 1) * n], zpad[:, 0:n]], axis=0), ones_lo],
                              axis=1)
        both = _dot_tn(lhs, rhs)
        hout_ref[0, gs, :] = both[:, n:2 * n] * h0 + both[:, 0:n]


def _ssm_step(proj, dt_raw, conv_buf, h0, sp, batch, q):
    t = proj.shape[0]
    assert t == batch * q and q == SUBLANES
    col = lambda off, w: off // w
    full = lambda shape: pl.BlockSpec(shape, lambda b: (0,) * len(shape))
    gsum = ((_iota((SSM_GROUPS * SSM_STATE, LANES), 0) // SSM_STATE
             == _iota((SSM_GROUPS * SSM_STATE, LANES), 1) // (SSM_HEADS // SSM_GROUPS))
            & (_iota((SSM_GROUPS * SSM_STATE, LANES), 1) < SSM_HEADS)).astype(BF16)
    return pl.pallas_call(
        _ssm_step_kernel,
        grid=(batch,),
        in_specs=[
            pl.BlockSpec((q, D_INNER), lambda b: (b, col(OFF_Z, D_INNER))),
            pl.BlockSpec((q, D_INNER), lambda b: (b, col(OFF_X, D_INNER))),
            pl.BlockSpec((q, 1024), lambda b: (b, col(OFF_B, 1024))),
            pl.BlockSpec((q, 1024), lambda b: (b, col(OFF_C, 1024))),
            pl.BlockSpec((q, LANES), lambda b: (b, 0)),
            pl.BlockSpec((1, CONV_WIDTH - 1, CONV_DIM), lambda b: (b, 0, 0)),
            pl.BlockSpec((1, D_INNER, SSM_STATE), lambda b: (b, 0, 0)),
            full((CONV_WIDTH, CONV_DIM)),
            full((1, CONV_DIM)),
            full((1, LANES)),
            full((1, LANES)),
            full((1, D_INNER)),
            full((1, D_INNER)),
            full((LANES, D_INNER)),
            full((SSM_GROUPS * SSM_STATE, LANES)),
        ],
        out_specs=[
            pl.BlockSpec((q, D_INNER), lambda b: (b, 0)),
            pl.BlockSpec((1, D_INNER, SSM_STATE), lambda b: (b, 0, 0)),
        ],
        out_shape=[
            jax.ShapeDtypeStruct((t, D_INNER), F32),
            jax.ShapeDtypeStruct((batch, D_INNER, SSM_STATE), F32),
        ],
        scratch_shapes=[pltpu.VMEM((SUBLANES, CONV_DIM), F32)],
        compiler_params=pltpu.CompilerParams(
            dimension_semantics=("parallel",), vmem_limit_bytes=VMEM_LIMIT),
        name="ssm_step",
    )(proj, proj, proj, proj, dt_raw, conv_buf, h0, sp["w_conv"], sp["b_conv"], sp["dt_bias"],
      sp["a_log"], sp["d_skip"], sp["w_norm"], sp["expand"], gsum)


def _prep_params(w_in, norm_mix, w_conv, b_conv, dt_bias, a_log, d_skip, w_ssm_norm, w_q_norm, w_k_norm):
    o_z, o_xbc = 0, D_INNER
    o_dt = o_xbc + CONV_DIM
    o_q = o_dt + SSM_HEADS
    o_k = o_q + ATTN_WIDTH
    o_v = o_k + ATTN_WIDTH
    o_g = o_v + ATTN_WIDTH
    w_main = jnp.concatenate(
        [w_in[:, o_z:o_dt], w_in[:, o_g:], w_in[:, o_q:o_g]], axis=1).astype(BF16)
    w_dt = jnp.pad(w_in[:, o_dt:o_q], ((0, 0), (0, LANES - SSM_HEADS))).astype(BF16)
    qk_w = jnp.ones((N_PROJ_TILES, 1, PROJ_TILE), F32)
    qk_w = qk_w.at[Q_TILE0:K_TILE0, 0].set(jnp.tile(w_q_norm, HEADS_PER_GROUP)[None])
    qk_w = qk_w.at[K_TILE0:V_TILE0, 0].set(jnp.tile(w_k_norm, HEADS_PER_GROUP)[None])
    pad_h = lambda v: jnp.pad(v, (0, LANES - SSM_HEADS))[None]
    expand = (_iota((LANES, D_INNER), 0) == _iota((LANES, D_INNER), 1) // SSM_HEAD_DIM).astype(BF16)
    ssm = dict(w_conv=w_conv, b_conv=b_conv[None], dt_bias=pad_h(dt_bias), a_log=pad_h(a_log),
               d_skip=jnp.repeat(d_skip, SSM_HEAD_DIM)[None], w_norm=w_ssm_norm[None], expand=expand)
    return dict(w_main=w_main, w_dt=w_dt, qk_w=qk_w, norm_mix=norm_mix[None], ssm=ssm)


def _attn_prompt_kernel(q_ref, k_ref, v_ref, o_ref, lse_ref, kp_sc, vp_sc, *, span, dil):
    i = pl.program_id(1)
    blk = ATTN_BLOCK
    pairs = q_ref.shape[1] // LANES

    @pl.when(i == 0)
    def _():
        kp_sc[...] = jnp.zeros_like(kp_sc)
        vp_sc[...] = jnp.zeros_like(vp_sc)

    qi = _iota((blk, 2 * blk), 0)
    kc = _iota((blk, 2 * blk), 1)
    dist = blk + qi - kc
    mask = (dist >= 0) & (dist <= span) & ((kc >= blk) | (i > 0))
    lo_q = _iota((blk, LANES), 1) < ATTN_HEAD_DIM
    lo_kv = _iota((2 * blk, LANES), 1) < ATTN_HEAD_DIM

    def residue(r, carry):
        rows = pl.ds(r, blk, stride=dil) if dil > 1 else slice(None)
        for pr in range(pairs):
            ls = slice(pr * LANES, (pr + 1) * LANES)
            q = q_ref[rows, ls].astype(BF16)
            k2 = jnp.concatenate([kp_sc[rows, ls], k_ref[rows, ls]], axis=0).astype(BF16)
            v2 = jnp.concatenate([vp_sc[rows, ls], v_ref[rows, ls]], axis=0).astype(BF16)
            ps, lses = [], []
            for keep in (lo_q, jnp.logical_not(lo_q)):
                s = jnp.where(mask, _dot_nt(jnp.where(keep, q, 0.0), k2), -jnp.inf)
                m = jnp.max(s, axis=-1, keepdims=True)
                p = jnp.exp(s - m)
                den = jnp.sum(p, axis=-1, keepdims=True)
                ps.append((p / den).astype(BF16))
                lses.append(m + jnp.log(den))
            v_bd = jnp.concatenate([jnp.where(lo_kv, v2, 0.0), jnp.where(lo_kv, 0.0, v2)], axis=0)
            o_ref[rows, ls] = _dot(jnp.concatenate(ps, axis=1), v_bd)
            lse_ref[rows, ls] = jnp.where(lo_q, lses[0], lses[1])
        return carry

    if dil > 1:
        lax.fori_loop(0, dil, residue, 0, unroll=4)
    else:
        residue(0, 0)
    kp_sc[...] = k_ref[...]
    vp_sc[...] = v_ref[...]


def _attn_prompt_grid(s, gi):
    window, dil = ATTN_PATTERNS[gi]
    chunk = ATTN_BLOCK * dil
    assert s % chunk == 0 and window // dil <= ATTN_BLOCK
    w = LANES if dil > 1 else ATTN_GROUP_WIDTH
    return w, ATTN_GROUP_WIDTH // w, s // chunk


def _attn_prompt(proj, gi, rider=None):
    window, dil = ATTN_PATTERNS[gi]
    s = proj.shape[0]
    w, split, chunks = _attn_prompt_grid(s, gi)
    col = lambda tile0: (lambda j, i: (i, (tile0 + gi) * split + j))
    blk = (ATTN_BLOCK * dil, w)
    n_in, n_out = 3, 2
    rider = rider or dict(kernel=None, in_specs=[], out_specs=[], out_shape=[], args=())
    r_in, r_out = len(rider["in_specs"]), len(rider["out_specs"])

    def body(*refs):
        ins, refs = refs[:n_in], refs[n_in:]
        r_ins, refs = refs[:r_in], refs[r_in:]
        outs, refs = refs[:n_out], refs[n_out:]
        r_outs, scratch = refs[:r_out], refs[r_out:]
        _attn_prompt_kernel(*ins, *outs, *scratch, span=window // dil, dil=dil)
        if rider["kernel"] is not None:
            rider["kernel"](*r_ins, *r_outs)

    outs = pl.pallas_call(
        body,
        grid=(split, chunks),
        in_specs=[pl.BlockSpec(blk, col(Q_TILE0)), pl.BlockSpec(blk, col(K_TILE0)),
                  pl.BlockSpec(blk, col(V_TILE0))] + list(rider["in_specs"]),
        out_specs=[pl.BlockSpec(blk, lambda j, i: (i, j))] * 2 + list(rider["out_specs"]),
        out_shape=[jax.ShapeDtypeStruct((s, ATTN_GROUP_WIDTH), F32)] * 2 + list(rider["out_shape"]),
        scratch_shapes=[pltpu.VMEM(blk, F32), pltpu.VMEM(blk, F32)],
        compiler_params=pltpu.CompilerParams(
            dimension_semantics=("parallel", "arbitrary"), vmem_limit_bytes=VMEM_LIMIT),
        name=f"attn_prompt_g{gi}",
    )(proj, proj, proj, *rider["args"])
    return tuple(outs[:n_out]), tuple(outs[n_out:])


def _attn_sample_kernel(q_ref, k_ref, v_ref, kc_ref, vc_ref, o_ref, lse_ref, ko_ref, vo_ref,
                        *, span, dil, pos0):
    l = SUBLANES
    nb = kc_ref.shape[0]
    nh, hd = HEADS_PER_GROUP, ATTN_HEAD_DIM
    rows = kc_ref.shape[3]
    w = nh * hd
    t_c = _iota((l, rows), 0)
    delta_c = rows + t_c - _iota((l, rows), 1)
    ok_c = ((delta_c <= span * dil) & ((delta_c & (dil - 1)) == 0) & (pos0 + t_c - delta_c >= 0))
    delta_n = _iota((l, l), 0) - _iota((l, l), 1)
    ok_n = (delta_n >= 0) & (delta_n <= span * dil) & ((delta_n & (dil - 1)) == 0)
    lane = _iota((hd, LANES), 1)
    hl = nh * l
    own = _shr(_iota((hl, w), 0), l) == _shr(_iota((hl, w), 1), hd)
    ok_c = jnp.concatenate([ok_c] * nh, axis=0)
    ok_n = jnp.concatenate([ok_n] * nh, axis=0)
    for bb in range(nb):
        rs = slice(bb * l, (bb + 1) * l)
        q, kn, vn = q_ref[rs, :], k_ref[rs, :], v_ref[rs, :]
        q_bd = jnp.where(own, jnp.concatenate([q] * nh, axis=0), 0.0)
        kt = kc_ref[bb].reshape(w, rows)
        vt = vc_ref[bb].reshape(w, rows)
        s_c = jnp.where(ok_c, _dot(q_bd, kt), -jnp.inf)
        s_n = jnp.where(ok_n, _dot_nt(q_bd, kn), -jnp.inf)
        m = jnp.maximum(jnp.max(s_c, axis=-1, keepdims=True), jnp.max(s_n, axis=-1, keepdims=True))
        p_c = jnp.exp(s_c - m)
        p_n = jnp.exp(s_n - m)
        den = jnp.sum(p_c, axis=-1, keepdims=True) + jnp.sum(p_n, axis=-1, keepdims=True)
        res = jnp.where(own, _dot_nt(p_c / den, vt) + _dot(p_n / den, vn), 0.0)
        lse = jnp.where(own, m + jnp.log(den), 0.0)
        o, lse_x = res[0:l], lse[0:l]
        for h in range(1, nh):
            o = o + res[h * l:(h + 1) * l]
            lse_x = lse_x + lse[h * l:(h + 1) * l]
        o_ref[rs, :] = o
        lse_ref[rs, :] = lse_x

        for new, c_ref, out_ref in ((kn, kc_ref, ko_ref), (vn, vc_ref, vo_ref)):
            padded = jnp.concatenate([jnp.zeros((LANES - l, w), F32), new], axis=0)
            tail = padded.T
            for h in range(nh):
                shifted = pltpu.roll(c_ref[bb, h], rows - l, axis=1)
                if rows > LANES:
                    out_ref[bb, h, :, 0:rows - LANES] = shifted[:, 0:rows - LANES]
                out_ref[bb, h, :, rows - LANES:rows] = jnp.where(
                    lane < LANES - l, shifted[:, rows - LANES:rows], tail[h * hd:(h + 1) * hd])


SAMPLE_CACHE_ROWS_PER_STEP = 2048
SAMPLE_MAX_SEQS_PER_STEP = 8


def _attn_sample_steps(rows, batch, steps=None):
    cap = max(1, min(batch, SAMPLE_CACHE_ROWS_PER_STEP // rows, SAMPLE_MAX_SEQS_PER_STEP))
    if steps is None:
        assert batch % cap == 0
        return cap, batch // cap
    if batch % steps or batch // steps > cap:
        return None
    return batch // steps, steps


def _attn_sample_parts(proj, k_cache, v_cache, gi, batch, pos0, step_of, steps=None):
    window, dil = ATTN_PATTERNS[gi]
    rows = k_cache.shape[1]
    l = proj.shape[0] // batch
    assert l == SUBLANES and rows % LANES == 0
    nb, _ = _attn_sample_steps(rows, batch, steps)
    tok = lambda tile0: pl.BlockSpec((nb * l, ATTN_GROUP_WIDTH), lambda *g: (step_of(*g), tile0 + gi))
    cache = pl.BlockSpec((nb, HEADS_PER_GROUP, ATTN_HEAD_DIM, rows), lambda *g: (step_of(*g), 0, 0, 0))
    out_tok = pl.BlockSpec((nb * l, ATTN_GROUP_WIDTH), lambda *g: (step_of(*g), 0))
    to_t = lambda c: jnp.transpose(c, (0, 2, 3, 1))
    t_shape = jax.ShapeDtypeStruct((batch, HEADS_PER_GROUP, ATTN_HEAD_DIM, rows), F32)
    return dict(
        kernel=functools.partial(_attn_sample_kernel, span=window // dil, dil=dil, pos0=pos0),
        in_specs=[tok(Q_TILE0), tok(K_TILE0), tok(V_TILE0), cache, cache],
        out_specs=[out_tok, out_tok, cache, cache],
        out_shape=[jax.ShapeDtypeStruct((batch * l, ATTN_GROUP_WIDTH), F32)] * 2 + [t_shape] * 2,
        args=(proj, proj, proj, to_t(k_cache), to_t(v_cache)),
    )


def _attn_sample_finish(o, lse, k_t, v_t):
    from_t = lambda c: jnp.transpose(c, (0, 3, 1, 2))
    return o, lse, from_t(k_t), from_t(v_t)


def _attn_sample(proj, k_cache, v_cache, gi, batch, pos0):
    parts = _attn_sample_parts(proj, k_cache, v_cache, gi, batch, pos0, lambda b: b)
    _, steps = _attn_sample_steps(k_cache.shape[1], batch)
    outs = pl.pallas_call(
        parts["kernel"],
        grid=(steps,),
        in_specs=parts["in_specs"],
        out_specs=parts["out_specs"],
        out_shape=parts["out_shape"],
        compiler_params=pltpu.CompilerParams(
            dimension_semantics=("parallel",), vmem_limit_bytes=VMEM_LIMIT),
        name=f"attn_sample_g{gi}",
    )(*parts["args"])
    return _attn_sample_finish(*outs)


def _merge_kernel(x_ref, ys_ref, o0_ref, o1_ref, o2_ref, l0_ref, l1_ref, l2_ref, g0_ref, g1_ref,
                  wbs_ref, wba_ref, wo_ref, nf_ref, wr1_ref, wr2_ref, br_ref,
                  x1_ref, xn_ref, lg_ref):
    l0, l1, l2 = l0_ref[...], l1_ref[...], l2_ref[...]
    m = jnp.maximum(jnp.maximum(l0, l1), l2)
    e0, e1, e2 = jnp.exp(l0 - m), jnp.exp(l1 - m), jnp.exp(l2 - m)
    den = e0 + e1 + e2
    y_attn = (e0 / den) * o0_ref[...] + (e1 / den) * o1_ref[...] + (e2 / den) * o2_ref[...]
    br_s = _dot(ys_ref[...].astype(BF16), wbs_ref[...])
    br_a = _dot(y_attn.astype(BF16), wba_ref[...])
    merged = jax.nn.sigmoid(g0_ref[...]) * br_s + jax.nn.sigmoid(g1_ref[...]) * br_a
    x1 = x_ref[...] + _dot(merged.astype(BF16), wo_ref[...])
    x1_ref[...] = x1
    xn = x1 * lax.rsqrt(jnp.mean(x1 * x1, axis=-1, keepdims=True) + EPS) * nf_ref[...]
    xn1 = xn.astype(BF16)
    xn_ref[...] = xn1
    xn2 = (xn - xn1.astype(F32)).astype(BF16)
    lg_ref[...] = (_dot(xn1, wr1_ref[...]) + _dot(xn2, wr1_ref[...]) + _dot(xn1, wr2_ref[...])
                   + br_ref[...])


def _merge(x, proj, y_ssm, attn, mp, tm):
    t = x.shape[0]
    row = lambda w: pl.BlockSpec((tm, w), lambda i: (i, 0))
    full = lambda a: pl.BlockSpec(a.shape, lambda i: (0,) * a.ndim)
    (o0, l0), (o1, l1), (o2, l2) = attn
    weights = [mp["w_br_ssm"], mp["w_br_attn"], mp["w_out"], mp["norm_ffn"], mp["w_r1"], mp["w_r2"], mp["b_r"]]
    return pl.pallas_call(
        _merge_kernel,
        grid=(t // tm,),
        in_specs=[row(D_MODEL), row(D_INNER)] + [row(ATTN_GROUP_WIDTH)] * 6 + [
            pl.BlockSpec((tm, D_MODEL), lambda i: (i, OFF_G // D_MODEL)),
            pl.BlockSpec((tm, D_MODEL), lambda i: (i, OFF_G // D_MODEL + 1)),
        ] + [full(w) for w in weights],
        out_specs=[row(D_MODEL), row(D_MODEL), row(ROUTER_WIDTH)],
        out_shape=[
            jax.ShapeDtypeStruct((t, D_MODEL), F32),
            jax.ShapeDtypeStruct((t, D_MODEL), BF16),
            jax.ShapeDtypeStruct((t, ROUTER_WIDTH), F32),
        ],
        compiler_params=pltpu.CompilerParams(
            dimension_semantics=("parallel",), vmem_limit_bytes=VMEM_LIMIT),
        name="merge",
    )(x, y_ssm, o0, o1, o2, l0, l1, l2, proj, proj, *weights)


def _prep_merge_params(w_br_ssm, w_br_attn, w_out, norm_ffn, w_router_group, b_router_group,
                       w_router_expert, b_router_expert):
    w_r = jnp.concatenate([w_router_group, w_router_expert.reshape(D_MODEL, N_EXPERTS)], axis=1)
    w_r = jnp.pad(w_r, ((0, 0), (0, ROUTER_WIDTH - w_r.shape[1])))
    w_r1 = w_r.astype(BF16)
    w_r2 = (w_r - w_r1.astype(F32)).astype(BF16)
    b_r = jnp.concatenate([b_router_group, b_router_expert.reshape(N_EXPERTS)])
    b_r = jnp.pad(b_r, (0, ROUTER_WIDTH - b_r.shape[0]))[None]
    return dict(w_br_ssm=w_br_ssm.astype(BF16), w_br_attn=w_br_attn.astype(BF16), w_out=w_out.astype(BF16),
                norm_ffn=norm_ffn[None], w_r1=w_r1, w_r2=w_r2, b_r=b_r)


def _route(logits):
    lane = _iota(logits.shape, 1)
    big = jnp.int32(2 ** 30)
    neg = -jnp.inf
    gl = jnp.where(lane < N_EXPERT_GROUPS, logits, neg)
    gmax = jnp.max(gl, axis=-1, keepdims=True)
    gidx = jnp.min(jnp.where(gl == gmax, lane, big), axis=-1, keepdims=True)
    g_w = 1.0 / jnp.sum(jnp.exp(gl - gmax), axis=-1, keepdims=True)
    lo = N_EXPERT_GROUPS + gidx * EXPERTS_PER_GROUP
    el = jnp.where((lane >= lo) & (lane < lo + EXPERTS_PER_GROUP), logits, neg)
    v1 = jnp.max(el, axis=-1, keepdims=True)
    i1 = jnp.min(jnp.where(el == v1, lane, big), axis=-1, keepdims=True)
    el2 = jnp.where(lane == i1, neg, el)
    v2 = jnp.max(el2, axis=-1, keepdims=True)
    i2 = jnp.min(jnp.where(el2 == v2, lane, big), axis=-1, keepdims=True)
    e2 = jnp.exp(v2 - v1)
    w1 = (1.0 / (1.0 + e2)) * g_w
    w2 = (e2 / (1.0 + e2)) * g_w
    return jnp.where(lane == i1, w1, jnp.where(lane == i2, w2, 0.0)), gidx


MOE_BLOCK = 256
MOE_SMALL_BLOCK = 64


def _moe_kernel(x1_ref, xn_ref, lg_ref, wg_ref, wu_ref, wd_ref, out_ref,
                comb_sc, oh_sc, cum_sc, oht_sc, cumt_sc):
    g = pl.program_id(1)
    tm = x1_ref.shape[0]
    blk = MOE_BLOCK

    @pl.when(g == 0)
    def _():
        comb, gidx = _route(lg_ref[...])
        c1, c2, c3 = _split3(comb)
        comb_sc[0] = c1
        comb_sc[1] = c2
        comb_sc[2] = c3
        onehot = (_iota((tm, LANES), 1) == gidx).astype(F32)
        oh_bf = onehot.astype(BF16)
        lower = (_iota((tm, tm), 0) > _iota((tm, tm), 1)).astype(BF16)
        upper = (_iota((tm, tm), 0) < _iota((tm, tm), 1)).astype(BF16)
        oh_sc[...] = onehot
        cum_sc[...] = _dot(lower, oh_bf)
        oht_sc[...] = onehot.T[0:SUBLANES]
        cumt_sc[...] = _dot_tn(oh_bf, upper)[0:SUBLANES]
        out_ref[...] = x1_ref[...]

    lane = _iota((tm, LANES), 1)
    pick = lane == g
    member_c = jnp.sum(jnp.where(pick, oh_sc[...], 0.0), axis=-1, keepdims=True)
    rank_c = jnp.sum(jnp.where(pick, cum_sc[...], 0.0), axis=-1, keepdims=True)
    member_r = oht_sc[pl.ds(g, 1), :]
    rank_r = cumt_sc[pl.ds(g, 1), :]
    count = jnp.sum(member_r).astype(jnp.int32)
    small = MOE_SMALL_BLOCK
    nfull = count // blk
    rem = count - nfull * blk
    use_small = rem <= blk // 2
    n_big = nfull + jnp.where(use_small, 0, 1)
    n_small = jnp.where(use_small, (rem + (small - 1)) // small, 0)
    xn = xn_ref[...]
    wd = wd_ref[...].reshape(EXPERTS_PER_GROUP * D_EXPERT, D_MODEL)

    def run_block(first_row, blk):
        lane_b = _iota((blk, LANES), 1)
        base = first_row.astype(F32)
        want_r = base + _iota((blk, tm), 0).astype(F32)
        sel = jnp.where((rank_r == want_r) & (member_r > 0.0), 1.0, 0.0).astype(BF16)
        want_c = base + _iota((tm, blk), 1).astype(F32)
        sel_t = jnp.where((rank_c == want_c) & (member_c > 0.0), 1.0, 0.0).astype(BF16)
        xg = _dot(sel, xn).astype(BF16)
        cg = _dot(sel, comb_sc[0]) + _dot(sel, comb_sc[1]) + _dot(sel, comb_sc[2])
        hidden = []
        for e in range(EXPERTS_PER_GROUP):
            col = N_EXPERT_GROUPS + g * EXPERTS_PER_GROUP + e
            cw = jnp.sum(jnp.where(lane_b == col, cg, 0.0), axis=-1, keepdims=True)
            h = jax.nn.silu(_dot(xg, wg_ref[e])) * _dot(xg, wu_ref[e]) * cw
            hidden.append(h.astype(BF16))
        out = _dot(jnp.concatenate(hidden, axis=1), wd)
        o1 = out.astype(BF16)
        o2 = (out - o1.astype(F32)).astype(BF16)
        out_ref[...] += _dot(sel_t, o1) + _dot(sel_t, o2)

    def big(k, carry):
        run_block(k * blk, blk)
        return carry

    def little(k, carry):
        run_block(nfull * blk + k * small, small)
        return carry

    lax.fori_loop(0, n_big, big, 0)
    lax.fori_loop(0, n_small, little, 0)


def _moe(x1, xn, logits, wg, wu, wd, tm):
    t = x1.shape[0]
    ge = EXPERTS_PER_GROUP
    return pl.pallas_call(
        _moe_kernel,
        grid=(t // tm, N_EXPERT_GROUPS),
        in_specs=[
            pl.BlockSpec((tm, D_MODEL), lambda i, g: (i, 0), pipeline_mode=pl.Buffered(1)),
            pl.BlockSpec((tm, D_MODEL), lambda i, g: (i, 0)),
            pl.BlockSpec((tm, ROUTER_WIDTH), lambda i, g: (i, 0), pipeline_mode=pl.Buffered(1)),
            pl.BlockSpec((ge, D_MODEL, D_EXPERT), lambda i, g: (g, 0, 0)),
            pl.BlockSpec((ge, D_MODEL, D_EXPERT), lambda i, g: (g, 0, 0)),
            pl.BlockSpec((ge, D_EXPERT, D_MODEL), lambda i, g: (g, 0, 0)),
        ],
        out_specs=pl.BlockSpec((tm, D_MODEL), lambda i, g: (i, 0)),
        out_shape=jax.ShapeDtypeStruct((t, D_MODEL), F32),
        scratch_shapes=[
            pltpu.VMEM((3, tm, ROUTER_WIDTH), BF16),
            pltpu.VMEM((tm, LANES), F32),
            pltpu.VMEM((tm, LANES), F32),
            pltpu.VMEM((SUBLANES, tm), F32),
            pltpu.VMEM((SUBLANES, tm), F32),
        ],
        compiler_params=pltpu.CompilerParams(
            dimension_semantics=("parallel", "arbitrary"), vmem_limit_bytes=VMEM_LIMIT),
        name="moe",
    )(x1, xn, logits, wg, wu, wd)


def _tile_rows(t, cap):
    tm = min(t, cap)
    assert t % tm == 0
    return tm


def _finish_layer(x2, proj, y_ssm, attn, h_new, new_kv, b, l, mp, ep):
    t, d = x2.shape
    new_conv = proj.reshape(b, l, PROJ_WIDTH)[:, l - (CONV_WIDTH - 1):, OFF_X:OFF_X + CONV_DIM]
    x1, xn, logits = _merge(x2, proj, y_ssm, attn, mp, _tile_rows(t, 512))
    y = _moe(x1, xn, logits, ep["wg"], ep["wu"], ep["wd"], _tile_rows(t, 1024))
    new_ssm = h_new.reshape(b, SSM_HEADS, SSM_HEAD_DIM, SSM_STATE)
    return y.reshape(b, l, d), (new_conv, new_ssm) + tuple(new_kv)


def _layer_pair(x_prompt, x_sample, conv_s, ssm_s, kv_s, pos0, pp, mp, ep):
    bp, lp, d = x_prompt.shape
    bs, ls, _ = x_sample.shape
    assert bp == 1 and lp % SSD_CHUNK == 0
    tp, ts = bp * lp, bs * ls
    xp2, xs2 = x_prompt.reshape(tp, d), x_sample.reshape(ts, d)
    proj_p, dt_p = _in_proj(xp2, pp["norm_mix"], pp["w_main"], pp["w_dt"], pp["qk_w"], _tile_rows(tp, 2048))
    proj_s, dt_s = _in_proj(xs2, pp["norm_mix"], pp["w_main"], pp["w_dt"], pp["qk_w"], _tile_rows(ts, 2048))

    nc = lp // SSD_CHUNK
    rider_gi = next((gi for gi in reversed(range(N_ATTN_GROUPS))
                     if _attn_sample_steps(kv_s[2 * gi].shape[1], bs)[1] == nc), None)
    rider = None
    if rider_gi is not None:
        rider = _attn_sample_parts(proj_s, kv_s[2 * rider_gi], kv_s[2 * rider_gi + 1], rider_gi, bs, pos0,
                                   lambda b, c: c)
    zero_conv = jnp.zeros((bp, CONV_WIDTH - 1, CONV_DIM), F32)
    zero_h0 = jnp.zeros((bp, D_INNER, SSM_STATE), F32)
    (y_ssm_p, h_p), rider_out = _ssm_chunked(proj_p, dt_p, zero_conv, zero_h0, pp["ssm"], bp, SSD_CHUNK, rider)

    riders_out = {rider_gi: rider_out}
    grids = [_attn_prompt_grid(tp, pg) for pg in range(N_ATTN_GROUPS)]
    pairing = {}
    for exact in (True, False):
        for pg, (_, split, chunks) in enumerate(grids):
            for g in range(N_ATTN_GROUPS):
                if pg in pairing or g in riders_out or g in pairing.values():
                    continue
                rows = kv_s[2 * g].shape[1]
                if exact and _attn_sample_steps(rows, bs)[1] == split * chunks:
                    pairing[pg] = g
                elif not exact and _attn_sample_steps(rows, bs, split * chunks) is not None:
                    pairing[pg] = g
    attn_p = []
    for pg, (_, split, chunks) in enumerate(grids):
        sg, parts = pairing.get(pg), None
        if sg is not None:
            parts = _attn_sample_parts(proj_s, kv_s[2 * sg], kv_s[2 * sg + 1], sg, bs, pos0,
                                       lambda j, i, n=chunks: j * n + i, steps=split * chunks)
        o_lse, out = _attn_prompt(proj_p, pg, parts)
        attn_p.append(o_lse)
        if sg is not None:
            riders_out[sg] = out
    proj_p3 = proj_p.reshape(bp, lp, PROJ_WIDTH)
    kv_p = []
    for gi, (window, _) in enumerate(ATTN_PATTERNS):
        keep = min(window, lp)
        for off in (OFF_K, OFF_V):
            c0 = off + gi * ATTN_GROUP_WIDTH
            kv_p.append(proj_p3[:, lp - keep:, c0:c0 + ATTN_GROUP_WIDTH]
                        .reshape(bp, keep, HEADS_PER_GROUP, ATTN_HEAD_DIM))

    y_ssm_s, h_s = _ssm_step(proj_s, dt_s, conv_s, ssm_s.reshape(bs, D_INNER, SSM_STATE), pp["ssm"], bs, ls)
    attn_s, kv_new = [], []
    for gi in range(N_ATTN_GROUPS):
        if riders_out.get(gi):
            o, lse, kn, vn = _attn_sample_finish(*riders_out[gi])
        else:
            o, lse, kn, vn = _attn_sample(proj_s, kv_s[2 * gi], kv_s[2 * gi + 1], gi, bs, pos0)
        attn_s.append((o, lse))
        kv_new += [kn, vn]

    yp, st_p = _finish_layer(xp2, proj_p, y_ssm_p, attn_p, h_p, kv_p, bp, lp, mp, ep)
    ys, st_s = _finish_layer(xs2, proj_s, y_ssm_s, attn_s, h_s, kv_new, bs, ls, mp, ep)
    return yp, st_p, ys, st_s


def kernel(x_prompt, x_sample, cache_conv, state_ssm, cache_k_w128, cache_v_w128, cache_k_w512, cache_v_w512,
           cache_k_w2048, cache_v_w2048, w_in, norm_mix, w_conv, b_conv, dt_bias, a_log, d_skip, w_ssm_norm,
           w_q_norm, w_k_norm, w_br_ssm, w_br_attn, w_out, norm_ffn, w_router_group, b_router_group,
           w_router_expert, b_router_expert, w_expert_gate, w_expert_up, w_expert_down):
    depth = w_in.shape[0]
    yp, ys = x_prompt, x_sample
    prompt_states, sample_states = [], []
    for i in range(depth):
        pp = _prep_params(w_in[i], norm_mix[i], w_conv[i], b_conv[i], dt_bias[i], a_log[i], d_skip[i],
                          w_ssm_norm[i], w_q_norm[i], w_k_norm[i])
        mp = _prep_merge_params(w_br_ssm[i], w_br_attn[i], w_out[i], norm_ffn[i], w_router_group[i],
                                b_router_group[i], w_router_expert[i], b_router_expert[i])
        ep = dict(wg=w_expert_gate[i].astype(BF16), wu=w_expert_up[i].astype(BF16),
                  wd=w_expert_down[i].astype(BF16))
        kv_bufs = (cache_k_w128[i], cache_v_w128[i], cache_k_w512[i], cache_v_w512[i],
                   cache_k_w2048[i], cache_v_w2048[i])
        yp, st_p, ys, st_s = _layer_pair(yp, ys, cache_conv[i], state_ssm[i], kv_bufs, PAST_LEN, pp, mp, ep)
        prompt_states.append(st_p)
        sample_states.append(st_s)
    stack = lambda states: [jnp.stack([s[j] for s in states]) for j in range(8)]
    return (yp, ys) + tuple(stack(prompt_states)) + tuple(stack(sample_states))
```

```python
import functools
import math

import jax
import jax.numpy as jnp
from jax import lax
from jax.experimental import pallas as pl
from jax.experimental.pallas import tpu as pltpu

F32 = jnp.float32
BF16 = jnp.bfloat16

PAST_LEN = 8192
D_MODEL = 1024
D_INNER = 2048
SSM_HEAD_DIM = 64
SSM_HEADS = 32
SSM_GROUPS = 8
SSM_STATE = 128
CONV_WIDTH = 4
CONV_DIM = D_INNER + 2 * SSM_GROUPS * SSM_STATE
SSD_CHUNK = 128
ATTN_HEAD_DIM = 64
HEADS_PER_GROUP = 8
ATTN_PATTERNS = ((128, 1), (512, 4), (2048, 16))
N_ATTN_GROUPS = 3
ATTN_GROUP_WIDTH = HEADS_PER_GROUP * ATTN_HEAD_DIM
ATTN_WIDTH = N_ATTN_GROUPS * ATTN_GROUP_WIDTH
ATTN_BLOCK = 128
N_EXPERT_GROUPS = 4
EXPERTS_PER_GROUP = 8
N_EXPERTS = 32
D_EXPERT = 256
EPS = 1e-6

LANES = 128
SUBLANES = 8
VMEM_LIMIT = 56 * 1024 * 1024

PROJ_TILE = 512
OFF_Z = 0
OFF_X = 2048
OFF_B = 4096
OFF_C = 5120
OFF_G = 6144
OFF_Q = 8192
OFF_K = OFF_Q + ATTN_WIDTH
OFF_V = OFF_K + ATTN_WIDTH
PROJ_WIDTH = OFF_V + ATTN_WIDTH
N_PROJ_TILES = PROJ_WIDTH // PROJ_TILE
Q_TILE0 = OFF_Q // PROJ_TILE
K_TILE0 = OFF_K // PROJ_TILE
V_TILE0 = OFF_V // PROJ_TILE
ROUTER_WIDTH = LANES


def _split3(x):
    x1 = x.astype(BF16)
    r1 = x - x1.astype(F32)
    x2 = r1.astype(BF16)
    r2 = r1 - x2.astype(F32)
    return x1, x2, r2.astype(BF16)


def _dot(a, b):
    return jnp.dot(a, b, preferred_element_type=F32)


def _dot3(x, m):
    x1, x2, x3 = _split3(x)
    return _dot(x1, m) + _dot(x2, m) + _dot(x3, m)


def _dot3f(x, m):
    x1, x2, x3 = _split3(x)
    return _dot(x1.astype(F32), m) + _dot(x2.astype(F32), m) + _dot(x3.astype(F32), m)


def _dot_nt(a, b):
    return lax.dot_general(a, b, (((1,), (1,)), ((), ())), preferred_element_type=F32)


def _dot_tn(a, b):
    return lax.dot_general(a, b, (((0,), (0,)), ((), ())), preferred_element_type=F32)


def _iota(shape, dim):
    return lax.broadcasted_iota(jnp.int32, shape, dim)


def _shr(x, pow2):
    assert pow2 & (pow2 - 1) == 0
    return lax.shift_right_logical(x, jnp.int32(int(math.log2(pow2))))


def _in_proj_kernel(x_ref, nw_ref, w_ref, wdt_ref, qkw_ref, seg_ref, out_ref, dt_ref, h_sc):
    j = pl.program_id(1)

    @pl.when(j == 0)
    def _():
        xf = x_ref[...]
        y = xf * lax.rsqrt(jnp.mean(xf * xf, axis=-1, keepdims=True) + EPS)
        h = (y * nw_ref[...]).astype(BF16)
        h_sc[...] = h
        dt_ref[...] = _dot(h, wdt_ref[...])

    acc = _dot(h_sc[...], w_ref[...])
    is_qk = jnp.logical_and(j >= Q_TILE0, j < V_TILE0)

    @pl.when(jnp.logical_not(is_qk))
    def _():
        out_ref[...] = acc

    @pl.when(is_qk)
    def _():
        sq = acc * acc
        s1 = sq.astype(BF16)
        s2 = (sq - s1.astype(F32)).astype(BF16)
        ms = (_dot(s1, seg_ref[...]) + _dot(s2, seg_ref[...])) * (1.0 / ATTN_HEAD_DIM)
        scale = jnp.where(j < K_TILE0, ATTN_HEAD_DIM ** -0.5, 1.0).astype(F32)
        w = qkw_ref[0] * scale
        out_ref[...] = acc * lax.rsqrt(ms + EPS) * w


def _in_proj(x, norm_w, w_bf, wdt_bf, qk_w, tm):
    t = x.shape[0]
    seg = (_iota((PROJ_TILE, PROJ_TILE), 0) // ATTN_HEAD_DIM
           == _iota((PROJ_TILE, PROJ_TILE), 1) // ATTN_HEAD_DIM).astype(BF16)
    return pl.pallas_call(
        _in_proj_kernel,
        grid=(t // tm, N_PROJ_TILES),
        in_specs=[
            pl.BlockSpec((tm, D_MODEL), lambda i, j: (i, 0)),
            pl.BlockSpec((1, D_MODEL), lambda i, j: (0, 0), pipeline_mode=pl.Buffered(1)),
            pl.BlockSpec((D_MODEL, PROJ_TILE), lambda i, j: (0, j)),
            pl.BlockSpec((D_MODEL, LANES), lambda i, j: (0, 0), pipeline_mode=pl.Buffered(1)),
            pl.BlockSpec((1, 1, PROJ_TILE), lambda i, j: (j, 0, 0)),
            pl.BlockSpec((PROJ_TILE, PROJ_TILE), lambda i, j: (0, 0), pipeline_mode=pl.Buffered(1)),
        ],
        out_specs=[
            pl.BlockSpec((tm, PROJ_TILE), lambda i, j: (i, j)),
            pl.BlockSpec((tm, LANES), lambda i, j: (i, 0)),
        ],
        out_shape=[
            jax.ShapeDtypeStruct((t, PROJ_WIDTH), F32),
            jax.ShapeDtypeStruct((t, LANES), F32),
        ],
        scratch_shapes=[pltpu.VMEM((tm, D_MODEL), BF16)],
        compiler_params=pltpu.CompilerParams(
            dimension_semantics=("parallel", "arbitrary"), vmem_limit_bytes=VMEM_LIMIT),
        name="in_proj",
    )(x, norm_w, w_bf, wdt_bf, qk_w, seg)


def _softplus(x):
    return jnp.maximum(x, 0.0) + jnp.log1p(jnp.exp(-jnp.abs(x)))


def _conv_silu(raw, tail, w, b):
    q = raw.shape[0]
    up = jnp.concatenate([tail, raw], axis=0)
    acc = b + w[CONV_WIDTH - 1:CONV_WIDTH] * raw
    for tap in range(CONV_WIDTH - 1):
        shifted = pltpu.roll(up, CONV_WIDTH - 1 - tap, axis=0)[SUBLANES:SUBLANES + q]
        acc = acc + w[tap:tap + 1] * shifted
    return jax.nn.silu(acc)


def _ssm_chunk_kernel(z_ref, xs_ref, bm_ref, cm_ref, dt_ref, cbuf_ref, h0_ref, wconv_ref, bconv_ref,
                      dtb_ref, alog_ref, dskip_ref, wnorm_ref, expand_ref,
                      y_ref, hout_ref, tail_sc, xbc_sc, ht_sc):
    c = pl.program_id(1)
    nc = pl.num_programs(1)
    q = xs_ref.shape[0]
    gw = D_INNER // SSM_GROUPS
    n = SSM_STATE

    @pl.when(c == 0)
    def _():
        tail_sc[...] = jnp.zeros_like(tail_sc)
        tail_sc[SUBLANES - (CONV_WIDTH - 1):SUBLANES, :] = cbuf_ref[0]
        for g in range(SSM_GROUPS):
            ht_sc[g] = h0_ref[0, g * gw:(g + 1) * gw, :].T

    cw = PROJ_TILE
    for k in range(CONV_DIM // cw):
        if k < D_INNER // cw:
            raw = xs_ref[:, k * cw:(k + 1) * cw]
        elif k < (D_INNER + SSM_GROUPS * n) // cw:
            kk = k - D_INNER // cw
            raw = bm_ref[:, kk * cw:(kk + 1) * cw]
        else:
            kk = k - (D_INNER + SSM_GROUPS * n) // cw
            raw = cm_ref[:, kk * cw:(kk + 1) * cw]
        sl = slice(k * cw, (k + 1) * cw)
        xbc_sc[:, sl] = _conv_silu(raw, tail_sc[:, sl], wconv_ref[:, sl], bconv_ref[:, sl])
        tail_sc[:, sl] = raw[q - SUBLANES:q]

    dt = _softplus(dt_ref[...] + dtb_ref[...])
    a = -jnp.exp(alog_ref[...])
    dta = dt * a
    tri = (_iota((q, q), 0) >= _iota((q, q), 1)).astype(BF16)
    d1, d2, d3 = _split3(dta)
    acs = _dot(tri, d1) + _dot(tri, d2) + _dot(tri, d3)
    eye = (_iota((LANES, LANES), 0) == _iota((LANES, LANES), 1)).astype(BF16)
    c1, c2, c3 = _split3(acs)
    acs_t = _dot_nt(eye, c1) + _dot_nt(eye, c2) + _dot_nt(eye, c3)
    a_end = acs[q - 1:q, :]
    e_in = jnp.exp(acs)
    e_out = jnp.exp(a_end - acs)
    e_all = jnp.exp(a_end)

    ex = expand_ref[...]
    dt_x = _dot3(dt, ex)
    e_in_x = _dot3(e_in, ex)
    e_out_x = _dot3(e_out, ex)
    e_all_x = _dot3(jnp.broadcast_to(e_all, (2 * SUBLANES, LANES)), ex)[0:1]

    causal = _iota((q, q), 0) >= _iota((q, q), 1)
    lane_lo = _iota((q, LANES), 1) < SSM_HEAD_DIM

    for g in range(SSM_GROUPS):
        gs = slice(g * gw, (g + 1) * gw)
        xg = xbc_sc[:, gs]
        bg = xbc_sc[:, D_INNER + g * n:D_INNER + (g + 1) * n].astype(BF16)
        cg = xbc_sc[:, D_INNER + SSM_GROUPS * n + g * n:D_INNER + SSM_GROUPS * n + (g + 1) * n].astype(BF16)
        xdt = xg * dt_x[:, gs]
        cb = _dot_nt(cg, bg)
        ht = ht_sc[g]
        y_off = _dot(cg, ht.astype(BF16)) * e_in_x[:, gs]
        pieces = []
        for pair in range(2):
            ms = []
            for hh in range(2):
                h = g * 4 + pair * 2 + hh
                diff = acs[:, h:h + 1] - acs_t[h:h + 1, :]
                lmat = jnp.where(causal, jnp.exp(jnp.where(causal, diff, 0.0)), 0.0)
                ms.append((cb * lmat).astype(BF16))
            xp = xdt[:, pair * LANES:(pair + 1) * LANES]
            rhs = jnp.concatenate([jnp.where(lane_lo, xp, 0.0), jnp.where(lane_lo, 0.0, xp)],
                                  axis=0).astype(BF16)
            pieces.append(_dot(jnp.concatenate(ms, axis=1), rhs))
        y_diag = jnp.concatenate(pieces, axis=1)
        yg = y_diag + y_off + dskip_ref[:, gs] * xg
        zg = z_ref[:, gs]
        u = yg * jax.nn.silu(zg)
        u = u * lax.rsqrt(jnp.mean(u * u, axis=-1, keepdims=True) + EPS)
        y_ref[:, gs] = (u * wnorm_ref[:, gs]).astype(y_ref.dtype)
        st = _dot_tn(bg, (xdt * e_out_x[:, gs]).astype(BF16))
        ht_sc[g] = e_all_x[:, gs] * ht + st

    @pl.when(c == nc - 1)
    def _():
        for g in range(SSM_GROUPS):
            hout_ref[0, g * gw:(g + 1) * gw, :] = ht_sc[g].T


def _ssm_chunked(proj, dt_raw, conv_buf, h0, sp, batch, q, rider=None):
    t = proj.shape[0]
    nc = t // batch // q
    row = lambda b, c: b * nc + c
    col = lambda off, w: off // w
    full = lambda shape: pl.BlockSpec(shape, lambda b, c: (0,) * len(shape), pipeline_mode=pl.Buffered(1))
    n_in, n_out = 14, 2
    rider = rider or dict(kernel=None, in_specs=[], out_specs=[], out_shape=[], args=())
    r_in, r_out = len(rider["in_specs"]), len(rider["out_specs"])

    def body(*refs):
        ins, refs = refs[:n_in], refs[n_in:]
        r_ins, refs = refs[:r_in], refs[r_in:]
        outs, refs = refs[:n_out], refs[n_out:]
        r_outs, scratch = refs[:r_out], refs[r_out:]
        _ssm_chunk_kernel(*ins, *outs, *scratch)
        if rider["kernel"] is not None:
            rider["kernel"](*r_ins, *r_outs)

    outs = pl.pallas_call(
        body,
        grid=(batch, nc),
        in_specs=[
            pl.BlockSpec((q, D_INNER), lambda b, c: (row(b, c), col(OFF_Z, D_INNER))),
            pl.BlockSpec((q, D_INNER), lambda b, c: (row(b, c), col(OFF_X, D_INNER))),
            pl.BlockSpec((q, 1024), lambda b, c: (row(b, c), col(OFF_B, 1024))),
            pl.BlockSpec((q, 1024), lambda b, c: (row(b, c), col(OFF_C, 1024))),
            pl.BlockSpec((q, LANES), lambda b, c: (row(b, c), 0)),
            pl.BlockSpec((1, CONV_WIDTH - 1, CONV_DIM), lambda b, c: (b, 0, 0)),
            pl.BlockSpec((1, D_INNER, SSM_STATE), lambda b, c: (b, 0, 0)),
            full((CONV_WIDTH, CONV_DIM)),
            full((1, CONV_DIM)),
            full((1, LANES)),
            full((1, LANES)),
            full((1, D_INNER)),
            full((1, D_INNER)),
            full((LANES, D_INNER)),
        ] + list(rider["in_specs"]),
        out_specs=[
            pl.BlockSpec((q, D_INNER), lambda b, c: (row(b, c), 0)),
            pl.BlockSpec((1, D_INNER, SSM_STATE), lambda b, c: (b, 0, 0)),
        ] + list(rider["out_specs"]),
        out_shape=[
            jax.ShapeDtypeStruct((t, D_INNER), BF16),
            jax.ShapeDtypeStruct((batch, D_INNER, SSM_STATE), F32),
        ] + list(rider["out_shape"]),
        scratch_shapes=[
            pltpu.VMEM((SUBLANES, CONV_DIM), F32),
            pltpu.VMEM((q, CONV_DIM), F32),
            pltpu.VMEM((SSM_GROUPS, SSM_STATE, D_INNER // SSM_GROUPS), F32),
        ],
        compiler_params=pltpu.CompilerParams(
            dimension_semantics=("parallel", "arbitrary"), vmem_limit_bytes=VMEM_LIMIT),
        name="ssm_chunked",
    )(proj, proj, proj, proj, dt_raw, conv_buf, h0, sp["w_conv"], sp["b_conv"], sp["dt_bias"],
      sp["a_log"], sp["d_skip"], sp["w_norm"], sp["expand"], *rider["args"])
    assert len(outs) == n_out + r_out
    return tuple(outs[:n_out]), tuple(outs[n_out:])


def _ssm_step_kernel(z_ref, xs_ref, bm_ref, cm_ref, dt_ref, cbuf_ref, h0_ref, wconv_ref, bconv_ref,
                     dtb_ref, alog_ref, dskip_ref, wnorm_ref, expand_ref, gsum_ref,
                     y_ref, hout_ref, tail_sc):
    q = xs_ref.shape[0]
    gw = D_INNER // SSM_GROUPS
    n = SSM_STATE
    tail_sc[...] = jnp.zeros_like(tail_sc)
    tail_sc[SUBLANES - (CONV_WIDTH - 1):SUBLANES, :] = cbuf_ref[0]
    xs = _conv_silu(xs_ref[...], tail_sc[:, 0:D_INNER], wconv_ref[:, 0:D_INNER], bconv_ref[:, 0:D_INNER])
    o_b, o_c = D_INNER, D_INNER + SSM_GROUPS * n
    bm = _conv_silu(bm_ref[...], tail_sc[:, o_b:o_c], wconv_ref[:, o_b:o_c], bconv_ref[:, o_b:o_c])
    cm = _conv_silu(cm_ref[...], tail_sc[:, o_c:], wconv_ref[:, o_c:], bconv_ref[:, o_c:])
    bm = bm.astype(BF16).astype(F32)
    cm = cm.astype(BF16).astype(F32)

    dt = _softplus(dt_ref[...] + dtb_ref[...])
    a = -jnp.exp(alog_ref[...])
    tri = (_iota((q, q), 0) >= _iota((q, q), 1)).astype(F32)
    d1, d2, d3 = _split3(dt * a)
    acs = _dot(tri, d1.astype(F32)) + _dot(tri, d2.astype(F32)) + _dot(tri, d3.astype(F32))
    a_end = acs[q - 1:q, :]

    row = _iota((q, LANES), 0)
    prods, lms = [], []
    for s in range(q):
        prods.append(cm * bm[s:s + 1, :])
        keep = row >= s
        lms.append(jnp.where(keep, jnp.exp(jnp.where(keep, acs - acs[s:s + 1, :], 0.0)), 0.0))
    qq = q * q
    cb3 = _dot(jnp.concatenate(_split3(jnp.concatenate(prods, axis=0)), axis=0), gsum_ref[...])
    cbs = cb3[0:qq] + cb3[qq:2 * qq] + cb3[2 * qq:3 * qq]
    gmat = (cbs * jnp.concatenate(lms, axis=0)).astype(BF16)
    quant = jnp.concatenate([dt, jnp.exp(acs), jnp.exp(a_end - acs),
                             jnp.broadcast_to(jnp.exp(a_end), (q, LANES))], axis=0)
    nq = 4 * q
    big = _dot(jnp.concatenate(_split3(quant) + (gmat,), axis=0), expand_ref[...])
    qx = big[0:nq] + big[nq:2 * nq] + big[2 * nq:3 * nq]
    dt_x, e_in_x, e_out_x, e_all_x = (qx[i * q:(i + 1) * q] for i in range(4))
    g_x = big[3 * nq:3 * nq + qq]
    xdt = xs * dt_x
    xdt_r = xdt.astype(BF16).astype(F32)
    y = dskip_ref[...] * xs
    for s in range(q):
        y = y + g_x[s * q:(s + 1) * q, :] * xdt_r[s:s + 1, :]
    xdte = xdt * e_out_x

    zpad = jnp.zeros((q, SSM_GROUPS * n), F32)
    cm16 = jnp.concatenate([cm, zpad], axis=0).astype(BF16)
    ones_lo = jnp.where(_iota((2 * q, n), 0) >= q, 1.0, 0.0)
    for g in range(SSM_GROUPS):
        gs = slice(g * gw, (g + 1) * gw)
        h0 = h0_ref[0, gs, :]
        yg = y[:, gs] + _dot_nt(cm16[:, g * n:(g + 1) * n], h0.astype(BF16))[0:q] * e_in_x[:, gs]
        u = yg * jax.nn.silu(z_ref[:, gs])
        u = u * lax.rsqrt(jnp.mean(u * u, axis=-1, keepdims=True) + EPS)
        y_ref[:, gs] = u * wnorm_ref[:, gs]
        e1, e2, e3 = _split3(e_all_x[:, gs])
        srow = _iota((q, gw), 0)
        e_rows = jnp.where(srow == 0, e1.astype(F32),
                           jnp.where(srow == 1, e2.astype(F32), jnp.where(srow == 2, e3.astype(F32), 0.0)))
        lhs = jnp.concatenate([xdte[:, gs].astype(BF16).astype(F32), e_rows], axis=0)
        rhs = jnp.concatenate([jnp.concatenate([bm[:, g * n:(g + 1) * n], zpad[:, 0:n]], axis=0), ones_lo],
                              axis=1)
        both = _dot_tn(lhs, rhs)
        hout_ref[0, gs, :] = both[:, n:2 * n] * h0 + both[:, 0:n]


def _ssm_step(proj, dt_raw, conv_buf, h0, sp, batch, q):
    t = proj.shape[0]
    assert t == batch * q and q == SUBLANES
    col = lambda off, w: off // w
    full = lambda shape: pl.BlockSpec(shape, lambda b: (0,) * len(shape), pipeline_mode=pl.Buffered(1))
    gsum = ((_iota((SSM_GROUPS * SSM_STATE, LANES), 0) // SSM_STATE
             == _iota((SSM_GROUPS * SSM_STATE, LANES), 1) // (SSM_HEADS // SSM_GROUPS))
            & (_iota((SSM_GROUPS * SSM_STATE, LANES), 1) < SSM_HEADS)).astype(BF16)
    return pl.pallas_call(
        _ssm_step_kernel,
        grid=(batch,),
        in_specs=[
            pl.BlockSpec((q, D_INNER), lambda b: (b, col(OFF_Z, D_INNER))),
            pl.BlockSpec((q, D_INNER), lambda b: (b, col(OFF_X, D_INNER))),
            pl.BlockSpec((q, 1024), lambda b: (b, col(OFF_B, 1024))),
            pl.BlockSpec((q, 1024), lambda b: (b, col(OFF_C, 1024))),
            pl.BlockSpec((q, LANES), lambda b: (b, 0)),
            pl.BlockSpec((1, CONV_WIDTH - 1, CONV_DIM), lambda b: (b, 0, 0)),
            pl.BlockSpec((1, D_INNER, SSM_STATE), lambda b: (b, 0, 0)),
            full((CONV_WIDTH, CONV_DIM)),
            full((1, CONV_DIM)),
            full((1, LANES)),
            full((1, LANES)),
            full((1, D_INNER)),
            full((1, D_INNER)),
            full((LANES, D_INNER)),
            full((SSM_GROUPS * SSM_STATE, LANES)),
        ],
        out_specs=[
            pl.BlockSpec((q, D_INNER), lambda b: (b, 0)),
            pl.BlockSpec((1, D_INNER, SSM_STATE), lambda b: (b, 0, 0)),
        ],
        out_shape=[
            jax.ShapeDtypeStruct((t, D_INNER), F32),
            jax.ShapeDtypeStruct((batch, D_INNER, SSM_STATE), F32),
        ],
        scratch_shapes=[pltpu.VMEM((SUBLANES, CONV_DIM), F32)],
        compiler_params=pltpu.CompilerParams(
            dimension_semantics=("parallel",), vmem_limit_bytes=VMEM_LIMIT),
        name="ssm_step",
    )(proj, proj, proj, proj, dt_raw, conv_buf, h0, sp["w_conv"], sp["b_conv"], sp["dt_bias"],
      sp["a_log"], sp["d_skip"], sp["w_norm"], sp["expand"], gsum)


def _prep_params(w_in, norm_mix, w_conv, b_conv, dt_bias, a_log, d_skip, w_ssm_norm, w_q_norm, w_k_norm):
    o_z, o_xbc = 0, D_INNER
    o_dt = o_xbc + CONV_DIM
    o_q = o_dt + SSM_HEADS
    o_k = o_q + ATTN_WIDTH
    o_v = o_k + ATTN_WIDTH
    o_g = o_v + ATTN_WIDTH
    w_main = jnp.concatenate(
        [w_in[:, o_z:o_dt], w_in[:, o_g:], w_in[:, o_q:o_g]], axis=1).astype(BF16)
    w_dt = jnp.pad(w_in[:, o_dt:o_q], ((0, 0), (0, LANES - SSM_HEADS))).astype(BF16)
    qk_w = jnp.ones((N_PROJ_TILES, 1, PROJ_TILE), F32)
    qk_w = qk_w.at[Q_TILE0:K_TILE0, 0].set(jnp.tile(w_q_norm, HEADS_PER_GROUP)[None])
    qk_w = qk_w.at[K_TILE0:V_TILE0, 0].set(jnp.tile(w_k_norm, HEADS_PER_GROUP)[None])
    pad_h = lambda v: jnp.pad(v, (0, LANES - SSM_HEADS))[None]
    expand = (_iota((LANES, D_INNER), 0) == _iota((LANES, D_INNER), 1) // SSM_HEAD_DIM).astype(BF16)
    ssm = dict(w_conv=w_conv, b_conv=b_conv[None], dt_bias=pad_h(dt_bias), a_log=pad_h(a_log),
               d_skip=jnp.repeat(d_skip, SSM_HEAD_DIM)[None], w_norm=w_ssm_norm[None], expand=expand)
    return dict(w_main=w_main, w_dt=w_dt, qk_w=qk_w, norm_mix=norm_mix[None], ssm=ssm)


def _attn_prompt_kernel(q_ref, k_ref, v_ref, o_ref, lse_ref, kp_sc, vp_sc, *, span, dil):
    i = pl.program_id(1)
    blk = ATTN_BLOCK
    pairs = q_ref.shape[1] // LANES

    @pl.when(i == 0)
    def _():
        kp_sc[...] = jnp.zeros_like(kp_sc)
        vp_sc[...] = jnp.zeros_like(vp_sc)

    qi = _iota((blk, 2 * blk), 0)
    kc = _iota((blk, 2 * blk), 1)
    dist = blk + qi - kc
    mask = (dist >= 0) & (dist <= span) & ((kc >= blk) | (i > 0))
    lo_q = _iota((blk, LANES), 1) < ATTN_HEAD_DIM
    lo_kv = _iota((2 * blk, LANES), 1) < ATTN_HEAD_DIM

    def residue(r, carry):
        rows = pl.ds(r, blk, stride=dil) if dil > 1 else slice(None)
        for pr in range(pairs):
            ls = slice(pr * LANES, (pr + 1) * LANES)
            q = q_ref[rows, ls].astype(BF16)
            k2 = jnp.concatenate([kp_sc[rows, ls], k_ref[rows, ls]], axis=0).astype(BF16)
            v2 = jnp.concatenate([vp_sc[rows, ls], v_ref[rows, ls]], axis=0).astype(BF16)
            ps, lses = [], []
            for keep in (lo_q, jnp.logical_not(lo_q)):
                s = jnp.where(mask, _dot_nt(jnp.where(keep, q, 0.0), k2), -jnp.inf)
                m = jnp.max(s, axis=-1, keepdims=True)
                p = jnp.exp(s - m)
                den = jnp.sum(p, axis=-1, keepdims=True)
                ps.append((p / den).astype(BF16))
                lses.append(m + jnp.log(den))
            v_bd = jnp.concatenate([jnp.where(lo_kv, v2, 0.0), jnp.where(lo_kv, 0.0, v2)], axis=0)
            o_ref[rows, ls] = _dot(jnp.concatenate(ps, axis=1), v_bd)
            lse_ref[rows, ls] = jnp.where(lo_q, lses[0], lses[1])
        return carry

    if dil > 1:
        lax.fori_loop(0, dil, residue, 0, unroll=4)
    else:
        residue(0, 0)
    kp_sc[...] = k_ref[...]
    vp_sc[...] = v_ref[...]


def _attn_prompt_grid(s, gi):
    window, dil = ATTN_PATTERNS[gi]
    chunk = ATTN_BLOCK * dil
    assert s % chunk == 0 and window // dil <= ATTN_BLOCK
    w = LANES if dil > 1 else ATTN_GROUP_WIDTH
    return w, ATTN_GROUP_WIDTH // w, s // chunk


def _attn_prompt(proj, gi, rider=None):
    window, dil = ATTN_PATTERNS[gi]
    s = proj.shape[0]
    w, split, chunks = _attn_prompt_grid(s, gi)
    col = lambda tile0: (lambda j, i: (i, (tile0 + gi) * split + j))
    blk = (ATTN_BLOCK * dil, w)
    n_in, n_out = 3, 2
    rider = rider or dict(kernel=None, in_specs=[], out_specs=[], out_shape=[], args=())
    r_in, r_out = len(rider["in_specs"]), len(rider["out_specs"])

    def body(*refs):
        ins, refs = refs[:n_in], refs[n_in:]
        r_ins, refs = refs[:r_in], refs[r_in:]
        outs, refs = refs[:n_out], refs[n_out:]
        r_outs, scratch = refs[:r_out], refs[r_out:]
        _attn_prompt_kernel(*ins, *outs, *scratch, span=window // dil, dil=dil)
        if rider["kernel"] is not None:
            rider["kernel"](*r_ins, *r_outs)

    outs = pl.pallas_call(
        body,
        grid=(split, chunks),
        in_specs=[pl.BlockSpec(blk, col(Q_TILE0)), pl.BlockSpec(blk, col(K_TILE0)),
                  pl.BlockSpec(blk, col(V_TILE0))] + list(rider["in_specs"]),
        out_specs=[pl.BlockSpec(blk, lambda j, i: (i, j))] * 2 + list(rider["out_specs"]),
        out_shape=[jax.ShapeDtypeStruct((s, ATTN_GROUP_WIDTH), F32)] * 2 + list(rider["out_shape"]),
        scratch_shapes=[pltpu.VMEM(blk, F32), pltpu.VMEM(blk, F32)],
        compiler_params=pltpu.CompilerParams(
            dimension_semantics=("parallel", "arbitrary"), vmem_limit_bytes=VMEM_LIMIT),
        name=f"attn_prompt_g{gi}",
    )(proj, proj, proj, *rider["args"])
    return tuple(outs[:n_out]), tuple(outs[n_out:])


def _attn_sample_kernel(q_ref, k_ref, v_ref, kc_ref, vc_ref, o_ref, lse_ref, ko_ref, vo_ref,
                        *, span, dil, pos0):
    l = SUBLANES
    nb = kc_ref.shape[0]
    nh, hd = HEADS_PER_GROUP, ATTN_HEAD_DIM
    rows = kc_ref.shape[3]
    w = nh * hd
    t_c = _iota((l, rows), 0)
    delta_c = rows + t_c - _iota((l, rows), 1)
    ok_c = ((delta_c <= span * dil) & ((delta_c & (dil - 1)) == 0) & (pos0 + t_c - delta_c >= 0))
    delta_n = _iota((l, l), 0) - _iota((l, l), 1)
    ok_n = (delta_n >= 0) & (delta_n <= span * dil) & ((delta_n & (dil - 1)) == 0)
    lane = _iota((hd, LANES), 1)
    hl = nh * l
    own = _shr(_iota((hl, w), 0), l) == _shr(_iota((hl, w), 1), hd)
    ok_c = jnp.concatenate([ok_c] * nh, axis=0)
    ok_n = jnp.concatenate([ok_n] * nh, axis=0)
    for bb in range(nb):
        rs = slice(bb * l, (bb + 1) * l)
        q, kn, vn = q_ref[rs, :], k_ref[rs, :], v_ref[rs, :]
        q_bd = jnp.where(own, jnp.concatenate([q] * nh, axis=0), 0.0)
        kt = kc_ref[bb].reshape(w, rows)
        vt = vc_ref[bb].reshape(w, rows)
        s_c = jnp.where(ok_c, _dot(q_bd, kt), -jnp.inf)
        s_n = jnp.where(ok_n, _dot_nt(q_bd, kn), -jnp.inf)
        m = jnp.maximum(jnp.max(s_c, axis=-1, keepdims=True), jnp.max(s_n, axis=-1, keepdims=True))
        p_c = jnp.exp(s_c - m)
        p_n = jnp.exp(s_n - m)
        den = jnp.sum(p_c, axis=-1, keepdims=True) + jnp.sum(p_n, axis=-1, keepdims=True)
        res = jnp.where(own, _dot_nt(p_c / den, vt) + _dot(p_n / den, vn), 0.0)
        lse = jnp.where(own, m + jnp.log(den), 0.0)
        o, lse_x = res[0:l], lse[0:l]
        for h in range(1, nh):
            o = o + res[h * l:(h + 1) * l]
            lse_x = lse_x + lse[h * l:(h + 1) * l]
        o_ref[rs, :] = o
        lse_ref[rs, :] = lse_x

        for new, c_ref, out_ref in ((kn, kc_ref, ko_ref), (vn, vc_ref, vo_ref)):
            padded = jnp.concatenate([jnp.zeros((LANES - l, w), F32), new], axis=0)
            tail = padded.T
            for h in range(nh):
                shifted = pltpu.roll(c_ref[bb, h], rows - l, axis=1)
                if rows > LANES:
                    out_ref[bb, h, :, 0:rows - LANES] = shifted[:, 0:rows - LANES]
                out_ref[bb, h, :, rows - LANES:rows] = jnp.where(
                    lane < LANES - l, shifted[:, rows - LANES:rows], tail[h * hd:(h + 1) * hd])


SAMPLE_CACHE_ROWS_PER_STEP = 2048
SAMPLE_MAX_SEQS_PER_STEP = 8


def _attn_sample_steps(rows, batch):
    nb = max(1, min(batch, SAMPLE_CACHE_ROWS_PER_STEP // rows, SAMPLE_MAX_SEQS_PER_STEP))
    assert batch % nb == 0
    return nb, batch // nb


def _attn_sample_parts(proj, k_cache, v_cache, gi, batch, pos0, step_of):
    window, dil = ATTN_PATTERNS[gi]
    rows = k_cache.shape[1]
    l = proj.shape[0] // batch
    assert l == SUBLANES and rows % LANES == 0
    nb, _ = _attn_sample_steps(rows, batch)
    tok = lambda tile0: pl.BlockSpec((nb * l, ATTN_GROUP_WIDTH), lambda *g: (step_of(*g), tile0 + gi))
    cache = pl.BlockSpec((nb, HEADS_PER_GROUP, ATTN_HEAD_DIM, rows), lambda *g: (step_of(*g), 0, 0, 0))
    out_tok = pl.BlockSpec((nb * l, ATTN_GROUP_WIDTH), lambda *g: (step_of(*g), 0))
    to_t = lambda c: jnp.transpose(c, (0, 2, 3, 1))
    t_shape = jax.ShapeDtypeStruct((batch, HEADS_PER_GROUP, ATTN_HEAD_DIM, rows), F32)
    return dict(
        kernel=functools.partial(_attn_sample_kernel, span=window // dil, dil=dil, pos0=pos0),
        in_specs=[tok(Q_TILE0), tok(K_TILE0), tok(V_TILE0), cache, cache],
        out_specs=[out_tok, out_tok, cache, cache],
        out_shape=[jax.ShapeDtypeStruct((batch * l, ATTN_GROUP_WIDTH), F32)] * 2 + [t_shape] * 2,
        args=(proj, proj, proj, to_t(k_cache), to_t(v_cache)),
    )


def _attn_sample_finish(o, lse, k_t, v_t):
    from_t = lambda c: jnp.transpose(c, (0, 3, 1, 2))
    return o, lse, from_t(k_t), from_t(v_t)


def _attn_sample(proj, k_cache, v_cache, gi, batch, pos0):
    parts = _attn_sample_parts(proj, k_cache, v_cache, gi, batch, pos0, lambda b: b)
    _, steps = _attn_sample_steps(k_cache.shape[1], batch)
    outs = pl.pallas_call(
        parts["kernel"],
        grid=(steps,),
        in_specs=parts["in_specs"],
        out_specs=parts["out_specs"],
        out_shape=parts["out_shape"],
        compiler_params=pltpu.CompilerParams(
            dimension_semantics=("parallel",), vmem_limit_bytes=VMEM_LIMIT),
        name=f"attn_sample_g{gi}",
    )(*parts["args"])
    return _attn_sample_finish(*outs)


def _merge_kernel(x_ref, ys_ref, o0_ref, o1_ref, o2_ref, l0_ref, l1_ref, l2_ref, g0_ref, g1_ref,
                  wbs_ref, wba_ref, wo_ref, nf_ref, wr1_ref, wr2_ref, br_ref,
                  x1_ref, xn_ref, lg_ref):
    l0, l1, l2 = l0_ref[...], l1_ref[...], l2_ref[...]
    m = jnp.maximum(jnp.maximum(l0, l1), l2)
    e0, e1, e2 = jnp.exp(l0 - m), jnp.exp(l1 - m), jnp.exp(l2 - m)
    den = e0 + e1 + e2
    y_attn = (e0 / den) * o0_ref[...] + (e1 / den) * o1_ref[...] + (e2 / den) * o2_ref[...]
    br_s = _dot(ys_ref[...].astype(BF16), wbs_ref[...])
    br_a = _dot(y_attn.astype(BF16), wba_ref[...])
    merged = jax.nn.sigmoid(g0_ref[...]) * br_s + jax.nn.sigmoid(g1_ref[...]) * br_a
    x1 = x_ref[...] + _dot(merged.astype(BF16), wo_ref[...])
    x1_ref[...] = x1
    xn = x1 * lax.rsqrt(jnp.mean(x1 * x1, axis=-1, keepdims=True) + EPS) * nf_ref[...]
    xn1 = xn.astype(BF16)
    xn_ref[...] = xn1
    xn2 = (xn - xn1.astype(F32)).astype(BF16)
    lg_ref[...] = (_dot(xn1, wr1_ref[...]) + _dot(xn2, wr1_ref[...]) + _dot(xn1, wr2_ref[...])
                   + br_ref[...])


def _merge(x, proj, y_ssm, attn, mp, tm):
    t = x.shape[0]
    row = lambda w: pl.BlockSpec((tm, w), lambda i: (i, 0))
    full = lambda a: pl.BlockSpec(a.shape, lambda i: (0,) * a.ndim, pipeline_mode=pl.Buffered(1))
    (o0, l0), (o1, l1), (o2, l2) = attn
    weights = [mp["w_br_ssm"], mp["w_br_attn"], mp["w_out"], mp["norm_ffn"], mp["w_r1"], mp["w_r2"], mp["b_r"]]
    return pl.pallas_call(
        _merge_kernel,
        grid=(t // tm,),
        in_specs=[row(D_MODEL), row(D_INNER)] + [row(ATTN_GROUP_WIDTH)] * 6 + [
            pl.BlockSpec((tm, D_MODEL), lambda i: (i, OFF_G // D_MODEL)),
            pl.BlockSpec((tm, D_MODEL), lambda i: (i, OFF_G // D_MODEL + 1)),
        ] + [full(w) for w in weights],
        out_specs=[row(D_MODEL), row(D_MODEL), row(ROUTER_WIDTH)],
        out_shape=[
            jax.ShapeDtypeStruct((t, D_MODEL), F32),
            jax.ShapeDtypeStruct((t, D_MODEL), BF16),
            jax.ShapeDtypeStruct((t, ROUTER_WIDTH), F32),
        ],
        compiler_params=pltpu.CompilerParams(
            dimension_semantics=("parallel",), vmem_limit_bytes=VMEM_LIMIT),
        name="merge",
    )(x, y_ssm, o0, o1, o2, l0, l1, l2, proj, proj, *weights)


def _prep_merge_params(w_br_ssm, w_br_attn, w_out, norm_ffn, w_router_group, b_router_group,
                       w_router_expert, b_router_expert):
    w_r = jnp.concatenate([w_router_group, w_router_expert.reshape(D_MODEL, N_EXPERTS)], axis=1)
    w_r = jnp.pad(w_r, ((0, 0), (0, ROUTER_WIDTH - w_r.shape[1])))
    w_r1 = w_r.astype(BF16)
    w_r2 = (w_r - w_r1.astype(F32)).astype(BF16)
    b_r = jnp.concatenate([b_router_group, b_router_expert.reshape(N_EXPERTS)])
    b_r = jnp.pad(b_r, (0, ROUTER_WIDTH - b_r.shape[0]))[None]
    return dict(w_br_ssm=w_br_ssm.astype(BF16), w_br_attn=w_br_attn.astype(BF16), w_out=w_out.astype(BF16),
                norm_ffn=norm_ffn[None], w_r1=w_r1, w_r2=w_r2, b_r=b_r)


def _route(logits):
    lane = _iota(logits.shape, 1)
    big = jnp.int32(2 ** 30)
    neg = -jnp.inf
    gl = jnp.where(lane < N_EXPERT_GROUPS, logits, neg)
    gmax = jnp.max(gl, axis=-1, keepdims=True)
    gidx = jnp.min(jnp.where(gl == gmax, lane, big), axis=-1, keepdims=True)
    g_w = 1.0 / jnp.sum(jnp.exp(gl - gmax), axis=-1, keepdims=True)
    lo = N_EXPERT_GROUPS + gidx * EXPERTS_PER_GROUP
    el = jnp.where((lane >= lo) & (lane < lo + EXPERTS_PER_GROUP), logits, neg)
    v1 = jnp.max(el, axis=-1, keepdims=True)
    i1 = jnp.min(jnp.where(el == v1, lane, big), axis=-1, keepdims=True)
    el2 = jnp.where(lane == i1, neg, el)
    v2 = jnp.max(el2, axis=-1, keepdims=True)
    i2 = jnp.min(jnp.where(el2 == v2, lane, big), axis=-1, keepdims=True)
    e2 = jnp.exp(v2 - v1)
    w1 = (1.0 / (1.0 + e2)) * g_w
    w2 = (e2 / (1.0 + e2)) * g_w
    return jnp.where(lane == i1, w1, jnp.where(lane == i2, w2, 0.0)), gidx


MOE_BLOCK = 256
MOE_SMALL_BLOCK = 64


def _moe_kernel(x1_ref, xn_ref, lg_ref, wg_ref, wu_ref, wd_ref, out_ref,
                comb_sc, oh_sc, cum_sc, oht_sc, cumt_sc):
    g = pl.program_id(1)
    tm = x1_ref.shape[0]
    blk = MOE_BLOCK

    @pl.when(g == 0)
    def _():
        comb, gidx = _route(lg_ref[...])
        c1, c2, c3 = _split3(comb)
        comb_sc[0] = c1
        comb_sc[1] = c2
        comb_sc[2] = c3
        onehot = (_iota((tm, LANES), 1) == gidx).astype(F32)
        oh_bf = onehot.astype(BF16)
        lower = (_iota((tm, tm), 0) > _iota((tm, tm), 1)).astype(BF16)
        upper = (_iota((tm, tm), 0) < _iota((tm, tm), 1)).astype(BF16)
        oh_sc[...] = onehot
        cum_sc[...] = _dot(lower, oh_bf)
        oht_sc[...] = onehot.T[0:SUBLANES]
        cumt_sc[...] = _dot_tn(oh_bf, upper)[0:SUBLANES]
        out_ref[...] = x1_ref[...]

    lane = _iota((tm, LANES), 1)
    pick = lane == g
    member_c = jnp.sum(jnp.where(pick, oh_sc[...], 0.0), axis=-1, keepdims=True)
    rank_c = jnp.sum(jnp.where(pick, cum_sc[...], 0.0), axis=-1, keepdims=True)
    member_r = oht_sc[pl.ds(g, 1), :]
    rank_r = cumt_sc[pl.ds(g, 1), :]
    count = jnp.sum(member_r).astype(jnp.int32)
    small = MOE_SMALL_BLOCK
    nfull = count // blk
    rem = count - nfull * blk
    use_small = rem <= blk // 2
    n_big = nfull + jnp.where(use_small, 0, 1)
    n_small = jnp.where(use_small, (rem + (small - 1)) // small, 0)
    xn = xn_ref[...]
    wd = wd_ref[...].reshape(EXPERTS_PER_GROUP * D_EXPERT, D_MODEL)

    def run_block(first_row, blk):
        lane_b = _iota((blk, LANES), 1)
        base = first_row.astype(F32)
        want_r = base + _iota((blk, tm), 0).astype(F32)
        sel = jnp.where((rank_r == want_r) & (member_r > 0.0), 1.0, 0.0).astype(BF16)
        want_c = base + _iota((tm, blk), 1).astype(F32)
        sel_t = jnp.where((rank_c == want_c) & (member_c > 0.0), 1.0, 0.0).astype(BF16)
        xg = _dot(sel, xn).astype(BF16)
        cg = _dot(sel, comb_sc[0]) + _dot(sel, comb_sc[1]) + _dot(sel, comb_sc[2])
        hidden = []
        for e in range(EXPERTS_PER_GROUP):
            col = N_EXPERT_GROUPS + g * EXPERTS_PER_GROUP + e
            cw = jnp.sum(jnp.where(lane_b == col, cg, 0.0), axis=-1, keepdims=True)
            h = jax.nn.silu(_dot(xg, wg_ref[e])) * _dot(xg, wu_ref[e]) * cw
            hidden.append(h.astype(BF16))
        out = _dot(jnp.concatenate(hidden, axis=1), wd)
        o1 = out.astype(BF16)
        o2 = (out - o1.astype(F32)).astype(BF16)
        out_ref[...] += _dot(sel_t, o1) + _dot(sel_t, o2)

    def big(k, carry):
        run_block(k * blk, blk)
        return carry

    def little(k, carry):
        run_block(nfull * blk + k * small, small)
        return carry

    lax.fori_loop(0, n_big, big, 0)
    lax.fori_loop(0, n_small, little, 0)


def _moe(x1, xn, logits, wg, wu, wd, tm):
    t = x1.shape[0]
    ge = EXPERTS_PER_GROUP
    return pl.pallas_call(
        _moe_kernel,
        grid=(t // tm, N_EXPERT_GROUPS),
        in_specs=[
            pl.BlockSpec((tm, D_MODEL), lambda i, g: (i, 0), pipeline_mode=pl.Buffered(1)),
            pl.BlockSpec((tm, D_MODEL), lambda i, g: (i, 0)),
            pl.BlockSpec((tm, ROUTER_WIDTH), lambda i, g: (i, 0), pipeline_mode=pl.Buffered(1)),
            pl.BlockSpec((ge, D_MODEL, D_EXPERT), lambda i, g: (g, 0, 0)),
            pl.BlockSpec((ge, D_MODEL, D_EXPERT), lambda i, g: (g, 0, 0)),
            pl.BlockSpec((ge, D_EXPERT, D_MODEL), lambda i, g: (g, 0, 0)),
        ],
        out_specs=pl.BlockSpec((tm, D_MODEL), lambda i, g: (i, 0)),
        out_shape=jax.ShapeDtypeStruct((t, D_MODEL), F32),
        scratch_shapes=[
            pltpu.VMEM((3, tm, ROUTER_WIDTH), BF16),
            pltpu.VMEM((tm, LANES), F32),
            pltpu.VMEM((tm, LANES), F32),
            pltpu.VMEM((SUBLANES, tm), F32),
            pltpu.VMEM((SUBLANES, tm), F32),
        ],
        compiler_params=pltpu.CompilerParams(
            dimension_semantics=("parallel", "arbitrary"), vmem_limit_bytes=VMEM_LIMIT),
        name="moe",
    )(x1, xn, logits, wg, wu, wd)


def _tile_rows(t, cap):
    tm = min(t, cap)
    assert t % tm == 0
    return tm


def _finish_layer(x2, proj, y_ssm, attn, h_new, new_kv, b, l, mp, ep):
    t, d = x2.shape
    new_conv = proj.reshape(b, l, PROJ_WIDTH)[:, l - (CONV_WIDTH - 1):, OFF_X:OFF_X + CONV_DIM]
    x1, xn, logits = _merge(x2, proj, y_ssm, attn, mp, _tile_rows(t, 512))
    y = _moe(x1, xn, logits, ep["wg"], ep["wu"], ep["wd"], _tile_rows(t, 1024))
    new_ssm = h_new.reshape(b, SSM_HEADS, SSM_HEAD_DIM, SSM_STATE)
    return y.reshape(b, l, d), (new_conv, new_ssm) + tuple(new_kv)


def _layer_pair(x_prompt, x_sample, conv_s, ssm_s, kv_s, pos0, pp, mp, ep):
    bp, lp, d = x_prompt.shape
    bs, ls, _ = x_sample.shape
    assert bp == 1 and lp % SSD_CHUNK == 0
    tp, ts = bp * lp, bs * ls
    xp2, xs2 = x_prompt.reshape(tp, d), x_sample.reshape(ts, d)
    proj_p, dt_p = _in_proj(xp2, pp["norm_mix"], pp["w_main"], pp["w_dt"], pp["qk_w"], _tile_rows(tp, 2048))
    proj_s, dt_s = _in_proj(xs2, pp["norm_mix"], pp["w_main"], pp["w_dt"], pp["qk_w"], _tile_rows(ts, 2048))

    nc = lp // SSD_CHUNK
    rider_gi = next((gi for gi in reversed(range(N_ATTN_GROUPS))
                     if _attn_sample_steps(kv_s[2 * gi].shape[1], bs)[1] == nc), None)
    rider = None
    if rider_gi is not None:
        rider = _attn_sample_parts(proj_s, kv_s[2 * rider_gi], kv_s[2 * rider_gi + 1], rider_gi, bs, pos0,
                                   lambda b, c: c)
    zero_conv = jnp.zeros((bp, CONV_WIDTH - 1, CONV_DIM), F32)
    zero_h0 = jnp.zeros((bp, D_INNER, SSM_STATE), F32)
    (y_ssm_p, h_p), rider_out = _ssm_chunked(proj_p, dt_p, zero_conv, zero_h0, pp["ssm"], bp, SSD_CHUNK, rider)

    riders_out = {rider_gi: rider_out}
    attn_p = []
    for pg in range(N_ATTN_GROUPS):
        _, split, chunks = _attn_prompt_grid(tp, pg)
        sg = next((g for g in range(N_ATTN_GROUPS) if g not in riders_out
                   and _attn_sample_steps(kv_s[2 * g].shape[1], bs)[1] == split * chunks), None)
        parts = None
        if sg is not None:
            parts = _attn_sample_parts(proj_s, kv_s[2 * sg], kv_s[2 * sg + 1], sg, bs, pos0,
                                       lambda j, i, n=chunks: j * n + i)
        o_lse, out = _attn_prompt(proj_p, pg, parts)
        attn_p.append(o_lse)
        if sg is not None:
            riders_out[sg] = out
    proj_p3 = proj_p.reshape(bp, lp, PROJ_WIDTH)
    kv_p = []
    for gi, (window, _) in enumerate(ATTN_PATTERNS):
        keep = min(window, lp)
        for off in (OFF_K, OFF_V):
            c0 = off + gi * ATTN_GROUP_WIDTH
            kv_p.append(proj_p3[:, lp - keep:, c0:c0 + ATTN_GROUP_WIDTH]
                        .reshape(bp, keep, HEADS_PER_GROUP, ATTN_HEAD_DIM))

    y_ssm_s, h_s = _ssm_step(proj_s, dt_s, conv_s, ssm_s.reshape(bs, D_INNER, SSM_STATE), pp["ssm"], bs, ls)
    attn_s, kv_new = [], []
    for gi in range(N_ATTN_GROUPS):
        if riders_out.get(gi):
            o, lse, kn, vn = _attn_sample_finish(*riders_out[gi])
        else:
            o, lse, kn, vn = _attn_sample(proj_s, kv_s[2 * gi], kv_s[2 * gi + 1], gi, bs, pos0)
        attn_s.append((o, lse))
        kv_new += [kn, vn]

    yp, st_p = _finish_layer(xp2, proj_p, y_ssm_p, attn_p, h_p, kv_p, bp, lp, mp, ep)
    ys, st_s = _finish_layer(xs2, proj_s, y_ssm_s, attn_s, h_s, kv_new, bs, ls, mp, ep)
    return yp, st_p, ys, st_s


def kernel(x_prompt, x_sample, cache_conv, state_ssm, cache_k_w128, cache_v_w128, cache_k_w512, cache_v_w512,
           cache_k_w2048, cache_v_w2048, w_in, norm_mix, w_conv, b_conv, dt_bias, a_log, d_skip, w_ssm_norm,
           w_q_norm, w_k_norm, w_br_ssm, w_br_attn, w_out, norm_ffn, w_router_group, b_router_group,
           w_router_expert, b_router_expert, w_expert_gate, w_expert_up, w_expert_down):
    depth = w_in.shape[0]
    yp, ys = x_prompt, x_sample
    prompt_states, sample_states = [], []
    for i in range(depth):
        pp = _prep_params(w_in[i], norm_mix[i], w_conv[i], b_conv[i], dt_bias[i], a_log[i], d_skip[i],
                          w_ssm_norm[i], w_q_norm[i], w_k_norm[i])
        mp = _prep_merge_params(w_br_ssm[i], w_br_attn[i], w_out[i], norm_ffn[i], w_router_group[i],
                                b_router_group[i], w_router_expert[i], b_router_expert[i])
        ep = dict(wg=w_expert_gate[i].astype(BF16), wu=w_expert_up[i].astype(BF16),
                  wd=w_expert_down[i].astype(BF16))
        kv_bufs = (cache_k_w128[i], cache_v_w128[i], cache_k_w512[i], cache_v_w512[i],
                   cache_k_w2048[i], cache_v_w2048[i])
        yp, st_p, ys, st_s = _layer_pair(yp, ys, cache_conv[i], state_ssm[i], kv_bufs, PAST_LEN, pp, mp, ep)
        prompt_states.append(st_p)
        sample_states.append(st_s)
    stack = lambda states: [jnp.stack([s[j] for s in states]) for j in range(8)]
    return (yp, ys) + tuple(stack(prompt_states)) + tuple(stack(sample_states))
```
